```python
import jax, jax.numpy as jnp
from jax import lax
import numpy as np

D_MODEL = 2048
BATCH = 8
SEQ = 8192
DEPTH = 4

CHUNK = 64
EPS = 1e-6
MIN_FORGET = 1e-30
A_VAL = 128
A_KEY = 128
A_VW = D_MODEL // 2
A_HEADS = A_VW // A_VAL
A_KW = A_HEADS * A_KEY
B_WIDTH = D_MODEL // 2
B_WINDOWS = (2, 4, 8, 16)
B_GROUPS = len(B_WINDOWS)
B_GW = B_WIDTH // B_GROUPS
IN_SIZES = (A_KW, A_KW, A_VW, A_VW, B_WIDTH, B_WIDTH, D_MODEL, D_MODEL)
IN_COLS = sum(IN_SIZES)
IN_SPLITS = [int(v) for v in np.cumsum(IN_SIZES)[:-1]]

kernel_name = "hybrid_hgrn2_pool_gated_trunk"


def rmsnorm(x, gain):
    xf = x.astype(jnp.float32)
    y = xf * lax.rsqrt(jnp.mean(xf * xf, axis=-1, keepdims=True) + EPS)
    return y.astype(x.dtype) * gain


def hgrn2_mixer(q_raw, f_raw, v, lb):
    f32 = jnp.float32
    bsz, seq, _ = q_raw.shape
    nc = seq // CHUNK
    q = jax.nn.silu(q_raw.astype(f32))
    a = f_raw.astype(f32)
    lb = lb.astype(f32)
    f = lb + (1.0 - lb) * jax.nn.sigmoid(a)
    log_f = jnp.log(jnp.maximum(f, MIN_FORGET))
    k = (1.0 - lb) * jax.nn.sigmoid(-a)
    v = v.astype(f32)

    def to_chunks(t, d):
        return t.reshape(bsz, nc, CHUNK, A_HEADS, d).transpose(1, 0, 3, 2, 4)

    qc, kc, lfc = to_chunks(q, A_KEY), to_chunks(k, A_KEY), to_chunks(log_f, A_KEY)
    vc = to_chunks(v, A_VAL)
    causal = jnp.tril(jnp.ones((CHUNK, CHUNK), dtype=bool))[:, :, None]
    causal_f = causal.astype(f32)

    def step(state, inp):
        qi, ki, vi, lfi = inp
        b = jnp.cumsum(lfi, axis=2)
        o_inter = jnp.einsum('bhtk,bhkv->bhtv', qi * jnp.exp(b), state)
        diff = b[:, :, :, None, :] - b[:, :, None, :, :]
        decay = jnp.exp(jnp.where(causal, diff, 0.0)) * causal_f
        scores = jnp.einsum('bhtk,bhtsk,bhsk->bhts', qi, decay, ki)
        o = o_inter + jnp.einsum('bhts,bhsv->bhtv', scores, vi)
        b_last = b[:, :, -1:, :]
        state = (jnp.exp(b_last[:, :, 0, :])[..., None] * state
                 + jnp.einsum('bhsk,bhsv->bhkv', ki * jnp.exp(b_last - b), vi))
        return state, o

    state0 = jnp.zeros((bsz, A_HEADS, A_KEY, A_VAL), f32)
    _, o = lax.scan(step, state0, (qc, kc, vc, lfc))
    return o.transpose(1, 0, 3, 2, 4).reshape(bsz, seq, A_HEADS, A_VAL)


def pool_mixer(v, pool_w, pool_scale):
    f32 = jnp.float32
    bsz, seq, _ = v.shape
    vf = v.astype(f32)
    cs = jnp.concatenate([jnp.zeros((bsz, 1, B_WIDTH), f32), lax.cumsum(vf, axis=1)], axis=1)
    pos = jnp.arange(1, seq + 1, dtype=f32)[None, :, None]
    outs = []
    for g, w in enumerate(B_WINDOWS):
        sl = slice(g * B_GW, (g + 1) * B_GW)
        cs_g = cs[:, :, sl]
        shifted = jnp.pad(cs_g, ((0, 0), (w, 0), (0, 0)))[:, :seq + 1]
        mean = (cs_g - shifted)[:, 1:] / jnp.minimum(pos, float(w))
        outs.append(mean - vf[:, :, sl])
    pooled = jnp.stack(outs, axis=2)
    mixed = jnp.einsum('bsgc,gcd->bsgd', pooled, pool_w.astype(f32)).reshape(bsz, seq, B_WIDTH)
    return (mixed * pool_scale.astype(f32)).astype(v.dtype)


def _fwd_setup_inputs(seed: int = 0) -> dict:
    key = jax.random.key(seed)
    ks = jax.random.split(key, 16)
    nrm = jax.random.normal
    f32 = jnp.float32
    return {
        "x": nrm(ks[0], (BATCH, SEQ, D_MODEL), f32),
        "c": nrm(ks[1], (BATCH, D_MODEL), f32),
        "w_ada": nrm(ks[2], (DEPTH, D_MODEL, 3 * D_MODEL), f32) * (0.5 * D_MODEL ** -0.5),
        "b_ada": nrm(ks[3], (DEPTH, 3 * D_MODEL), f32) * 0.02,
        "norm_pre": 1.0 + 0.1 * nrm(ks[4], (DEPTH, D_MODEL), f32),
        "norm_post": 1.0 + 0.1 * nrm(ks[5], (DEPTH, D_MODEL), f32),
        "w_in": nrm(ks[6], (DEPTH, D_MODEL, IN_COLS), f32) * D_MODEL ** -0.5,
        "lower_bounds": nrm(ks[7], (DEPTH, A_KW), f32),
        "hgrn_norm": 1.0 + 0.1 * nrm(ks[8], (DEPTH, A_VW), f32),
        "pool_w": nrm(ks[9], (DEPTH, B_GROUPS, B_GW, B_GW), f32) * B_GW ** -0.5,
        "pool_scale": 1.0 + 0.1 * nrm(ks[10], (DEPTH, B_WIDTH), f32),
        "w_proj_a": nrm(ks[11], (DEPTH, A_VW, D_MODEL), f32) * A_VW ** -0.5,
        "w_proj_b": nrm(ks[12], (DEPTH, B_WIDTH, D_MODEL), f32) * B_WIDTH ** -0.5,
        "w_out": nrm(ks[13], (DEPTH, D_MODEL, D_MODEL), f32) * D_MODEL ** -0.5,
    }


def _fwd_reference(x, c, w_ada, b_ada, norm_pre, norm_post, w_in, lower_bounds, hgrn_norm,
              pool_w, pool_scale, w_proj_a, w_proj_b, w_out):
    bsz, seq, _ = x.shape
    sm = jax.nn.softmax(lower_bounds.astype(jnp.float32), axis=0)
    lb_all = jnp.cumsum(sm, axis=0) - sm[0:1]
    c_act = jax.nn.silu(c)
    for l in range(DEPTH):
        mod = c_act @ w_ada[l] + b_ada[l]
        shift, scale, gate = jnp.split(mod, 3, axis=-1)
        h = rmsnorm(x, norm_pre[l]) * (1.0 + scale[:, None, :]) + shift[:, None, :]
        proj = h @ w_in[l]
        q_a, f_a, v_a, z_a, v_b, z_b, g_a, g_b = jnp.split(proj, IN_SPLITS, axis=-1)
        o_a = hgrn2_mixer(q_a, f_a, v_a, lb_all[l])
        o_a = o_a * lax.rsqrt(jnp.mean(o_a * o_a, axis=-1, keepdims=True) + EPS)
        y_a = o_a.reshape(bsz, seq, A_VW).astype(x.dtype) * hgrn_norm[l] * jax.nn.silu(z_a)
        y_b = pool_mixer(v_b, pool_w[l], pool_scale[l]) * jax.nn.silu(z_b)
        merged = jax.nn.sigmoid(g_a) * (y_a @ w_proj_a[l]) + jax.nn.sigmoid(g_b) * (y_b @ w_proj_b[l])
        out = merged @ w_out[l]
        x = x + gate[:, None, :] * rmsnorm(out, norm_post[l])
    return x


import jax as _jax
import jax.numpy as _jnp

TWIN_FORMAT = 'train_step'
FWD_PARAMS = ['x', 'c', 'w_ada', 'b_ada', 'norm_pre', 'norm_post', 'w_in', 'lower_bounds', 'hgrn_norm', 'pool_w', 'pool_scale', 'w_proj_a', 'w_proj_b', 'w_out']
TWIN_WEIGHTS = ['w_ada', 'b_ada', 'norm_pre', 'norm_post', 'w_in', 'lower_bounds', 'hgrn_norm', 'pool_w', 'pool_scale', 'w_proj_a', 'w_proj_b', 'w_out']
TWIN_DIFF_INPUT = 'x'
TWIN_INPUTS = ['x', 'c', 'w_ada', 'b_ada', 'norm_pre', 'norm_post', 'w_in', 'lower_bounds', 'hgrn_norm', 'pool_w', 'pool_scale', 'w_proj_a', 'w_proj_b', 'w_out', 'loss_target', 'm_w_ada', 'm_b_ada', 'm_norm_pre', 'm_norm_post', 'm_w_in', 'm_lower_bounds', 'm_hgrn_norm', 'm_pool_w', 'm_pool_scale', 'm_w_proj_a', 'm_w_proj_b', 'm_w_out', 'v_w_ada', 'v_b_ada', 'v_norm_pre', 'v_norm_post', 'v_w_in', 'v_lower_bounds', 'v_hgrn_norm', 'v_pool_w', 'v_pool_scale', 'v_w_proj_a', 'v_w_proj_b', 'v_w_out']
TWIN_OUTPUTS = ['loss', 'grad_x', 'grad_w_ada', 'grad_b_ada', 'grad_norm_pre', 'grad_norm_post', 'grad_w_in', 'grad_lower_bounds', 'grad_hgrn_norm', 'grad_pool_w', 'grad_pool_scale', 'grad_w_proj_a', 'grad_w_proj_b', 'grad_w_out', 'delta_w_ada', 'delta_b_ada', 'delta_norm_pre', 'delta_norm_post', 'delta_w_in', 'delta_lower_bounds', 'delta_hgrn_norm', 'delta_pool_w', 'delta_pool_scale', 'delta_w_proj_a', 'delta_w_proj_b', 'delta_w_out', 'new_m_w_ada', 'new_m_b_ada', 'new_m_norm_pre', 'new_m_norm_post', 'new_m_w_in', 'new_m_lower_bounds', 'new_m_hgrn_norm', 'new_m_pool_w', 'new_m_pool_scale', 'new_m_w_proj_a', 'new_m_w_proj_b', 'new_m_w_out', 'new_v_w_ada', 'new_v_b_ada', 'new_v_norm_pre', 'new_v_norm_post', 'new_v_w_in', 'new_v_lower_bounds', 'new_v_hgrn_norm', 'new_v_pool_w', 'new_v_pool_scale', 'new_v_w_proj_a', 'new_v_w_proj_b', 'new_v_w_out']
TWIN_LEAF_KINDS = {'loss': 'loss', 'grad_x': 'grad_x', 'grad_w_ada': 'grad_w', 'grad_b_ada': 'grad_w', 'grad_norm_pre': 'grad_w', 'grad_norm_post': 'grad_w', 'grad_w_in': 'grad_w', 'grad_lower_bounds': 'grad_w', 'grad_hgrn_norm': 'grad_w', 'grad_pool_w': 'grad_w', 'grad_pool_scale': 'grad_w', 'grad_w_proj_a': 'grad_w', 'grad_w_proj_b': 'grad_w', 'grad_w_out': 'grad_w', 'delta_w_ada': 'delta_w', 'delta_b_ada': 'delta_w', 'delta_norm_pre': 'delta_w', 'delta_norm_post': 'delta_w', 'delta_w_in': 'delta_w', 'delta_lower_bounds': 'delta_w', 'delta_hgrn_norm': 'delta_w', 'delta_pool_w': 'delta_w', 'delta_pool_scale': 'delta_w', 'delta_w_proj_a': 'delta_w', 'delta_w_proj_b': 'delta_w', 'delta_w_out': 'delta_w', 'new_m_w_ada': 'new_m', 'new_m_b_ada': 'new_m', 'new_m_norm_pre': 'new_m', 'new_m_norm_post': 'new_m', 'new_m_w_in': 'new_m', 'new_m_lower_bounds': 'new_m', 'new_m_hgrn_norm': 'new_m', 'new_m_pool_w': 'new_m', 'new_m_pool_scale': 'new_m', 'new_m_w_proj_a': 'new_m', 'new_m_w_proj_b': 'new_m', 'new_m_w_out': 'new_m', 'new_v_w_ada': 'new_v', 'new_v_b_ada': 'new_v', 'new_v_norm_pre': 'new_v', 'new_v_norm_post': 'new_v', 'new_v_w_in': 'new_v', 'new_v_lower_bounds': 'new_v', 'new_v_hgrn_norm': 'new_v', 'new_v_pool_w': 'new_v', 'new_v_pool_scale': 'new_v', 'new_v_w_proj_a': 'new_v', 'new_v_w_proj_b': 'new_v', 'new_v_w_out': 'new_v'}


def _forward(args):
    return _fwd_reference(*[args[k] for k in FWD_PARAMS])


def _output_shape():
    def fwd():
        inp = _fwd_setup_inputs(0)
        return _fwd_reference(*[inp[k] for k in FWD_PARAMS])
    out = _jax.eval_shape(fwd)
    return out.shape, out.dtype

N_MICROBATCH = 1
ADAM_LR = 0.001
ADAM_B1 = 0.9
ADAM_B2 = 0.999
ADAM_EPS = 1e-08
ADAM_WD = 0.01
ADAM_STEP = 10
PER_EXAMPLE_BATCH_AXIS = {'x': 0, 'c': 0, 'loss_target': 0}
SHARED_INPUTS = []
_WEIGHT_DTYPES = {'w_ada': _jnp.float32, 'b_ada': _jnp.float32, 'norm_pre': _jnp.float32, 'norm_post': _jnp.float32, 'w_in': _jnp.float32, 'lower_bounds': _jnp.float32, 'hgrn_norm': _jnp.float32, 'pool_w': _jnp.float32, 'pool_scale': _jnp.float32, 'w_proj_a': _jnp.float32, 'w_proj_b': _jnp.float32, 'w_out': _jnp.float32}
MOMENT_SCALE = {'w_ada': 1.191858e+00, 'b_ada': 2.622781e+00, 'norm_pre': 9.323498e-02, 'norm_post': 3.219047e+00, 'w_in': 5.110595e-02, 'lower_bounds': 4.623262e-03, 'hgrn_norm': 1.062766e-01, 'pool_w': 6.606641e-02, 'pool_scale': 6.959197e-02, 'w_proj_a': 7.009250e-02, 'w_proj_b': 4.815446e-02, 'w_out': 8.718872e-02}


def _to_microbatches(a, axis):
    t = _jnp.moveaxis(a, axis, 0)
    t = t.reshape((N_MICROBATCH, t.shape[0] // N_MICROBATCH) + t.shape[1:])
    return _jnp.moveaxis(t, 1, axis + 1)


def setup_inputs(seed: int = 0) -> dict:
    inp = _fwd_setup_inputs(seed)
    key = _jax.random.fold_in(_jax.random.key(seed), 7919)
    shape, _ = _output_shape()
    out = dict(inp)
    out["loss_target"] = _jax.random.normal(_jax.random.fold_in(key, 0), shape, _jnp.float32)
    for i, name in enumerate(TWIN_WEIGHTS):
        w = inp[name].astype(_jnp.float32)
        if MOMENT_SCALE is None:
            s = _jnp.sqrt(_jnp.mean(_jnp.square(w)) + 1e-30)
        else:
            s = MOMENT_SCALE[name]
        km, kv = _jax.random.split(_jax.random.fold_in(key, i + 1))
        out[name] = w
        out["m_" + name] = s * _jax.random.normal(km, w.shape, _jnp.float32)
        out["v_" + name] = (s * s) * _jax.random.uniform(kv, w.shape, _jnp.float32, 0.5, 1.5)
    if N_MICROBATCH > 1:
        for name, axis in PER_EXAMPLE_BATCH_AXIS.items():
            out[name] = _to_microbatches(out[name], axis)
    return {'x': out['x'], 'c': out['c'], 'w_ada': out['w_ada'], 'b_ada': out['b_ada'], 'norm_pre': out['norm_pre'], 'norm_post': out['norm_post'], 'w_in': out['w_in'], 'lower_bounds': out['lower_bounds'], 'hgrn_norm': out['hgrn_norm'], 'pool_w': out['pool_w'], 'pool_scale': out['pool_scale'], 'w_proj_a': out['w_proj_a'], 'w_proj_b': out['w_proj_b'], 'w_out': out['w_out'], 'loss_target': out['loss_target'], 'm_w_ada': out['m_w_ada'], 'm_b_ada': out['m_b_ada'], 'm_norm_pre': out['m_norm_pre'], 'm_norm_post': out['m_norm_post'], 'm_w_in': out['m_w_in'], 'm_lower_bounds': out['m_lower_bounds'], 'm_hgrn_norm': out['m_hgrn_norm'], 'm_pool_w': out['m_pool_w'], 'm_pool_scale': out['m_pool_scale'], 'm_w_proj_a': out['m_w_proj_a'], 'm_w_proj_b': out['m_w_proj_b'], 'm_w_out': out['m_w_out'], 'v_w_ada': out['v_w_ada'], 'v_b_ada': out['v_b_ada'], 'v_norm_pre': out['v_norm_pre'], 'v_norm_post': out['v_norm_post'], 'v_w_in': out['v_w_in'], 'v_lower_bounds': out['v_lower_bounds'], 'v_hgrn_norm': out['v_hgrn_norm'], 'v_pool_w': out['v_pool_w'], 'v_pool_scale': out['v_pool_scale'], 'v_w_proj_a': out['v_w_proj_a'], 'v_w_proj_b': out['v_w_proj_b'], 'v_w_out': out['v_w_out']}


def _loss(weights, diff, rest, loss_target):
    with _jax.named_scope("forward"):
        args = {**rest, TWIN_DIFF_INPUT: diff, **{k: w.astype(_WEIGHT_DTYPES[k]) for k, w in weights.items()}}
        y = _forward(args)
    with _jax.named_scope("loss_head"):
        err = _jnp.square(y.astype(_jnp.float32) - loss_target)
        return 0.5 * _jnp.sum(_jnp.mean(err, axis=-1)) if err.ndim else 0.5 * err


def _adamw(w, g, m, v):
    m = ADAM_B1 * m + (1.0 - ADAM_B1) * g
    v = ADAM_B2 * v + (1.0 - ADAM_B2) * _jnp.square(g)
    m_hat = m / (1.0 - ADAM_B1 ** ADAM_STEP)
    v_hat = v / (1.0 - ADAM_B2 ** ADAM_STEP)
    delta = -ADAM_LR * (m_hat / (_jnp.sqrt(v_hat) + ADAM_EPS) + ADAM_WD * w)
    return delta, m, v


def reference(x, c, w_ada, b_ada, norm_pre, norm_post, w_in, lower_bounds, hgrn_norm, pool_w, pool_scale, w_proj_a, w_proj_b, w_out, loss_target, m_w_ada, m_b_ada, m_norm_pre, m_norm_post, m_w_in, m_lower_bounds, m_hgrn_norm, m_pool_w, m_pool_scale, m_w_proj_a, m_w_proj_b, m_w_out, v_w_ada, v_b_ada, v_norm_pre, v_norm_post, v_w_in, v_lower_bounds, v_hgrn_norm, v_pool_w, v_pool_scale, v_w_proj_a, v_w_proj_b, v_w_out):
    given = dict(x=x, c=c, w_ada=w_ada, b_ada=b_ada, norm_pre=norm_pre, norm_post=norm_post, w_in=w_in, lower_bounds=lower_bounds, hgrn_norm=hgrn_norm, pool_w=pool_w, pool_scale=pool_scale, w_proj_a=w_proj_a, w_proj_b=w_proj_b, w_out=w_out, loss_target=loss_target, m_w_ada=m_w_ada, m_b_ada=m_b_ada, m_norm_pre=m_norm_pre, m_norm_post=m_norm_post, m_w_in=m_w_in, m_lower_bounds=m_lower_bounds, m_hgrn_norm=m_hgrn_norm, m_pool_w=m_pool_w, m_pool_scale=m_pool_scale, m_w_proj_a=m_w_proj_a, m_w_proj_b=m_w_proj_b, m_w_out=m_w_out, v_w_ada=v_w_ada, v_b_ada=v_b_ada, v_norm_pre=v_norm_pre, v_norm_post=v_norm_post, v_w_in=v_w_in, v_lower_bounds=v_lower_bounds, v_hgrn_norm=v_hgrn_norm, v_pool_w=v_pool_w, v_pool_scale=v_pool_scale, v_w_proj_a=v_w_proj_a, v_w_proj_b=v_w_proj_b, v_w_out=v_w_out)
    weights = {n: given[n] for n in TWIN_WEIGHTS}
    shared = {n: given[n] for n in SHARED_INPUTS}
    per_example = {n: given[n] for n in ['x', 'c']}
    grad_fn = _jax.value_and_grad(_loss, argnums=(0, 1))

    def one_microbatch(ex, loss_target):
        ex = dict(ex)
        diff = ex.pop(TWIN_DIFF_INPUT)
        return grad_fn(weights, diff, {**shared, **ex}, loss_target)

    if N_MICROBATCH == 1:
        loss, (grad_w, grad_x) = one_microbatch(per_example, given["loss_target"])
    else:
        def body(carry, xs):
            loss_sum, grad_sum = carry
            l_k, (gw_k, gx_k) = one_microbatch(xs[0], xs[1])
            with _jax.named_scope("update"):
                return (loss_sum + l_k, _jax.tree.map(_jnp.add, grad_sum, gw_k)), gx_k

        init = (_jnp.zeros((), _jnp.float32), _jax.tree.map(_jnp.zeros_like, weights))
        (loss, grad_w), grad_x = _jax.lax.scan(body, init, (per_example, given["loss_target"]))
    with _jax.named_scope("update"):
        delta_w, new_m, new_v = {}, {}, {}
        for n in TWIN_WEIGHTS:
            delta_w[n], new_m[n], new_v[n] = _adamw(weights[n], grad_w[n], given["m_" + n], given["v_" + n])
    return (loss, grad_x, *[grad_w[n] for n in TWIN_WEIGHTS], *[delta_w[n] for n in TWIN_WEIGHTS],
            *[new_m[n] for n in TWIN_WEIGHTS], *[new_v[n] for n in TWIN_WEIGHTS])
```

```python
import functools

import jax
import jax.numpy as jnp
from jax import lax
from jax.experimental import pallas as pl
from jax.experimental.pallas import tpu as pltpu

F32 = jnp.float32
BF16 = jnp.bfloat16
MESH = pl.DeviceIdType.MESH

N_DEV = 8
EPS = 1e-6
MIN_FORGET = 1e-30
D_MODEL = 2048
HEADS = 8
HEAD_DIM = 128
CHUNK = 64
SUB = 16
N_SUB = CHUNK // SUB
WIDTH = 1024
POOL_WINDOWS = (2, 4, 8, 16)
POOL_GW = 256
HALO = 16
IN_COLS = 10240
LANE = 128
N_COLBLK = IN_COLS // LANE
GATE_COLS = 4096
HEAD_COLS = 4 * HEAD_DIM
POOL_COLS = 2 * POOL_GW
MAX_EXP = 80.0

ADAM_LR = 0.001
ADAM_B1 = 0.9
ADAM_B2 = 0.999
ADAM_EPS = 1e-08
ADAM_WD = 0.01
ADAM_STEP = 10

VMEM_LIMIT = 56 * 1024 * 1024


def _cparams(sem=None):
    return pltpu.CompilerParams(dimension_semantics=sem, vmem_limit_bytes=VMEM_LIMIT)


def _sigmoid(v):
    return 1.0 / (1.0 + jnp.exp(-v))


def _dot(a, b):
    return jnp.dot(a, b, preferred_element_type=F32)


def _dot_nt(a, b):
    return lax.dot_general(a, b, (((1,), (1,)), ((), ())), preferred_element_type=F32)


def _dot_tn(a, b):
    return lax.dot_general(a, b, (((0,), (0,)), ((), ())), preferred_element_type=F32)


def _colsum(v):
    return jnp.sum(v, axis=0, keepdims=True)


def _rowmean(v):
    return jnp.mean(v, axis=-1, keepdims=True)


def _orig_block(n):
    m1 = n - 32
    head = 8 * (m1 % 4) + m1 // 4
    m2 = n - 64
    t2 = m2 % 4
    pool = 32 + 2 * (m2 // 4) + (t2 % 2) + 8 * (t2 // 2)
    return jnp.where(n < 32, n + 48, jnp.where(n < 64, head, pool))


def matmul_nn(a, b, *, tn, out_dtype, name):
    m, k = a.shape
    blocked = b.ndim == 3
    n = b.shape[0] * b.shape[2] if blocked else b.shape[1]
    tm = min(512, m)

    def body(a_ref, b_ref, o_ref):
        o_ref[...] = _dot(a_ref[...], b_ref[...]).astype(o_ref.dtype)

    b_spec = (pl.BlockSpec((None, k, tn), lambda j, i: (j, 0, 0)) if blocked
              else pl.BlockSpec((k, tn), lambda j, i: (0, j)))
    return pl.pallas_call(
        body, grid=(n // tn, m // tm),
        in_specs=[pl.BlockSpec((tm, k), lambda j, i: (i, 0)), b_spec],
        out_specs=pl.BlockSpec((tm, tn), lambda j, i: (i, j)),
        out_shape=jax.ShapeDtypeStruct((m, n), out_dtype),
        compiler_params=_cparams(("arbitrary", "arbitrary")), name=name)(a, b)


def matmul_nt(a, b, *, tn, out_dtype, name):
    m, n = a.shape
    blocked = b.ndim == 3
    k = b.shape[1] if blocked else b.shape[0]
    tm = min(512, m)
    steps = n // tn

    def body(a_ref, b_ref, o_ref, acc_ref):
        j = pl.program_id(1)
        prod = _dot_nt(a_ref[...], b_ref[...])

        @pl.when(j == 0)
        def _():
            acc_ref[...] = prod

        @pl.when(j > 0)
        def _():
            acc_ref[...] += prod

        @pl.when(j == steps - 1)
        def _():
            o_ref[...] = acc_ref[...].astype(o_ref.dtype)

    b_spec = (pl.BlockSpec((None, k, tn), lambda i, j: (j, 0, 0)) if blocked
              else pl.BlockSpec((k, tn), lambda i, j: (0, j)))
    return pl.pallas_call(
        body, grid=(m // tm, steps),
        in_specs=[pl.BlockSpec((tm, tn), lambda i, j: (i, j)), b_spec],
        out_specs=pl.BlockSpec((tm, k), lambda i, j: (i, 0)),
        out_shape=jax.ShapeDtypeStruct((m, k), out_dtype),
        scratch_shapes=[pltpu.VMEM((tm, k), F32)],
        compiler_params=_cparams(("arbitrary", "arbitrary")), name=name)(a, b)


def matmul_tn(a, b, *, tn, out_dtype, blocked_out, name):
    m, k = a.shape
    n = b.shape[1]
    tm = min(512, m)
    steps = m // tm

    def body(a_ref, b_ref, o_ref, acc_ref):
        i = pl.program_id(1)
        prod = _dot_tn(a_ref[...], b_ref[...])

        @pl.when(i == 0)
        def _():
            acc_ref[...] = prod

        @pl.when(i > 0)
        def _():
            acc_ref[...] += prod

        @pl.when(i == steps - 1)
        def _():
            o_ref[...] = acc_ref[...].astype(o_ref.dtype)

    if blocked_out:
        out_spec = pl.BlockSpec((None, k, tn), lambda j, i: (j, 0, 0))
        out_shape = jax.ShapeDtypeStruct((n // tn, k, tn), out_dtype)
    else:
        out_spec = pl.BlockSpec((k, tn), lambda j, i: (0, j))
        out_shape = jax.ShapeDtypeStruct((k, n), out_dtype)
    return pl.pallas_call(
        body, grid=(n // tn, steps),
        in_specs=[pl.BlockSpec((tm, k), lambda j, i: (i, 0)), pl.BlockSpec((tm, tn), lambda j, i: (i, j))],
        out_specs=out_spec, out_shape=out_shape,
        scratch_shapes=[pltpu.VMEM((k, tn), F32)],
        compiler_params=_cparams(("arbitrary", "arbitrary")), name=name)(a, b)


def permute_w_in(staged, *, name):
    k = staged.shape[1]

    def body(i_ref, o_ref):
        o_ref[...] = i_ref[...]

    def src(nb):
        ob = _orig_block(nb)
        return (ob // 10, 0, ob % 10)

    return pl.pallas_call(
        body, grid=(N_COLBLK,),
        in_specs=[pl.BlockSpec((None, k, LANE), src)],
        out_specs=pl.BlockSpec((k, LANE), lambda nb: (0, nb)),
        out_shape=jax.ShapeDtypeStruct((k, IN_COLS), staged.dtype),
        compiler_params=_cparams(("arbitrary",)), name=name)(staged)


def unpermute_w_in(dw, *, name):
    k = dw.shape[0]

    def body(i_ref, o_ref):
        o_ref[...] = i_ref[...]

    def dst(nb):
        ob = _orig_block(nb)
        return (ob // 10, 0, ob % 10)

    return pl.pallas_call(
        body, grid=(N_COLBLK,),
        in_specs=[pl.BlockSpec((k, LANE), lambda nb: (0, nb))],
        out_specs=pl.BlockSpec((None, k, LANE), dst),
        out_shape=jax.ShapeDtypeStruct((N_DEV, k, IN_COLS // N_DEV), dw.dtype),
        compiler_params=_cparams(("arbitrary",)), name=name)(dw)


def _row_tile(s):
    return min(256, s)


def _row_spec(t, w, col=0):
    return pl.BlockSpec((t, w), lambda i: (i, col))


def _vec_spec(w):
    return pl.BlockSpec((1, w), lambda i: (0, 0))


def prenorm_fwd(x, gain, shift, scale, *, name):
    s, d = x.shape
    t = _row_tile(s)

    def body(x_ref, g_ref, sh_ref, sc_ref, h_ref):
        xv = x_ref[...]
        r = lax.rsqrt(_rowmean(xv * xv) + EPS)
        h_ref[...] = ((xv * r) * g_ref[...] * (1.0 + sc_ref[...]) + sh_ref[...]).astype(h_ref.dtype)

    return pl.pallas_call(
        body, grid=(s // t,),
        in_specs=[_row_spec(t, d), _vec_spec(d), _vec_spec(d), _vec_spec(d)],
        out_specs=_row_spec(t, d), out_shape=jax.ShapeDtypeStruct((s, d), BF16),
        compiler_params=_cparams(("arbitrary",)), name=name)(x, gain, shift, scale)


def prenorm_bwd(dh, x, gain, scale, g_res, *, name):
    s, d = x.shape
    t = _row_tile(s)

    def body(dh_ref, x_ref, g_ref, sc_ref, gr_ref, dx_ref, dsh_ref, dsc_ref, dg_ref):
        i = pl.program_id(0)
        xv = x_ref[...]
        dhv = dh_ref[...]
        r = lax.rsqrt(_rowmean(xv * xv) + EPS)
        xn = xv * r
        gain_v = g_ref[...]
        one_sc = 1.0 + sc_ref[...]
        dyn = dhv * one_sc
        dxn = dyn * gain_v
        dx_ref[...] = r * (dxn - xn * _rowmean(dxn * xn)) + gr_ref[...]
        p_sh = _colsum(dhv)
        p_sc = _colsum(dhv * (xn * gain_v))
        p_g = _colsum(dyn * xn)

        @pl.when(i == 0)
        def _():
            dsh_ref[...] = p_sh
            dsc_ref[...] = p_sc
            dg_ref[...] = p_g

        @pl.when(i > 0)
        def _():
            dsh_ref[...] += p_sh
            dsc_ref[...] += p_sc
            dg_ref[...] += p_g

    vec = jax.ShapeDtypeStruct((1, d), F32)
    return pl.pallas_call(
        body, grid=(s // t,),
        in_specs=[_row_spec(t, d), _row_spec(t, d), _vec_spec(d), _vec_spec(d), _row_spec(t, d)],
        out_specs=[_row_spec(t, d), _vec_spec(d), _vec_spec(d), _vec_spec(d)],
        out_shape=[jax.ShapeDtypeStruct((s, d), F32), vec, vec, vec],
        compiler_params=_cparams(("arbitrary",)), name=name)(dh, x, gain, scale, g_res)


def postnorm_fwd(x, out, gain, gate, *, name):
    s, d = x.shape
    t = _row_tile(s)

    def body(x_ref, o_ref, g_ref, gt_ref, y_ref):
        ov = o_ref[...]
        r = lax.rsqrt(_rowmean(ov * ov) + EPS)
        y_ref[...] = x_ref[...] + gt_ref[...] * ((ov * r) * g_ref[...])

    return pl.pallas_call(
        body, grid=(s // t,),
        in_specs=[_row_spec(t, d), _row_spec(t, d), _vec_spec(d), _vec_spec(d)],
        out_specs=_row_spec(t, d), out_shape=jax.ShapeDtypeStruct((s, d), F32),
        compiler_params=_cparams(("arbitrary",)), name=name)(x, out, gain, gate)


def postnorm_bwd(g, out, gain, gate, *, name):
    s, d = out.shape
    t = _row_tile(s)

    def body(g_ref, o_ref, gn_ref, gt_ref, do_ref, dgt_ref, dgn_ref):
        i = pl.program_id(0)
        ov = o_ref[...]
        gv = g_ref[...]
        r = lax.rsqrt(_rowmean(ov * ov) + EPS)
        on = ov * r
        gain_v = gn_ref[...]
        gate_v = gt_ref[...]
        dn = gv * gate_v
        don = dn * gain_v
        do_ref[...] = (r * (don - on * _rowmean(don * on))).astype(do_ref.dtype)
        p_gt = _colsum(gv * (on * gain_v))
        p_gn = _colsum(dn * on)

        @pl.when(i == 0)
        def _():
            dgt_ref[...] = p_gt
            dgn_ref[...] = p_gn

        @pl.when(i > 0)
        def _():
            dgt_ref[...] += p_gt
            dgn_ref[...] += p_gn

    vec = jax.ShapeDtypeStruct((1, d), F32)
    return pl.pallas_call(
        body, grid=(s // t,),
        in_specs=[_row_spec(t, d), _row_spec(t, d), _vec_spec(d), _vec_spec(d)],
        out_specs=[_row_spec(t, d), _vec_spec(d), _vec_spec(d)],
        out_shape=[jax.ShapeDtypeStruct((s, d), BF16), vec, vec],
        compiler_params=_cparams(("arbitrary",)), name=name)(g, out, gain, gate)


def loss_head(y, target, *, name):
    s, d = y.shape
    t = _row_tile(s)
    steps = s // t

    def body(y_ref, t_ref, dy_ref, loss_ref, acc_ref):
        i = pl.program_id(0)
        err = y_ref[...] - t_ref[...]
        dy_ref[...] = err * (1.0 / d)
        part = _colsum(err * err)

        @pl.when(i == 0)
        def _():
            acc_ref[...] = part

        @pl.when(i > 0)
        def _():
            acc_ref[...] += part

        @pl.when(i == steps - 1)
        def _():
            loss_ref[...] = jnp.sum(acc_ref[...], axis=1, keepdims=True) * (0.5 / d)

    return pl.pallas_call(
        body, grid=(steps,),
        in_specs=[_row_spec(t, d), _row_spec(t, d)],
        out_specs=[_row_spec(t, d), pl.BlockSpec((1, 1), lambda i: (0, 0))],
        out_shape=[jax.ShapeDtypeStruct((s, d), F32), jax.ShapeDtypeStruct((1, 1), F32)],
        scratch_shapes=[pltpu.VMEM((1, d), F32)],
        compiler_params=_cparams(("arbitrary",)), name=name)(y, target)


def gate_fwd(proj, pa, pb, *, name):
    s, d = pa.shape
    t = _row_tile(s)

    def body(ga_ref, gb_ref, pa_ref, pb_ref, m_ref):
        m_ref[...] = (_sigmoid(ga_ref[...]) * pa_ref[...] + _sigmoid(gb_ref[...]) * pb_ref[...]).astype(m_ref.dtype)

    return pl.pallas_call(
        body, grid=(s // t,),
        in_specs=[_row_spec(t, d, 0), _row_spec(t, d, 1), _row_spec(t, d), _row_spec(t, d)],
        out_specs=_row_spec(t, d), out_shape=jax.ShapeDtypeStruct((s, d), BF16),
        compiler_params=_cparams(("arbitrary",)), name=name)(proj, proj, pa, pb)


def gate_bwd(dmerged, proj, pa, pb, *, name):
    s, d = pa.shape
    t = _row_tile(s)

    def body(dm_ref, ga_ref, gb_ref, pa_ref, pb_ref, dpa_ref, dpb_ref, dp_ref):
        dm = dm_ref[...]
        sa = _sigmoid(ga_ref[...])
        sb = _sigmoid(gb_ref[...])
        dpa_ref[...] = (dm * sa).astype(dpa_ref.dtype)
        dpb_ref[...] = (dm * sb).astype(dpb_ref.dtype)
        dp_ref[:, :d] = (dm * pa_ref[...] * sa * (1.0 - sa)).astype(dp_ref.dtype)
        dp_ref[:, d:] = (dm * pb_ref[...] * sb * (1.0 - sb)).astype(dp_ref.dtype)

    return pl.pallas_call(
        body, grid=(s // t,),
        in_specs=[_row_spec(t, d), _row_spec(t, d, 0), _row_spec(t, d, 1), _row_spec(t, d), _row_spec(t, d)],
        out_specs=[_row_spec(t, d), _row_spec(t, d), _row_spec(t, 2 * d, 0)],
        out_shape=[jax.ShapeDtypeStruct((s, d), BF16), jax.ShapeDtypeStruct((s, d), BF16),
                   jax.ShapeDtypeStruct((s, IN_COLS), BF16)],
        compiler_params=_cparams(("arbitrary",)), name=name)(dmerged, proj, proj, pa, pb)


def _pool_tile(s):
    return min(256, s)


def pool_fwd(proj, pw, ps, *, name):
    s = proj.shape[0]
    t = _pool_tile(s)
    pool_blk = (GATE_COLS + HEADS * HEAD_COLS) // (len(POOL_WINDOWS) * POOL_COLS)

    def body(p_ref, halo_ref, pw_ref, ps_ref, yb_ref, pooled_ref, mixed_ref):
        i = pl.program_id(0)
        halo = jnp.where(i == 0, 0.0, halo_ref[...])
        row = i * t + lax.broadcasted_iota(jnp.int32, (t, 1), 0)
        for g, w in enumerate(POOL_WINDOWS):
            vb = p_ref[:, g * POOL_COLS:g * POOL_COLS + POOL_GW]
            zb = p_ref[:, g * POOL_COLS + POOL_GW:(g + 1) * POOL_COLS]
            acc = jnp.concatenate([halo[:, g * POOL_COLS:g * POOL_COLS + POOL_GW], vb], axis=0)
            sh = 1
            while sh < w:
                acc = acc + pltpu.roll(acc, sh, axis=0)
                sh *= 2
            cnt = jnp.minimum(row + 1, w).astype(F32)
            pooled = acc[HALO:, :] / cnt - vb
            mixed = _dot(pooled.astype(BF16), pw_ref[g])
            cols = slice(g * POOL_GW, (g + 1) * POOL_GW)
            yb = mixed * ps_ref[:, cols] * (zb * _sigmoid(zb))
            yb_ref[:, cols] = yb.astype(yb_ref.dtype)
            pooled_ref[:, cols] = pooled.astype(pooled_ref.dtype)
            mixed_ref[:, cols] = mixed

    wide = len(POOL_WINDOWS) * POOL_COLS
    return pl.pallas_call(
        body, grid=(s // t,),
        in_specs=[pl.BlockSpec((t, wide), lambda i: (i, pool_blk)),
                  pl.BlockSpec((HALO, wide), lambda i: (jnp.maximum(i * (t // HALO) - 1, 0), pool_blk)),
                  pl.BlockSpec((len(POOL_WINDOWS), POOL_GW, POOL_GW), lambda i: (0, 0, 0)),
                  _vec_spec(WIDTH)],
        out_specs=[_row_spec(t, WIDTH)] * 3,
        out_shape=[jax.ShapeDtypeStruct((s, WIDTH), BF16), jax.ShapeDtypeStruct((s, WIDTH), BF16),
                   jax.ShapeDtypeStruct((s, WIDTH), F32)],
        compiler_params=_cparams(("arbitrary",)), name=name)(proj, proj, pw, ps)


def pool_bwd(dyb, proj, pooled, mixed, pw, ps, dproj, *, name):
    s = proj.shape[0]
    t = _pool_tile(s)
    nblk = s // t
    ng = len(POOL_WINDOWS)
    wide = ng * POOL_COLS
    pool_blk = (GATE_COLS + HEADS * HEAD_COLS) // wide

    def body(dy_ref, p_ref, pooled_ref, mixed_ref, pw_ref, ps_ref, dp_any, dp_ref, dpw_ref, dps_ref, carry):
        del dp_any
        i = pl.program_id(0)
        ii = nblk - 1 - i

        @pl.when(i == 0)
        def _():
            carry[...] = jnp.zeros_like(carry)
            dpw_ref[...] = jnp.zeros_like(dpw_ref)
            dps_ref[...] = jnp.zeros_like(dps_ref)

        row = ii * t + lax.broadcasted_iota(jnp.int32, (t, 1), 0)
        for g, w in enumerate(POOL_WINDOWS):
            cols = slice(g * POOL_GW, (g + 1) * POOL_GW)
            zb = p_ref[:, g * POOL_COLS + POOL_GW:(g + 1) * POOL_COLS]
            dy = dy_ref[:, cols]
            mx = mixed_ref[:, cols]
            sc = ps_ref[:, cols]
            sg = _sigmoid(zb)
            dzb = dy * (mx * sc) * (sg * (1.0 + zb * (1.0 - sg)))
            dpm = dy * (zb * sg)
            dps_ref[:, cols] += _colsum(dpm * mx)
            dmixed = (dpm * sc).astype(BF16)
            dpooled = _dot_nt(dmixed, pw_ref[g])
            dpw_ref[g] += _dot_tn(pooled_ref[:, cols], dmixed)
            cnt = jnp.minimum(row + 1, w).astype(F32)
            u = dpooled / cnt
            acc = jnp.concatenate([u, carry[:, cols]], axis=0)
            sh = 1
            while sh < w:
                acc = acc + pltpu.roll(acc, t + HALO - sh, axis=0)
                sh *= 2
            carry[:, cols] = u[:HALO, :]
            dp_ref[:, g * POOL_COLS:g * POOL_COLS + POOL_GW] = (acc[:t, :] - dpooled).astype(dp_ref.dtype)
            dp_ref[:, g * POOL_COLS + POOL_GW:(g + 1) * POOL_COLS] = dzb.astype(dp_ref.dtype)

    rev = lambda i: (nblk - 1 - i, 0)
    return pl.pallas_call(
        body, grid=(nblk,),
        in_specs=[pl.BlockSpec((t, WIDTH), rev),
                  pl.BlockSpec((t, wide), lambda i: (nblk - 1 - i, pool_blk)),
                  pl.BlockSpec((t, WIDTH), rev), pl.BlockSpec((t, WIDTH), rev),
                  pl.BlockSpec((ng, POOL_GW, POOL_GW), lambda i: (0, 0, 0)),
                  _vec_spec(WIDTH),
                  pl.BlockSpec(memory_space=pl.ANY)],
        out_specs=[pl.BlockSpec((t, wide), lambda i: (nblk - 1 - i, pool_blk)),
                   pl.BlockSpec((ng, POOL_GW, POOL_GW), lambda i: (0, 0, 0)),
                   _vec_spec(WIDTH)],
        out_shape=[jax.ShapeDtypeStruct(dproj.shape, dproj.dtype),
                   jax.ShapeDtypeStruct((ng, POOL_GW, POOL_GW), F32),
                   jax.ShapeDtypeStruct((1, WIDTH), F32)],
        scratch_shapes=[pltpu.VMEM((HALO, WIDTH), F32)],
        input_output_aliases={6: 0},
        compiler_params=_cparams(("arbitrary",)), name=name)(dyb, proj, pooled, mixed, pw, ps, dproj)


def _hgrn_tile(s):
    return min(512, s)


def _chunk_consts():
    tt = lax.broadcasted_iota(jnp.int32, (CHUNK, CHUNK), 0)
    ss = lax.broadcasted_iota(jnp.int32, (CHUNK, CHUNK), 1)
    within = (ss <= tt) & (ss // SUB == tt // SUB)
    before = ss < (tt // SUB) * SUB
    cums = jnp.concatenate([within.astype(F32), before.astype(F32)], axis=0).astype(BF16)
    causal = ss <= tt
    upper = (ss >= tt).astype(F32).astype(BF16)
    row = lax.broadcasted_iota(jnp.int32, (CHUNK, 1), 0)
    return cums, causal, upper, row


def _dot_split(mat01, v):
    hi = v.astype(BF16)
    r1 = v - hi.astype(F32)
    mid = r1.astype(BF16)
    lo = (r1 - mid.astype(F32)).astype(BF16)
    return _dot(mat01, hi) + _dot(mat01, mid) + _dot(mat01, lo)


def _hgrn_chunk(qa, fa, lb, cums, row):
    sq = _sigmoid(qa)
    q = qa * sq
    sa = _sigmoid(fa)
    sna = _sigmoid(-fa)
    oml = 1.0 - lb
    f = lb + oml * sa
    fc = jnp.maximum(f, MIN_FORGET)
    lf = jnp.log(fc)
    k = oml * sna
    cb = _dot_split(cums, lf)
    c = cb[:CHUNK]
    bt = cb[CHUNK:]
    ec = jnp.exp(c)
    enc = jnp.exp(jnp.minimum(-c, MAX_EXP))
    qt = q * ec
    kt = k * enc
    dms, lhs, rhs = [], [], []
    for j in range(N_SUB):
        bj = bt[j * SUB:j * SUB + 1, :]
        dm = jnp.where(row >= j * SUB, jnp.exp(jnp.minimum(bt - bj, 0.0)), 0.0)
        dms.append(dm)
        lhs.append(qt * dm)
        rhs.append(jnp.where(row // SUB == j, kt, 0.0))
    lhs = jnp.concatenate(lhs, axis=1).astype(BF16)
    rhs = jnp.concatenate(rhs, axis=1).astype(BF16)
    b = bt + c
    bl = b[CHUNK - 1:CHUNK, :]
    ebl = jnp.exp(bl)
    edec = jnp.exp(bl - b)
    eb = ec * dms[0]
    return dict(sq=sq, q=q, sa=sa, sna=sna, oml=oml, f=f, fc=fc, k=k, ec=ec, enc=enc, dms=dms,
                lhs=lhs, rhs=rhs, ebl=ebl, edec=edec, eb=eb, qd=q * eb, kdec=k * edec)


def hgrn_fwd(proj, lb, hn, *, name):
    s = proj.shape[0]
    t = _hgrn_tile(s)
    nblk = s // t
    ncht = t // CHUNK
    head_blk0 = GATE_COLS // HEAD_COLS

    def body(p_ref, lb_ref, hn_ref, ya_ref, o_ref, st_ref, state):
        i = pl.program_id(1)

        @pl.when(i == 0)
        def _():
            state[...] = jnp.zeros_like(state)

        cums, causal, _, row = _chunk_consts()
        lbv = lb_ref[...]
        hnv = hn_ref[...]

        def chunk(ci, carry):
            rows = pl.ds(pl.multiple_of(ci * CHUNK, CHUNK), CHUNK)
            qa = p_ref[rows, 0:HEAD_DIM]
            fa = p_ref[rows, HEAD_DIM:2 * HEAD_DIM]
            va = p_ref[rows, 2 * HEAD_DIM:3 * HEAD_DIM].astype(BF16)
            za = p_ref[rows, 3 * HEAD_DIM:4 * HEAD_DIM]
            pre = _hgrn_chunk(qa, fa, lbv, cums, row)
            st0 = state[...]
            st0b = st0.astype(BF16)
            st_ref[ci, 0] = st0b
            a = jnp.where(causal, _dot_nt(pre["lhs"], pre["rhs"]), 0.0)
            o = _dot_nt(pre["qd"].astype(BF16), st0b) + _dot(a.astype(BF16), va)
            state[...] = st0 * pre["ebl"] + _dot_tn(va, pre["kdec"].astype(BF16))
            r = lax.rsqrt(_rowmean(o * o) + EPS)
            o_ref[rows, :] = o
            ya_ref[rows, :] = ((o * r) * hnv * (za * _sigmoid(za))).astype(ya_ref.dtype)
            return carry

        lax.fori_loop(0, ncht, chunk, 0)

    return pl.pallas_call(
        body, grid=(HEADS, nblk),
        in_specs=[pl.BlockSpec((t, HEAD_COLS), lambda h, i: (i, head_blk0 + h)),
                  pl.BlockSpec((1, HEAD_DIM), lambda h, i: (0, h)),
                  pl.BlockSpec((1, HEAD_DIM), lambda h, i: (0, h))],
        out_specs=[pl.BlockSpec((t, HEAD_DIM), lambda h, i: (i, h)),
                   pl.BlockSpec((t, HEAD_DIM), lambda h, i: (i, h)),
                   pl.BlockSpec((ncht, 1, HEAD_DIM, HEAD_DIM), lambda h, i: (i, h, 0, 0))],
        out_shape=[jax.ShapeDtypeStruct((s, WIDTH), BF16), jax.ShapeDtypeStruct((s, WIDTH), F32),
                   jax.ShapeDtypeStruct((s // CHUNK, HEADS, HEAD_DIM, HEAD_DIM), BF16)],
        scratch_shapes=[pltpu.VMEM((HEAD_DIM, HEAD_DIM), F32)],
        compiler_params=_cparams(("arbitrary", "arbitrary")), name=name)(proj, lb, hn)


def hgrn_bwd(dya, proj, o_all, states, lb, hn, dproj, *, name):
    s = proj.shape[0]
    t = _hgrn_tile(s)
    nblk = s // t
    ncht = t // CHUNK
    head_blk0 = GATE_COLS // HEAD_COLS

    def body(dy_ref, p_ref, o_ref, st_ref, lb_ref, hn_ref, dp_any, dp_ref, dhn_ref, dlb_ref, dstate):
        del dp_any
        i = pl.program_id(1)

        @pl.when(i == 0)
        def _():
            dstate[...] = jnp.zeros_like(dstate)
            dhn_ref[...] = jnp.zeros_like(dhn_ref)
            dlb_ref[...] = jnp.zeros_like(dlb_ref)

        cums, causal, upper, row = _chunk_consts()
        lbv = lb_ref[...]
        hnv = hn_ref[...]

        def chunk(kk, carry):
            ci = ncht - 1 - kk
            rows = pl.ds(pl.multiple_of(ci * CHUNK, CHUNK), CHUNK)
            qa = p_ref[rows, 0:HEAD_DIM]
            fa = p_ref[rows, HEAD_DIM:2 * HEAD_DIM]
            vb = p_ref[rows, 2 * HEAD_DIM:3 * HEAD_DIM].astype(BF16)
            za = p_ref[rows, 3 * HEAD_DIM:4 * HEAD_DIM]
            o = o_ref[rows, :]
            dy = dy_ref[rows, :]
            st0b = st_ref[ci, 0]
            r = lax.rsqrt(_rowmean(o * o) + EPS)
            on = o * r
            sgz = _sigmoid(za)
            sz = za * sgz
            dza = dy * on * hnv * (sgz * (1.0 + za * (1.0 - sgz)))
            dhn_ref[...] += _colsum(dy * on * sz)
            don = dy * hnv * sz
            do = r * (don - on * _rowmean(don * on))
            dob = do.astype(BF16)
            pre = _hgrn_chunk(qa, fa, lbv, cums, row)
            q, k = pre["q"], pre["k"]
            a = jnp.where(causal, _dot_nt(pre["lhs"], pre["rhs"]), 0.0)
            dst1 = dstate[...]
            dst1b = dst1.astype(BF16)
            dq_inter = _dot(dob, st0b) * pre["eb"]
            da = jnp.where(causal, _dot_nt(dob, vb), 0.0).astype(BF16)
            dv = _dot_tn(a.astype(BF16), dob) + _dot_nt(pre["kdec"].astype(BF16), dst1b)
            dk_state = _dot(vb, dst1b) * pre["edec"]
            dlhs = _dot(da, pre["rhs"])
            drhs = _dot_tn(da, pre["lhs"])
            dq_a = jnp.zeros_like(q)
            dk_a = jnp.zeros_like(k)
            for j in range(N_SUB):
                dq_a = dq_a + pre["dms"][j] * dlhs[:, j * HEAD_DIM:(j + 1) * HEAD_DIM]
                dk_a = dk_a + jnp.where(row // SUB == j, drhs[:, j * HEAD_DIM:(j + 1) * HEAD_DIM], 0.0)
            dq = dq_inter + pre["ec"] * dq_a
            dk = dk_state + pre["enc"] * dk_a
            db = q * dq - k * dk
            dbl = _colsum(k * dk_state) + pre["ebl"] * _colsum(dst1 * st0b.astype(F32))
            dlf = _dot_split(upper, db) + dbl
            dstate[...] = dst1 * pre["ebl"] + _dot_tn(dob, pre["qd"].astype(BF16))
            sq, sa, sna, oml = pre["sq"], pre["sa"], pre["sna"], pre["oml"]
            dqa = dq * (sq * (1.0 + qa * (1.0 - sq)))
            dlf_f = jnp.where(pre["f"] >= MIN_FORGET, dlf / pre["fc"], 0.0)
            dfa = (dlf_f - dk) * (oml * sa * sna)
            dlb_ref[...] += _colsum((dlf_f - dk) * sna)
            dp_ref[rows, :] = jnp.concatenate([dqa, dfa, dv, dza], axis=1).astype(dp_ref.dtype)
            return carry

        lax.fori_loop(0, ncht, chunk, 0)

    rev = lambda h, i: (nblk - 1 - i, h)
    return pl.pallas_call(
        body, grid=(HEADS, nblk),
        in_specs=[pl.BlockSpec((t, HEAD_DIM), rev),
                  pl.BlockSpec((t, HEAD_COLS), lambda h, i: (nblk - 1 - i, head_blk0 + h)),
                  pl.BlockSpec((t, HEAD_DIM), rev),
                  pl.BlockSpec((ncht, 1, HEAD_DIM, HEAD_DIM), lambda h, i: (nblk - 1 - i, h, 0, 0)),
                  pl.BlockSpec((1, HEAD_DIM), lambda h, i: (0, h)),
                  pl.BlockSpec((1, HEAD_DIM), lambda h, i: (0, h)),
                  pl.BlockSpec(memory_space=pl.ANY)],
        out_specs=[pl.BlockSpec((t, HEAD_COLS), lambda h, i: (nblk - 1 - i, head_blk0 + h)),
                   pl.BlockSpec((1, HEAD_DIM), lambda h, i: (0, h)),
                   pl.BlockSpec((1, HEAD_DIM), lambda h, i: (0, h))],
        out_shape=[jax.ShapeDtypeStruct(dproj.shape, dproj.dtype),
                   jax.ShapeDtypeStruct((1, WIDTH), F32), jax.ShapeDtypeStruct((1, WIDTH), F32)],
        scratch_shapes=[pltpu.VMEM((HEAD_DIM, HEAD_DIM), F32)],
        input_output_aliases={6: 0},
        compiler_params=_cparams(("arbitrary", "arbitrary")), name=name)(dya, proj, o_all, states, lb, hn, dproj)


def _softmax_rows(lower):
    mx = jnp.max(lower, axis=0, keepdims=True)
    e = jnp.exp(lower - mx)
    return e / jnp.sum(e, axis=0, keepdims=True)


def lb_table(lower, *, name):
    depth, w = lower.shape

    def body(l_ref, o_ref):
        sm = _softmax_rows(l_ref[...])
        acc = jnp.zeros((1, w), F32)
        o_ref[0:1, :] = acc
        for l in range(1, depth):
            acc = acc + sm[l:l + 1, :]
            o_ref[l:l + 1, :] = acc

    return pl.pallas_call(body, out_shape=jax.ShapeDtypeStruct((depth, w), F32), name=name)(lower)


def lb_table_bwd(lower, dlb, *, name):
    depth, w = lower.shape

    def body(l_ref, d_ref, o_ref):
        sm = _softmax_rows(l_ref[...])
        dlbv = d_ref[...]
        dsm = [jnp.zeros((1, w), F32)]
        for i in range(1, depth):
            acc = jnp.zeros((1, w), F32)
            for l in range(i, depth):
                acc = acc + dlbv[l:l + 1, :]
            dsm.append(acc)
        inner = jnp.zeros((1, w), F32)
        for i in range(depth):
            inner = inner + sm[i:i + 1, :] * dsm[i]
        for i in range(depth):
            o_ref[i:i + 1, :] = sm[i:i + 1, :] * (dsm[i] - inner)

    return pl.pallas_call(body, out_shape=jax.ShapeDtypeStruct((depth, w), F32), name=name)(lower, dlb)


def w_ada_grad(c_all, dmod_cols, *, name):
    depth, _, cols = dmod_cols.shape
    d = c_all.shape[1]

    def body(c_ref, dm_ref, o_ref):
        cv = c_ref[...]
        ca = cv * _sigmoid(cv)
        o_ref[...] = _dot_tn(ca, dm_ref[...])

    return pl.pallas_call(
        body, grid=(depth,),
        in_specs=[pl.BlockSpec((N_DEV, d), lambda l: (0, 0)), pl.BlockSpec((None, N_DEV, cols), lambda l: (l, 0, 0))],
        out_specs=pl.BlockSpec((None, d, cols), lambda l: (l, 0, 0)),
        out_shape=jax.ShapeDtypeStruct((depth, d, cols), F32),
        compiler_params=_cparams(("arbitrary",)), name=name)(c_all, dmod_cols)


def sum_parts(parts, *, name):
    p, r, c = parts.shape

    def body(p_ref, o_ref):
        acc = p_ref[0]
        for j in range(1, p):
            acc = acc + p_ref[j]
        o_ref[...] = acc

    return pl.pallas_call(body, out_shape=jax.ShapeDtypeStruct((r, c), F32), name=name)(parts)


def adamw(w, m, v, gparts, *, name):
    r, c = w.shape
    p = gparts.shape[0]
    tr = r
    while tr * c * 4 > (1 << 20) and tr % 16 == 0:
        tr //= 2

    def body(w_ref, m_ref, v_ref, g_ref, go_ref, d_ref, mo_ref, vo_ref):
        g = g_ref[0].astype(F32)
        for j in range(1, p):
            g = g + g_ref[j].astype(F32)
        mn = ADAM_B1 * m_ref[...] + (1.0 - ADAM_B1) * g
        vn = ADAM_B2 * v_ref[...] + (1.0 - ADAM_B2) * (g * g)
        m_hat = mn / (1.0 - ADAM_B1 ** ADAM_STEP)
        v_hat = vn / (1.0 - ADAM_B2 ** ADAM_STEP)
        go_ref[...] = g
        d_ref[...] = -ADAM_LR * (m_hat / (jnp.sqrt(v_hat) + ADAM_EPS) + ADAM_WD * w_ref[...])
        mo_ref[...] = mn
        vo_ref[...] = vn

    spec = pl.BlockSpec((tr, c), lambda i: (i, 0))
    shp = jax.ShapeDtypeStruct((r, c), F32)
    return pl.pallas_call(
        body, grid=(r // tr,),
        in_specs=[spec, spec, spec, pl.BlockSpec((p, tr, c), lambda i: (0, i, 0))],
        out_specs=[spec] * 4, out_shape=[shp] * 4,
        compiler_params=_cparams(("arbitrary",)), name=name)(w, m, v, gparts)


def _position():
    x, y, c = lax.axis_index("x"), lax.axis_index("y"), lax.axis_index("c")
    return x, y, c


def _dev_index(x, y, c):
    return 4 * x + 2 * y + c


def all_gather(arrs, *, name):
    n = len(arrs)
    hbm = pl.BlockSpec(memory_space=pl.ANY)

    def body(*refs):
        ins, outs = refs[:n], refs[n:2 * n]
        send_sems, recv_sems, local_sems = refs[2 * n:]
        x, y, c = _position()
        me, sibling = (x, y, c), (x, y, 1 - c)
        chips = [(1 - x, y), (x, 1 - y), (1 - x, 1 - y)]

        def copy(a, k, block, to, own):
            slot = outs[a].at[_dev_index(*block)]
            return pltpu.make_async_remote_copy(
                src_ref=ins[a] if own else slot, dst_ref=slot,
                send_sem=send_sems.at[a * 7 + k], recv_sem=recv_sems.at[a * 7 + k],
                device_id=to, device_id_type=MESH)

        mine = [pltpu.make_async_copy(ins[a], outs[a].at[_dev_index(*me)], local_sems.at[a]) for a in range(n)]
        for cp in mine:
            cp.start()
        first = []
        for a in range(n):
            first.append(copy(a, 0, me, sibling, True))
            first += [copy(a, 1 + j, me, (*chip, c), True) for j, chip in enumerate(chips)]
        for cp in first:
            cp.start()
        passed = []
        for j, chip in enumerate(chips):
            for a in range(n):
                copy(a, 1 + j, (*chip, c), me, False).wait_recv()
                fwd = copy(a, 4 + j, (*chip, c), sibling, False)
                fwd.start()
                passed.append(fwd)
        for a in range(n):
            copy(a, 0, sibling, me, False).wait_recv()
            for j, chip in enumerate(chips):
                copy(a, 4 + j, (*chip, 1 - c), me, False).wait_recv()
        for cp in first + passed:
            cp.wait_send()
        for cp in mine:
            cp.wait()

    return pl.pallas_call(
        body,
        out_shape=[jax.ShapeDtypeStruct((N_DEV,) + a.shape, a.dtype) for a in arrs],
        in_specs=[hbm] * n, out_specs=[hbm] * n,
        scratch_shapes=[pltpu.SemaphoreType.DMA((7 * n,)), pltpu.SemaphoreType.DMA((7 * n,)),
                        pltpu.SemaphoreType.DMA((n,))],
        name=name)(*arrs)


def scatter_parts(arrs, *, name):
    n = len(arrs)
    hbm = pl.BlockSpec(memory_space=pl.ANY)

    def body(*refs):
        ins, outs = refs[:n], refs[n:2 * n]
        send_sems, recv_sems, local_sems = refs[2 * n:]
        x, y, c = _position()
        me = _dev_index(x, y, c)

        def peer(r):
            return (x ^ (r >> 2), y ^ ((r >> 1) & 1), c ^ (r & 1))

        def copy(a, r):
            to = peer(r)
            return pltpu.make_async_remote_copy(
                src_ref=ins[a].at[_dev_index(*to)], dst_ref=outs[a].at[me],
                send_sem=send_sems.at[a * 7 + r - 1], recv_sem=recv_sems.at[a * 7 + r - 1],
                device_id=to, device_id_type=MESH)

        def arrival(a, r):
            return pltpu.make_async_remote_copy(
                src_ref=ins[a].at[me], dst_ref=outs[a].at[_dev_index(*peer(r))],
                send_sem=send_sems.at[a * 7 + r - 1], recv_sem=recv_sems.at[a * 7 + r - 1],
                device_id=peer(r), device_id_type=MESH)

        mine = [pltpu.make_async_copy(ins[a].at[me], outs[a].at[me], local_sems.at[a]) for a in range(n)]
        for cp in mine:
            cp.start()
        sends = [copy(a, r) for r in range(1, N_DEV) for a in range(n)]
        for cp in sends:
            cp.start()
        for r in range(1, N_DEV):
            for a in range(n):
                arrival(a, r).wait_recv()
        for cp in sends:
            cp.wait_send()
        for cp in mine:
            cp.wait()

    return pl.pallas_call(
        body,
        out_shape=[jax.ShapeDtypeStruct(a.shape, a.dtype) for a in arrs],
        in_specs=[hbm] * n, out_specs=[hbm] * n,
        scratch_shapes=[pltpu.SemaphoreType.DMA((7 * n,)), pltpu.SemaphoreType.DMA((7 * n,)),
                        pltpu.SemaphoreType.DMA((n,))],
        name=name)(*arrs)


def mod_exchange(c_all, w_ada, b_cols, *, name):
    depth, d, cols = w_ada.shape
    hbm = pl.BlockSpec(memory_space=pl.ANY)
    vmem = pl.BlockSpec(memory_space=pltpu.VMEM)

    def body(c_ref, w_ref, b_ref, out_ref, wbuf, sendbuf, send_sems, recv_sems, load_sem):
        x, y, c = _position()
        me = _dev_index(x, y, c)
        cv = c_ref[...]
        ca = cv * _sigmoid(cv)
        for l in range(depth):
            load = pltpu.make_async_copy(w_ref.at[l], wbuf, load_sem)
            load.start()
            load.wait()
            part = jnp.dot(ca, wbuf[...], preferred_element_type=F32,
                           precision=lax.Precision.HIGHEST) + b_ref[l:l + 1, :]
            for bi in range(N_DEV):
                sendbuf[bi, l:l + 1, :] = part[bi:bi + 1, :]

        def peer(r):
            return (x ^ (r >> 2), y ^ ((r >> 1) & 1), c ^ (r & 1))

        def copy(r):
            to = peer(r)
            return pltpu.make_async_remote_copy(
                src_ref=sendbuf.at[_dev_index(*to)], dst_ref=out_ref.at[me],
                send_sem=send_sems.at[r - 1], recv_sem=recv_sems.at[r - 1],
                device_id=to, device_id_type=MESH)

        def arrival(r):
            return pltpu.make_async_remote_copy(
                src_ref=sendbuf.at[me], dst_ref=out_ref.at[_dev_index(*peer(r))],
                send_sem=send_sems.at[r - 1], recv_sem=recv_sems.at[r - 1],
                device_id=peer(r), device_id_type=MESH)

        out_ref[me] = sendbuf[me]
        sends = [copy(r) for r in range(1, N_DEV)]
        for cp in sends:
            cp.start()
        for r in range(1, N_DEV):
            arrival(r).wait_recv()
        for cp in sends:
            cp.wait_send()

    return pl.pallas_call(
        body,
        out_shape=jax.ShapeDtypeStruct((N_DEV, depth, cols), F32),
        in_specs=[vmem, hbm, vmem], out_specs=vmem,
        scratch_shapes=[pltpu.VMEM((d, cols), F32), pltpu.VMEM((N_DEV, depth, cols), F32),
                        pltpu.SemaphoreType.DMA((7,)), pltpu.SemaphoreType.DMA((7,)), pltpu.SemaphoreType.DMA],
        compiler_params=pltpu.CompilerParams(vmem_limit_bytes=VMEM_LIMIT),
        name=name)(c_all, w_ada, b_cols)


def kernel(x, c, w_ada, b_ada, norm_pre, norm_post, w_in, lower_bounds, hgrn_norm, pool_w, pool_scale, w_proj_a, w_proj_b, w_out, loss_target, m_w_ada, m_b_ada, m_norm_pre, m_norm_post, m_w_in, m_lower_bounds, m_hgrn_norm, m_pool_w, m_pool_scale, m_w_proj_a, m_w_proj_b, m_w_out, v_w_ada, v_b_ada, v_norm_pre, v_norm_post, v_w_in, v_lower_bounds, v_hgrn_norm, v_pool_w, v_pool_scale, v_w_proj_a, v_w_proj_b, v_w_out):
    depth = w_in.shape[0]
    d = D_MODEL
    ada_cols = w_ada.shape[2]
    xi, yi, ci = _position()
    me = _dev_index(xi, yi, ci)
    xs = x[0]
    target = loss_target[0]
    ng = len(POOL_WINDOWS)

    gathered = []
    for l in range(depth):
        shards = [w_in[l].astype(BF16), w_proj_a[l].astype(BF16), w_proj_b[l].astype(BF16),
                  w_out[l].astype(BF16), pool_w[l].astype(BF16)]
        g_in, g_pa, g_pb, g_out, g_pool = all_gather(shards, name="gather_weights")
        gathered.append(dict(
            w_in=permute_w_in(g_in, name="permute_w_in"),
            pa=g_pa, pb=g_pb,
            w_out=g_out.reshape(d, d),
            pool=jnp.transpose(g_pool, (1, 0, 2, 3)).reshape(ng, POOL_GW, POOL_GW)))

    (c_all,) = all_gather([c], name="gather_c")
    c_all = c_all.reshape(N_DEV, d)
    b_cols = lax.dynamic_slice_in_dim(b_ada, me * ada_cols, ada_cols, axis=1)
    mod_parts = mod_exchange(c_all, w_ada, b_cols, name="mod_exchange")
    mod = jnp.transpose(mod_parts, (1, 0, 2)).reshape(depth, 3 * d)
    lb_all = lb_table(lower_bounds, name="lb_table")

    saved = []
    cur = xs
    for l in range(depth):
        w = gathered[l]
        shift, scale, gate = mod[l:l + 1, :d], mod[l:l + 1, d:2 * d], mod[l:l + 1, 2 * d:]
        h = prenorm_fwd(cur, norm_pre[l:l + 1], shift, scale, name="prenorm_fwd")
        proj = matmul_nn(h, w["w_in"], tn=1024, out_dtype=F32, name="mm_w_in")
        y_a, o_all, states = hgrn_fwd(proj, lb_all[l:l + 1], hgrn_norm[l:l + 1], name="hgrn_fwd")
        y_b, pooled, mixed = pool_fwd(proj, w["pool"], pool_scale[l:l + 1], name="pool_fwd")
        pa = matmul_nn(y_a, w["pa"], tn=256, out_dtype=F32, name="mm_proj_a")
        pb = matmul_nn(y_b, w["pb"], tn=256, out_dtype=F32, name="mm_proj_b")
        merged = gate_fwd(proj, pa, pb, name="gate_fwd")
        out = matmul_nn(merged, w["w_out"], tn=1024, out_dtype=F32, name="mm_w_out")
        nxt = postnorm_fwd(cur, out, norm_post[l:l + 1], gate, name="postnorm_fwd")
        saved.append(dict(x=cur, h=h, proj=proj, y_a=y_a, o=o_all, states=states, y_b=y_b, pooled=pooled,
                          mixed=mixed, pa=pa, pb=pb, merged=merged, out=out, scale=scale, gate=gate))
        cur = nxt

    g, loss_part = loss_head(cur, target, name="loss_head")
    loss = lax.psum(loss_part[0, 0], ("x", "y", "c"))

    small = [None] * depth
    big = [None] * depth
    for l in reversed(range(depth)):
        w, sv = gathered[l], saved[l]
        dout, dgate, dnpost = postnorm_bwd(g, sv["out"], norm_post[l:l + 1], sv["gate"], name="postnorm_bwd")
        dmerged = matmul_nt(dout, w["w_out"], tn=1024, out_dtype=F32, name="mm_w_out_dx")
        dw_out = matmul_tn(sv["merged"], dout, tn=1024, out_dtype=BF16, blocked_out=False, name="mm_w_out_dw")
        dpa, dpb, dproj = gate_bwd(dmerged, sv["proj"], sv["pa"], sv["pb"], name="gate_bwd")
        dya = matmul_nt(dpa, w["pa"], tn=256, out_dtype=F32, name="mm_proj_a_dx")
        dyb = matmul_nt(dpb, w["pb"], tn=256, out_dtype=F32, name="mm_proj_b_dx")
        dw_pa = matmul_tn(sv["y_a"], dpa, tn=256, out_dtype=BF16, blocked_out=True, name="mm_proj_a_dw")
        dw_pb = matmul_tn(sv["y_b"], dpb, tn=256, out_dtype=BF16, blocked_out=True, name="mm_proj_b_dw")
        dproj, dpool_w, dpool_scale = pool_bwd(dyb, sv["proj"], sv["pooled"], sv["mixed"], w["pool"],
                                               pool_scale[l:l + 1], dproj, name="pool_bwd")
        dproj, dhn, dlb = hgrn_bwd(dya, sv["proj"], sv["o"], sv["states"], lb_all[l:l + 1],
                                   hgrn_norm[l:l + 1], dproj, name="hgrn_bwd")
        dh = matmul_nt(dproj, w["w_in"], tn=1024, out_dtype=F32, name="mm_w_in_dx")
        dw_in = matmul_tn(sv["h"], dproj, tn=1024, out_dtype=BF16, blocked_out=False, name="mm_w_in_dw")
        g, dshift, dscale, dnpre = prenorm_bwd(dh, sv["x"], norm_pre[l:l + 1], sv["scale"], g, name="prenorm_bwd")
        small[l] = jnp.concatenate([dshift, dscale, dgate, dnpre, dnpost, dlb, dhn, dpool_scale], axis=1)
        parts = [unpermute_w_in(dw_in, name="unpermute_w_in"), dw_pa, dw_pb,
                 dw_out.reshape(N_DEV, d // N_DEV, d),
                 jnp.transpose(dpool_w.astype(BF16).reshape(ng, N_DEV, POOL_GW // N_DEV, POOL_GW), (1, 0, 2, 3))]
        big[l] = scatter_parts(parts, name="scatter_grads")
    grad_x = g[None]

    small_mine = jnp.concatenate(small, axis=0)
    (small_all,) = all_gather([small_mine], name="gather_small")
    small_sum = sum_parts(small_all, name="sum_small")
    dmod_all = small_all[:, :, :3 * d]
    dmod_cols = jnp.transpose(lax.dynamic_slice_in_dim(dmod_all, me * ada_cols, ada_cols, axis=2), (1, 0, 2))
    g_w_ada = w_ada_grad(c_all, dmod_cols, name="w_ada_grad")
    off = 3 * d
    g_b_ada = small_sum[:, :off]
    g_npre = small_sum[:, off:off + d]
    g_npost = small_sum[:, off + d:off + 2 * d]
    g_lb_tab = small_sum[:, off + 2 * d:off + 2 * d + WIDTH]
    g_hn = small_sum[:, off + 2 * d + WIDTH:off + 2 * d + 2 * WIDTH]
    g_ps = small_sum[:, off + 2 * d + 2 * WIDTH:]
    g_lower = lb_table_bwd(lower_bounds, g_lb_tab, name="lb_table_bwd")

    def update(wt, mt, vt, gparts, shape2, name):
        outs = adamw(wt.reshape(shape2), mt.reshape(shape2), vt.reshape(shape2), gparts, name=name)
        return [o.reshape(wt.shape) for o in outs]

    def update_layers(wt, mt, vt, kind, name):
        shape2 = (-1, wt.shape[-1])
        per = []
        for l in range(depth):
            gp = big[l][kind]
            gp = gp.reshape((N_DEV,) + wt[l].reshape(shape2).shape)
            per.append(update(wt[l], mt[l], vt[l], gp, shape2, name))
        return [jnp.stack([p[k] for p in per], axis=0) for k in range(4)]

    def update_small(wt, mt, vt, gt, name):
        shape2 = (-1, wt.shape[-1])
        return update(wt, mt, vt, gt.reshape(shape2)[None], shape2, name)

    res = {
        "w_ada": update_small(w_ada, m_w_ada, v_w_ada, g_w_ada, "adamw_w_ada"),
        "b_ada": update_small(b_ada, m_b_ada, v_b_ada, g_b_ada, "adamw_b_ada"),
        "norm_pre": update_small(norm_pre, m_norm_pre, v_norm_pre, g_npre, "adamw_norm_pre"),
        "norm_post": update_small(norm_post, m_norm_post, v_norm_post, g_npost, "adamw_norm_post"),
        "w_in": update_layers(w_in, m_w_in, v_w_in, 0, "adamw_w_in"),
        "lower_bounds": update_small(lower_bounds, m_lower_bounds, v_lower_bounds, g_lower, "adamw_lower_bounds"),
        "hgrn_norm": update_small(hgrn_norm, m_hgrn_norm, v_hgrn_norm, g_hn, "adamw_hgrn_norm"),
        "pool_w": update_layers(pool_w, m_pool_w, v_pool_w, 4, "adamw_pool_w"),
        "pool_scale": update_small(pool_scale, m_pool_scale, v_pool_scale, g_ps, "adamw_pool_scale"),
        "w_proj_a": update_layers(w_proj_a, m_w_proj_a, v_w_proj_a, 1, "adamw_w_proj_a"),
        "w_proj_b": update_layers(w_proj_b, m_w_proj_b, v_w_proj_b, 2, "adamw_w_proj_b"),
        "w_out": update_layers(w_out, m_w_out, v_w_out, 3, "adamw_w_out"),
    }
    order = ["w_ada", "b_ada", "norm_pre", "norm_post", "w_in", "lower_bounds", "hgrn_norm", "pool_w",
             "pool_scale", "w_proj_a", "w_proj_b", "w_out"]
    outs = [loss, grad_x]
    for k in range(4):
        outs += [res[nm][k] for nm in order]
    return tuple(outs)
```

```python
import functools

import jax
import jax.numpy as jnp
from jax import lax
from jax.experimental import pallas as pl
from jax.experimental.pallas import tpu as pltpu

F32 = jnp.float32
BF16 = jnp.bfloat16
MESH = pl.DeviceIdType.MESH

N_DEV = 8
EPS = 1e-6
MIN_FORGET = 1e-30
D_MODEL = 2048
HEADS = 8
HEAD_DIM = 128
CHUNK = 64
SUB = 16
N_SUB = CHUNK // SUB
WIDTH = 1024
POOL_WINDOWS = (2, 4, 8, 16)
POOL_GW = 256
HALO = 16
IN_COLS = 10240
LANE = 128
N_COLBLK = IN_COLS // LANE
GATE_COLS = 4096
HEAD_COLS = 4 * HEAD_DIM
POOL_COLS = 2 * POOL_GW
MAX_EXP = 80.0

ADAM_LR = 0.001
ADAM_B1 = 0.9
ADAM_B2 = 0.999
ADAM_EPS = 1e-08
ADAM_WD = 0.01
ADAM_STEP = 10

VMEM_LIMIT = 56 * 1024 * 1024


def _cparams(sem=None):
    return pltpu.CompilerParams(dimension_semantics=sem, vmem_limit_bytes=VMEM_LIMIT)


def _sigmoid(v):
    return 1.0 / (1.0 + jnp.exp(-v))


def _dot(a, b):
    return jnp.dot(a, b, preferred_element_type=F32)


def _dot_nt(a, b):
    return lax.dot_general(a, b, (((1,), (1,)), ((), ())), preferred_element_type=F32)


def _dot_tn(a, b):
    return lax.dot_general(a, b, (((0,), (0,)), ((), ())), preferred_element_type=F32)


def _colsum(v):
    return jnp.sum(v, axis=0, keepdims=True)


def _rowmean(v):
    return jnp.mean(v, axis=-1, keepdims=True)


def _orig_block(n):
    m1 = n - 32
    head = 8 * (m1 % 4) + m1 // 4
    m2 = n - 64
    t2 = m2 % 4
    pool = 32 + 2 * (m2 // 4) + (t2 % 2) + 8 * (t2 // 2)
    return jnp.where(n < 32, n + 48, jnp.where(n < 64, head, pool))


def _accumulate(step, steps, prod, o_ref, acc_ref):
    if steps == 1:
        o_ref[...] = prod.astype(o_ref.dtype)
        return

    @pl.when(step == 0)
    def _():
        acc_ref[...] = prod

    @pl.when(step > 0)
    def _():
        acc_ref[...] += prod

    @pl.when(step == steps - 1)
    def _():
        o_ref[...] = acc_ref[...].astype(o_ref.dtype)


def matmul_nn(a, b, *, tm, tn, out_dtype, name):
    m, k = a.shape
    n = b.shape[1]
    tm = min(tm, m)

    def body(a_ref, b_ref, o_ref):
        o_ref[...] = _dot(a_ref[...], b_ref[...]).astype(o_ref.dtype)

    return pl.pallas_call(
        body, grid=(n // tn, m // tm),
        in_specs=[pl.BlockSpec((tm, k), lambda j, i: (i, 0)), pl.BlockSpec((k, tn), lambda j, i: (0, j))],
        out_specs=pl.BlockSpec((tm, tn), lambda j, i: (i, j)),
        out_shape=jax.ShapeDtypeStruct((m, n), out_dtype),
        compiler_params=_cparams(("arbitrary", "arbitrary")), name=name)(a, b)


def matmul_nt(a, b, *, tm, tn, out_dtype, name):
    m, n = a.shape
    k = b.shape[0]
    tm = min(tm, m)
    steps = n // tn

    def body(a_ref, b_ref, o_ref, *acc):
        _accumulate(pl.program_id(1), steps, _dot_nt(a_ref[...], b_ref[...]), o_ref, acc[0] if acc else None)

    return pl.pallas_call(
        body, grid=(m // tm, steps),
        in_specs=[pl.BlockSpec((tm, tn), lambda i, j: (i, j)), pl.BlockSpec((k, tn), lambda i, j: (0, j))],
        out_specs=pl.BlockSpec((tm, k), lambda i, j: (i, 0)),
        out_shape=jax.ShapeDtypeStruct((m, k), out_dtype),
        scratch_shapes=[pltpu.VMEM((tm, k), F32)] if steps > 1 else [],
        compiler_params=_cparams(("arbitrary", "arbitrary")), name=name)(a, b)


def matmul_tn(a, b, *, tm, tn, out_dtype, name):
    m, k = a.shape
    n = b.shape[1]
    tm = min(tm, m)
    steps = m // tm

    def body(a_ref, b_ref, o_ref, *acc):
        _accumulate(pl.program_id(1), steps, _dot_tn(a_ref[...], b_ref[...]), o_ref, acc[0] if acc else None)

    return pl.pallas_call(
        body, grid=(n // tn, steps),
        in_specs=[pl.BlockSpec((tm, k), lambda j, i: (i, 0)), pl.BlockSpec((tm, tn), lambda j, i: (i, j))],
        out_specs=pl.BlockSpec((k, tn), lambda j, i: (0, j)),
        out_shape=jax.ShapeDtypeStruct((k, n), out_dtype),
        scratch_shapes=[pltpu.VMEM((k, tn), F32)] if steps > 1 else [],
        compiler_params=_cparams(("arbitrary", "arbitrary")), name=name)(a, b)


def permute_w_in(staged, *, name):
    k = staged.shape[1]

    def body(i_ref, o_ref):
        o_ref[...] = i_ref[...]

    def src(nb):
        ob = _orig_block(nb)
        return (ob // 10, 0, ob % 10)

    return pl.pallas_call(
        body, grid=(N_COLBLK,),
        in_specs=[pl.BlockSpec((None, k, LANE), src)],
        out_specs=pl.BlockSpec((k, LANE), lambda nb: (0, nb)),
        out_shape=jax.ShapeDtypeStruct((k, IN_COLS), staged.dtype),
        compiler_params=_cparams(("arbitrary",)), name=name)(staged)


def unpermute_w_in(dw, *, name):
    k = dw.shape[0]

    def body(i_ref, o_ref):
        o_ref[...] = i_ref[...]

    def dst(nb):
        ob = _orig_block(nb)
        return (ob // 10, 0, ob % 10)

    return pl.pallas_call(
        body, grid=(N_COLBLK,),
        in_specs=[pl.BlockSpec((k, LANE), lambda nb: (0, nb))],
        out_specs=pl.BlockSpec((None, k, LANE), dst),
        out_shape=jax.ShapeDtypeStruct((N_DEV, k, IN_COLS // N_DEV), dw.dtype),
        compiler_params=_cparams(("arbitrary",)), name=name)(dw)


def _row_tile(s):
    return min(256, s)


def _row_spec(t, w, col=0):
    return pl.BlockSpec((t, w), lambda i: (i, col))


def _vec_spec(w):
    return pl.BlockSpec((1, w), lambda i: (0, 0))


def prenorm_fwd(x, gain, shift, scale, *, name):
    s, d = x.shape
    t = _row_tile(s)

    def body(x_ref, g_ref, sh_ref, sc_ref, h_ref):
        xv = x_ref[...]
        r = lax.rsqrt(_rowmean(xv * xv) + EPS)
        h_ref[...] = ((xv * r) * g_ref[...] * (1.0 + sc_ref[...]) + sh_ref[...]).astype(h_ref.dtype)

    return pl.pallas_call(
        body, grid=(s // t,),
        in_specs=[_row_spec(t, d), _vec_spec(d), _vec_spec(d), _vec_spec(d)],
        out_specs=_row_spec(t, d), out_shape=jax.ShapeDtypeStruct((s, d), BF16),
        compiler_params=_cparams(("arbitrary",)), name=name)(x, gain, shift, scale)


def prenorm_bwd(dh, x, gain, scale, g_res, *, name):
    s, d = x.shape
    t = _row_tile(s)

    def body(dh_ref, x_ref, g_ref, sc_ref, gr_ref, dx_ref, dsh_ref, dsc_ref, dg_ref):
        i = pl.program_id(0)
        xv = x_ref[...]
        dhv = dh_ref[...]
        r = lax.rsqrt(_rowmean(xv * xv) + EPS)
        xn = xv * r
        gain_v = g_ref[...]
        one_sc = 1.0 + sc_ref[...]
        dyn = dhv * one_sc
        dxn = dyn * gain_v
        dx_ref[...] = r * (dxn - xn * _rowmean(dxn * xn)) + gr_ref[...]
        p_sh = _colsum(dhv)
        p_sc = _colsum(dhv * (xn * gain_v))
        p_g = _colsum(dyn * xn)

        @pl.when(i == 0)
        def _():
            dsh_ref[...] = p_sh
            dsc_ref[...] = p_sc
            dg_ref[...] = p_g

        @pl.when(i > 0)
        def _():
            dsh_ref[...] += p_sh
            dsc_ref[...] += p_sc
            dg_ref[...] += p_g

    vec = jax.ShapeDtypeStruct((1, d), F32)
    return pl.pallas_call(
        body, grid=(s // t,),
        in_specs=[_row_spec(t, d), _row_spec(t, d), _vec_spec(d), _vec_spec(d), _row_spec(t, d)],
        out_specs=[_row_spec(t, d), _vec_spec(d), _vec_spec(d), _vec_spec(d)],
        out_shape=[jax.ShapeDtypeStruct((s, d), F32), vec, vec, vec],
        compiler_params=_cparams(("arbitrary",)), name=name)(dh, x, gain, scale, g_res)


def postnorm_fwd(x, out, gain, gate, *, name):
    s, d = x.shape
    t = _row_tile(s)

    def body(x_ref, o_ref, g_ref, gt_ref, y_ref):
        ov = o_ref[...]
        r = lax.rsqrt(_rowmean(ov * ov) + EPS)
        y_ref[...] = x_ref[...] + gt_ref[...] * ((ov * r) * g_ref[...])

    return pl.pallas_call(
        body, grid=(s // t,),
        in_specs=[_row_spec(t, d), _row_spec(t, d), _vec_spec(d), _vec_spec(d)],
        out_specs=_row_spec(t, d), out_shape=jax.ShapeDtypeStruct((s, d), F32),
        compiler_params=_cparams(("arbitrary",)), name=name)(x, out, gain, gate)


def postnorm_bwd(g, out, gain, gate, *, name):
    s, d = out.shape
    t = _row_tile(s)

    def body(g_ref, o_ref, gn_ref, gt_ref, do_ref, dgt_ref, dgn_ref):
        i = pl.program_id(0)
        ov = o_ref[...]
        gv = g_ref[...]
        r = lax.rsqrt(_rowmean(ov * ov) + EPS)
        on = ov * r
        gain_v = gn_ref[...]
        gate_v = gt_ref[...]
        dn = gv * gate_v
        don = dn * gain_v
        do_ref[...] = (r * (don - on * _rowmean(don * on))).astype(do_ref.dtype)
        p_gt = _colsum(gv * (on * gain_v))
        p_gn = _colsum(dn * on)

        @pl.when(i == 0)
        def _():
            dgt_ref[...] = p_gt
            dgn_ref[...] = p_gn

        @pl.when(i > 0)
        def _():
            dgt_ref[...] += p_gt
            dgn_ref[...] += p_gn

    vec = jax.ShapeDtypeStruct((1, d), F32)
    return pl.pallas_call(
        body, grid=(s // t,),
        in_specs=[_row_spec(t, d), _row_spec(t, d), _vec_spec(d), _vec_spec(d)],
        out_specs=[_row_spec(t, d), _vec_spec(d), _vec_spec(d)],
        out_shape=[jax.ShapeDtypeStruct((s, d), BF16), vec, vec],
        compiler_params=_cparams(("arbitrary",)), name=name)(g, out, gain, gate)


def loss_head(y, target, *, name):
    s, d = y.shape
    t = _row_tile(s)
    steps = s // t

    def body(y_ref, t_ref, dy_ref, loss_ref, acc_ref):
        i = pl.program_id(0)
        err = y_ref[...] - t_ref[...]
        dy_ref[...] = err * (1.0 / d)
        part = _colsum(err * err)

        @pl.when(i == 0)
        def _():
            acc_ref[...] = part

        @pl.when(i > 0)
        def _():
            acc_ref[...] += part

        @pl.when(i == steps - 1)
        def _():
            loss_ref[...] = jnp.sum(acc_ref[...], axis=1, keepdims=True) * (0.5 / d)

    return pl.pallas_call(
        body, grid=(steps,),
        in_specs=[_row_spec(t, d), _row_spec(t, d)],
        out_specs=[_row_spec(t, d), pl.BlockSpec((1, 1), lambda i: (0, 0))],
        out_shape=[jax.ShapeDtypeStruct((s, d), F32), jax.ShapeDtypeStruct((1, 1), F32)],
        scratch_shapes=[pltpu.VMEM((1, d), F32)],
        compiler_params=_cparams(("arbitrary",)), name=name)(y, target)


def gate_fwd(proj, pa, pb, *, name):
    s, d = pa.shape
    t = _row_tile(s)

    def body(ga_ref, gb_ref, pa_ref, pb_ref, m_ref):
        m_ref[...] = (_sigmoid(ga_ref[...]) * pa_ref[...] + _sigmoid(gb_ref[...]) * pb_ref[...]).astype(m_ref.dtype)

    return pl.pallas_call(
        body, grid=(s // t,),
        in_specs=[_row_spec(t, d, 0), _row_spec(t, d, 1), _row_spec(t, d), _row_spec(t, d)],
        out_specs=_row_spec(t, d), out_shape=jax.ShapeDtypeStruct((s, d), BF16),
        compiler_params=_cparams(("arbitrary",)), name=name)(proj, proj, pa, pb)


def gate_bwd(dmerged, proj, pa, pb, *, name):
    s, d = pa.shape
    t = _row_tile(s)

    def body(dm_ref, ga_ref, gb_ref, pa_ref, pb_ref, dpa_ref, dpb_ref, dp_ref):
        dm = dm_ref[...]
        sa = _sigmoid(ga_ref[...])
        sb = _sigmoid(gb_ref[...])
        dpa_ref[...] = (dm * sa).astype(dpa_ref.dtype)
        dpb_ref[...] = (dm * sb).astype(dpb_ref.dtype)
        dp_ref[:, :d] = (dm * pa_ref[...] * sa * (1.0 - sa)).astype(dp_ref.dtype)
        dp_ref[:, d:] = (dm * pb_ref[...] * sb * (1.0 - sb)).astype(dp_ref.dtype)

    return pl.pallas_call(
        body, grid=(s // t,),
        in_specs=[_row_spec(t, d), _row_spec(t, d, 0), _row_spec(t, d, 1), _row_spec(t, d), _row_spec(t, d)],
        out_specs=[_row_spec(t, d), _row_spec(t, d), _row_spec(t, 2 * d, 0)],
        out_shape=[jax.ShapeDtypeStruct((s, d), BF16), jax.ShapeDtypeStruct((s, d), BF16),
                   jax.ShapeDtypeStruct((s, IN_COLS), BF16)],
        compiler_params=_cparams(("arbitrary",)), name=name)(dmerged, proj, proj, pa, pb)


def _pool_tile(s):
    return min(256, s)


def pool_fwd(proj, pw, ps, *, name):
    s = proj.shape[0]
    t = _pool_tile(s)
    pool_blk = (GATE_COLS + HEADS * HEAD_COLS) // (len(POOL_WINDOWS) * POOL_COLS)

    def body(p_ref, halo_ref, pw_ref, ps_ref, yb_ref, pooled_ref, mixed_ref):
        i = pl.program_id(0)
        halo = jnp.where(i == 0, 0.0, halo_ref[...])
        row = i * t + lax.broadcasted_iota(jnp.int32, (t, 1), 0)
        for g, w in enumerate(POOL_WINDOWS):
            vb = p_ref[:, g * POOL_COLS:g * POOL_COLS + POOL_GW]
            zb = p_ref[:, g * POOL_COLS + POOL_GW:(g + 1) * POOL_COLS]
            acc = jnp.concatenate([halo[:, g * POOL_COLS:g * POOL_COLS + POOL_GW], vb], axis=0)
            sh = 1
            while sh < w:
                acc = acc + pltpu.roll(acc, sh, axis=0)
                sh *= 2
            cnt = jnp.minimum(row + 1, w).astype(F32)
            pooled = acc[HALO:, :] / cnt - vb
            mixed = _dot(pooled.astype(BF16), pw_ref[g])
            cols = slice(g * POOL_GW, (g + 1) * POOL_GW)
            yb = mixed * ps_ref[:, cols] * (zb * _sigmoid(zb))
            yb_ref[:, cols] = yb.astype(yb_ref.dtype)
            pooled_ref[:, cols] = pooled.astype(pooled_ref.dtype)
            mixed_ref[:, cols] = mixed

    wide = len(POOL_WINDOWS) * POOL_COLS
    return pl.pallas_call(
        body, grid=(s // t,),
        in_specs=[pl.BlockSpec((t, wide), lambda i: (i, pool_blk)),
                  pl.BlockSpec((HALO, wide), lambda i: (jnp.maximum(i * (t // HALO) - 1, 0), pool_blk)),
                  pl.BlockSpec((len(POOL_WINDOWS), POOL_GW, POOL_GW), lambda i: (0, 0, 0)),
                  _vec_spec(WIDTH)],
        out_specs=[_row_spec(t, WIDTH)] * 3,
        out_shape=[jax.ShapeDtypeStruct((s, WIDTH), BF16), jax.ShapeDtypeStruct((s, WIDTH), BF16),
                   jax.ShapeDtypeStruct((s, WIDTH), F32)],
        compiler_params=_cparams(("arbitrary",)), name=name)(proj, proj, pw, ps)


def pool_bwd(dyb, proj, pooled, mixed, pw, ps, dproj, *, name):
    s = proj.shape[0]
    t = _pool_tile(s)
    nblk = s // t
    ng = len(POOL_WINDOWS)
    wide = ng * POOL_COLS
    pool_blk = (GATE_COLS + HEADS * HEAD_COLS) // wide

    def body(dy_ref, p_ref, pooled_ref, mixed_ref, pw_ref, ps_ref, dp_any, dp_ref, dpw_ref, dps_ref, carry):
        del dp_any
        i = pl.program_id(0)
        ii = nblk - 1 - i

        @pl.when(i == 0)
        def _():
            carry[...] = jnp.zeros_like(carry)
            dpw_ref[...] = jnp.zeros_like(dpw_ref)
            dps_ref[...] = jnp.zeros_like(dps_ref)

        row = ii * t + lax.broadcasted_iota(jnp.int32, (t, 1), 0)
        for g, w in enumerate(POOL_WINDOWS):
            cols = slice(g * POOL_GW, (g + 1) * POOL_GW)
            zb = p_ref[:, g * POOL_COLS + POOL_GW:(g + 1) * POOL_COLS]
            dy = dy_ref[:, cols]
            mx = mixed_ref[:, cols]
            sc = ps_ref[:, cols]
            sg = _sigmoid(zb)
            dzb = dy * (mx * sc) * (sg * (1.0 + zb * (1.0 - sg)))
            dpm = dy * (zb * sg)
            dps_ref[:, cols] += _colsum(dpm * mx)
            dmixed = (dpm * sc).astype(BF16)
            dpooled = _dot_nt(dmixed, pw_ref[g])
            dpw_ref[g] += _dot_tn(pooled_ref[:, cols], dmixed)
            cnt = jnp.minimum(row + 1, w).astype(F32)
            u = dpooled / cnt
            acc = jnp.concatenate([u, carry[:, cols]], axis=0)
            sh = 1
            while sh < w:
                acc = acc + pltpu.roll(acc, t + HALO - sh, axis=0)
                sh *= 2
            carry[:, cols] = u[:HALO, :]
            dp_ref[:, g * POOL_COLS:g * POOL_COLS + POOL_GW] = (acc[:t, :] - dpooled).astype(dp_ref.dtype)
            dp_ref[:, g * POOL_COLS + POOL_GW:(g + 1) * POOL_COLS] = dzb.astype(dp_ref.dtype)

    rev = lambda i: (nblk - 1 - i, 0)
    return pl.pallas_call(
        body, grid=(nblk,),
        in_specs=[pl.BlockSpec((t, WIDTH), rev),
                  pl.BlockSpec((t, wide), lambda i: (nblk - 1 - i, pool_blk)),
                  pl.BlockSpec((t, WIDTH), rev), pl.BlockSpec((t, WIDTH), rev),
                  pl.BlockSpec((ng, POOL_GW, POOL_GW), lambda i: (0, 0, 0)),
                  _vec_spec(WIDTH),
                  pl.BlockSpec(memory_space=pl.ANY)],
        out_specs=[pl.BlockSpec((t, wide), lambda i: (nblk - 1 - i, pool_blk)),
                   pl.BlockSpec((ng, POOL_GW, POOL_GW), lambda i: (0, 0, 0)),
                   _vec_spec(WIDTH)],
        out_shape=[jax.ShapeDtypeStruct(dproj.shape, dproj.dtype),
                   jax.ShapeDtypeStruct((ng, POOL_GW, POOL_GW), F32),
                   jax.ShapeDtypeStruct((1, WIDTH), F32)],
        scratch_shapes=[pltpu.VMEM((HALO, WIDTH), F32)],
        input_output_aliases={6: 0},
        compiler_params=_cparams(("arbitrary",)), name=name)(dyb, proj, pooled, mixed, pw, ps, dproj)


def _hgrn_tile(s):
    return min(256, s)


def _chunk_consts():
    tt = lax.broadcasted_iota(jnp.int32, (CHUNK, CHUNK), 0)
    ss = lax.broadcasted_iota(jnp.int32, (CHUNK, CHUNK), 1)
    within = (ss <= tt) & (ss // SUB == tt // SUB)
    before = ss < (tt // SUB) * SUB
    cums = jnp.concatenate([within.astype(F32), before.astype(F32)], axis=0).astype(BF16)
    causal = ss <= tt
    upper = (ss >= tt).astype(F32).astype(BF16)
    row = lax.broadcasted_iota(jnp.int32, (CHUNK, 1), 0)
    return cums, causal, upper, row


def _dot_split(mat01, v):
    hi = v.astype(BF16)
    r1 = v - hi.astype(F32)
    mid = r1.astype(BF16)
    lo = (r1 - mid.astype(F32)).astype(BF16)
    return _dot(mat01, hi) + _dot(mat01, mid) + _dot(mat01, lo)


def _hgrn_chunk(qa, fa, lb, cums, row):
    sq = _sigmoid(qa)
    q = qa * sq
    sa = _sigmoid(fa)
    sna = _sigmoid(-fa)
    oml = 1.0 - lb
    f = lb + oml * sa
    fc = jnp.maximum(f, MIN_FORGET)
    lf = jnp.log(fc)
    k = oml * sna
    cb = _dot_split(cums, lf)
    c = cb[:CHUNK]
    bt = cb[CHUNK:]
    ec = jnp.exp(c)
    enc = jnp.exp(jnp.minimum(-c, MAX_EXP))
    qt = q * ec
    kt = k * enc
    dms, lhs, rhs = [], [], []
    for j in range(N_SUB):
        bj = bt[j * SUB:j * SUB + 1, :]
        dm = jnp.where(row >= j * SUB, jnp.exp(jnp.minimum(bt - bj, 0.0)), 0.0)
        dms.append(dm)
        lhs.append(qt * dm)
        rhs.append(jnp.where(row // SUB == j, kt, 0.0))
    lhs = jnp.concatenate(lhs, axis=1).astype(BF16)
    rhs = jnp.concatenate(rhs, axis=1).astype(BF16)
    b = bt + c
    bl = b[CHUNK - 1:CHUNK, :]
    ebl = jnp.exp(bl)
    edec = jnp.exp(bl - b)
    eb = ec * dms[0]
    return dict(sq=sq, q=q, sa=sa, sna=sna, oml=oml, f=f, fc=fc, k=k, ec=ec, enc=enc, dms=dms,
                lhs=lhs, rhs=rhs, ebl=ebl, edec=edec, eb=eb, qd=q * eb, kdec=k * edec)


def hgrn_fwd(proj, lb, hn, *, name):
    s = proj.shape[0]
    t = _hgrn_tile(s)
    nblk = s // t
    ncht = t // CHUNK
    head_blk0 = GATE_COLS // HEAD_COLS

    def body(p_ref, lb_ref, hn_ref, ya_ref, o_ref, st_ref, state):
        i = pl.program_id(1)

        @pl.when(i == 0)
        def _():
            state[...] = jnp.zeros_like(state)

        cums, causal, _, row = _chunk_consts()
        lbv = lb_ref[...]
        hnv = hn_ref[...]

        st = state[...]
        for ci in range(ncht):
            rows = slice(ci * CHUNK, (ci + 1) * CHUNK)
            qa = p_ref[rows, 0:HEAD_DIM]
            fa = p_ref[rows, HEAD_DIM:2 * HEAD_DIM]
            va = p_ref[rows, 2 * HEAD_DIM:3 * HEAD_DIM].astype(BF16)
            za = p_ref[rows, 3 * HEAD_DIM:4 * HEAD_DIM]
            pre = _hgrn_chunk(qa, fa, lbv, cums, row)
            stb = st.astype(BF16)
            st_ref[ci, 0] = stb
            a = jnp.where(causal, _dot_nt(pre["lhs"], pre["rhs"]), 0.0)
            o = _dot_nt(pre["qd"].astype(BF16), stb) + _dot(a.astype(BF16), va)
            st = st * pre["ebl"] + _dot_tn(va, pre["kdec"].astype(BF16))
            r = lax.rsqrt(_rowmean(o * o) + EPS)
            o_ref[rows, :] = o
            ya_ref[rows, :] = ((o * r) * hnv * (za * _sigmoid(za))).astype(ya_ref.dtype)
        state[...] = st

    return pl.pallas_call(
        body, grid=(HEADS, nblk),
        in_specs=[pl.BlockSpec((t, HEAD_COLS), lambda h, i: (i, head_blk0 + h)),
                  pl.BlockSpec((1, HEAD_DIM), lambda h, i: (0, h)),
                  pl.BlockSpec((1, HEAD_DIM), lambda h, i: (0, h))],
        out_specs=[pl.BlockSpec((t, HEAD_DIM), lambda h, i: (i, h)),
                   pl.BlockSpec((t, HEAD_DIM), lambda h, i: (i, h)),
                   pl.BlockSpec((ncht, 1, HEAD_DIM, HEAD_DIM), lambda h, i: (i, h, 0, 0))],
        out_shape=[jax.ShapeDtypeStruct((s, WIDTH), BF16), jax.ShapeDtypeStruct((s, WIDTH), F32),
                   jax.ShapeDtypeStruct((s // CHUNK, HEADS, HEAD_DIM, HEAD_DIM), BF16)],
        scratch_shapes=[pltpu.VMEM((HEAD_DIM, HEAD_DIM), F32)],
        compiler_params=_cparams(("arbitrary", "arbitrary")), name=name)(proj, lb, hn)


def hgrn_bwd(dya, proj, o_all, states, lb, hn, dproj, *, name):
    s = proj.shape[0]
    t = _hgrn_tile(s)
    nblk = s // t
    ncht = t // CHUNK
    head_blk0 = GATE_COLS // HEAD_COLS

    def body(dy_ref, p_ref, o_ref, st_ref, lb_ref, hn_ref, dp_any, dp_ref, dhn_ref, dlb_ref, dstate):
        del dp_any
        i = pl.program_id(1)

        @pl.when(i == 0)
        def _():
            dstate[...] = jnp.zeros_like(dstate)
            dhn_ref[...] = jnp.zeros_like(dhn_ref)
            dlb_ref[...] = jnp.zeros_like(dlb_ref)

        cums, causal, upper, row = _chunk_consts()
        lbv = lb_ref[...]
        hnv = hn_ref[...]

        dst1 = dstate[...]
        dhn_acc = jnp.zeros_like(hnv)
        dlb_acc = jnp.zeros_like(lbv)
        for ci in reversed(range(ncht)):
            rows = slice(ci * CHUNK, (ci + 1) * CHUNK)
            qa = p_ref[rows, 0:HEAD_DIM]
            fa = p_ref[rows, HEAD_DIM:2 * HEAD_DIM]
            vb = p_ref[rows, 2 * HEAD_DIM:3 * HEAD_DIM].astype(BF16)
            za = p_ref[rows, 3 * HEAD_DIM:4 * HEAD_DIM]
            o = o_ref[rows, :]
            dy = dy_ref[rows, :]
            st0b = st_ref[ci, 0]
            r = lax.rsqrt(_rowmean(o * o) + EPS)
            on = o * r
            sgz = _sigmoid(za)
            sz = za * sgz
            dza = dy * on * hnv * (sgz * (1.0 + za * (1.0 - sgz)))
            dhn_acc = dhn_acc + _colsum(dy * on * sz)
            don = dy * hnv * sz
            do = r * (don - on * _rowmean(don * on))
            dob = do.astype(BF16)
            pre = _hgrn_chunk(qa, fa, lbv, cums, row)
            q, k = pre["q"], pre["k"]
            a = jnp.where(causal, _dot_nt(pre["lhs"], pre["rhs"]), 0.0)
            dst1b = dst1.astype(BF16)
            dq_inter = _dot(dob, st0b) * pre["eb"]
            da = jnp.where(causal, _dot_nt(dob, vb), 0.0).astype(BF16)
            dv = _dot_tn(a.astype(BF16), dob) + _dot_nt(pre["kdec"].astype(BF16), dst1b)
            dk_state = _dot(vb, dst1b) * pre["edec"]
            dlhs = _dot(da, pre["rhs"])
            drhs = _dot_tn(da, pre["lhs"])
            dq_a = jnp.zeros_like(q)
            dk_a = jnp.zeros_like(k)
            for j in range(N_SUB):
                dq_a = dq_a + pre["dms"][j] * dlhs[:, j * HEAD_DIM:(j + 1) * HEAD_DIM]
                dk_a = dk_a + jnp.where(row // SUB == j, drhs[:, j * HEAD_DIM:(j + 1) * HEAD_DIM], 0.0)
            dq = dq_inter + pre["ec"] * dq_a
            dk = dk_state + pre["enc"] * dk_a
            db = q * dq - k * dk
            dbl = _colsum(k * dk_state) + pre["ebl"] * _colsum(dst1 * st0b.astype(F32))
            dlf = _dot_split(upper, db) + dbl
            dst1 = dst1 * pre["ebl"] + _dot_tn(dob, pre["qd"].astype(BF16))
            sq, sa, sna, oml = pre["sq"], pre["sa"], pre["sna"], pre["oml"]
            dqa = dq * (sq * (1.0 + qa * (1.0 - sq)))
            dlf_f = jnp.where(pre["f"] >= MIN_FORGET, dlf / pre["fc"], 0.0)
            dfa = (dlf_f - dk) * (oml * sa * sna)
            dlb_acc = dlb_acc + _colsum((dlf_f - dk) * sna)
            dp_ref[rows, :] = jnp.concatenate([dqa, dfa, dv, dza], axis=1).astype(dp_ref.dtype)
        dstate[...] = dst1
        dhn_ref[...] += dhn_acc
        dlb_ref[...] += dlb_acc

    rev = lambda h, i: (nblk - 1 - i, h)
    return pl.pallas_call(
        body, grid=(HEADS, nblk),
        in_specs=[pl.BlockSpec((t, HEAD_DIM), rev),
                  pl.BlockSpec((t, HEAD_COLS), lambda h, i: (nblk - 1 - i, head_blk0 + h)),
                  pl.BlockSpec((t, HEAD_DIM), rev),
                  pl.BlockSpec((ncht, 1, HEAD_DIM, HEAD_DIM), lambda h, i: (nblk - 1 - i, h, 0, 0)),
                  pl.BlockSpec((1, HEAD_DIM), lambda h, i: (0, h)),
                  pl.BlockSpec((1, HEAD_DIM), lambda h, i: (0, h)),
                  pl.BlockSpec(memory_space=pl.ANY)],
        out_specs=[pl.BlockSpec((t, HEAD_COLS), lambda h, i: (nblk - 1 - i, head_blk0 + h)),
                   pl.BlockSpec((1, HEAD_DIM), lambda h, i: (0, h)),
                   pl.BlockSpec((1, HEAD_DIM), lambda h, i: (0, h))],
        out_shape=[jax.ShapeDtypeStruct(dproj.shape, dproj.dtype),
                   jax.ShapeDtypeStruct((1, WIDTH), F32), jax.ShapeDtypeStruct((1, WIDTH), F32)],
        scratch_shapes=[pltpu.VMEM((HEAD_DIM, HEAD_DIM), F32)],
        input_output_aliases={6: 0},
        compiler_params=_cparams(("arbitrary", "arbitrary")), name=name)(dya, proj, o_all, states, lb, hn, dproj)


def _softmax_rows(lower):
    mx = jnp.max(lower, axis=0, keepdims=True)
    e = jnp.exp(lower - mx)
    return e / jnp.sum(e, axis=0, keepdims=True)


def lb_table(lower, *, name):
    depth, w = lower.shape

    def body(l_ref, o_ref):
        sm = _softmax_rows(l_ref[...])
        acc = jnp.zeros((1, w), F32)
        o_ref[0:1, :] = acc
        for l in range(1, depth):
            acc = acc + sm[l:l + 1, :]
            o_ref[l:l + 1, :] = acc

    return pl.pallas_call(body, out_shape=jax.ShapeDtypeStruct((depth, w), F32), name=name)(lower)


def lb_table_bwd(lower, dlb, *, name):
    depth, w = lower.shape

    def body(l_ref, d_ref, o_ref):
        sm = _softmax_rows(l_ref[...])
        dlbv = d_ref[...]
        dsm = [jnp.zeros((1, w), F32)]
        for i in range(1, depth):
            acc = jnp.zeros((1, w), F32)
            for l in range(i, depth):
                acc = acc + dlbv[l:l + 1, :]
            dsm.append(acc)
        inner = jnp.zeros((1, w), F32)
        for i in range(depth):
            inner = inner + sm[i:i + 1, :] * dsm[i]
        for i in range(depth):
            o_ref[i:i + 1, :] = sm[i:i + 1, :] * (dsm[i] - inner)

    return pl.pallas_call(body, out_shape=jax.ShapeDtypeStruct((depth, w), F32), name=name)(lower, dlb)


def w_ada_grad(c_all, dmod_cols, *, name):
    depth, _, cols = dmod_cols.shape
    d = c_all.shape[1]

    def body(c_ref, dm_ref, o_ref):
        cv = c_ref[...]
        ca = cv * _sigmoid(cv)
        o_ref[...] = _dot_tn(ca, dm_ref[...])

    return pl.pallas_call(
        body, grid=(depth,),
        in_specs=[pl.BlockSpec((N_DEV, d), lambda l: (0, 0)), pl.BlockSpec((None, N_DEV, cols), lambda l: (l, 0, 0))],
        out_specs=pl.BlockSpec((None, d, cols), lambda l: (l, 0, 0)),
        out_shape=jax.ShapeDtypeStruct((depth, d, cols), F32),
        compiler_params=_cparams(("arbitrary",)), name=name)(c_all, dmod_cols)


def sum_parts(parts, *, name):
    p, r, c = parts.shape

    def body(p_ref, o_ref):
        acc = p_ref[0]
        for j in range(1, p):
            acc = acc + p_ref[j]
        o_ref[...] = acc

    return pl.pallas_call(body, out_shape=jax.ShapeDtypeStruct((r, c), F32), name=name)(parts)


def adamw(w, m, v, gparts, *, name):
    r, c = w.shape
    p = gparts.shape[0]
    tr = r
    while tr * c * 4 > (1 << 20) and tr % 16 == 0:
        tr //= 2

    def body(w_ref, m_ref, v_ref, g_ref, go_ref, d_ref, mo_ref, vo_ref):
        g = g_ref[0].astype(F32)
        for j in range(1, p):
            g = g + g_ref[j].astype(F32)
        mn = ADAM_B1 * m_ref[...] + (1.0 - ADAM_B1) * g
        vn = ADAM_B2 * v_ref[...] + (1.0 - ADAM_B2) * (g * g)
        m_hat = mn / (1.0 - ADAM_B1 ** ADAM_STEP)
        v_hat = vn / (1.0 - ADAM_B2 ** ADAM_STEP)
        go_ref[...] = g
        d_ref[...] = -ADAM_LR * (m_hat / (jnp.sqrt(v_hat) + ADAM_EPS) + ADAM_WD * w_ref[...])
        mo_ref[...] = mn
        vo_ref[...] = vn

    spec = pl.BlockSpec((tr, c), lambda i: (i, 0))
    shp = jax.ShapeDtypeStruct((r, c), F32)
    return pl.pallas_call(
        body, grid=(r // tr,),
        in_specs=[spec, spec, spec, pl.BlockSpec((p, tr, c), lambda i: (0, i, 0))],
        out_specs=[spec] * 4, out_shape=[shp] * 4,
        compiler_params=_cparams(("arbitrary",)), name=name)(w, m, v, gparts)


def _position():
    x, y, c = lax.axis_index("x"), lax.axis_index("y"), lax.axis_index("c")
    return x, y, c


def _dev_index(x, y, c):
    return 4 * x + 2 * y + c


def all_gather(arrs, *, name):
    n = len(arrs)
    hbm = pl.BlockSpec(memory_space=pl.ANY)

    def body(*refs):
        ins, outs = refs[:n], refs[n:2 * n]
        send_sems, recv_sems, local_sems = refs[2 * n:]
        x, y, c = _position()
        me, sibling = (x, y, c), (x, y, 1 - c)
        chips = [(1 - x, y), (x, 1 - y), (1 - x, 1 - y)]

        def copy(a, k, block, to, own):
            slot = outs[a].at[_dev_index(*block)]
            return pltpu.make_async_remote_copy(
                src_ref=ins[a] if own else slot, dst_ref=slot,
                send_sem=send_sems.at[a * 7 + k], recv_sem=recv_sems.at[a * 7 + k],
                device_id=to, device_id_type=MESH)

        mine = [pltpu.make_async_copy(ins[a], outs[a].at[_dev_index(*me)], local_sems.at[a]) for a in range(n)]
        for cp in mine:
            cp.start()
        first = []
        for a in range(n):
            first.append(copy(a, 0, me, sibling, True))
            first += [copy(a, 1 + j, me, (*chip, c), True) for j, chip in enumerate(chips)]
        for cp in first:
            cp.start()
        passed = []
        for j, chip in enumerate(chips):
            for a in range(n):
                copy(a, 1 + j, (*chip, c), me, False).wait_recv()
                fwd = copy(a, 4 + j, (*chip, c), sibling, False)
                fwd.start()
                passed.append(fwd)
        for a in range(n):
            copy(a, 0, sibling, me, False).wait_recv()
            for j, chip in enumerate(chips):
                copy(a, 4 + j, (*chip, 1 - c), me, False).wait_recv()
        for cp in first + passed:
            cp.wait_send()
        for cp in mine:
            cp.wait()

    return pl.pallas_call(
        body,
        out_shape=[jax.ShapeDtypeStruct((N_DEV,) + a.shape, a.dtype) for a in arrs],
        in_specs=[hbm] * n, out_specs=[hbm] * n,
        scratch_shapes=[pltpu.SemaphoreType.DMA((7 * n,)), pltpu.SemaphoreType.DMA((7 * n,)),
                        pltpu.SemaphoreType.DMA((n,))],
        name=name)(*arrs)


def scatter_parts(arrs, *, name):
    n = len(arrs)
    hbm = pl.BlockSpec(memory_space=pl.ANY)

    def body(*refs):
        ins, outs = refs[:n], refs[n:2 * n]
        send_sems, recv_sems, local_sems = refs[2 * n:]
        x, y, c = _position()
        me = _dev_index(x, y, c)

        def peer(r):
            return (x ^ (r >> 2), y ^ ((r >> 1) & 1), c ^ (r & 1))

        def copy(a, r):
            to = peer(r)
            return pltpu.make_async_remote_copy(
                src_ref=ins[a].at[_dev_index(*to)], dst_ref=outs[a].at[me],
                send_sem=send_sems.at[a * 7 + r - 1], recv_sem=recv_sems.at[a * 7 + r - 1],
                device_id=to, device_id_type=MESH)

        def arrival(a, r):
            return pltpu.make_async_remote_copy(
                src_ref=ins[a].at[me], dst_ref=outs[a].at[_dev_index(*peer(r))],
                send_sem=send_sems.at[a * 7 + r - 1], recv_sem=recv_sems.at[a * 7 + r - 1],
                device_id=peer(r), device_id_type=MESH)

        mine = [pltpu.make_async_copy(ins[a].at[me], outs[a].at[me], local_sems.at[a]) for a in range(n)]
        for cp in mine:
            cp.start()
        sends = [copy(a, r) for r in range(1, N_DEV) for a in range(n)]
        for cp in sends:
            cp.start()
        for r in range(1, N_DEV):
            for a in range(n):
                arrival(a, r).wait_recv()
        for cp in sends:
            cp.wait_send()
        for cp in mine:
            cp.wait()

    return pl.pallas_call(
        body,
        out_shape=[jax.ShapeDtypeStruct(a.shape, a.dtype) for a in arrs],
        in_specs=[hbm] * n, out_specs=[hbm] * n,
        scratch_shapes=[pltpu.SemaphoreType.DMA((7 * n,)), pltpu.SemaphoreType.DMA((7 * n,)),
                        pltpu.SemaphoreType.DMA((n,))],
        name=name)(*arrs)


def mod_exchange(c_all, w_ada, b_cols, *, name):
    depth, d, cols = w_ada.shape
    hbm = pl.BlockSpec(memory_space=pl.ANY)
    vmem = pl.BlockSpec(memory_space=pltpu.VMEM)

    def body(c_ref, w_ref, b_ref, out_ref, wbuf, sendbuf, send_sems, recv_sems, load_sem):
        x, y, c = _position()
        me = _dev_index(x, y, c)
        cv = c_ref[...]
        ca = cv * _sigmoid(cv)
        for l in range(depth):
            load = pltpu.make_async_copy(w_ref.at[l], wbuf, load_sem)
            load.start()
            load.wait()
            part = jnp.dot(ca, wbuf[...], preferred_element_type=F32,
                           precision=lax.Precision.HIGHEST) + b_ref[l:l + 1, :]
            for bi in range(N_DEV):
                sendbuf[bi, l:l + 1, :] = part[bi:bi + 1, :]

        def peer(r):
            return (x ^ (r >> 2), y ^ ((r >> 1) & 1), c ^ (r & 1))

        def copy(r):
            to = peer(r)
            return pltpu.make_async_remote_copy(
                src_ref=sendbuf.at[_dev_index(*to)], dst_ref=out_ref.at[me],
                send_sem=send_sems.at[r - 1], recv_sem=recv_sems.at[r - 1],
                device_id=to, device_id_type=MESH)

        def arrival(r):
            return pltpu.make_async_remote_copy(
                src_ref=sendbuf.at[me], dst_ref=out_ref.at[_dev_index(*peer(r))],
                send_sem=send_sems.at[r - 1], recv_sem=recv_sems.at[r - 1],
                device_id=peer(r), device_id_type=MESH)

        out_ref[me] = sendbuf[me]
        sends = [copy(r) for r in range(1, N_DEV)]
        for cp in sends:
            cp.start()
        for r in range(1, N_DEV):
            arrival(r).wait_recv()
        for cp in sends:
            cp.wait_send()

    return pl.pallas_call(
        body,
        out_shape=jax.ShapeDtypeStruct((N_DEV, depth, cols), F32),
        in_specs=[vmem, hbm, vmem], out_specs=vmem,
        scratch_shapes=[pltpu.VMEM((d, cols), F32), pltpu.VMEM((N_DEV, depth, cols), F32),
                        pltpu.SemaphoreType.DMA((7,)), pltpu.SemaphoreType.DMA((7,)), pltpu.SemaphoreType.DMA],
        compiler_params=pltpu.CompilerParams(vmem_limit_bytes=VMEM_LIMIT),
        name=name)(c_all, w_ada, b_cols)


def kernel(x, c, w_ada, b_ada, norm_pre, norm_post, w_in, lower_bounds, hgrn_norm, pool_w, pool_scale, w_proj_a, w_proj_b, w_out, loss_target, m_w_ada, m_b_ada, m_norm_pre, m_norm_post, m_w_in, m_lower_bounds, m_hgrn_norm, m_pool_w, m_pool_scale, m_w_proj_a, m_w_proj_b, m_w_out, v_w_ada, v_b_ada, v_norm_pre, v_norm_post, v_w_in, v_lower_bounds, v_hgrn_norm, v_pool_w, v_pool_scale, v_w_proj_a, v_w_proj_b, v_w_out):
    depth = w_in.shape[0]
    d = D_MODEL
    ada_cols = w_ada.shape[2]
    xi, yi, ci = _position()
    me = _dev_index(xi, yi, ci)
    xs = x[0]
    target = loss_target[0]
    ng = len(POOL_WINDOWS)

    gathered = []
    for l in range(depth):
        shards = [w_in[l].astype(BF16), w_proj_a[l].astype(BF16), w_proj_b[l].astype(BF16),
                  w_out[l].astype(BF16), pool_w[l].astype(BF16)]
        g_in, g_pa, g_pb, g_out, g_pool = all_gather(shards, name="gather_weights")
        gathered.append(dict(
            w_in=permute_w_in(g_in, name="permute_w_in"),
            pa=jnp.transpose(g_pa, (1, 0, 2)).reshape(WIDTH, d),
            pb=jnp.transpose(g_pb, (1, 0, 2)).reshape(WIDTH, d),
            w_out=g_out.reshape(d, d),
            pool=jnp.transpose(g_pool, (1, 0, 2, 3)).reshape(ng, POOL_GW, POOL_GW)))

    (c_all,) = all_gather([c], name="gather_c")
    c_all = c_all.reshape(N_DEV, d)
    b_cols = lax.dynamic_slice_in_dim(b_ada, me * ada_cols, ada_cols, axis=1)
    mod_parts = mod_exchange(c_all, w_ada, b_cols, name="mod_exchange")
    mod = jnp.transpose(mod_parts, (1, 0, 2)).reshape(depth, 3 * d)
    lb_all = lb_table(lower_bounds, name="lb_table")

    saved = []
    cur = xs
    for l in range(depth):
        w = gathered[l]
        shift, scale, gate = mod[l:l + 1, :d], mod[l:l + 1, d:2 * d], mod[l:l + 1, 2 * d:]
        h = prenorm_fwd(cur, norm_pre[l:l + 1], shift, scale, name="prenorm_fwd")
        proj = matmul_nn(h, w["w_in"], tm=1024, tn=1024, out_dtype=F32, name="mm_w_in")
        y_a, o_all, states = hgrn_fwd(proj, lb_all[l:l + 1], hgrn_norm[l:l + 1], name="hgrn_fwd")
        y_b, pooled, mixed = pool_fwd(proj, w["pool"], pool_scale[l:l + 1], name="pool_fwd")
        pa = matmul_nn(y_a, w["pa"], tm=1024, tn=2048, out_dtype=F32, name="mm_proj_a")
        pb = matmul_nn(y_b, w["pb"], tm=1024, tn=2048, out_dtype=F32, name="mm_proj_b")
        merged = gate_fwd(proj, pa, pb, name="gate_fwd")
        out = matmul_nn(merged, w["w_out"], tm=1024, tn=1024, out_dtype=F32, name="mm_w_out")
        nxt = postnorm_fwd(cur, out, norm_post[l:l + 1], gate, name="postnorm_fwd")
        saved.append(dict(x=cur, h=h, proj=proj, y_a=y_a, o=o_all, states=states, y_b=y_b, pooled=pooled,
                          mixed=mixed, pa=pa, pb=pb, merged=merged, out=out, scale=scale, gate=gate))
        cur = nxt

    g, loss_part = loss_head(cur, target, name="loss_head")
    loss = lax.psum(loss_part[0, 0], ("x", "y", "c"))

    small = [None] * depth
    big = [None] * depth
    for l in reversed(range(depth)):
        w, sv = gathered[l], saved[l]
        dout, dgate, dnpost = postnorm_bwd(g, sv["out"], norm_post[l:l + 1], sv["gate"], name="postnorm_bwd")
        dmerged = matmul_nt(dout, w["w_out"], tm=512, tn=2048, out_dtype=F32, name="mm_w_out_dx")
        dw_out = matmul_tn(sv["merged"], dout, tm=2048, tn=1024, out_dtype=BF16, name="mm_w_out_dw")
        dpa, dpb, dproj = gate_bwd(dmerged, sv["proj"], sv["pa"], sv["pb"], name="gate_bwd")
        dya = matmul_nt(dpa, w["pa"], tm=1024, tn=2048, out_dtype=F32, name="mm_proj_a_dx")
        dyb = matmul_nt(dpb, w["pb"], tm=1024, tn=2048, out_dtype=F32, name="mm_proj_b_dx")
        dw_pa = matmul_tn(sv["y_a"], dpa, tm=2048, tn=2048, out_dtype=BF16, name="mm_proj_a_dw")
        dw_pb = matmul_tn(sv["y_b"], dpb, tm=2048, tn=2048, out_dtype=BF16, name="mm_proj_b_dw")
        dproj, dpool_w, dpool_scale = pool_bwd(dyb, sv["proj"], sv["pooled"], sv["mixed"], w["pool"],
                                               pool_scale[l:l + 1], dproj, name="pool_bwd")
        dproj, dhn, dlb = hgrn_bwd(dya, sv["proj"], sv["o"], sv["states"], lb_all[l:l + 1],
                                   hgrn_norm[l:l + 1], dproj, name="hgrn_bwd")
        dh = matmul_nt(dproj, w["w_in"], tm=512, tn=2048, out_dtype=F32, name="mm_w_in_dx")
        dw_in = matmul_tn(sv["h"], dproj, tm=2048, tn=1024, out_dtype=BF16, name="mm_w_in_dw")
        g, dshift, dscale, dnpre = prenorm_bwd(dh, sv["x"], norm_pre[l:l + 1], sv["scale"], g, name="prenorm_bwd")
        small[l] = jnp.concatenate([dshift, dscale, dgate, dnpre, dnpost, dlb, dhn, dpool_scale], axis=1)
        by_owner = lambda t: jnp.transpose(t.reshape(WIDTH, N_DEV, d // N_DEV), (1, 0, 2))
        parts = [unpermute_w_in(dw_in, name="unpermute_w_in"), by_owner(dw_pa), by_owner(dw_pb),
                 dw_out.reshape(N_DEV, d // N_DEV, d),
                 jnp.transpose(dpool_w.astype(BF16).reshape(ng, N_DEV, POOL_GW // N_DEV, POOL_GW), (1, 0, 2, 3))]
        big[l] = scatter_parts(parts, name="scatter_grads")
    grad_x = g[None]

    small_mine = jnp.concatenate(small, axis=0)
    (small_all,) = all_gather([small_mine], name="gather_small")
    small_sum = sum_parts(small_all, name="sum_small")
    dmod_all = small_all[:, :, :3 * d]
    dmod_cols = jnp.transpose(lax.dynamic_slice_in_dim(dmod_all, me * ada_cols, ada_cols, axis=2), (1, 0, 2))
    g_w_ada = w_ada_grad(c_all, dmod_cols, name="w_ada_grad")
    off = 3 * d
    g_b_ada = small_sum[:, :off]
    g_npre = small_sum[:, off:off + d]
    g_npost = small_sum[:, off + d:off + 2 * d]
    g_lb_tab = small_sum[:, off + 2 * d:off + 2 * d + WIDTH]
    g_hn = small_sum[:, off + 2 * d + WIDTH:off + 2 * d + 2 * WIDTH]
    g_ps = small_sum[:, off + 2 * d + 2 * WIDTH:]
    g_lower = lb_table_bwd(lower_bounds, g_lb_tab, name="lb_table_bwd")

    def update(wt, mt, vt, gparts, shape2, name):
        outs = adamw(wt.reshape(shape2), mt.reshape(shape2), vt.reshape(shape2), gparts, name=name)
        return [o.reshape(wt.shape) for o in outs]

    def update_layers(wt, mt, vt, kind, name):
        shape2 = (-1, wt.shape[-1])
        per = []
        for l in range(depth):
            gp = big[l][kind]
            gp = gp.reshape((N_DEV,) + wt[l].reshape(shape2).shape)
            per.append(update(wt[l], mt[l], vt[l], gp, shape2, name))
        return [jnp.stack([p[k] for p in per], axis=0) for k in range(4)]

    def update_small(wt, mt, vt, gt, name):
        shape2 = (-1, wt.shape[-1])
        return update(wt, mt, vt, gt.reshape(shape2)[None], shape2, name)

    res = {
        "w_ada": update_small(w_ada, m_w_ada, v_w_ada, g_w_ada, "adamw_w_ada"),
        "b_ada": update_small(b_ada, m_b_ada, v_b_ada, g_b_ada, "adamw_b_ada"),
        "norm_pre": update_small(norm_pre, m_norm_pre, v_norm_pre, g_npre, "adamw_norm_pre"),
        "norm_post": update_small(norm_post, m_norm_post, v_norm_post, g_npost, "adamw_norm_post"),
        "w_in": update_layers(w_in, m_w_in, v_w_in, 0, "adamw_w_in"),
        "lower_bounds": update_small(lower_bounds, m_lower_bounds, v_lower_bounds, g_lower, "adamw_lower_bounds"),
        "hgrn_norm": update_small(hgrn_norm, m_hgrn_norm, v_hgrn_norm, g_hn, "adamw_hgrn_norm"),
        "pool_w": update_layers(pool_w, m_pool_w, v_pool_w, 4, "adamw_pool_w"),
        "pool_scale": update_small(pool_scale, m_pool_scale, v_pool_scale, g_ps, "adamw_pool_scale"),
        "w_proj_a": update_layers(w_proj_a, m_w_proj_a, v_w_proj_a, 1, "adamw_w_proj_a"),
        "w_proj_b": update_layers(w_proj_b, m_w_proj_b, v_w_proj_b, 2, "adamw_w_proj_b"),
        "w_out": update_layers(w_out, m_w_out, v_w_out, 3, "adamw_w_out"),
    }
    order = ["w_ada", "b_ada", "norm_pre", "norm_post", "w_in", "lower_bounds", "hgrn_norm", "pool_w",
             "pool_scale", "w_proj_a", "w_proj_b", "w_out"]
    outs = [loss, grad_x]
    for k in range(4):
        outs += [res[nm][k] for nm in order]
    return tuple(outs)
```

```python
import functools

import jax
import jax.numpy as jnp
from jax import lax
from jax.experimental import pallas as pl
from jax.experimental.pallas import tpu as pltpu

F32 = jnp.float32
BF16 = jnp.bfloat16
MESH = pl.DeviceIdType.MESH

N_DEV = 8
EPS = 1e-6
MIN_FORGET = 1e-30
D_MODEL = 2048
HEADS = 8
HEAD_DIM = 128
CHUNK = 64
SUB = 16
N_SUB = CHUNK // SUB
WIDTH = 1024
POOL_WINDOWS = (2, 4, 8, 16)
POOL_GW = 256
HALO = 16
IN_COLS = 10240
LANE = 128
N_COLBLK = IN_COLS // LANE
GATE_COLS = 4096
HEAD_COLS = 4 * HEAD_DIM
POOL_COLS = 2 * POOL_GW
MAX_EXP = 80.0

ADAM_LR = 0.001
ADAM_B1 = 0.9
ADAM_B2 = 0.999
ADAM_EPS = 1e-08
ADAM_WD = 0.01
ADAM_STEP = 10

VMEM_LIMIT = 56 * 1024 * 1024


def _cparams(sem=None):
    return pltpu.CompilerParams(dimension_semantics=sem, vmem_limit_bytes=VMEM_LIMIT)


def _sigmoid(v):
    return 1.0 / (1.0 + jnp.exp(-v))


def _dot(a, b):
    return jnp.dot(a, b, preferred_element_type=F32)


def _dot_nt(a, b):
    return lax.dot_general(a, b, (((1,), (1,)), ((), ())), preferred_element_type=F32)


def _dot_tn(a, b):
    return lax.dot_general(a, b, (((0,), (0,)), ((), ())), preferred_element_type=F32)


def _colsum(v):
    return jnp.sum(v, axis=0, keepdims=True)


def _rowmean(v):
    return jnp.mean(v, axis=-1, keepdims=True)


def _orig_block(n):
    m1 = n - 32
    head = 8 * (m1 % 4) + m1 // 4
    m2 = n - 64
    t2 = m2 % 4
    pool = 32 + 2 * (m2 // 4) + (t2 % 2) + 8 * (t2 // 2)
    return jnp.where(n < 32, n + 48, jnp.where(n < 64, head, pool))


def _accumulate(step, steps, prod, o_ref, acc_ref):
    if steps == 1:
        o_ref[...] = prod.astype(o_ref.dtype)
        return

    @pl.when(step == 0)
    def _():
        acc_ref[...] = prod

    @pl.when(step > 0)
    def _():
        acc_ref[...] += prod

    @pl.when(step == steps - 1)
    def _():
        o_ref[...] = acc_ref[...].astype(o_ref.dtype)


def matmul_nn(a, b, *, tm, tn, out_dtype, name):
    m, k = a.shape
    n = b.shape[1]
    tm = min(tm, m)

    def body(a_ref, b_ref, o_ref):
        o_ref[...] = _dot(a_ref[...], b_ref[...]).astype(o_ref.dtype)

    return pl.pallas_call(
        body, grid=(n // tn, m // tm),
        in_specs=[pl.BlockSpec((tm, k), lambda j, i: (i, 0)), pl.BlockSpec((k, tn), lambda j, i: (0, j))],
        out_specs=pl.BlockSpec((tm, tn), lambda j, i: (i, j)),
        out_shape=jax.ShapeDtypeStruct((m, n), out_dtype),
        compiler_params=_cparams(("arbitrary", "arbitrary")), name=name)(a, b)


def matmul_nt(a, b, *, tm, tn, out_dtype, name):
    m, n = a.shape
    k = b.shape[0]
    tm = min(tm, m)
    steps = n // tn

    def body(a_ref, b_ref, o_ref, *acc):
        _accumulate(pl.program_id(1), steps, _dot_nt(a_ref[...], b_ref[...]), o_ref, acc[0] if acc else None)

    return pl.pallas_call(
        body, grid=(m // tm, steps),
        in_specs=[pl.BlockSpec((tm, tn), lambda i, j: (i, j)), pl.BlockSpec((k, tn), lambda i, j: (0, j))],
        out_specs=pl.BlockSpec((tm, k), lambda i, j: (i, 0)),
        out_shape=jax.ShapeDtypeStruct((m, k), out_dtype),
        scratch_shapes=[pltpu.VMEM((tm, k), F32)] if steps > 1 else [],
        compiler_params=_cparams(("arbitrary", "arbitrary")), name=name)(a, b)


def matmul_tn(a, b, *, tm, tn, out_dtype, name):
    m, k = a.shape
    n = b.shape[1]
    tm = min(tm, m)
    steps = m // tm

    def body(a_ref, b_ref, o_ref, *acc):
        _accumulate(pl.program_id(1), steps, _dot_tn(a_ref[...], b_ref[...]), o_ref, acc[0] if acc else None)

    return pl.pallas_call(
        body, grid=(n // tn, steps),
        in_specs=[pl.BlockSpec((tm, k), lambda j, i: (i, 0)), pl.BlockSpec((tm, tn), lambda j, i: (i, j))],
        out_specs=pl.BlockSpec((k, tn), lambda j, i: (0, j)),
        out_shape=jax.ShapeDtypeStruct((k, n), out_dtype),
        scratch_shapes=[pltpu.VMEM((k, tn), F32)] if steps > 1 else [],
        compiler_params=_cparams(("arbitrary", "arbitrary")), name=name)(a, b)


def permute_w_in(staged, *, name):
    k = staged.shape[1]

    def body(i_ref, o_ref):
        o_ref[...] = i_ref[...]

    def src(nb):
        ob = _orig_block(nb)
        return (ob // 10, 0, ob % 10)

    return pl.pallas_call(
        body, grid=(N_COLBLK,),
        in_specs=[pl.BlockSpec((None, k, LANE), src)],
        out_specs=pl.BlockSpec((k, LANE), lambda nb: (0, nb)),
        out_shape=jax.ShapeDtypeStruct((k, IN_COLS), staged.dtype),
        compiler_params=_cparams(("arbitrary",)), name=name)(staged)


def unpermute_w_in(dw, *, name):
    k = dw.shape[0]

    def body(i_ref, o_ref):
        o_ref[...] = i_ref[...]

    def dst(nb):
        ob = _orig_block(nb)
        return (ob // 10, 0, ob % 10)

    return pl.pallas_call(
        body, grid=(N_COLBLK,),
        in_specs=[pl.BlockSpec((k, LANE), lambda nb: (0, nb))],
        out_specs=pl.BlockSpec((None, k, LANE), dst),
        out_shape=jax.ShapeDtypeStruct((N_DEV, k, IN_COLS // N_DEV), dw.dtype),
        compiler_params=_cparams(("arbitrary",)), name=name)(dw)


def _row_tile(s):
    return min(256, s)


def _row_spec(t, w, col=0):
    return pl.BlockSpec((t, w), lambda i: (i, col))


def _vec_spec(w):
    return pl.BlockSpec((1, w), lambda i: (0, 0))


def prenorm_fwd(x, gain, shift, scale, *, name):
    s, d = x.shape
    t = _row_tile(s)

    def body(x_ref, g_ref, sh_ref, sc_ref, h_ref):
        xv = x_ref[...]
        r = lax.rsqrt(_rowmean(xv * xv) + EPS)
        h_ref[...] = ((xv * r) * g_ref[...] * (1.0 + sc_ref[...]) + sh_ref[...]).astype(h_ref.dtype)

    return pl.pallas_call(
        body, grid=(s // t,),
        in_specs=[_row_spec(t, d), _vec_spec(d), _vec_spec(d), _vec_spec(d)],
        out_specs=_row_spec(t, d), out_shape=jax.ShapeDtypeStruct((s, d), BF16),
        compiler_params=_cparams(("arbitrary",)), name=name)(x, gain, shift, scale)


def prenorm_bwd(dh, x, gain, scale, g_res, *, name):
    s, d = x.shape
    t = _row_tile(s)

    def body(dh_ref, x_ref, g_ref, sc_ref, gr_ref, dx_ref, dsh_ref, dsc_ref, dg_ref):
        i = pl.program_id(0)
        xv = x_ref[...]
        dhv = dh_ref[...]
        r = lax.rsqrt(_rowmean(xv * xv) + EPS)
        xn = xv * r
        gain_v = g_ref[...]
        one_sc = 1.0 + sc_ref[...]
        dyn = dhv * one_sc
        dxn = dyn * gain_v
        dx_ref[...] = r * (dxn - xn * _rowmean(dxn * xn)) + gr_ref[...]
        p_sh = _colsum(dhv)
        p_sc = _colsum(dhv * (xn * gain_v))
        p_g = _colsum(dyn * xn)

        @pl.when(i == 0)
        def _():
            dsh_ref[...] = p_sh
            dsc_ref[...] = p_sc
            dg_ref[...] = p_g

        @pl.when(i > 0)
        def _():
            dsh_ref[...] += p_sh
            dsc_ref[...] += p_sc
            dg_ref[...] += p_g

    vec = jax.ShapeDtypeStruct((1, d), F32)
    return pl.pallas_call(
        body, grid=(s // t,),
        in_specs=[_row_spec(t, d), _row_spec(t, d), _vec_spec(d), _vec_spec(d), _row_spec(t, d)],
        out_specs=[_row_spec(t, d), _vec_spec(d), _vec_spec(d), _vec_spec(d)],
        out_shape=[jax.ShapeDtypeStruct((s, d), F32), vec, vec, vec],
        compiler_params=_cparams(("arbitrary",)), name=name)(dh, x, gain, scale, g_res)


def postnorm_fwd(x, out, gain, gate, *, name):
    s, d = x.shape
    t = _row_tile(s)

    def body(x_ref, o_ref, g_ref, gt_ref, y_ref):
        ov = o_ref[...]
        r = lax.rsqrt(_rowmean(ov * ov) + EPS)
        y_ref[...] = x_ref[...] + gt_ref[...] * ((ov * r) * g_ref[...])

    return pl.pallas_call(
        body, grid=(s // t,),
        in_specs=[_row_spec(t, d), _row_spec(t, d), _vec_spec(d), _vec_spec(d)],
        out_specs=_row_spec(t, d), out_shape=jax.ShapeDtypeStruct((s, d), F32),
        compiler_params=_cparams(("arbitrary",)), name=name)(x, out, gain, gate)


def postnorm_bwd(g, out, gain, gate, *, name):
    s, d = out.shape
    t = _row_tile(s)

    def body(g_ref, o_ref, gn_ref, gt_ref, do_ref, dgt_ref, dgn_ref):
        i = pl.program_id(0)
        ov = o_ref[...]
        gv = g_ref[...]
        r = lax.rsqrt(_rowmean(ov * ov) + EPS)
        on = ov * r
        gain_v = gn_ref[...]
        gate_v = gt_ref[...]
        dn = gv * gate_v
        don = dn * gain_v
        do_ref[...] = (r * (don - on * _rowmean(don * on))).astype(do_ref.dtype)
        p_gt = _colsum(gv * (on * gain_v))
        p_gn = _colsum(dn * on)

        @pl.when(i == 0)
        def _():
            dgt_ref[...] = p_gt
            dgn_ref[...] = p_gn

        @pl.when(i > 0)
        def _():
            dgt_ref[...] += p_gt
            dgn_ref[...] += p_gn

    vec = jax.ShapeDtypeStruct((1, d), F32)
    return pl.pallas_call(
        body, grid=(s // t,),
        in_specs=[_row_spec(t, d), _row_spec(t, d), _vec_spec(d), _vec_spec(d)],
        out_specs=[_row_spec(t, d), _vec_spec(d), _vec_spec(d)],
        out_shape=[jax.ShapeDtypeStruct((s, d), BF16), vec, vec],
        compiler_params=_cparams(("arbitrary",)), name=name)(g, out, gain, gate)


def loss_head(y, target, *, name):
    s, d = y.shape
    t = _row_tile(s)
    steps = s // t

    def body(y_ref, t_ref, dy_ref, loss_ref, acc_ref):
        i = pl.program_id(0)
        err = y_ref[...] - t_ref[...]
        dy_ref[...] = err * (1.0 / d)
        part = _colsum(err * err)

        @pl.when(i == 0)
        def _():
            acc_ref[...] = part

        @pl.when(i > 0)
        def _():
            acc_ref[...] += part

        @pl.when(i == steps - 1)
        def _():
            loss_ref[...] = jnp.sum(acc_ref[...], axis=1, keepdims=True) * (0.5 / d)

    return pl.pallas_call(
        body, grid=(steps,),
        in_specs=[_row_spec(t, d), _row_spec(t, d)],
        out_specs=[_row_spec(t, d), pl.BlockSpec((1, 1), lambda i: (0, 0))],
        out_shape=[jax.ShapeDtypeStruct((s, d), F32), jax.ShapeDtypeStruct((1, 1), F32)],
        scratch_shapes=[pltpu.VMEM((1, d), F32)],
        compiler_params=_cparams(("arbitrary",)), name=name)(y, target)


def gate_fwd(proj, pa, pb, *, name):
    s, d = pa.shape
    t = _row_tile(s)

    def body(ga_ref, gb_ref, pa_ref, pb_ref, m_ref):
        m_ref[...] = (_sigmoid(ga_ref[...]) * pa_ref[...] + _sigmoid(gb_ref[...]) * pb_ref[...]).astype(m_ref.dtype)

    return pl.pallas_call(
        body, grid=(s // t,),
        in_specs=[_row_spec(t, d, 0), _row_spec(t, d, 1), _row_spec(t, d), _row_spec(t, d)],
        out_specs=_row_spec(t, d), out_shape=jax.ShapeDtypeStruct((s, d), BF16),
        compiler_params=_cparams(("arbitrary",)), name=name)(proj, proj, pa, pb)


def gate_bwd(dmerged, proj, pa, pb, *, name):
    s, d = pa.shape
    t = _row_tile(s)

    def body(dm_ref, ga_ref, gb_ref, pa_ref, pb_ref, dpa_ref, dpb_ref, dp_ref):
        dm = dm_ref[...]
        sa = _sigmoid(ga_ref[...])
        sb = _sigmoid(gb_ref[...])
        dpa_ref[...] = (dm * sa).astype(dpa_ref.dtype)
        dpb_ref[...] = (dm * sb).astype(dpb_ref.dtype)
        dp_ref[:, :d] = (dm * pa_ref[...] * sa * (1.0 - sa)).astype(dp_ref.dtype)
        dp_ref[:, d:] = (dm * pb_ref[...] * sb * (1.0 - sb)).astype(dp_ref.dtype)

    return pl.pallas_call(
        body, grid=(s // t,),
        in_specs=[_row_spec(t, d), _row_spec(t, d, 0), _row_spec(t, d, 1), _row_spec(t, d), _row_spec(t, d)],
        out_specs=[_row_spec(t, d), _row_spec(t, d), _row_spec(t, 2 * d, 0)],
        out_shape=[jax.ShapeDtypeStruct((s, d), BF16), jax.ShapeDtypeStruct((s, d), BF16),
                   jax.ShapeDtypeStruct((s, IN_COLS), BF16)],
        compiler_params=_cparams(("arbitrary",)), name=name)(dmerged, proj, proj, pa, pb)


def _pool_tile(s):
    return min(256, s)


def pool_fwd(proj, pw, ps, *, name):
    s = proj.shape[0]
    t = _pool_tile(s)
    pool_blk = (GATE_COLS + HEADS * HEAD_COLS) // (len(POOL_WINDOWS) * POOL_COLS)

    def body(p_ref, halo_ref, pw_ref, ps_ref, yb_ref, pooled_ref, mixed_ref):
        i = pl.program_id(0)
        halo = jnp.where(i == 0, 0.0, halo_ref[...])
        row = i * t + lax.broadcasted_iota(jnp.int32, (t, 1), 0)
        for g, w in enumerate(POOL_WINDOWS):
            vb = p_ref[:, g * POOL_COLS:g * POOL_COLS + POOL_GW]
            zb = p_ref[:, g * POOL_COLS + POOL_GW:(g + 1) * POOL_COLS]
            acc = jnp.concatenate([halo[:, g * POOL_COLS:g * POOL_COLS + POOL_GW], vb], axis=0)
            sh = 1
            while sh < w:
                acc = acc + pltpu.roll(acc, sh, axis=0)
                sh *= 2
            cnt = jnp.minimum(row + 1, w).astype(F32)
            pooled = acc[HALO:, :] / cnt - vb
            mixed = _dot(pooled.astype(BF16), pw_ref[g])
            cols = slice(g * POOL_GW, (g + 1) * POOL_GW)
            yb = mixed * ps_ref[:, cols] * (zb * _sigmoid(zb))
            yb_ref[:, cols] = yb.astype(yb_ref.dtype)
            pooled_ref[:, cols] = pooled.astype(pooled_ref.dtype)
            mixed_ref[:, cols] = mixed

    wide = len(POOL_WINDOWS) * POOL_COLS
    return pl.pallas_call(
        body, grid=(s // t,),
        in_specs=[pl.BlockSpec((t, wide), lambda i: (i, pool_blk)),
                  pl.BlockSpec((HALO, wide), lambda i: (jnp.maximum(i * (t // HALO) - 1, 0), pool_blk)),
                  pl.BlockSpec((len(POOL_WINDOWS), POOL_GW, POOL_GW), lambda i: (0, 0, 0)),
                  _vec_spec(WIDTH)],
        out_specs=[_row_spec(t, WIDTH)] * 3,
        out_shape=[jax.ShapeDtypeStruct((s, WIDTH), BF16), jax.ShapeDtypeStruct((s, WIDTH), BF16),
                   jax.ShapeDtypeStruct((s, WIDTH), F32)],
        compiler_params=_cparams(("arbitrary",)), name=name)(proj, proj, pw, ps)


def pool_bwd(dyb, proj, pooled, mixed, pw, ps, dproj, *, name):
    s = proj.shape[0]
    t = _pool_tile(s)
    nblk = s // t
    ng = len(POOL_WINDOWS)
    wide = ng * POOL_COLS
    pool_blk = (GATE_COLS + HEADS * HEAD_COLS) // wide

    def body(dy_ref, p_ref, pooled_ref, mixed_ref, pw_ref, ps_ref, dp_any, dp_ref, dpw_ref, dps_ref, carry):
        del dp_any
        i = pl.program_id(0)
        ii = nblk - 1 - i

        @pl.when(i == 0)
        def _():
            carry[...] = jnp.zeros_like(carry)
            dpw_ref[...] = jnp.zeros_like(dpw_ref)
            dps_ref[...] = jnp.zeros_like(dps_ref)

        row = ii * t + lax.broadcasted_iota(jnp.int32, (t, 1), 0)
        for g, w in enumerate(POOL_WINDOWS):
            cols = slice(g * POOL_GW, (g + 1) * POOL_GW)
            zb = p_ref[:, g * POOL_COLS + POOL_GW:(g + 1) * POOL_COLS]
            dy = dy_ref[:, cols]
            mx = mixed_ref[:, cols]
            sc = ps_ref[:, cols]
            sg = _sigmoid(zb)
            dzb = dy * (mx * sc) * (sg * (1.0 + zb * (1.0 - sg)))
            dpm = dy * (zb * sg)
            dps_ref[:, cols] += _colsum(dpm * mx)
            dmixed = (dpm * sc).astype(BF16)
            dpooled = _dot_nt(dmixed, pw_ref[g])
            dpw_ref[g] += _dot_tn(pooled_ref[:, cols], dmixed)
            cnt = jnp.minimum(row + 1, w).astype(F32)
            u = dpooled / cnt
            acc = jnp.concatenate([u, carry[:, cols]], axis=0)
            sh = 1
            while sh < w:
                acc = acc + pltpu.roll(acc, t + HALO - sh, axis=0)
                sh *= 2
            carry[:, cols] = u[:HALO, :]
            dp_ref[:, g * POOL_COLS:g * POOL_COLS + POOL_GW] = (acc[:t, :] - dpooled).astype(dp_ref.dtype)
            dp_ref[:, g * POOL_COLS + POOL_GW:(g + 1) * POOL_COLS] = dzb.astype(dp_ref.dtype)

    rev = lambda i: (nblk - 1 - i, 0)
    return pl.pallas_call(
        body, grid=(nblk,),
        in_specs=[pl.BlockSpec((t, WIDTH), rev),
                  pl.BlockSpec((t, wide), lambda i: (nblk - 1 - i, pool_blk)),
                  pl.BlockSpec((t, WIDTH), rev), pl.BlockSpec((t, WIDTH), rev),
                  pl.BlockSpec((ng, POOL_GW, POOL_GW), lambda i: (0, 0, 0)),
                  _vec_spec(WIDTH),
                  pl.BlockSpec(memory_space=pl.ANY)],
        out_specs=[pl.BlockSpec((t, wide), lambda i: (nblk - 1 - i, pool_blk)),
                   pl.BlockSpec((ng, POOL_GW, POOL_GW), lambda i: (0, 0, 0)),
                   _vec_spec(WIDTH)],
        out_shape=[jax.ShapeDtypeStruct(dproj.shape, dproj.dtype),
                   jax.ShapeDtypeStruct((ng, POOL_GW, POOL_GW), F32),
                   jax.ShapeDtypeStruct((1, WIDTH), F32)],
        scratch_shapes=[pltpu.VMEM((HALO, WIDTH), F32)],
        input_output_aliases={6: 0},
        compiler_params=_cparams(("arbitrary",)), name=name)(dyb, proj, pooled, mixed, pw, ps, dproj)


def _hgrn_tile(s):
    return min(256, s)


def _chunk_consts():
    tt = lax.broadcasted_iota(jnp.int32, (CHUNK, CHUNK), 0)
    ss = lax.broadcasted_iota(jnp.int32, (CHUNK, CHUNK), 1)
    within = (ss <= tt) & (ss // SUB == tt // SUB)
    before = ss < (tt // SUB) * SUB
    cums = jnp.concatenate([within.astype(F32), before.astype(F32)], axis=0).astype(BF16)
    causal = ss <= tt
    upper = (ss >= tt).astype(F32).astype(BF16)
    row = lax.broadcasted_iota(jnp.int32, (CHUNK, 1), 0)
    return cums, causal, upper, row


def _dot_split(mat01, v):
    hi = v.astype(BF16)
    r1 = v - hi.astype(F32)
    mid = r1.astype(BF16)
    lo = (r1 - mid.astype(F32)).astype(BF16)
    return _dot(mat01, hi) + _dot(mat01, mid) + _dot(mat01, lo)


def _hgrn_chunk(qa, fa, lb, cums, row):
    sq = _sigmoid(qa)
    q = qa * sq
    sa = _sigmoid(fa)
    sna = _sigmoid(-fa)
    oml = 1.0 - lb
    f = lb + oml * sa
    fc = jnp.maximum(f, MIN_FORGET)
    lf = jnp.log(fc)
    k = oml * sna
    cb = _dot_split(cums, lf)
    c = cb[:CHUNK]
    bt = cb[CHUNK:]
    ec = jnp.exp(c)
    enc = jnp.exp(jnp.minimum(-c, MAX_EXP))
    qt = q * ec
    kt = k * enc
    dms, lhs, rhs = [], [], []
    for j in range(N_SUB):
        bj = bt[j * SUB:j * SUB + 1, :]
        dm = jnp.where(row >= j * SUB, jnp.exp(jnp.minimum(bt - bj, 0.0)), 0.0)
        dms.append(dm)
        lhs.append(qt * dm)
        rhs.append(jnp.where(row // SUB == j, kt, 0.0))
    lhs = jnp.concatenate(lhs, axis=1).astype(BF16)
    rhs = jnp.concatenate(rhs, axis=1).astype(BF16)
    b = bt + c
    bl = b[CHUNK - 1:CHUNK, :]
    ebl = jnp.exp(bl)
    edec = jnp.exp(bl - b)
    eb = ec * dms[0]
    return dict(sq=sq, q=q, sa=sa, sna=sna, oml=oml, f=f, fc=fc, k=k, ec=ec, enc=enc, dms=dms,
                lhs=lhs, rhs=rhs, ebl=ebl, edec=edec, eb=eb, qd=q * eb, kdec=k * edec)


def _rider_mid_step(total):
    return total - min(32, total // 2)


def hgrn_fwd(proj, lb, hn, *, rider=None, name):
    s = proj.shape[0]
    t = _hgrn_tile(s)
    nblk = s // t
    ncht = t // CHUNK
    head_blk0 = GATE_COLS // HEAD_COLS
    nr = rider.n if rider else 0
    hbm = pl.BlockSpec(memory_space=pl.ANY)

    def body(*refs):
        p_ref, lb_ref, hn_ref = refs[:3]
        ya_ref, o_ref, st_ref = refs[3 + nr:6 + nr]
        state = refs[6 + 2 * nr]
        i = pl.program_id(1)
        if rider:
            total = HEADS * nblk
            rider.emit(pl.program_id(0) * nblk + i, total, _rider_mid_step(total),
                       refs[3:3 + nr], refs[6 + nr:6 + 2 * nr], refs[7 + 2 * nr:])

        @pl.when(i == 0)
        def _():
            state[...] = jnp.zeros_like(state)

        cums, causal, _, row = _chunk_consts()
        lbv = lb_ref[...]
        hnv = hn_ref[...]

        st = state[...]
        for ci in range(ncht):
            rows = slice(ci * CHUNK, (ci + 1) * CHUNK)
            qa = p_ref[rows, 0:HEAD_DIM]
            fa = p_ref[rows, HEAD_DIM:2 * HEAD_DIM]
            va = p_ref[rows, 2 * HEAD_DIM:3 * HEAD_DIM].astype(BF16)
            za = p_ref[rows, 3 * HEAD_DIM:4 * HEAD_DIM]
            pre = _hgrn_chunk(qa, fa, lbv, cums, row)
            stb = st.astype(BF16)
            st_ref[ci, 0] = stb
            a = jnp.where(causal, _dot_nt(pre["lhs"], pre["rhs"]), 0.0)
            o = _dot_nt(pre["qd"].astype(BF16), stb) + _dot(a.astype(BF16), va)
            st = st * pre["ebl"] + _dot_tn(va, pre["kdec"].astype(BF16))
            r = lax.rsqrt(_rowmean(o * o) + EPS)
            o_ref[rows, :] = o
            ya_ref[rows, :] = ((o * r) * hnv * (za * _sigmoid(za))).astype(ya_ref.dtype)
        state[...] = st

    return pl.pallas_call(
        body, grid=(HEADS, nblk),
        in_specs=[pl.BlockSpec((t, HEAD_COLS), lambda h, i: (i, head_blk0 + h)),
                  pl.BlockSpec((1, HEAD_DIM), lambda h, i: (0, h)),
                  pl.BlockSpec((1, HEAD_DIM), lambda h, i: (0, h))] + [hbm] * nr,
        out_specs=[pl.BlockSpec((t, HEAD_DIM), lambda h, i: (i, h)),
                   pl.BlockSpec((t, HEAD_DIM), lambda h, i: (i, h)),
                   pl.BlockSpec((ncht, 1, HEAD_DIM, HEAD_DIM), lambda h, i: (i, h, 0, 0))] + [hbm] * nr,
        out_shape=[jax.ShapeDtypeStruct((s, WIDTH), BF16), jax.ShapeDtypeStruct((s, WIDTH), F32),
                   jax.ShapeDtypeStruct((s // CHUNK, HEADS, HEAD_DIM, HEAD_DIM), BF16)]
        + (rider.out_shape if rider else []),
        scratch_shapes=[pltpu.VMEM((HEAD_DIM, HEAD_DIM), F32)] + (rider.scratch if rider else []),
        compiler_params=_cparams(("arbitrary", "arbitrary")), name=name)(proj, lb, hn, *(rider.arrs if rider else []))


def hgrn_bwd(dya, proj, o_all, states, lb, hn, dproj, *, rider=None, name):
    s = proj.shape[0]
    t = _hgrn_tile(s)
    nblk = s // t
    ncht = t // CHUNK
    head_blk0 = GATE_COLS // HEAD_COLS
    nr = rider.n if rider else 0
    hbm = pl.BlockSpec(memory_space=pl.ANY)

    def body(*refs):
        dy_ref, p_ref, o_ref, st_ref, lb_ref, hn_ref = refs[:6]
        dp_ref, dhn_ref, dlb_ref = refs[7 + nr:10 + nr]
        dstate = refs[10 + 2 * nr]
        i = pl.program_id(1)
        if rider:
            total = HEADS * nblk
            rider.emit(pl.program_id(0) * nblk + i, total, _rider_mid_step(total),
                       refs[7:7 + nr], refs[10 + nr:10 + 2 * nr], refs[11 + 2 * nr:])

        @pl.when(i == 0)
        def _():
            dstate[...] = jnp.zeros_like(dstate)
            dhn_ref[...] = jnp.zeros_like(dhn_ref)
            dlb_ref[...] = jnp.zeros_like(dlb_ref)

        cums, causal, upper, row = _chunk_consts()
        lbv = lb_ref[...]
        hnv = hn_ref[...]

        dst1 = dstate[...]
        dhn_acc = jnp.zeros_like(hnv)
        dlb_acc = jnp.zeros_like(lbv)
        for ci in reversed(range(ncht)):
            rows = slice(ci * CHUNK, (ci + 1) * CHUNK)
            qa = p_ref[rows, 0:HEAD_DIM]
            fa = p_ref[rows, HEAD_DIM:2 * HEAD_DIM]
            vb = p_ref[rows, 2 * HEAD_DIM:3 * HEAD_DIM].astype(BF16)
            za = p_ref[rows, 3 * HEAD_DIM:4 * HEAD_DIM]
            o = o_ref[rows, :]
            dy = dy_ref[rows, :]
            st0b = st_ref[ci, 0]
            r = lax.rsqrt(_rowmean(o * o) + EPS)
            on = o * r
            sgz = _sigmoid(za)
            sz = za * sgz
            dza = dy * on * hnv * (sgz * (1.0 + za * (1.0 - sgz)))
            dhn_acc = dhn_acc + _colsum(dy * on * sz)
            don = dy * hnv * sz
            do = r * (don - on * _rowmean(don * on))
            dob = do.astype(BF16)
            pre = _hgrn_chunk(qa, fa, lbv, cums, row)
            q, k = pre["q"], pre["k"]
            a = jnp.where(causal, _dot_nt(pre["lhs"], pre["rhs"]), 0.0)
            dst1b = dst1.astype(BF16)
            dq_inter = _dot(dob, st0b) * pre["eb"]
            da = jnp.where(causal, _dot_nt(dob, vb), 0.0).astype(BF16)
            dv = _dot_tn(a.astype(BF16), dob) + _dot_nt(pre["kdec"].astype(BF16), dst1b)
            dk_state = _dot(vb, dst1b) * pre["edec"]
            dlhs = _dot(da, pre["rhs"])
            drhs = _dot_tn(da, pre["lhs"])
            dq_a = jnp.zeros_like(q)
            dk_a = jnp.zeros_like(k)
            for j in range(N_SUB):
                dq_a = dq_a + pre["dms"][j] * dlhs[:, j * HEAD_DIM:(j + 1) * HEAD_DIM]
                dk_a = dk_a + jnp.where(row // SUB == j, drhs[:, j * HEAD_DIM:(j + 1) * HEAD_DIM], 0.0)
            dq = dq_inter + pre["ec"] * dq_a
            dk = dk_state + pre["enc"] * dk_a
            db = q * dq - k * dk
            dbl = _colsum(k * dk_state) + pre["ebl"] * _colsum(dst1 * st0b.astype(F32))
            dlf = _dot_split(upper, db) + dbl
            dst1 = dst1 * pre["ebl"] + _dot_tn(dob, pre["qd"].astype(BF16))
            sq, sa, sna, oml = pre["sq"], pre["sa"], pre["sna"], pre["oml"]
            dqa = dq * (sq * (1.0 + qa * (1.0 - sq)))
            dlf_f = jnp.where(pre["f"] >= MIN_FORGET, dlf / pre["fc"], 0.0)
            dfa = (dlf_f - dk) * (oml * sa * sna)
            dlb_acc = dlb_acc + _colsum((dlf_f - dk) * sna)
            dp_ref[rows, :] = jnp.concatenate([dqa, dfa, dv, dza], axis=1).astype(dp_ref.dtype)
        dstate[...] = dst1
        dhn_ref[...] += dhn_acc
        dlb_ref[...] += dlb_acc

    rev = lambda h, i: (nblk - 1 - i, h)
    return pl.pallas_call(
        body, grid=(HEADS, nblk),
        in_specs=[pl.BlockSpec((t, HEAD_DIM), rev),
                  pl.BlockSpec((t, HEAD_COLS), lambda h, i: (nblk - 1 - i, head_blk0 + h)),
                  pl.BlockSpec((t, HEAD_DIM), rev),
                  pl.BlockSpec((ncht, 1, HEAD_DIM, HEAD_DIM), lambda h, i: (nblk - 1 - i, h, 0, 0)),
                  pl.BlockSpec((1, HEAD_DIM), lambda h, i: (0, h)),
                  pl.BlockSpec((1, HEAD_DIM), lambda h, i: (0, h)),
                  hbm] + [hbm] * nr,
        out_specs=[pl.BlockSpec((t, HEAD_COLS), lambda h, i: (nblk - 1 - i, head_blk0 + h)),
                   pl.BlockSpec((1, HEAD_DIM), lambda h, i: (0, h)),
                   pl.BlockSpec((1, HEAD_DIM), lambda h, i: (0, h))] + [hbm] * nr,
        out_shape=[jax.ShapeDtypeStruct(dproj.shape, dproj.dtype),
                   jax.ShapeDtypeStruct((1, WIDTH), F32), jax.ShapeDtypeStruct((1, WIDTH), F32)]
        + (rider.out_shape if rider else []),
        scratch_shapes=[pltpu.VMEM((HEAD_DIM, HEAD_DIM), F32)] + (rider.scratch if rider else []),
        input_output_aliases={6: 0},
        compiler_params=_cparams(("arbitrary", "arbitrary")),
        name=name)(dya, proj, o_all, states, lb, hn, dproj, *(rider.arrs if rider else []))


def _softmax_rows(lower):
    mx = jnp.max(lower, axis=0, keepdims=True)
    e = jnp.exp(lower - mx)
    return e / jnp.sum(e, axis=0, keepdims=True)


def lb_table(lower, *, name):
    depth, w = lower.shape

    def body(l_ref, o_ref):
        sm = _softmax_rows(l_ref[...])
        acc = jnp.zeros((1, w), F32)
        o_ref[0:1, :] = acc
        for l in range(1, depth):
            acc = acc + sm[l:l + 1, :]
            o_ref[l:l + 1, :] = acc

    return pl.pallas_call(body, out_shape=jax.ShapeDtypeStruct((depth, w), F32), name=name)(lower)


def lb_table_bwd(lower, dlb, *, name):
    depth, w = lower.shape

    def body(l_ref, d_ref, o_ref):
        sm = _softmax_rows(l_ref[...])
        dlbv = d_ref[...]
        dsm = [jnp.zeros((1, w), F32)]
        for i in range(1, depth):
            acc = jnp.zeros((1, w), F32)
            for l in range(i, depth):
                acc = acc + dlbv[l:l + 1, :]
            dsm.append(acc)
        inner = jnp.zeros((1, w), F32)
        for i in range(depth):
            inner = inner + sm[i:i + 1, :] * dsm[i]
        for i in range(depth):
            o_ref[i:i + 1, :] = sm[i:i + 1, :] * (dsm[i] - inner)

    return pl.pallas_call(body, out_shape=jax.ShapeDtypeStruct((depth, w), F32), name=name)(lower, dlb)


def w_ada_grad(c_all, dmod_cols, *, name):
    depth, _, cols = dmod_cols.shape
    d = c_all.shape[1]

    def body(c_ref, dm_ref, o_ref):
        cv = c_ref[...]
        ca = cv * _sigmoid(cv)
        o_ref[...] = _dot_tn(ca, dm_ref[...])

    return pl.pallas_call(
        body, grid=(depth,),
        in_specs=[pl.BlockSpec((N_DEV, d), lambda l: (0, 0)), pl.BlockSpec((None, N_DEV, cols), lambda l: (l, 0, 0))],
        out_specs=pl.BlockSpec((None, d, cols), lambda l: (l, 0, 0)),
        out_shape=jax.ShapeDtypeStruct((depth, d, cols), F32),
        compiler_params=_cparams(("arbitrary",)), name=name)(c_all, dmod_cols)


def sum_parts(parts, *, name):
    p, r, c = parts.shape

    def body(p_ref, o_ref):
        acc = p_ref[0]
        for j in range(1, p):
            acc = acc + p_ref[j]
        o_ref[...] = acc

    return pl.pallas_call(body, out_shape=jax.ShapeDtypeStruct((r, c), F32), name=name)(parts)


def adamw(w, m, v, gparts, *, name):
    r, c = w.shape
    p = gparts.shape[0]
    tr = r
    while tr * c * 4 > (1 << 20) and tr % 16 == 0:
        tr //= 2

    def body(w_ref, m_ref, v_ref, g_ref, go_ref, d_ref, mo_ref, vo_ref):
        g = g_ref[0].astype(F32)
        for j in range(1, p):
            g = g + g_ref[j].astype(F32)
        mn = ADAM_B1 * m_ref[...] + (1.0 - ADAM_B1) * g
        vn = ADAM_B2 * v_ref[...] + (1.0 - ADAM_B2) * (g * g)
        m_hat = mn / (1.0 - ADAM_B1 ** ADAM_STEP)
        v_hat = vn / (1.0 - ADAM_B2 ** ADAM_STEP)
        go_ref[...] = g
        d_ref[...] = -ADAM_LR * (m_hat / (jnp.sqrt(v_hat) + ADAM_EPS) + ADAM_WD * w_ref[...])
        mo_ref[...] = mn
        vo_ref[...] = vn

    spec = pl.BlockSpec((tr, c), lambda i: (i, 0))
    shp = jax.ShapeDtypeStruct((r, c), F32)
    return pl.pallas_call(
        body, grid=(r // tr,),
        in_specs=[spec, spec, spec, pl.BlockSpec((p, tr, c), lambda i: (0, i, 0))],
        out_specs=[spec] * 4, out_shape=[shp] * 4,
        compiler_params=_cparams(("arbitrary",)), name=name)(w, m, v, gparts)


def _position():
    x, y, c = lax.axis_index("x"), lax.axis_index("y"), lax.axis_index("c")
    return x, y, c


def _dev_index(x, y, c):
    return 4 * x + 2 * y + c


def _gather_phases(ins, outs, send_sems, recv_sems, local_sems):
    n = len(ins)
    x, y, c = _position()
    me, sibling = (x, y, c), (x, y, 1 - c)
    chips = [(1 - x, y), (x, 1 - y), (1 - x, 1 - y)]

    def copy(a, k, block, to, own=False):
        slot = outs[a].at[_dev_index(*block)]
        return pltpu.make_async_remote_copy(
            src_ref=ins[a] if own else slot, dst_ref=slot,
            send_sem=send_sems.at[a * 7 + k], recv_sem=recv_sems.at[a * 7 + k],
            device_id=to, device_id_type=MESH)

    def mine(a):
        return pltpu.make_async_copy(ins[a], outs[a].at[_dev_index(*me)], local_sems.at[a])

    def first(a):
        return [copy(a, 0, me, sibling, True)] + [copy(a, 1 + j, me, (*chip, c), True) for j, chip in enumerate(chips)]

    def passed(a):
        return [copy(a, 4 + j, (*chip, c), sibling) for j, chip in enumerate(chips)]

    def start():
        for a in range(n):
            mine(a).start()
        for a in range(n):
            for cp in first(a):
                cp.start()

    def mid():
        for j, chip in enumerate(chips):
            for a in range(n):
                copy(a, 1 + j, (*chip, c), me).wait_recv()
                passed(a)[j].start()

    def finish():
        for a in range(n):
            copy(a, 0, sibling, me).wait_recv()
            for j, chip in enumerate(chips):
                copy(a, 4 + j, (*chip, 1 - c), me).wait_recv()
        for a in range(n):
            for cp in first(a) + passed(a):
                cp.wait_send()
            mine(a).wait()

    return start, mid, finish


def _scatter_phases(ins, outs, send_sems, recv_sems, local_sems):
    n = len(ins)
    x, y, c = _position()
    me = _dev_index(x, y, c)

    def peer(r):
        return (x ^ (r >> 2), y ^ ((r >> 1) & 1), c ^ (r & 1))

    def copy(a, r):
        to = peer(r)
        return pltpu.make_async_remote_copy(
            src_ref=ins[a].at[_dev_index(*to)], dst_ref=outs[a].at[me],
            send_sem=send_sems.at[a * 7 + r - 1], recv_sem=recv_sems.at[a * 7 + r - 1],
            device_id=to, device_id_type=MESH)

    def arrival(a, r):
        return pltpu.make_async_remote_copy(
            src_ref=ins[a].at[me], dst_ref=outs[a].at[_dev_index(*peer(r))],
            send_sem=send_sems.at[a * 7 + r - 1], recv_sem=recv_sems.at[a * 7 + r - 1],
            device_id=peer(r), device_id_type=MESH)

    def mine(a):
        return pltpu.make_async_copy(ins[a].at[me], outs[a].at[me], local_sems.at[a])

    def start():
        for a in range(n):
            mine(a).start()
        for r in range(1, N_DEV):
            for a in range(n):
                copy(a, r).start()

    def finish():
        for r in range(1, N_DEV):
            for a in range(n):
                arrival(a, r).wait_recv()
        for r in range(1, N_DEV):
            for a in range(n):
                copy(a, r).wait_send()
        for a in range(n):
            mine(a).wait()

    return start, None, finish


class Rider:
    def __init__(self, kind, arrs):
        self.kind, self.arrs, self.n = kind, list(arrs), len(arrs)
        lead = (N_DEV,) if kind == "gather" else ()
        self.out_shape = [jax.ShapeDtypeStruct(lead + a.shape, a.dtype) for a in self.arrs]
        self.scratch = [pltpu.SemaphoreType.DMA((7 * self.n,)), pltpu.SemaphoreType.DMA((7 * self.n,)),
                        pltpu.SemaphoreType.DMA((self.n,))]

    def phases(self, ins, outs, sems):
        make = _gather_phases if self.kind == "gather" else _scatter_phases
        return make(ins, outs, *sems)

    def emit(self, step, total, mid_step, ins, outs, sems):
        start, mid, finish = self.phases(ins, outs, sems)
        pl.when(step == 0)(start)
        if mid is not None:
            pl.when(step == mid_step)(mid)
        pl.when(step == total - 1)(finish)


def _standalone(rider, name):
    n = rider.n
    hbm = pl.BlockSpec(memory_space=pl.ANY)

    def body(*refs):
        start, mid, finish = rider.phases(refs[:n], refs[n:2 * n], refs[2 * n:])
        start()
        if mid is not None:
            mid()
        finish()

    return pl.pallas_call(body, out_shape=rider.out_shape, in_specs=[hbm] * n, out_specs=[hbm] * n,
                          scratch_shapes=rider.scratch, name=name)(*rider.arrs)


def all_gather(arrs, *, name):
    return _standalone(Rider("gather", arrs), name)


def scatter_parts(arrs, *, name):
    return _standalone(Rider("scatter", arrs), name)


def mod_exchange(c_all, w_ada, b_cols, *, name):
    depth, d, cols = w_ada.shape
    hbm = pl.BlockSpec(memory_space=pl.ANY)
    vmem = pl.BlockSpec(memory_space=pltpu.VMEM)

    def body(c_ref, w_ref, b_ref, out_ref, wbuf, sendbuf, send_sems, recv_sems, load_sem):
        x, y, c = _position()
        me = _dev_index(x, y, c)
        cv = c_ref[...]
        ca = cv * _sigmoid(cv)
        for l in range(depth):
            load = pltpu.make_async_copy(w_ref.at[l], wbuf, load_sem)
            load.start()
            load.wait()
            part = jnp.dot(ca, wbuf[...], preferred_element_type=F32,
                           precision=lax.Precision.HIGHEST) + b_ref[l:l + 1, :]
            for bi in range(N_DEV):
                sendbuf[bi, l:l + 1, :] = part[bi:bi + 1, :]

        def peer(r):
            return (x ^ (r >> 2), y ^ ((r >> 1) & 1), c ^ (r & 1))

        def copy(r):
            to = peer(r)
            return pltpu.make_async_remote_copy(
                src_ref=sendbuf.at[_dev_index(*to)], dst_ref=out_ref.at[me],
                send_sem=send_sems.at[r - 1], recv_sem=recv_sems.at[r - 1],
                device_id=to, device_id_type=MESH)

        def arrival(r):
            return pltpu.make_async_remote_copy(
                src_ref=sendbuf.at[me], dst_ref=out_ref.at[_dev_index(*peer(r))],
                send_sem=send_sems.at[r - 1], recv_sem=recv_sems.at[r - 1],
                device_id=peer(r), device_id_type=MESH)

        out_ref[me] = sendbuf[me]
        sends = [copy(r) for r in range(1, N_DEV)]
        for cp in sends:
            cp.start()
        for r in range(1, N_DEV):
            arrival(r).wait_recv()
        for cp in sends:
            cp.wait_send()

    return pl.pallas_call(
        body,
        out_shape=jax.ShapeDtypeStruct((N_DEV, depth, cols), F32),
        in_specs=[vmem, hbm, vmem], out_specs=vmem,
        scratch_shapes=[pltpu.VMEM((d, cols), F32), pltpu.VMEM((N_DEV, depth, cols), F32),
                        pltpu.SemaphoreType.DMA((7,)), pltpu.SemaphoreType.DMA((7,)), pltpu.SemaphoreType.DMA],
        compiler_params=pltpu.CompilerParams(vmem_limit_bytes=VMEM_LIMIT),
        name=name)(c_all, w_ada, b_cols)


def kernel(x, c, w_ada, b_ada, norm_pre, norm_post, w_in, lower_bounds, hgrn_norm, pool_w, pool_scale, w_proj_a, w_proj_b, w_out, loss_target, m_w_ada, m_b_ada, m_norm_pre, m_norm_post, m_w_in, m_lower_bounds, m_hgrn_norm, m_pool_w, m_pool_scale, m_w_proj_a, m_w_proj_b, m_w_out, v_w_ada, v_b_ada, v_norm_pre, v_norm_post, v_w_in, v_lower_bounds, v_hgrn_norm, v_pool_w, v_pool_scale, v_w_proj_a, v_w_proj_b, v_w_out):
    depth = w_in.shape[0]
    d = D_MODEL
    ada_cols = w_ada.shape[2]
    xi, yi, ci = _position()
    me = _dev_index(xi, yi, ci)
    xs = x[0]
    target = loss_target[0]
    ng = len(POOL_WINDOWS)

    def shards(l):
        return [w_in[l].astype(BF16), w_proj_a[l].astype(BF16), w_proj_b[l].astype(BF16),
                w_out[l].astype(BF16), pool_w[l].astype(BF16)]

    def full_weights(g_in, g_pa, g_pb, g_out, g_pool):
        return dict(
            w_in=permute_w_in(g_in, name="permute_w_in"),
            pa=jnp.transpose(g_pa, (1, 0, 2)).reshape(WIDTH, d),
            pb=jnp.transpose(g_pb, (1, 0, 2)).reshape(WIDTH, d),
            w_out=g_out.reshape(d, d),
            pool=jnp.transpose(g_pool, (1, 0, 2, 3)).reshape(ng, POOL_GW, POOL_GW))

    gathered = [full_weights(*all_gather(shards(0), name="gather_weights"))]

    (c_all,) = all_gather([c], name="gather_c")
    c_all = c_all.reshape(N_DEV, d)
    b_cols = lax.dynamic_slice_in_dim(b_ada, me * ada_cols, ada_cols, axis=1)
    mod_parts = mod_exchange(c_all, w_ada, b_cols, name="mod_exchange")
    mod = jnp.transpose(mod_parts, (1, 0, 2)).reshape(depth, 3 * d)
    lb_all = lb_table(lower_bounds, name="lb_table")

    saved = []
    cur = xs
    for l in range(depth):
        w = gathered[l]
        shift, scale, gate = mod[l:l + 1, :d], mod[l:l + 1, d:2 * d], mod[l:l + 1, 2 * d:]
        h = prenorm_fwd(cur, norm_pre[l:l + 1], shift, scale, name="prenorm_fwd")
        proj = matmul_nn(h, w["w_in"], tm=1024, tn=1024, out_dtype=F32, name="mm_w_in")
        if l + 1 < depth:
            y_a, o_all, states, *nxt_w = hgrn_fwd(proj, lb_all[l:l + 1], hgrn_norm[l:l + 1],
                                                   rider=Rider("gather", shards(l + 1)), name="hgrn_fwd_gather")
            gathered.append(full_weights(*nxt_w))
        else:
            y_a, o_all, states = hgrn_fwd(proj, lb_all[l:l + 1], hgrn_norm[l:l + 1], name="hgrn_fwd")
        y_b, pooled, mixed = pool_fwd(proj, w["pool"], pool_scale[l:l + 1], name="pool_fwd")
        pa = matmul_nn(y_a, w["pa"], tm=1024, tn=2048, out_dtype=F32, name="mm_proj_a")
        pb = matmul_nn(y_b, w["pb"], tm=1024, tn=2048, out_dtype=F32, name="mm_proj_b")
        merged = gate_fwd(proj, pa, pb, name="gate_fwd")
        out = matmul_nn(merged, w["w_out"], tm=1024, tn=1024, out_dtype=F32, name="mm_w_out")
        nxt = postnorm_fwd(cur, out, norm_post[l:l + 1], gate, name="postnorm_fwd")
        saved.append(dict(x=cur, h=h, proj=proj, y_a=y_a, o=o_all, states=states, y_b=y_b, pooled=pooled,
                          mixed=mixed, pa=pa, pb=pb, merged=merged, out=out, scale=scale, gate=gate))
        cur = nxt

    g, loss_part = loss_head(cur, target, name="loss_head")
    loss = lax.psum(loss_part[0, 0], ("x", "y", "c"))

    small = [None] * depth
    big = [None] * depth
    pending = None
    for l in reversed(range(depth)):
        w, sv = gathered[l], saved[l]
        dout, dgate, dnpost = postnorm_bwd(g, sv["out"], norm_post[l:l + 1], sv["gate"], name="postnorm_bwd")
        dmerged = matmul_nt(dout, w["w_out"], tm=512, tn=2048, out_dtype=F32, name="mm_w_out_dx")
        dw_out = matmul_tn(sv["merged"], dout, tm=2048, tn=1024, out_dtype=BF16, name="mm_w_out_dw")
        dpa, dpb, dproj = gate_bwd(dmerged, sv["proj"], sv["pa"], sv["pb"], name="gate_bwd")
        dya = matmul_nt(dpa, w["pa"], tm=1024, tn=2048, out_dtype=F32, name="mm_proj_a_dx")
        dyb = matmul_nt(dpb, w["pb"], tm=1024, tn=2048, out_dtype=F32, name="mm_proj_b_dx")
        dw_pa = matmul_tn(sv["y_a"], dpa, tm=2048, tn=2048, out_dtype=BF16, name="mm_proj_a_dw")
        dw_pb = matmul_tn(sv["y_b"], dpb, tm=2048, tn=2048, out_dtype=BF16, name="mm_proj_b_dw")
        dproj, dpool_w, dpool_scale = pool_bwd(dyb, sv["proj"], sv["pooled"], sv["mixed"], w["pool"],
                                               pool_scale[l:l + 1], dproj, name="pool_bwd")
        if pending is None:
            dproj, dhn, dlb = hgrn_bwd(dya, sv["proj"], sv["o"], sv["states"], lb_all[l:l + 1],
                                       hgrn_norm[l:l + 1], dproj, name="hgrn_bwd")
        else:
            dproj, dhn, dlb, *recv = hgrn_bwd(dya, sv["proj"], sv["o"], sv["states"], lb_all[l:l + 1],
                                              hgrn_norm[l:l + 1], dproj, rider=Rider("scatter", pending),
                                              name="hgrn_bwd_scatter")
            big[l + 1] = recv
        dh = matmul_nt(dproj, w["w_in"], tm=512, tn=2048, out_dtype=F32, name="mm_w_in_dx")
        dw_in = matmul_tn(sv["h"], dproj, tm=2048, tn=1024, out_dtype=BF16, name="mm_w_in_dw")
        g, dshift, dscale, dnpre = prenorm_bwd(dh, sv["x"], norm_pre[l:l + 1], sv["scale"], g, name="prenorm_bwd")
        small[l] = jnp.concatenate([dshift, dscale, dgate, dnpre, dnpost, dlb, dhn, dpool_scale], axis=1)
        by_owner = lambda t: jnp.transpose(t.reshape(WIDTH, N_DEV, d // N_DEV), (1, 0, 2))
        pending = [unpermute_w_in(dw_in, name="unpermute_w_in"), by_owner(dw_pa), by_owner(dw_pb),
                   dw_out.reshape(N_DEV, d // N_DEV, d),
                   jnp.transpose(dpool_w.astype(BF16).reshape(ng, N_DEV, POOL_GW // N_DEV, POOL_GW), (1, 0, 2, 3))]
    big[0] = scatter_parts(pending, name="scatter_grads")
    grad_x = g[None]

    small_mine = jnp.concatenate(small, axis=0)
    (small_all,) = all_gather([small_mine], name="gather_small")
    small_sum = sum_parts(small_all, name="sum_small")
    dmod_all = small_all[:, :, :3 * d]
    dmod_cols = jnp.transpose(lax.dynamic_slice_in_dim(dmod_all, me * ada_cols, ada_cols, axis=2), (1, 0, 2))
    g_w_ada = w_ada_grad(c_all, dmod_cols, name="w_ada_grad")
    off = 3 * d
    g_b_ada = small_sum[:, :off]
    g_npre = small_sum[:, off:off + d]
    g_npost = small_sum[:, off + d:off + 2 * d]
    g_lb_tab = small_sum[:, off + 2 * d:off + 2 * d + WIDTH]
    g_hn = small_sum[:, off + 2 * d + WIDTH:off + 2 * d + 2 * WIDTH]
    g_ps = small_sum[:, off + 2 * d + 2 * WIDTH:]
    g_lower = lb_table_bwd(lower_bounds, g_lb_tab, name="lb_table_bwd")

    def update(wt, mt, vt, gparts, shape2, name):
        outs = adamw(wt.reshape(shape2), mt.reshape(shape2), vt.reshape(shape2), gparts, name=name)
        return [o.reshape(wt.shape) for o in outs]

    def update_layers(wt, mt, vt, kind, name):
        shape2 = (-1, wt.shape[-1])
        per = []
        for l in range(depth):
            gp = big[l][kind]
            gp = gp.reshape((N_DEV,) + wt[l].reshape(shape2).shape)
            per.append(update(wt[l], mt[l], vt[l], gp, shape2, name))
        return [jnp.stack([p[k] for p in per], axis=0) for k in range(4)]

    def update_small(wt, mt, vt, gt, name):
        shape2 = (-1, wt.shape[-1])
        return update(wt, mt, vt, gt.reshape(shape2)[None], shape2, name)

    res = {
        "w_ada": update_small(w_ada, m_w_ada, v_w_ada, g_w_ada, "adamw_w_ada"),
        "b_ada": update_small(b_ada, m_b_ada, v_b_ada, g_b_ada, "adamw_b_ada"),
        "norm_pre": update_small(norm_pre, m_norm_pre, v_norm_pre, g_npre, "adamw_norm_pre"),
        "norm_post": update_small(norm_post, m_norm_post, v_norm_post, g_npost, "adamw_norm_post"),
        "w_in": update_layers(w_in, m_w_in, v_w_in, 0, "adamw_w_in"),
        "lower_bounds": update_small(lower_bounds, m_lower_bounds, v_lower_bounds, g_lower, "adamw_lower_bounds"),
        "hgrn_norm": update_small(hgrn_norm, m_hgrn_norm, v_hgrn_norm, g_hn, "adamw_hgrn_norm"),
        "pool_w": update_layers(pool_w, m_pool_w, v_pool_w, 4, "adamw_pool_w"),
        "pool_scale": update_small(pool_scale, m_pool_scale, v_pool_scale, g_ps, "adamw_pool_scale"),
        "w_proj_a": update_layers(w_proj_a, m_w_proj_a, v_w_proj_a, 1, "adamw_w_proj_a"),
        "w_proj_b": update_layers(w_proj_b, m_w_proj_b, v_w_proj_b, 2, "adamw_w_proj_b"),
        "w_out": update_layers(w_out, m_w_out, v_w_out, 3, "adamw_w_out"),
    }
    order = ["w_ada", "b_ada", "norm_pre", "norm_post", "w_in", "lower_bounds", "hgrn_norm", "pool_w",
             "pool_scale", "w_proj_a", "w_proj_b", "w_out"]
    outs = [loss, grad_x]
    for k in range(4):
        outs += [res[nm][k] for nm in order]
    return tuple(outs)
```

```python
import functools

import jax
import jax.numpy as jnp
from jax import lax
from jax.experimental import pallas as pl
from jax.experimental.pallas import tpu as pltpu

F32 = jnp.float32
BF16 = jnp.bfloat16
MESH = pl.DeviceIdType.MESH

N_DEV = 8
EPS = 1e-6
MIN_FORGET = 1e-30
D_MODEL = 2048
HEADS = 8
HEAD_DIM = 128
CHUNK = 64
SUB = 16
N_SUB = CHUNK // SUB
WIDTH = 1024
POOL_WINDOWS = (2, 4, 8, 16)
POOL_GW = 256
HALO = 16
IN_COLS = 10240
LANE = 128
N_COLBLK = IN_COLS // LANE
GATE_COLS = 4096
HEAD_COLS = 4 * HEAD_DIM
POOL_COLS = 2 * POOL_GW
MAX_EXP = 80.0

ADAM_LR = 0.001
ADAM_B1 = 0.9
ADAM_B2 = 0.999
ADAM_EPS = 1e-08
ADAM_WD = 0.01
ADAM_STEP = 10

VMEM_LIMIT = 56 * 1024 * 1024


def _cparams(sem=None):
    return pltpu.CompilerParams(dimension_semantics=sem, vmem_limit_bytes=VMEM_LIMIT)


def _sigmoid(v):
    return 1.0 / (1.0 + jnp.exp(-v))


def _dot(a, b):
    return jnp.dot(a, b, preferred_element_type=F32)


def _dot_nt(a, b):
    return lax.dot_general(a, b, (((1,), (1,)), ((), ())), preferred_element_type=F32)


def _dot_tn(a, b):
    return lax.dot_general(a, b, (((0,), (0,)), ((), ())), preferred_element_type=F32)


def _colsum(v):
    return jnp.sum(v, axis=0, keepdims=True)


def _rowmean(v):
    return jnp.mean(v, axis=-1, keepdims=True)


def _orig_block(n):
    m1 = n - 32
    head = 8 * (m1 % 4) + m1 // 4
    m2 = n - 64
    t2 = m2 % 4
    pool = 32 + 2 * (m2 // 4) + (t2 % 2) + 8 * (t2 // 2)
    return jnp.where(n < 32, n + 48, jnp.where(n < 64, head, pool))


def _accumulate(step, steps, prod, o_ref, acc_ref):
    if steps == 1:
        o_ref[...] = prod.astype(o_ref.dtype)
        return

    @pl.when(step == 0)
    def _():
        acc_ref[...] = prod

    @pl.when(step > 0)
    def _():
        acc_ref[...] += prod

    @pl.when(step == steps - 1)
    def _():
        o_ref[...] = acc_ref[...].astype(o_ref.dtype)


def _matmul_call(dot, a, b, *, grid, in_specs, out_spec, out_shape, acc_shape, steps, rider, name):
    nr = rider.n if rider else 0
    hbm = pl.BlockSpec(memory_space=pl.ANY)
    has_acc = steps > 1

    def body(*refs):
        a_ref, b_ref = refs[:2]
        o_ref = refs[2 + nr]
        scratch = refs[3 + 2 * nr:]
        if rider:
            total = grid[0] * grid[1]
            rider.emit(pl.program_id(0) * grid[1] + pl.program_id(1), total, _rider_mid_step(total),
                       refs[2:2 + nr], refs[3 + nr:3 + 2 * nr], scratch[1 if has_acc else 0:])
        _accumulate(pl.program_id(1), steps, dot(a_ref[...], b_ref[...]), o_ref, scratch[0] if has_acc else None)

    outs = pl.pallas_call(
        body, grid=grid,
        in_specs=in_specs + [hbm] * nr, out_specs=[out_spec] + [hbm] * nr,
        out_shape=[out_shape] + (rider.out_shape if rider else []),
        scratch_shapes=([pltpu.VMEM(acc_shape, F32)] if has_acc else []) + (rider.scratch if rider else []),
        compiler_params=_cparams(("arbitrary", "arbitrary")),
        name=name)(a, b, *(rider.arrs if rider else []))
    return outs if rider else outs[0]


def matmul_nn(a, b, *, tm, tn, out_dtype, name, rider=None):
    m, k = a.shape
    n = b.shape[1]
    tm = min(tm, m)
    return _matmul_call(
        _dot, a, b, grid=(n // tn, m // tm),
        in_specs=[pl.BlockSpec((tm, k), lambda j, i: (i, 0)), pl.BlockSpec((k, tn), lambda j, i: (0, j))],
        out_spec=pl.BlockSpec((tm, tn), lambda j, i: (i, j)),
        out_shape=jax.ShapeDtypeStruct((m, n), out_dtype), acc_shape=None, steps=1, rider=rider, name=name)


def matmul_nt(a, b, *, tm, tn, out_dtype, name, rider=None):
    m, n = a.shape
    k = b.shape[0]
    tm = min(tm, m)
    return _matmul_call(
        _dot_nt, a, b, grid=(m // tm, n // tn),
        in_specs=[pl.BlockSpec((tm, tn), lambda i, j: (i, j)), pl.BlockSpec((k, tn), lambda i, j: (0, j))],
        out_spec=pl.BlockSpec((tm, k), lambda i, j: (i, 0)),
        out_shape=jax.ShapeDtypeStruct((m, k), out_dtype), acc_shape=(tm, k), steps=n // tn, rider=rider, name=name)


def matmul_tn(a, b, *, tm, tn, out_dtype, name, rider=None):
    m, k = a.shape
    n = b.shape[1]
    tm = min(tm, m)
    return _matmul_call(
        _dot_tn, a, b, grid=(n // tn, m // tm),
        in_specs=[pl.BlockSpec((tm, k), lambda j, i: (i, 0)), pl.BlockSpec((tm, tn), lambda j, i: (i, j))],
        out_spec=pl.BlockSpec((k, tn), lambda j, i: (0, j)),
        out_shape=jax.ShapeDtypeStruct((k, n), out_dtype), acc_shape=(k, tn), steps=m // tm, rider=rider, name=name)


def permute_w_in(staged, *, name):
    k = staged.shape[1]

    def body(i_ref, o_ref):
        o_ref[...] = i_ref[...]

    def src(nb):
        ob = _orig_block(nb)
        return (ob // 10, 0, ob % 10)

    return pl.pallas_call(
        body, grid=(N_COLBLK,),
        in_specs=[pl.BlockSpec((None, k, LANE), src)],
        out_specs=pl.BlockSpec((k, LANE), lambda nb: (0, nb)),
        out_shape=jax.ShapeDtypeStruct((k, IN_COLS), staged.dtype),
        compiler_params=_cparams(("arbitrary",)), name=name)(staged)


def unpermute_w_in(dw, *, name):
    k = dw.shape[0]

    def body(i_ref, o_ref):
        o_ref[...] = i_ref[...]

    def dst(nb):
        ob = _orig_block(nb)
        return (ob // 10, 0, ob % 10)

    return pl.pallas_call(
        body, grid=(N_COLBLK,),
        in_specs=[pl.BlockSpec((k, LANE), lambda nb: (0, nb))],
        out_specs=pl.BlockSpec((None, k, LANE), dst),
        out_shape=jax.ShapeDtypeStruct((N_DEV, k, IN_COLS // N_DEV), dw.dtype),
        compiler_params=_cparams(("arbitrary",)), name=name)(dw)


def _row_tile(s):
    return min(256, s)


def _row_spec(t, w, col=0):
    return pl.BlockSpec((t, w), lambda i: (i, col))


def _vec_spec(w):
    return pl.BlockSpec((1, w), lambda i: (0, 0))


def prenorm_fwd(x, gain, shift, scale, *, name):
    s, d = x.shape
    t = _row_tile(s)

    def body(x_ref, g_ref, sh_ref, sc_ref, h_ref):
        xv = x_ref[...]
        r = lax.rsqrt(_rowmean(xv * xv) + EPS)
        h_ref[...] = ((xv * r) * g_ref[...] * (1.0 + sc_ref[...]) + sh_ref[...]).astype(h_ref.dtype)

    return pl.pallas_call(
        body, grid=(s // t,),
        in_specs=[_row_spec(t, d), _vec_spec(d), _vec_spec(d), _vec_spec(d)],
        out_specs=_row_spec(t, d), out_shape=jax.ShapeDtypeStruct((s, d), BF16),
        compiler_params=_cparams(("arbitrary",)), name=name)(x, gain, shift, scale)


def prenorm_bwd(dh, x, gain, scale, g_res, *, name):
    s, d = x.shape
    t = _row_tile(s)

    def body(dh_ref, x_ref, g_ref, sc_ref, gr_ref, dx_ref, dsh_ref, dsc_ref, dg_ref):
        i = pl.program_id(0)
        xv = x_ref[...]
        dhv = dh_ref[...]
        r = lax.rsqrt(_rowmean(xv * xv) + EPS)
        xn = xv * r
        gain_v = g_ref[...]
        one_sc = 1.0 + sc_ref[...]
        dyn = dhv * one_sc
        dxn = dyn * gain_v
        dx_ref[...] = r * (dxn - xn * _rowmean(dxn * xn)) + gr_ref[...]
        p_sh = _colsum(dhv)
        p_sc = _colsum(dhv * (xn * gain_v))
        p_g = _colsum(dyn * xn)

        @pl.when(i == 0)
        def _():
            dsh_ref[...] = p_sh
            dsc_ref[...] = p_sc
            dg_ref[...] = p_g

        @pl.when(i > 0)
        def _():
            dsh_ref[...] += p_sh
            dsc_ref[...] += p_sc
            dg_ref[...] += p_g

    vec = jax.ShapeDtypeStruct((1, d), F32)
    return pl.pallas_call(
        body, grid=(s // t,),
        in_specs=[_row_spec(t, d), _row_spec(t, d), _vec_spec(d), _vec_spec(d), _row_spec(t, d)],
        out_specs=[_row_spec(t, d), _vec_spec(d), _vec_spec(d), _vec_spec(d)],
        out_shape=[jax.ShapeDtypeStruct((s, d), F32), vec, vec, vec],
        compiler_params=_cparams(("arbitrary",)), name=name)(dh, x, gain, scale, g_res)


def postnorm_fwd(x, out, gain, gate, *, name):
    s, d = x.shape
    t = _row_tile(s)

    def body(x_ref, o_ref, g_ref, gt_ref, y_ref):
        ov = o_ref[...]
        r = lax.rsqrt(_rowmean(ov * ov) + EPS)
        y_ref[...] = x_ref[...] + gt_ref[...] * ((ov * r) * g_ref[...])

    return pl.pallas_call(
        body, grid=(s // t,),
        in_specs=[_row_spec(t, d), _row_spec(t, d), _vec_spec(d), _vec_spec(d)],
        out_specs=_row_spec(t, d), out_shape=jax.ShapeDtypeStruct((s, d), F32),
        compiler_params=_cparams(("arbitrary",)), name=name)(x, out, gain, gate)


def postnorm_bwd(g, out, gain, gate, *, name):
    s, d = out.shape
    t = _row_tile(s)

    def body(g_ref, o_ref, gn_ref, gt_ref, do_ref, dgt_ref, dgn_ref):
        i = pl.program_id(0)
        ov = o_ref[...]
        gv = g_ref[...]
        r = lax.rsqrt(_rowmean(ov * ov) + EPS)
        on = ov * r
        gain_v = gn_ref[...]
        gate_v = gt_ref[...]
        dn = gv * gate_v
        don = dn * gain_v
        do_ref[...] = (r * (don - on * _rowmean(don * on))).astype(do_ref.dtype)
        p_gt = _colsum(gv * (on * gain_v))
        p_gn = _colsum(dn * on)

        @pl.when(i == 0)
        def _():
            dgt_ref[...] = p_gt
            dgn_ref[...] = p_gn

        @pl.when(i > 0)
        def _():
            dgt_ref[...] += p_gt
            dgn_ref[...] += p_gn

    vec = jax.ShapeDtypeStruct((1, d), F32)
    return pl.pallas_call(
        body, grid=(s // t,),
        in_specs=[_row_spec(t, d), _row_spec(t, d), _vec_spec(d), _vec_spec(d)],
        out_specs=[_row_spec(t, d), _vec_spec(d), _vec_spec(d)],
        out_shape=[jax.ShapeDtypeStruct((s, d), BF16), vec, vec],
        compiler_params=_cparams(("arbitrary",)), name=name)(g, out, gain, gate)


def loss_head(y, target, *, name):
    s, d = y.shape
    t = _row_tile(s)
    steps = s // t

    def body(y_ref, t_ref, dy_ref, loss_ref, acc_ref):
        i = pl.program_id(0)
        err = y_ref[...] - t_ref[...]
        dy_ref[...] = err * (1.0 / d)
        part = _colsum(err * err)

        @pl.when(i == 0)
        def _():
            acc_ref[...] = part

        @pl.when(i > 0)
        def _():
            acc_ref[...] += part

        @pl.when(i == steps - 1)
        def _():
            loss_ref[...] = jnp.sum(acc_ref[...], axis=1, keepdims=True) * (0.5 / d)

    return pl.pallas_call(
        body, grid=(steps,),
        in_specs=[_row_spec(t, d), _row_spec(t, d)],
        out_specs=[_row_spec(t, d), pl.BlockSpec((1, 1), lambda i: (0, 0))],
        out_shape=[jax.ShapeDtypeStruct((s, d), F32), jax.ShapeDtypeStruct((1, 1), F32)],
        scratch_shapes=[pltpu.VMEM((1, d), F32)],
        compiler_params=_cparams(("arbitrary",)), name=name)(y, target)


def gate_fwd(proj, pa, pb, *, name):
    s, d = pa.shape
    t = _row_tile(s)

    def body(ga_ref, gb_ref, pa_ref, pb_ref, m_ref):
        m_ref[...] = (_sigmoid(ga_ref[...]) * pa_ref[...] + _sigmoid(gb_ref[...]) * pb_ref[...]).astype(m_ref.dtype)

    return pl.pallas_call(
        body, grid=(s // t,),
        in_specs=[_row_spec(t, d, 0), _row_spec(t, d, 1), _row_spec(t, d), _row_spec(t, d)],
        out_specs=_row_spec(t, d), out_shape=jax.ShapeDtypeStruct((s, d), BF16),
        compiler_params=_cparams(("arbitrary",)), name=name)(proj, proj, pa, pb)


def gate_bwd(dmerged, proj, pa, pb, *, name):
    s, d = pa.shape
    t = _row_tile(s)

    def body(dm_ref, ga_ref, gb_ref, pa_ref, pb_ref, dpa_ref, dpb_ref, dp_ref):
        dm = dm_ref[...]
        sa = _sigmoid(ga_ref[...])
        sb = _sigmoid(gb_ref[...])
        dpa_ref[...] = (dm * sa).astype(dpa_ref.dtype)
        dpb_ref[...] = (dm * sb).astype(dpb_ref.dtype)
        dp_ref[:, :d] = (dm * pa_ref[...] * sa * (1.0 - sa)).astype(dp_ref.dtype)
        dp_ref[:, d:] = (dm * pb_ref[...] * sb * (1.0 - sb)).astype(dp_ref.dtype)

    return pl.pallas_call(
        body, grid=(s // t,),
        in_specs=[_row_spec(t, d), _row_spec(t, d, 0), _row_spec(t, d, 1), _row_spec(t, d), _row_spec(t, d)],
        out_specs=[_row_spec(t, d), _row_spec(t, d), _row_spec(t, 2 * d, 0)],
        out_shape=[jax.ShapeDtypeStruct((s, d), BF16), jax.ShapeDtypeStruct((s, d), BF16),
                   jax.ShapeDtypeStruct((s, IN_COLS), BF16)],
        compiler_params=_cparams(("arbitrary",)), name=name)(dmerged, proj, proj, pa, pb)


def _pool_tile(s):
    return min(256, s)


def pool_fwd(proj, pw, ps, *, name):
    s = proj.shape[0]
    t = _pool_tile(s)
    pool_blk = (GATE_COLS + HEADS * HEAD_COLS) // (len(POOL_WINDOWS) * POOL_COLS)

    def body(p_ref, halo_ref, pw_ref, ps_ref, yb_ref, pooled_ref, mixed_ref):
        i = pl.program_id(0)
        halo = jnp.where(i == 0, 0.0, halo_ref[...])
        row = i * t + lax.broadcasted_iota(jnp.int32, (t, 1), 0)
        for g, w in enumerate(POOL_WINDOWS):
            vb = p_ref[:, g * POOL_COLS:g * POOL_COLS + POOL_GW]
            zb = p_ref[:, g * POOL_COLS + POOL_GW:(g + 1) * POOL_COLS]
            acc = jnp.concatenate([halo[:, g * POOL_COLS:g * POOL_COLS + POOL_GW], vb], axis=0)
            sh = 1
            while sh < w:
                acc = acc + pltpu.roll(acc, sh, axis=0)
                sh *= 2
            cnt = jnp.minimum(row + 1, w).astype(F32)
            pooled = acc[HALO:, :] / cnt - vb
            mixed = _dot(pooled.astype(BF16), pw_ref[g])
            cols = slice(g * POOL_GW, (g + 1) * POOL_GW)
            yb = mixed * ps_ref[:, cols] * (zb * _sigmoid(zb))
            yb_ref[:, cols] = yb.astype(yb_ref.dtype)
            pooled_ref[:, cols] = pooled.astype(pooled_ref.dtype)
            mixed_ref[:, cols] = mixed

    wide = len(POOL_WINDOWS) * POOL_COLS
    return pl.pallas_call(
        body, grid=(s // t,),
        in_specs=[pl.BlockSpec((t, wide), lambda i: (i, pool_blk)),
                  pl.BlockSpec((HALO, wide), lambda i: (jnp.maximum(i * (t // HALO) - 1, 0), pool_blk)),
                  pl.BlockSpec((len(POOL_WINDOWS), POOL_GW, POOL_GW), lambda i: (0, 0, 0)),
                  _vec_spec(WIDTH)],
        out_specs=[_row_spec(t, WIDTH)] * 3,
        out_shape=[jax.ShapeDtypeStruct((s, WIDTH), BF16), jax.ShapeDtypeStruct((s, WIDTH), BF16),
                   jax.ShapeDtypeStruct((s, WIDTH), F32)],
        compiler_params=_cparams(("arbitrary",)), name=name)(proj, proj, pw, ps)


def pool_bwd(dyb, proj, pooled, mixed, pw, ps, dproj, *, name):
    s = proj.shape[0]
    t = _pool_tile(s)
    nblk = s // t
    ng = len(POOL_WINDOWS)
    wide = ng * POOL_COLS
    pool_blk = (GATE_COLS + HEADS * HEAD_COLS) // wide

    def body(dy_ref, p_ref, pooled_ref, mixed_ref, pw_ref, ps_ref, dp_any, dp_ref, dpw_ref, dps_ref, carry):
        del dp_any
        i = pl.program_id(0)
        ii = nblk - 1 - i

        @pl.when(i == 0)
        def _():
            carry[...] = jnp.zeros_like(carry)
            dpw_ref[...] = jnp.zeros_like(dpw_ref)
            dps_ref[...] = jnp.zeros_like(dps_ref)

        row = ii * t + lax.broadcasted_iota(jnp.int32, (t, 1), 0)
        for g, w in enumerate(POOL_WINDOWS):
            cols = slice(g * POOL_GW, (g + 1) * POOL_GW)
            zb = p_ref[:, g * POOL_COLS + POOL_GW:(g + 1) * POOL_COLS]
            dy = dy_ref[:, cols]
            mx = mixed_ref[:, cols]
            sc = ps_ref[:, cols]
            sg = _sigmoid(zb)
            dzb = dy * (mx * sc) * (sg * (1.0 + zb * (1.0 - sg)))
            dpm = dy * (zb * sg)
            dps_ref[:, cols] += _colsum(dpm * mx)
            dmixed = (dpm * sc).astype(BF16)
            dpooled = _dot_nt(dmixed, pw_ref[g])
            dpw_ref[g] += _dot_tn(pooled_ref[:, cols], dmixed)
            cnt = jnp.minimum(row + 1, w).astype(F32)
            u = dpooled / cnt
            acc = jnp.concatenate([u, carry[:, cols]], axis=0)
            sh = 1
            while sh < w:
                acc = acc + pltpu.roll(acc, t + HALO - sh, axis=0)
                sh *= 2
            carry[:, cols] = u[:HALO, :]
            dp_ref[:, g * POOL_COLS:g * POOL_COLS + POOL_GW] = (acc[:t, :] - dpooled).astype(dp_ref.dtype)
            dp_ref[:, g * POOL_COLS + POOL_GW:(g + 1) * POOL_COLS] = dzb.astype(dp_ref.dtype)

    rev = lambda i: (nblk - 1 - i, 0)
    return pl.pallas_call(
        body, grid=(nblk,),
        in_specs=[pl.BlockSpec((t, WIDTH), rev),
                  pl.BlockSpec((t, wide), lambda i: (nblk - 1 - i, pool_blk)),
                  pl.BlockSpec((t, WIDTH), rev), pl.BlockSpec((t, WIDTH), rev),
                  pl.BlockSpec((ng, POOL_GW, POOL_GW), lambda i: (0, 0, 0)),
                  _vec_spec(WIDTH),
                  pl.BlockSpec(memory_space=pl.ANY)],
        out_specs=[pl.BlockSpec((t, wide), lambda i: (nblk - 1 - i, pool_blk)),
                   pl.BlockSpec((ng, POOL_GW, POOL_GW), lambda i: (0, 0, 0)),
                   _vec_spec(WIDTH)],
        out_shape=[jax.ShapeDtypeStruct(dproj.shape, dproj.dtype),
                   jax.ShapeDtypeStruct((ng, POOL_GW, POOL_GW), F32),
                   jax.ShapeDtypeStruct((1, WIDTH), F32)],
        scratch_shapes=[pltpu.VMEM((HALO, WIDTH), F32)],
        input_output_aliases={6: 0},
        compiler_params=_cparams(("arbitrary",)), name=name)(dyb, proj, pooled, mixed, pw, ps, dproj)


def _hgrn_tile(s):
    return min(512, s)


def _chunk_consts():
    tt = lax.broadcasted_iota(jnp.int32, (CHUNK, CHUNK), 0)
    ss = lax.broadcasted_iota(jnp.int32, (CHUNK, CHUNK), 1)
    within = (ss <= tt) & (ss // SUB == tt // SUB)
    before = ss < (tt // SUB) * SUB
    cums = jnp.concatenate([within.astype(F32), before.astype(F32)], axis=0).astype(BF16)
    causal = ss <= tt
    upper = (ss >= tt).astype(F32).astype(BF16)
    row = lax.broadcasted_iota(jnp.int32, (CHUNK, 1), 0)
    return cums, causal, upper, row


def _dot_split(mat01, v):
    hi = v.astype(BF16)
    r1 = v - hi.astype(F32)
    mid = r1.astype(BF16)
    lo = (r1 - mid.astype(F32)).astype(BF16)
    return _dot(mat01, hi) + _dot(mat01, mid) + _dot(mat01, lo)


def _hgrn_chunk(qa, fa, lb, cums, row):
    sq = _sigmoid(qa)
    q = qa * sq
    sa = _sigmoid(fa)
    sna = 1.0 - sa
    oml = 1.0 - lb
    f = lb + oml * sa
    fc = jnp.maximum(f, MIN_FORGET)
    lf = jnp.log(fc)
    k = oml * sna
    cb = _dot_split(cums, lf)
    c = cb[:CHUNK]
    bt = cb[CHUNK:]
    ec = jnp.exp(c)
    enc = jnp.exp(jnp.minimum(-c, MAX_EXP))
    qt = q * ec
    kt = k * enc
    dms, lhs, rhs = [], [], []
    for j in range(N_SUB):
        bj = bt[j * SUB:j * SUB + 1, :]
        dm = jnp.where(row >= j * SUB, jnp.exp(jnp.minimum(bt - bj, 0.0)), 0.0)
        dms.append(dm)
        lhs.append(qt * dm)
        rhs.append(jnp.where(row // SUB == j, kt, 0.0))
    lhs = jnp.concatenate(lhs, axis=1).astype(BF16)
    rhs = jnp.concatenate(rhs, axis=1).astype(BF16)
    b = bt + c
    bl = b[CHUNK - 1:CHUNK, :]
    ebl = jnp.exp(bl)
    edec = jnp.exp(bl - b)
    eb = ec * dms[0]
    return dict(sq=sq, q=q, sa=sa, sna=sna, oml=oml, f=f, fc=fc, k=k, ec=ec, enc=enc, dms=dms,
                lhs=lhs, rhs=rhs, ebl=ebl, edec=edec, eb=eb, qd=q * eb, kdec=k * edec)


def _rider_mid_step(total):
    return total - min(32, total // 2)


def hgrn_fwd(proj, lb, hn, *, rider=None, name):
    s = proj.shape[0]
    t = _hgrn_tile(s)
    nblk = s // t
    ncht = t // CHUNK
    head_blk0 = GATE_COLS // HEAD_COLS
    nr = rider.n if rider else 0
    hbm = pl.BlockSpec(memory_space=pl.ANY)

    def body(*refs):
        p_ref, lb_ref, hn_ref = refs[:3]
        ya_ref, o_ref, st_ref = refs[3 + nr:6 + nr]
        state = refs[6 + 2 * nr]
        i = pl.program_id(1)
        if rider:
            total = HEADS * nblk
            rider.emit(pl.program_id(0) * nblk + i, total, _rider_mid_step(total),
                       refs[3:3 + nr], refs[6 + nr:6 + 2 * nr], refs[7 + 2 * nr:])

        @pl.when(i == 0)
        def _():
            state[...] = jnp.zeros_like(state)

        cums, causal, _, row = _chunk_consts()
        lbv = lb_ref[...]
        hnv = hn_ref[...]

        st = state[...]
        for ci in range(ncht):
            rows = slice(ci * CHUNK, (ci + 1) * CHUNK)
            qa = p_ref[rows, 0:HEAD_DIM]
            fa = p_ref[rows, HEAD_DIM:2 * HEAD_DIM]
            va = p_ref[rows, 2 * HEAD_DIM:3 * HEAD_DIM].astype(BF16)
            za = p_ref[rows, 3 * HEAD_DIM:4 * HEAD_DIM]
            pre = _hgrn_chunk(qa, fa, lbv, cums, row)
            stb = st.astype(BF16)
            st_ref[ci, 0] = stb
            a = jnp.where(causal, _dot_nt(pre["lhs"], pre["rhs"]), 0.0)
            o = _dot_nt(pre["qd"].astype(BF16), stb) + _dot(a.astype(BF16), va)
            st = st * pre["ebl"] + _dot_tn(va, pre["kdec"].astype(BF16))
            r = lax.rsqrt(_rowmean(o * o) + EPS)
            o_ref[rows, :] = o
            ya_ref[rows, :] = ((o * r) * hnv * (za * _sigmoid(za))).astype(ya_ref.dtype)
        state[...] = st

    return pl.pallas_call(
        body, grid=(HEADS, nblk),
        in_specs=[pl.BlockSpec((t, HEAD_COLS), lambda h, i: (i, head_blk0 + h)),
                  pl.BlockSpec((1, HEAD_DIM), lambda h, i: (0, h)),
                  pl.BlockSpec((1, HEAD_DIM), lambda h, i: (0, h))] + [hbm] * nr,
        out_specs=[pl.BlockSpec((t, HEAD_DIM), lambda h, i: (i, h)),
                   pl.BlockSpec((t, HEAD_DIM), lambda h, i: (i, h)),
                   pl.BlockSpec((ncht, 1, HEAD_DIM, HEAD_DIM), lambda h, i: (i, h, 0, 0))] + [hbm] * nr,
        out_shape=[jax.ShapeDtypeStruct((s, WIDTH), BF16), jax.ShapeDtypeStruct((s, WIDTH), F32),
                   jax.ShapeDtypeStruct((s // CHUNK, HEADS, HEAD_DIM, HEAD_DIM), BF16)]
        + (rider.out_shape if rider else []),
        scratch_shapes=[pltpu.VMEM((HEAD_DIM, HEAD_DIM), F32)] + (rider.scratch if rider else []),
        compiler_params=_cparams(("arbitrary", "arbitrary")), name=name)(proj, lb, hn, *(rider.arrs if rider else []))


def hgrn_bwd(dya, proj, o_all, states, lb, hn, dproj, *, rider=None, name):
    s = proj.shape[0]
    t = _hgrn_tile(s)
    nblk = s // t
    ncht = t // CHUNK
    head_blk0 = GATE_COLS // HEAD_COLS
    nr = rider.n if rider else 0
    hbm = pl.BlockSpec(memory_space=pl.ANY)

    def body(*refs):
        dy_ref, p_ref, o_ref, st_ref, lb_ref, hn_ref = refs[:6]
        dp_ref, dhn_ref, dlb_ref = refs[7 + nr:10 + nr]
        dstate = refs[10 + 2 * nr]
        i = pl.program_id(1)
        if rider:
            total = HEADS * nblk
            rider.emit(pl.program_id(0) * nblk + i, total, _rider_mid_step(total),
                       refs[7:7 + nr], refs[10 + nr:10 + 2 * nr], refs[11 + 2 * nr:])

        @pl.when(i == 0)
        def _():
            dstate[...] = jnp.zeros_like(dstate)
            dhn_ref[...] = jnp.zeros_like(dhn_ref)
            dlb_ref[...] = jnp.zeros_like(dlb_ref)

        cums, causal, upper, row = _chunk_consts()
        lbv = lb_ref[...]
        hnv = hn_ref[...]

        dst1 = dstate[...]
        dhn_acc = jnp.zeros_like(hnv)
        dlb_acc = jnp.zeros_like(lbv)
        for ci in reversed(range(ncht)):
            rows = slice(ci * CHUNK, (ci + 1) * CHUNK)
            qa = p_ref[rows, 0:HEAD_DIM]
            fa = p_ref[rows, HEAD_DIM:2 * HEAD_DIM]
            vb = p_ref[rows, 2 * HEAD_DIM:3 * HEAD_DIM].astype(BF16)
            za = p_ref[rows, 3 * HEAD_DIM:4 * HEAD_DIM]
            o = o_ref[rows, :]
            dy = dy_ref[rows, :]
            st0b = st_ref[ci, 0]
            r = lax.rsqrt(_rowmean(o * o) + EPS)
            on = o * r
            sgz = _sigmoid(za)
            sz = za * sgz
            dza = dy * on * hnv * (sgz * (1.0 + za * (1.0 - sgz)))
            dhn_acc = dhn_acc + _colsum(dy * on * sz)
            don = dy * hnv * sz
            do = r * (don - on * _rowmean(don * on))
            dob = do.astype(BF16)
            pre = _hgrn_chunk(qa, fa, lbv, cums, row)
            q, k = pre["q"], pre["k"]
            a = jnp.where(causal, _dot_nt(pre["lhs"], pre["rhs"]), 0.0)
            dst1b = dst1.astype(BF16)
            dq_inter = _dot(dob, st0b) * pre["eb"]
            da = jnp.where(causal, _dot_nt(dob, vb), 0.0).astype(BF16)
            dv = _dot_tn(a.astype(BF16), dob) + _dot_nt(pre["kdec"].astype(BF16), dst1b)
            dk_state = _dot(vb, dst1b) * pre["edec"]
            dlhs = _dot(da, pre["rhs"])
            drhs = _dot_tn(da, pre["lhs"])
            dq_a = jnp.zeros_like(q)
            dk_a = jnp.zeros_like(k)
            for j in range(N_SUB):
                dq_a = dq_a + pre["dms"][j] * dlhs[:, j * HEAD_DIM:(j + 1) * HEAD_DIM]
                dk_a = dk_a + jnp.where(row // SUB == j, drhs[:, j * HEAD_DIM:(j + 1) * HEAD_DIM], 0.0)
            dq = dq_inter + pre["ec"] * dq_a
            dk = dk_state + pre["enc"] * dk_a
            db = q * dq - k * dk
            dbl = _colsum(k * dk_state) + pre["ebl"] * _colsum(dst1 * st0b.astype(F32))
            dlf = _dot_split(upper, db) + dbl
            dst1 = dst1 * pre["ebl"] + _dot_tn(dob, pre["qd"].astype(BF16))
            sq, sa, sna, oml = pre["sq"], pre["sa"], pre["sna"], pre["oml"]
            dqa = dq * (sq * (1.0 + qa * (1.0 - sq)))
            dlf_f = jnp.where(pre["f"] >= MIN_FORGET, dlf / pre["fc"], 0.0)
            dfa = (dlf_f - dk) * (oml * sa * sna)
            dlb_acc = dlb_acc + _colsum((dlf_f - dk) * sna)
            dp_ref[rows, :] = jnp.concatenate([dqa, dfa, dv, dza], axis=1).astype(dp_ref.dtype)
        dstate[...] = dst1
        dhn_ref[...] += dhn_acc
        dlb_ref[...] += dlb_acc

    rev = lambda h, i: (nblk - 1 - i, h)
    return pl.pallas_call(
        body, grid=(HEADS, nblk),
        in_specs=[pl.BlockSpec((t, HEAD_DIM), rev),
                  pl.BlockSpec((t, HEAD_COLS), lambda h, i: (nblk - 1 - i, head_blk0 + h)),
                  pl.BlockSpec((t, HEAD_DIM), rev),
                  pl.BlockSpec((ncht, 1, HEAD_DIM, HEAD_DIM), lambda h, i: (nblk - 1 - i, h, 0, 0)),
                  pl.BlockSpec((1, HEAD_DIM), lambda h, i: (0, h)),
                  pl.BlockSpec((1, HEAD_DIM), lambda h, i: (0, h)),
                  hbm] + [hbm] * nr,
        out_specs=[pl.BlockSpec((t, HEAD_COLS), lambda h, i: (nblk - 1 - i, head_blk0 + h)),
                   pl.BlockSpec((1, HEAD_DIM), lambda h, i: (0, h)),
                   pl.BlockSpec((1, HEAD_DIM), lambda h, i: (0, h))] + [hbm] * nr,
        out_shape=[jax.ShapeDtypeStruct(dproj.shape, dproj.dtype),
                   jax.ShapeDtypeStruct((1, WIDTH), F32), jax.ShapeDtypeStruct((1, WIDTH), F32)]
        + (rider.out_shape if rider else []),
        scratch_shapes=[pltpu.VMEM((HEAD_DIM, HEAD_DIM), F32)] + (rider.scratch if rider else []),
        input_output_aliases={6: 0},
        compiler_params=_cparams(("arbitrary", "arbitrary")),
        name=name)(dya, proj, o_all, states, lb, hn, dproj, *(rider.arrs if rider else []))


def _softmax_rows(lower):
    mx = jnp.max(lower, axis=0, keepdims=True)
    e = jnp.exp(lower - mx)
    return e / jnp.sum(e, axis=0, keepdims=True)


def lb_table(lower, *, name):
    depth, w = lower.shape

    def body(l_ref, o_ref):
        sm = _softmax_rows(l_ref[...])
        acc = jnp.zeros((1, w), F32)
        o_ref[0:1, :] = acc
        for l in range(1, depth):
            acc = acc + sm[l:l + 1, :]
            o_ref[l:l + 1, :] = acc

    return pl.pallas_call(body, out_shape=jax.ShapeDtypeStruct((depth, w), F32), name=name)(lower)


def lb_table_bwd(lower, dlb, *, name):
    depth, w = lower.shape

    def body(l_ref, d_ref, o_ref):
        sm = _softmax_rows(l_ref[...])
        dlbv = d_ref[...]
        dsm = [jnp.zeros((1, w), F32)]
        for i in range(1, depth):
            acc = jnp.zeros((1, w), F32)
            for l in range(i, depth):
                acc = acc + dlbv[l:l + 1, :]
            dsm.append(acc)
        inner = jnp.zeros((1, w), F32)
        for i in range(depth):
            inner = inner + sm[i:i + 1, :] * dsm[i]
        for i in range(depth):
            o_ref[i:i + 1, :] = sm[i:i + 1, :] * (dsm[i] - inner)

    return pl.pallas_call(body, out_shape=jax.ShapeDtypeStruct((depth, w), F32), name=name)(lower, dlb)


def w_ada_grad(c_all, dmod_cols, *, name):
    depth, _, cols = dmod_cols.shape
    d = c_all.shape[1]

    def body(c_ref, dm_ref, o_ref):
        cv = c_ref[...]
        ca = cv * _sigmoid(cv)
        o_ref[...] = _dot_tn(ca, dm_ref[...])

    return pl.pallas_call(
        body, grid=(depth,),
        in_specs=[pl.BlockSpec((N_DEV, d), lambda l: (0, 0)), pl.BlockSpec((None, N_DEV, cols), lambda l: (l, 0, 0))],
        out_specs=pl.BlockSpec((None, d, cols), lambda l: (l, 0, 0)),
        out_shape=jax.ShapeDtypeStruct((depth, d, cols), F32),
        compiler_params=_cparams(("arbitrary",)), name=name)(c_all, dmod_cols)


def sum_parts(parts, *, name):
    p, r, c = parts.shape

    def body(p_ref, o_ref):
        acc = p_ref[0]
        for j in range(1, p):
            acc = acc + p_ref[j]
        o_ref[...] = acc

    return pl.pallas_call(body, out_shape=jax.ShapeDtypeStruct((r, c), F32), name=name)(parts)


def _adam_rows(r, c):
    tr = r
    while tr * c * 4 > (1 << 20) and tr % 16 == 0:
        tr //= 2
    return tr


def _adam_update(w_ref, m_ref, v_ref, g_ref, go_ref, d_ref, mo_ref, vo_ref):
    g = g_ref[0].astype(F32)
    for j in range(1, g_ref.shape[0]):
        g = g + g_ref[j].astype(F32)
    mn = ADAM_B1 * m_ref[...] + (1.0 - ADAM_B1) * g
    vn = ADAM_B2 * v_ref[...] + (1.0 - ADAM_B2) * (g * g)
    m_hat = mn / (1.0 - ADAM_B1 ** ADAM_STEP)
    v_hat = vn / (1.0 - ADAM_B2 ** ADAM_STEP)
    go_ref[...] = g
    d_ref[...] = -ADAM_LR * (m_hat / (jnp.sqrt(v_hat) + ADAM_EPS) + ADAM_WD * w_ref[...])
    mo_ref[...] = mn
    vo_ref[...] = vn


def adamw(w, m, v, gparts, *, name):
    r, c = w.shape
    p = gparts.shape[0]
    tr = _adam_rows(r, c)
    spec = pl.BlockSpec((tr, c), lambda i: (i, 0))
    shp = jax.ShapeDtypeStruct((r, c), F32)
    return pl.pallas_call(
        functools.partial(_adam_update), grid=(r // tr,),
        in_specs=[spec, spec, spec, pl.BlockSpec((p, tr, c), lambda i: (0, i, 0))],
        out_specs=[spec] * 4, out_shape=[shp] * 4,
        compiler_params=_cparams(("arbitrary",)), name=name)(w, m, v, gparts)


def adamw_layers(w, m, v, gparts, *, name):
    depth, r, c = w.shape
    p = gparts[0].shape[0]
    tr = _adam_rows(r, c)

    def body(w_ref, m_ref, v_ref, *rest):
        g_refs, outs = rest[:depth], rest[depth:]
        layer = pl.program_id(0)
        for k in range(depth):
            @pl.when(layer == k)
            def _(k=k):
                _adam_update(w_ref, m_ref, v_ref, g_refs[k], *outs)

    spec = pl.BlockSpec((None, tr, c), lambda l, i: (l, i, 0))
    g_specs = [pl.BlockSpec((p, tr, c), functools.partial(lambda l, i, k: (0, jnp.where(l == k, i, 0), 0), k=k))
               for k in range(depth)]
    shp = jax.ShapeDtypeStruct((depth, r, c), F32)
    return pl.pallas_call(
        body, grid=(depth, r // tr),
        in_specs=[spec, spec, spec] + g_specs,
        out_specs=[spec] * 4, out_shape=[shp] * 4,
        compiler_params=_cparams(("arbitrary", "arbitrary")), name=name)(w, m, v, *gparts)


def _position():
    x, y, c = lax.axis_index("x"), lax.axis_index("y"), lax.axis_index("c")
    return x, y, c


def _dev_index(x, y, c):
    return 4 * x + 2 * y + c


def _gather_phases(ins, outs, send_sems, recv_sems, local_sems):
    n = len(ins)
    x, y, c = _position()
    me, sibling = (x, y, c), (x, y, 1 - c)
    chips = [(1 - x, y), (x, 1 - y), (1 - x, 1 - y)]

    def copy(a, k, block, to, own=False):
        slot = outs[a].at[_dev_index(*block)]
        return pltpu.make_async_remote_copy(
            src_ref=ins[a] if own else slot, dst_ref=slot,
            send_sem=send_sems.at[a * 7 + k], recv_sem=recv_sems.at[a * 7 + k],
            device_id=to, device_id_type=MESH)

    def mine(a):
        return pltpu.make_async_copy(ins[a], outs[a].at[_dev_index(*me)], local_sems.at[a])

    def first(a):
        return [copy(a, 0, me, sibling, True)] + [copy(a, 1 + j, me, (*chip, c), True) for j, chip in enumerate(chips)]

    def passed(a):
        return [copy(a, 4 + j, (*chip, c), sibling) for j, chip in enumerate(chips)]

    def start():
        for a in range(n):
            mine(a).start()
        for a in range(n):
            for cp in first(a):
                cp.start()

    def mid():
        for j, chip in enumerate(chips):
            for a in range(n):
                copy(a, 1 + j, (*chip, c), me).wait_recv()
                passed(a)[j].start()

    def finish():
        for a in range(n):
            copy(a, 0, sibling, me).wait_recv()
            for j, chip in enumerate(chips):
                copy(a, 4 + j, (*chip, 1 - c), me).wait_recv()
        for a in range(n):
            for cp in first(a) + passed(a):
                cp.wait_send()
            mine(a).wait()

    return start, mid, finish


def _scatter_phases(ins, outs, send_sems, recv_sems, local_sems):
    n = len(ins)
    x, y, c = _position()
    me = _dev_index(x, y, c)

    def peer(r):
        return (x ^ (r >> 2), y ^ ((r >> 1) & 1), c ^ (r & 1))

    def copy(a, r):
        to = peer(r)
        return pltpu.make_async_remote_copy(
            src_ref=ins[a].at[_dev_index(*to)], dst_ref=outs[a].at[me],
            send_sem=send_sems.at[a * 7 + r - 1], recv_sem=recv_sems.at[a * 7 + r - 1],
            device_id=to, device_id_type=MESH)

    def arrival(a, r):
        return pltpu.make_async_remote_copy(
            src_ref=ins[a].at[me], dst_ref=outs[a].at[_dev_index(*peer(r))],
            send_sem=send_sems.at[a * 7 + r - 1], recv_sem=recv_sems.at[a * 7 + r - 1],
            device_id=peer(r), device_id_type=MESH)

    def mine(a):
        return pltpu.make_async_copy(ins[a].at[me], outs[a].at[me], local_sems.at[a])

    def start():
        for a in range(n):
            mine(a).start()
        for r in range(1, N_DEV):
            for a in range(n):
                copy(a, r).start()

    def finish():
        for r in range(1, N_DEV):
            for a in range(n):
                arrival(a, r).wait_recv()
        for r in range(1, N_DEV):
            for a in range(n):
                copy(a, r).wait_send()
        for a in range(n):
            mine(a).wait()

    return start, None, finish


class Rider:
    def __init__(self, kind, arrs):
        self.kind, self.arrs, self.n = kind, list(arrs), len(arrs)
        lead = (N_DEV,) if kind == "gather" else ()
        self.out_shape = [jax.ShapeDtypeStruct(lead + a.shape, a.dtype) for a in self.arrs]
        self.scratch = [pltpu.SemaphoreType.DMA((7 * self.n,)), pltpu.SemaphoreType.DMA((7 * self.n,)),
                        pltpu.SemaphoreType.DMA((self.n,))]

    def phases(self, ins, outs, sems):
        make = _gather_phases if self.kind == "gather" else _scatter_phases
        return make(ins, outs, *sems)

    def emit(self, step, total, mid_step, ins, outs, sems):
        start, mid, finish = self.phases(ins, outs, sems)
        pl.when(step == 0)(start)
        if mid is not None:
            pl.when(step == mid_step)(mid)
        pl.when(step == total - 1)(finish)


def _standalone(rider, name):
    n = rider.n
    hbm = pl.BlockSpec(memory_space=pl.ANY)

    def body(*refs):
        start, mid, finish = rider.phases(refs[:n], refs[n:2 * n], refs[2 * n:])
        start()
        if mid is not None:
            mid()
        finish()

    return pl.pallas_call(body, out_shape=rider.out_shape, in_specs=[hbm] * n, out_specs=[hbm] * n,
                          scratch_shapes=rider.scratch, name=name)(*rider.arrs)


def all_gather(arrs, *, name):
    return _standalone(Rider("gather", arrs), name)


def scatter_parts(arrs, *, name):
    return _standalone(Rider("scatter", arrs), name)


def mod_exchange(c_all, w_ada, b_cols, *, name):
    depth, d, cols = w_ada.shape
    hbm = pl.BlockSpec(memory_space=pl.ANY)
    vmem = pl.BlockSpec(memory_space=pltpu.VMEM)

    def body(c_ref, w_ref, b_ref, out_ref, wbuf, sendbuf, send_sems, recv_sems, load_sem):
        x, y, c = _position()
        me = _dev_index(x, y, c)
        cv = c_ref[...]
        ca = cv * _sigmoid(cv)
        for l in range(depth):
            load = pltpu.make_async_copy(w_ref.at[l], wbuf, load_sem)
            load.start()
            load.wait()
            part = jnp.dot(ca, wbuf[...], preferred_element_type=F32,
                           precision=lax.Precision.HIGHEST) + b_ref[l:l + 1, :]
            for bi in range(N_DEV):
                sendbuf[bi, l:l + 1, :] = part[bi:bi + 1, :]

        def peer(r):
            return (x ^ (r >> 2), y ^ ((r >> 1) & 1), c ^ (r & 1))

        def copy(r):
            to = peer(r)
            return pltpu.make_async_remote_copy(
                src_ref=sendbuf.at[_dev_index(*to)], dst_ref=out_ref.at[me],
                send_sem=send_sems.at[r - 1], recv_sem=recv_sems.at[r - 1],
                device_id=to, device_id_type=MESH)

        def arrival(r):
            return pltpu.make_async_remote_copy(
                src_ref=sendbuf.at[me], dst_ref=out_ref.at[_dev_index(*peer(r))],
                send_sem=send_sems.at[r - 1], recv_sem=recv_sems.at[r - 1],
                device_id=peer(r), device_id_type=MESH)

        out_ref[me] = sendbuf[me]
        sends = [copy(r) for r in range(1, N_DEV)]
        for cp in sends:
            cp.start()
        for r in range(1, N_DEV):
            arrival(r).wait_recv()
        for cp in sends:
            cp.wait_send()

    return pl.pallas_call(
        body,
        out_shape=jax.ShapeDtypeStruct((N_DEV, depth, cols), F32),
        in_specs=[vmem, hbm, vmem], out_specs=vmem,
        scratch_shapes=[pltpu.VMEM((d, cols), F32), pltpu.VMEM((N_DEV, depth, cols), F32),
                        pltpu.SemaphoreType.DMA((7,)), pltpu.SemaphoreType.DMA((7,)), pltpu.SemaphoreType.DMA],
        compiler_params=pltpu.CompilerParams(vmem_limit_bytes=VMEM_LIMIT),
        name=name)(c_all, w_ada, b_cols)


def kernel(x, c, w_ada, b_ada, norm_pre, norm_post, w_in, lower_bounds, hgrn_norm, pool_w, pool_scale, w_proj_a, w_proj_b, w_out, loss_target, m_w_ada, m_b_ada, m_norm_pre, m_norm_post, m_w_in, m_lower_bounds, m_hgrn_norm, m_pool_w, m_pool_scale, m_w_proj_a, m_w_proj_b, m_w_out, v_w_ada, v_b_ada, v_norm_pre, v_norm_post, v_w_in, v_lower_bounds, v_hgrn_norm, v_pool_w, v_pool_scale, v_w_proj_a, v_w_proj_b, v_w_out):
    depth = w_in.shape[0]
    d = D_MODEL
    ada_cols = w_ada.shape[2]
    xi, yi, ci = _position()
    me = _dev_index(xi, yi, ci)
    xs = x[0]
    target = loss_target[0]
    ng = len(POOL_WINDOWS)

    def shards(l):
        return [w_in[l].astype(BF16), w_proj_a[l].astype(BF16), w_proj_b[l].astype(BF16),
                w_out[l].astype(BF16), pool_w[l].astype(BF16)]

    def full_weights(g_in, g_pa, g_pb, g_out, g_pool, w_in=None):
        return dict(
            w_in=permute_w_in(g_in, name="permute_w_in") if w_in is None else w_in,
            pa=jnp.transpose(g_pa, (1, 0, 2)).reshape(WIDTH, d),
            pb=jnp.transpose(g_pb, (1, 0, 2)).reshape(WIDTH, d),
            w_out=g_out.reshape(d, d),
            pool=jnp.transpose(g_pool, (1, 0, 2, 3)).reshape(ng, POOL_GW, POOL_GW))

    (g_in0,) = all_gather(shards(0)[:1], name="gather_w_in")
    gathered = [dict(w_in=permute_w_in(g_in0, name="permute_w_in"))]

    (c_all,) = all_gather([c], name="gather_c")
    c_all = c_all.reshape(N_DEV, d)
    b_cols = lax.dynamic_slice_in_dim(b_ada, me * ada_cols, ada_cols, axis=1)
    mod_parts = mod_exchange(c_all, w_ada, b_cols, name="mod_exchange")
    mod = jnp.transpose(mod_parts, (1, 0, 2)).reshape(depth, 3 * d)
    lb_all = lb_table(lower_bounds, name="lb_table")

    saved = []
    cur = xs
    for l in range(depth):
        w = gathered[l]
        shift, scale, gate = mod[l:l + 1, :d], mod[l:l + 1, d:2 * d], mod[l:l + 1, 2 * d:]
        h = prenorm_fwd(cur, norm_pre[l:l + 1], shift, scale, name="prenorm_fwd")
        if l == 0:
            proj, *rest_w = matmul_nn(h, w["w_in"], tm=1024, tn=1024, out_dtype=F32,
                                      rider=Rider("gather", shards(0)[1:]), name="mm_w_in_gather")
            w = gathered[0] = full_weights(g_in0, *rest_w, w_in=w["w_in"])
        else:
            proj = matmul_nn(h, w["w_in"], tm=1024, tn=1024, out_dtype=F32, name="mm_w_in")
        if l + 1 < depth:
            y_a, o_all, states, *nxt_w = hgrn_fwd(proj, lb_all[l:l + 1], hgrn_norm[l:l + 1],
                                                   rider=Rider("gather", shards(l + 1)), name="hgrn_fwd_gather")
            gathered.append(full_weights(*nxt_w))
        else:
            y_a, o_all, states = hgrn_fwd(proj, lb_all[l:l + 1], hgrn_norm[l:l + 1], name="hgrn_fwd")
        y_b, pooled, mixed = pool_fwd(proj, w["pool"], pool_scale[l:l + 1], name="pool_fwd")
        pa = matmul_nn(y_a, w["pa"], tm=1024, tn=2048, out_dtype=F32, name="mm_proj_a")
        pb = matmul_nn(y_b, w["pb"], tm=1024, tn=2048, out_dtype=F32, name="mm_proj_b")
        merged = gate_fwd(proj, pa, pb, name="gate_fwd")
        out = matmul_nn(merged, w["w_out"], tm=1024, tn=1024, out_dtype=F32, name="mm_w_out")
        nxt = postnorm_fwd(cur, out, norm_post[l:l + 1], gate, name="postnorm_fwd")
        saved.append(dict(x=cur, h=h, proj=proj, y_a=y_a, o=o_all, states=states, y_b=y_b, pooled=pooled,
                          mixed=mixed, pa=pa, pb=pb, merged=merged, out=out, scale=scale, gate=gate))
        cur = nxt

    g, loss_part = loss_head(cur, target, name="loss_head")
    loss = lax.psum(loss_part[0, 0], ("x", "y", "c"))

    small = [None] * depth
    big = [None] * depth
    pending = None
    for l in reversed(range(depth)):
        w, sv = gathered[l], saved[l]
        dout, dgate, dnpost = postnorm_bwd(g, sv["out"], norm_post[l:l + 1], sv["gate"], name="postnorm_bwd")
        dmerged = matmul_nt(dout, w["w_out"], tm=512, tn=2048, out_dtype=F32, name="mm_w_out_dx")
        dw_out = matmul_tn(sv["merged"], dout, tm=2048, tn=1024, out_dtype=BF16, name="mm_w_out_dw")
        dpa, dpb, dproj = gate_bwd(dmerged, sv["proj"], sv["pa"], sv["pb"], name="gate_bwd")
        dya = matmul_nt(dpa, w["pa"], tm=1024, tn=2048, out_dtype=F32, name="mm_proj_a_dx")
        dyb = matmul_nt(dpb, w["pb"], tm=1024, tn=2048, out_dtype=F32, name="mm_proj_b_dx")
        dw_pa = matmul_tn(sv["y_a"], dpa, tm=2048, tn=2048, out_dtype=BF16, name="mm_proj_a_dw")
        dw_pb = matmul_tn(sv["y_b"], dpb, tm=2048, tn=2048, out_dtype=BF16, name="mm_proj_b_dw")
        dproj, dpool_w, dpool_scale = pool_bwd(dyb, sv["proj"], sv["pooled"], sv["mixed"], w["pool"],
                                               pool_scale[l:l + 1], dproj, name="pool_bwd")
        if pending is None:
            dproj, dhn, dlb = hgrn_bwd(dya, sv["proj"], sv["o"], sv["states"], lb_all[l:l + 1],
                                       hgrn_norm[l:l + 1], dproj, name="hgrn_bwd")
        else:
            dproj, dhn, dlb, *recv = hgrn_bwd(dya, sv["proj"], sv["o"], sv["states"], lb_all[l:l + 1],
                                              hgrn_norm[l:l + 1], dproj, rider=Rider("scatter", pending),
                                              name="hgrn_bwd_scatter")
            big[l + 1] = recv
        by_owner = lambda t: jnp.transpose(t.reshape(WIDTH, N_DEV, d // N_DEV), (1, 0, 2))
        others = [by_owner(dw_pa), by_owner(dw_pb), dw_out.reshape(N_DEV, d // N_DEV, d),
                  jnp.transpose(dpool_w.astype(BF16).reshape(ng, N_DEV, POOL_GW // N_DEV, POOL_GW), (1, 0, 2, 3))]
        if l > 0:
            dh = matmul_nt(dproj, w["w_in"], tm=512, tn=2048, out_dtype=F32, name="mm_w_in_dx")
            dw_in = matmul_tn(sv["h"], dproj, tm=2048, tn=1024, out_dtype=BF16, name="mm_w_in_dw")
            pending = [unpermute_w_in(dw_in, name="unpermute_w_in")] + others
        else:
            dw_in, *recv_others = matmul_tn(sv["h"], dproj, tm=2048, tn=1024, out_dtype=BF16,
                                            rider=Rider("scatter", others), name="mm_w_in_dw_scatter")
            dh, recv_in = matmul_nt(dproj, w["w_in"], tm=512, tn=2048, out_dtype=F32,
                                    rider=Rider("scatter", [unpermute_w_in(dw_in, name="unpermute_w_in")]),
                                    name="mm_w_in_dx_scatter")
            big[0] = [recv_in] + recv_others
        g, dshift, dscale, dnpre = prenorm_bwd(dh, sv["x"], norm_pre[l:l + 1], sv["scale"], g, name="prenorm_bwd")
        small[l] = jnp.concatenate([dshift, dscale, dgate, dnpre, dnpost, dlb, dhn, dpool_scale], axis=1)
    grad_x = g[None]

    small_mine = jnp.concatenate(small, axis=0)
    (small_all,) = all_gather([small_mine], name="gather_small")
    small_sum = sum_parts(small_all, name="sum_small")
    dmod_all = small_all[:, :, :3 * d]
    dmod_cols = jnp.transpose(lax.dynamic_slice_in_dim(dmod_all, me * ada_cols, ada_cols, axis=2), (1, 0, 2))
    g_w_ada = w_ada_grad(c_all, dmod_cols, name="w_ada_grad")
    off = 3 * d
    g_b_ada = small_sum[:, :off]
    g_npre = small_sum[:, off:off + d]
    g_npost = small_sum[:, off + d:off + 2 * d]
    g_lb_tab = small_sum[:, off + 2 * d:off + 2 * d + WIDTH]
    g_hn = small_sum[:, off + 2 * d + WIDTH:off + 2 * d + 2 * WIDTH]
    g_ps = small_sum[:, off + 2 * d + 2 * WIDTH:]
    g_lower = lb_table_bwd(lower_bounds, g_lb_tab, name="lb_table_bwd")

    def update(wt, mt, vt, gparts, shape2, name):
        outs = adamw(wt.reshape(shape2), mt.reshape(shape2), vt.reshape(shape2), gparts, name=name)
        return [o.reshape(wt.shape) for o in outs]

    def update_layers(wt, mt, vt, kind, name):
        shape3 = (depth, -1, wt.shape[-1])
        w3 = wt.reshape(shape3)
        gps = [big[l][kind].reshape((N_DEV,) + w3.shape[1:]) for l in range(depth)]
        outs = adamw_layers(w3, mt.reshape(shape3), vt.reshape(shape3), gps, name=name)
        return [o.reshape(wt.shape) for o in outs]

    def update_small(wt, mt, vt, gt, name):
        shape2 = (-1, wt.shape[-1])
        return update(wt, mt, vt, gt.reshape(shape2)[None], shape2, name)

    res = {
        "w_ada": update_small(w_ada, m_w_ada, v_w_ada, g_w_ada, "adamw_w_ada"),
        "b_ada": update_small(b_ada, m_b_ada, v_b_ada, g_b_ada, "adamw_b_ada"),
        "norm_pre": update_small(norm_pre, m_norm_pre, v_norm_pre, g_npre, "adamw_norm_pre"),
        "norm_post": update_small(norm_post, m_norm_post, v_norm_post, g_npost, "adamw_norm_post"),
        "w_in": update_layers(w_in, m_w_in, v_w_in, 0, "adamw_w_in"),
        "lower_bounds": update_small(lower_bounds, m_lower_bounds, v_lower_bounds, g_lower, "adamw_lower_bounds"),
        "hgrn_norm": update_small(hgrn_norm, m_hgrn_norm, v_hgrn_norm, g_hn, "adamw_hgrn_norm"),
        "pool_w": update_layers(pool_w, m_pool_w, v_pool_w, 4, "adamw_pool_w"),
        "pool_scale": update_small(pool_scale, m_pool_scale, v_pool_scale, g_ps, "adamw_pool_scale"),
        "w_proj_a": update_layers(w_proj_a, m_w_proj_a, v_w_proj_a, 1, "adamw_w_proj_a"),
        "w_proj_b": update_layers(w_proj_b, m_w_proj_b, v_w_proj_b, 2, "adamw_w_proj_b"),
        "w_out": update_layers(w_out, m_w_out, v_w_out, 3, "adamw_w_out"),
    }
    order = ["w_ada", "b_ada", "norm_pre", "norm_post", "w_in", "lower_bounds", "hgrn_norm", "pool_w",
             "pool_scale", "w_proj_a", "w_proj_b", "w_out"]
    outs = [loss, grad_x]
    for k in range(4):
        outs += [res[nm][k] for nm in order]
    return tuple(outs)
```

```python
import functools

import jax
import jax.numpy as jnp
from jax import lax
from jax.experimental import pallas as pl
from jax.experimental.pallas import tpu as pltpu

F32 = jnp.float32
BF16 = jnp.bfloat16
MESH = pl.DeviceIdType.MESH

N_DEV = 8
EPS = 1e-6
MIN_FORGET = 1e-30
D_MODEL = 2048
HEADS = 8
HEAD_DIM = 128
CHUNK = 64
SUB = 16
N_SUB = CHUNK // SUB
WIDTH = 1024
POOL_WINDOWS = (2, 4, 8, 16)
POOL_GW = 256
HALO = 16
IN_COLS = 10240
LANE = 128
N_COLBLK = IN_COLS // LANE
GATE_COLS = 4096
HEAD_COLS = 4 * HEAD_DIM
POOL_COLS = 2 * POOL_GW
MAX_EXP = 80.0

ADAM_LR = 0.001
ADAM_B1 = 0.9
ADAM_B2 = 0.999
ADAM_EPS = 1e-08
ADAM_WD = 0.01
ADAM_STEP = 10

VMEM_LIMIT = 56 * 1024 * 1024


def _cparams(sem=None):
    return pltpu.CompilerParams(dimension_semantics=sem, vmem_limit_bytes=VMEM_LIMIT)


def _sigmoid(v):
    return 1.0 / (1.0 + jnp.exp(-v))


def _dot(a, b):
    return jnp.dot(a, b, preferred_element_type=F32)


def _dot_nt(a, b):
    return lax.dot_general(a, b, (((1,), (1,)), ((), ())), preferred_element_type=F32)


def _dot_tn(a, b):
    return lax.dot_general(a, b, (((0,), (0,)), ((), ())), preferred_element_type=F32)


def _colsum(v):
    return jnp.sum(v, axis=0, keepdims=True)


def _rowmean(v):
    return jnp.mean(v, axis=-1, keepdims=True)


def _orig_block_static(n):
    if n < 32:
        return n + 48
    if n < 64:
        m = n - 32
        return 8 * (m % 4) + m // 4
    m = n - 64
    t = m % 4
    return 32 + 2 * (m // 4) + (t % 2) + 8 * (t // 2)


def _accumulate(step, steps, prod, o_ref, acc_ref):
    if steps == 1:
        o_ref[...] = prod.astype(o_ref.dtype)
        return

    @pl.when(step == 0)
    def _():
        acc_ref[...] = prod

    @pl.when(step > 0)
    def _():
        acc_ref[...] += prod

    @pl.when(step == steps - 1)
    def _():
        o_ref[...] = acc_ref[...].astype(o_ref.dtype)


def _matmul_call(dot, a, b, *, grid, in_specs, out_spec, out_shape, acc_shape, steps, rider, name):
    nr = rider.n if rider else 0
    hbm = pl.BlockSpec(memory_space=pl.ANY)
    has_acc = steps > 1

    def body(*refs):
        a_ref, b_ref = refs[:2]
        o_ref = refs[2 + nr]
        scratch = refs[3 + 2 * nr:]
        if rider:
            total = grid[0] * grid[1]
            rider.emit(pl.program_id(0) * grid[1] + pl.program_id(1), total, _rider_mid_step(total),
                       refs[2:2 + nr], refs[3 + nr:3 + 2 * nr], scratch[1 if has_acc else 0:])
        _accumulate(pl.program_id(1), steps, dot(a_ref[...], b_ref[...]), o_ref, scratch[0] if has_acc else None)

    outs = pl.pallas_call(
        body, grid=grid,
        in_specs=in_specs + [hbm] * nr, out_specs=[out_spec] + [hbm] * nr,
        out_shape=[out_shape] + (rider.out_shape if rider else []),
        scratch_shapes=([pltpu.VMEM(acc_shape, F32)] if has_acc else []) + (rider.scratch if rider else []),
        compiler_params=_cparams(("arbitrary", "arbitrary")),
        name=name)(a, b, *(rider.arrs if rider else []))
    return outs if rider else outs[0]


def matmul_nn(a, b, *, tm, tn, out_dtype, name, rider=None):
    m, k = a.shape
    n = b.shape[1]
    tm = min(tm, m)
    return _matmul_call(
        _dot, a, b, grid=(n // tn, m // tm),
        in_specs=[pl.BlockSpec((tm, k), lambda j, i: (i, 0)), pl.BlockSpec((k, tn), lambda j, i: (0, j))],
        out_spec=pl.BlockSpec((tm, tn), lambda j, i: (i, j)),
        out_shape=jax.ShapeDtypeStruct((m, n), out_dtype), acc_shape=None, steps=1, rider=rider, name=name)


def matmul_nt(a, b, *, tm, tn, out_dtype, name, rider=None):
    m, n = a.shape
    k = b.shape[0]
    tm = min(tm, m)
    return _matmul_call(
        _dot_nt, a, b, grid=(m // tm, n // tn),
        in_specs=[pl.BlockSpec((tm, tn), lambda i, j: (i, j)), pl.BlockSpec((k, tn), lambda i, j: (0, j))],
        out_spec=pl.BlockSpec((tm, k), lambda i, j: (i, 0)),
        out_shape=jax.ShapeDtypeStruct((m, k), out_dtype), acc_shape=(tm, k), steps=n // tn, rider=rider, name=name)


def matmul_tn(a, b, *, tm, tn, out_dtype, name, rider=None):
    m, k = a.shape
    n = b.shape[1]
    tm = min(tm, m)
    return _matmul_call(
        _dot_tn, a, b, grid=(n // tn, m // tm),
        in_specs=[pl.BlockSpec((tm, k), lambda j, i: (i, 0)), pl.BlockSpec((tm, tn), lambda j, i: (i, j))],
        out_spec=pl.BlockSpec((k, tn), lambda j, i: (0, j)),
        out_shape=jax.ShapeDtypeStruct((k, n), out_dtype), acc_shape=(k, tn), steps=m // tm, rider=rider, name=name)


def permute_w_in(staged, *, name):
    k = staged.shape[1]
    own = IN_COLS // N_DEV
    tr = min(256, k)

    def body(i_ref, o_ref):
        for nb in range(N_COLBLK):
            dev, col = divmod(_orig_block_static(nb) * LANE, own)
            o_ref[:, nb * LANE:(nb + 1) * LANE] = i_ref[dev, :, col:col + LANE]

    return pl.pallas_call(
        body, grid=(k // tr,),
        in_specs=[pl.BlockSpec((N_DEV, tr, own), lambda i: (0, i, 0))],
        out_specs=pl.BlockSpec((tr, IN_COLS), lambda i: (i, 0)),
        out_shape=jax.ShapeDtypeStruct((k, IN_COLS), staged.dtype),
        compiler_params=_cparams(("arbitrary",)), name=name)(staged)


def unpermute_w_in(dw, *, name):
    k = dw.shape[0]
    own = IN_COLS // N_DEV
    tr = min(256, k)

    def body(i_ref, o_ref):
        for nb in range(N_COLBLK):
            dev, col = divmod(_orig_block_static(nb) * LANE, own)
            o_ref[dev, :, col:col + LANE] = i_ref[:, nb * LANE:(nb + 1) * LANE]

    return pl.pallas_call(
        body, grid=(k // tr,),
        in_specs=[pl.BlockSpec((tr, IN_COLS), lambda i: (i, 0))],
        out_specs=pl.BlockSpec((N_DEV, tr, own), lambda i: (0, i, 0)),
        out_shape=jax.ShapeDtypeStruct((N_DEV, k, own), dw.dtype),
        compiler_params=_cparams(("arbitrary",)), name=name)(dw)


def _row_tile(s):
    return min(256, s)


def _row_spec(t, w, col=0):
    return pl.BlockSpec((t, w), lambda i: (i, col))


def _vec_spec(w):
    return pl.BlockSpec((1, w), lambda i: (0, 0))


def prenorm_fwd(x, gain, shift, scale, *, name):
    s, d = x.shape
    t = _row_tile(s)

    def body(x_ref, g_ref, sh_ref, sc_ref, h_ref):
        xv = x_ref[...]
        r = lax.rsqrt(_rowmean(xv * xv) + EPS)
        h_ref[...] = ((xv * r) * g_ref[...] * (1.0 + sc_ref[...]) + sh_ref[...]).astype(h_ref.dtype)

    return pl.pallas_call(
        body, grid=(s // t,),
        in_specs=[_row_spec(t, d), _vec_spec(d), _vec_spec(d), _vec_spec(d)],
        out_specs=_row_spec(t, d), out_shape=jax.ShapeDtypeStruct((s, d), BF16),
        compiler_params=_cparams(("arbitrary",)), name=name)(x, gain, shift, scale)


def prenorm_bwd(dh, x, gain, scale, g_res, *, name):
    s, d = x.shape
    t = _row_tile(s)

    def body(dh_ref, x_ref, g_ref, sc_ref, gr_ref, dx_ref, dsh_ref, dsc_ref, dg_ref):
        i = pl.program_id(0)
        xv = x_ref[...]
        dhv = dh_ref[...]
        r = lax.rsqrt(_rowmean(xv * xv) + EPS)
        xn = xv * r
        gain_v = g_ref[...]
        one_sc = 1.0 + sc_ref[...]
        dyn = dhv * one_sc
        dxn = dyn * gain_v
        dx_ref[...] = r * (dxn - xn * _rowmean(dxn * xn)) + gr_ref[...]
        p_sh = _colsum(dhv)
        p_sc = _colsum(dhv * (xn * gain_v))
        p_g = _colsum(dyn * xn)

        @pl.when(i == 0)
        def _():
            dsh_ref[...] = p_sh
            dsc_ref[...] = p_sc
            dg_ref[...] = p_g

        @pl.when(i > 0)
        def _():
            dsh_ref[...] += p_sh
            dsc_ref[...] += p_sc
            dg_ref[...] += p_g

    vec = jax.ShapeDtypeStruct((1, d), F32)
    return pl.pallas_call(
        body, grid=(s // t,),
        in_specs=[_row_spec(t, d), _row_spec(t, d), _vec_spec(d), _vec_spec(d), _row_spec(t, d)],
        out_specs=[_row_spec(t, d), _vec_spec(d), _vec_spec(d), _vec_spec(d)],
        out_shape=[jax.ShapeDtypeStruct((s, d), F32), vec, vec, vec],
        compiler_params=_cparams(("arbitrary",)), name=name)(dh, x, gain, scale, g_res)


def postnorm_fwd(x, out, gain, gate, *, name):
    s, d = x.shape
    t = _row_tile(s)

    def body(x_ref, o_ref, g_ref, gt_ref, y_ref):
        ov = o_ref[...]
        r = lax.rsqrt(_rowmean(ov * ov) + EPS)
        y_ref[...] = x_ref[...] + gt_ref[...] * ((ov * r) * g_ref[...])

    return pl.pallas_call(
        body, grid=(s // t,),
        in_specs=[_row_spec(t, d), _row_spec(t, d), _vec_spec(d), _vec_spec(d)],
        out_specs=_row_spec(t, d), out_shape=jax.ShapeDtypeStruct((s, d), F32),
        compiler_params=_cparams(("arbitrary",)), name=name)(x, out, gain, gate)


def postnorm_bwd(g, out, gain, gate, *, name):
    s, d = out.shape
    t = _row_tile(s)

    def body(g_ref, o_ref, gn_ref, gt_ref, do_ref, dgt_ref, dgn_ref):
        i = pl.program_id(0)
        ov = o_ref[...]
        gv = g_ref[...]
        r = lax.rsqrt(_rowmean(ov * ov) + EPS)
        on = ov * r
        gain_v = gn_ref[...]
        gate_v = gt_ref[...]
        dn = gv * gate_v
        don = dn * gain_v
        do_ref[...] = (r * (don - on * _rowmean(don * on))).astype(do_ref.dtype)
        p_gt = _colsum(gv * (on * gain_v))
        p_gn = _colsum(dn * on)

        @pl.when(i == 0)
        def _():
            dgt_ref[...] = p_gt
            dgn_ref[...] = p_gn

        @pl.when(i > 0)
        def _():
            dgt_ref[...] += p_gt
            dgn_ref[...] += p_gn

    vec = jax.ShapeDtypeStruct((1, d), F32)
    return pl.pallas_call(
        body, grid=(s // t,),
        in_specs=[_row_spec(t, d), _row_spec(t, d), _vec_spec(d), _vec_spec(d)],
        out_specs=[_row_spec(t, d), _vec_spec(d), _vec_spec(d)],
        out_shape=[jax.ShapeDtypeStruct((s, d), BF16), vec, vec],
        compiler_params=_cparams(("arbitrary",)), name=name)(g, out, gain, gate)


def loss_head(y, target, *, name):
    s, d = y.shape
    t = _row_tile(s)
    steps = s // t

    def body(y_ref, t_ref, dy_ref, loss_ref, acc_ref):
        i = pl.program_id(0)
        err = y_ref[...] - t_ref[...]
        dy_ref[...] = err * (1.0 / d)
        part = _colsum(err * err)

        @pl.when(i == 0)
        def _():
            acc_ref[...] = part

        @pl.when(i > 0)
        def _():
            acc_ref[...] += part

        @pl.when(i == steps - 1)
        def _():
            loss_ref[...] = jnp.sum(acc_ref[...], axis=1, keepdims=True) * (0.5 / d)

    return pl.pallas_call(
        body, grid=(steps,),
        in_specs=[_row_spec(t, d), _row_spec(t, d)],
        out_specs=[_row_spec(t, d), pl.BlockSpec((1, 1), lambda i: (0, 0))],
        out_shape=[jax.ShapeDtypeStruct((s, d), F32), jax.ShapeDtypeStruct((1, 1), F32)],
        scratch_shapes=[pltpu.VMEM((1, d), F32)],
        compiler_params=_cparams(("arbitrary",)), name=name)(y, target)


def _full_spec(shape):
    return pl.BlockSpec(shape, lambda i: (0,) * len(shape))


def proj_gate_fwd(y_a, y_b, w_pa, w_pb, proj, *, name):
    s, width = y_a.shape
    d = w_pa.shape[1]
    t = _row_tile(s)

    def body(ya_ref, yb_ref, wa_ref, wb_ref, ga_ref, gb_ref, pa_ref, pb_ref, m_ref):
        pa = _dot(ya_ref[...], wa_ref[...])
        pb = _dot(yb_ref[...], wb_ref[...])
        pa_ref[...] = pa.astype(pa_ref.dtype)
        pb_ref[...] = pb.astype(pb_ref.dtype)
        m_ref[...] = (_sigmoid(ga_ref[...]) * pa + _sigmoid(gb_ref[...]) * pb).astype(m_ref.dtype)

    out = jax.ShapeDtypeStruct((s, d), BF16)
    return pl.pallas_call(
        body, grid=(s // t,),
        in_specs=[_row_spec(t, width), _row_spec(t, width), _full_spec((width, d)), _full_spec((width, d)),
                  _row_spec(t, d, 0), _row_spec(t, d, 1)],
        out_specs=[_row_spec(t, d)] * 3, out_shape=[out] * 3,
        compiler_params=_cparams(("arbitrary",)), name=name)(y_a, y_b, w_pa, w_pb, proj, proj)


def gate_bwd(dout, w_out, proj, pa, pb, *, name):
    s, d = pa.shape
    t = _row_tile(s)

    def body(do_ref, w_ref, ga_ref, gb_ref, pa_ref, pb_ref, dpa_ref, dpb_ref, dp_ref):
        dm = _dot_nt(do_ref[...], w_ref[...])
        sa = _sigmoid(ga_ref[...])
        sb = _sigmoid(gb_ref[...])
        dpa_ref[...] = (dm * sa).astype(dpa_ref.dtype)
        dpb_ref[...] = (dm * sb).astype(dpb_ref.dtype)
        dp_ref[:, :d] = (dm * pa_ref[...].astype(F32) * sa * (1.0 - sa)).astype(dp_ref.dtype)
        dp_ref[:, d:] = (dm * pb_ref[...].astype(F32) * sb * (1.0 - sb)).astype(dp_ref.dtype)

    return pl.pallas_call(
        body, grid=(s // t,),
        in_specs=[_row_spec(t, d), _full_spec((d, d)), _row_spec(t, d, 0), _row_spec(t, d, 1),
                  _row_spec(t, d), _row_spec(t, d)],
        out_specs=[_row_spec(t, d), _row_spec(t, d), _row_spec(t, 2 * d, 0)],
        out_shape=[jax.ShapeDtypeStruct((s, d), BF16), jax.ShapeDtypeStruct((s, d), BF16),
                   jax.ShapeDtypeStruct((s, IN_COLS), BF16)],
        compiler_params=_cparams(("arbitrary",)), name=name)(dout, w_out, proj, proj, pa, pb)


def _pool_tile(s):
    return min(256, s)


def pool_fwd(proj, pw, ps, *, name):
    s = proj.shape[0]
    t = _pool_tile(s)
    pool_blk = (GATE_COLS + HEADS * HEAD_COLS) // (len(POOL_WINDOWS) * POOL_COLS)

    def body(p_ref, halo_ref, pw_ref, ps_ref, yb_ref, pooled_ref, mixed_ref):
        i = pl.program_id(0)
        halo = jnp.where(i == 0, 0.0, halo_ref[...])
        row = i * t + lax.broadcasted_iota(jnp.int32, (t, 1), 0)
        for g, w in enumerate(POOL_WINDOWS):
            vb = p_ref[:, g * POOL_COLS:g * POOL_COLS + POOL_GW]
            zb = p_ref[:, g * POOL_COLS + POOL_GW:(g + 1) * POOL_COLS]
            acc = jnp.concatenate([halo[:, g * POOL_COLS:g * POOL_COLS + POOL_GW], vb], axis=0)
            sh = 1
            while sh < w:
                acc = acc + pltpu.roll(acc, sh, axis=0)
                sh *= 2
            cnt = jnp.minimum(row + 1, w).astype(F32)
            pooled = acc[HALO:, :] / cnt - vb
            mixed = _dot(pooled.astype(BF16), pw_ref[g])
            cols = slice(g * POOL_GW, (g + 1) * POOL_GW)
            yb = mixed * ps_ref[:, cols] * (zb * _sigmoid(zb))
            yb_ref[:, cols] = yb.astype(yb_ref.dtype)
            pooled_ref[:, cols] = pooled.astype(pooled_ref.dtype)
            mixed_ref[:, cols] = mixed

    wide = len(POOL_WINDOWS) * POOL_COLS
    return pl.pallas_call(
        body, grid=(s // t,),
        in_specs=[pl.BlockSpec((t, wide), lambda i: (i, pool_blk)),
                  pl.BlockSpec((HALO, wide), lambda i: (jnp.maximum(i * (t // HALO) - 1, 0), pool_blk)),
                  pl.BlockSpec((len(POOL_WINDOWS), POOL_GW, POOL_GW), lambda i: (0, 0, 0)),
                  _vec_spec(WIDTH)],
        out_specs=[_row_spec(t, WIDTH)] * 3,
        out_shape=[jax.ShapeDtypeStruct((s, WIDTH), BF16), jax.ShapeDtypeStruct((s, WIDTH), BF16),
                   jax.ShapeDtypeStruct((s, WIDTH), F32)],
        compiler_params=_cparams(("arbitrary",)), name=name)(proj, proj, pw, ps)


def pool_bwd(dyb, proj, pooled, mixed, pw, ps, dproj, *, name):
    s = proj.shape[0]
    t = _pool_tile(s)
    nblk = s // t
    ng = len(POOL_WINDOWS)
    wide = ng * POOL_COLS
    pool_blk = (GATE_COLS + HEADS * HEAD_COLS) // wide

    def body(dy_ref, p_ref, pooled_ref, mixed_ref, pw_ref, ps_ref, dp_any, dp_ref, dpw_ref, dps_ref, carry):
        del dp_any
        i = pl.program_id(0)
        ii = nblk - 1 - i

        @pl.when(i == 0)
        def _():
            carry[...] = jnp.zeros_like(carry)
            dpw_ref[...] = jnp.zeros_like(dpw_ref)
            dps_ref[...] = jnp.zeros_like(dps_ref)

        row = ii * t + lax.broadcasted_iota(jnp.int32, (t, 1), 0)
        for g, w in enumerate(POOL_WINDOWS):
            cols = slice(g * POOL_GW, (g + 1) * POOL_GW)
            zb = p_ref[:, g * POOL_COLS + POOL_GW:(g + 1) * POOL_COLS]
            dy = dy_ref[:, cols]
            mx = mixed_ref[:, cols]
            sc = ps_ref[:, cols]
            sg = _sigmoid(zb)
            dzb = dy * (mx * sc) * (sg * (1.0 + zb * (1.0 - sg)))
            dpm = dy * (zb * sg)
            dps_ref[:, cols] += _colsum(dpm * mx)
            dmixed = (dpm * sc).astype(BF16)
            dpooled = _dot_nt(dmixed, pw_ref[g])
            dpw_ref[g] += _dot_tn(pooled_ref[:, cols], dmixed)
            cnt = jnp.minimum(row + 1, w).astype(F32)
            u = dpooled / cnt
            acc = jnp.concatenate([u, carry[:, cols]], axis=0)
            sh = 1
            while sh < w:
                acc = acc + pltpu.roll(acc, t + HALO - sh, axis=0)
                sh *= 2
            carry[:, cols] = u[:HALO, :]
            dp_ref[:, g * POOL_COLS:g * POOL_COLS + POOL_GW] = (acc[:t, :] - dpooled).astype(dp_ref.dtype)
            dp_ref[:, g * POOL_COLS + POOL_GW:(g + 1) * POOL_COLS] = dzb.astype(dp_ref.dtype)

    rev = lambda i: (nblk - 1 - i, 0)
    return pl.pallas_call(
        body, grid=(nblk,),
        in_specs=[pl.BlockSpec((t, WIDTH), rev),
                  pl.BlockSpec((t, wide), lambda i: (nblk - 1 - i, pool_blk)),
                  pl.BlockSpec((t, WIDTH), rev), pl.BlockSpec((t, WIDTH), rev),
                  pl.BlockSpec((ng, POOL_GW, POOL_GW), lambda i: (0, 0, 0)),
                  _vec_spec(WIDTH),
                  pl.BlockSpec(memory_space=pl.ANY)],
        out_specs=[pl.BlockSpec((t, wide), lambda i: (nblk - 1 - i, pool_blk)),
                   pl.BlockSpec((ng, POOL_GW, POOL_GW), lambda i: (0, 0, 0)),
                   _vec_spec(WIDTH)],
        out_shape=[jax.ShapeDtypeStruct(dproj.shape, dproj.dtype),
                   jax.ShapeDtypeStruct((ng, POOL_GW, POOL_GW), F32),
                   jax.ShapeDtypeStruct((1, WIDTH), F32)],
        scratch_shapes=[pltpu.VMEM((HALO, WIDTH), F32)],
        input_output_aliases={6: 0},
        compiler_params=_cparams(("arbitrary",)), name=name)(dyb, proj, pooled, mixed, pw, ps, dproj)


def _hgrn_tile(s):
    return min(512, s)


def _chunk_consts():
    tt = lax.broadcasted_iota(jnp.int32, (CHUNK, CHUNK), 0)
    ss = lax.broadcasted_iota(jnp.int32, (CHUNK, CHUNK), 1)
    within = (ss <= tt) & (ss // SUB == tt // SUB)
    before = ss < (tt // SUB) * SUB
    cums = jnp.concatenate([within.astype(F32), before.astype(F32)], axis=0).astype(BF16)
    causal = ss <= tt
    upper = (ss >= tt).astype(F32).astype(BF16)
    row = lax.broadcasted_iota(jnp.int32, (CHUNK, 1), 0)
    return cums, causal, upper, row


def _dot_split(mat01, v):
    hi = v.astype(BF16)
    r1 = v - hi.astype(F32)
    mid = r1.astype(BF16)
    lo = (r1 - mid.astype(F32)).astype(BF16)
    return _dot(mat01, hi) + _dot(mat01, mid) + _dot(mat01, lo)


def _hgrn_chunk(qa, fa, lb, cums, row):
    sq = _sigmoid(qa)
    q = qa * sq
    sa = _sigmoid(fa)
    sna = 1.0 - sa
    oml = 1.0 - lb
    f = lb + oml * sa
    fc = jnp.maximum(f, MIN_FORGET)
    lf = jnp.log(fc)
    k = oml * sna
    cb = _dot_split(cums, lf)
    c = cb[:CHUNK]
    bt = cb[CHUNK:]
    ec = jnp.exp(c)
    enc = jnp.exp(jnp.minimum(-c, MAX_EXP))
    qt = q * ec
    kt = k * enc
    dms, lhs, rhs = [], [], []
    for j in range(N_SUB):
        bj = bt[j * SUB:j * SUB + 1, :]
        dm = jnp.where(row >= j * SUB, jnp.exp(jnp.minimum(bt - bj, 0.0)), 0.0)
        dms.append(dm)
        lhs.append(qt * dm)
        rhs.append(jnp.where(row // SUB == j, kt, 0.0))
    lhs = jnp.concatenate(lhs, axis=1).astype(BF16)
    rhs = jnp.concatenate(rhs, axis=1).astype(BF16)
    b = bt + c
    bl = b[CHUNK - 1:CHUNK, :]
    ebl = jnp.exp(bl)
    edec = jnp.exp(bl - b)
    eb = ec * dms[0]
    return dict(sq=sq, q=q, sa=sa, sna=sna, oml=oml, f=f, fc=fc, k=k, ec=ec, enc=enc, dms=dms,
                lhs=lhs, rhs=rhs, ebl=ebl, edec=edec, eb=eb, qd=q * eb, kdec=k * edec)


def _rider_mid_step(total):
    return total - min(32, total // 2)


def hgrn_fwd(proj, lb, hn, *, rider=None, name):
    s = proj.shape[0]
    t = _hgrn_tile(s)
    nblk = s // t
    ncht = t // CHUNK
    head_blk0 = GATE_COLS // HEAD_COLS
    nr = rider.n if rider else 0
    hbm = pl.BlockSpec(memory_space=pl.ANY)

    def body(*refs):
        p_ref, lb_ref, hn_ref = refs[:3]
        ya_ref, o_ref, st_ref = refs[3 + nr:6 + nr]
        state = refs[6 + 2 * nr]
        i = pl.program_id(1)
        if rider:
            total = HEADS * nblk
            rider.emit(pl.program_id(0) * nblk + i, total, _rider_mid_step(total),
                       refs[3:3 + nr], refs[6 + nr:6 + 2 * nr], refs[7 + 2 * nr:])

        @pl.when(i == 0)
        def _():
            state[...] = jnp.zeros_like(state)

        cums, causal, _, row = _chunk_consts()
        lbv = lb_ref[...]
        hnv = hn_ref[...]

        st = state[...]
        for ci in range(ncht):
            rows = slice(ci * CHUNK, (ci + 1) * CHUNK)
            qa = p_ref[rows, 0:HEAD_DIM]
            fa = p_ref[rows, HEAD_DIM:2 * HEAD_DIM]
            va = p_ref[rows, 2 * HEAD_DIM:3 * HEAD_DIM].astype(BF16)
            za = p_ref[rows, 3 * HEAD_DIM:4 * HEAD_DIM]
            pre = _hgrn_chunk(qa, fa, lbv, cums, row)
            stb = st.astype(BF16)
            st_ref[ci, 0] = stb
            a = jnp.where(causal, _dot_nt(pre["lhs"], pre["rhs"]), 0.0)
            o = _dot_nt(pre["qd"].astype(BF16), stb) + _dot(a.astype(BF16), va)
            st = st * pre["ebl"] + _dot_tn(va, pre["kdec"].astype(BF16))
            r = lax.rsqrt(_rowmean(o * o) + EPS)
            o_ref[rows, :] = o
            ya_ref[rows, :] = ((o * r) * hnv * (za * _sigmoid(za))).astype(ya_ref.dtype)
        state[...] = st

    return pl.pallas_call(
        body, grid=(HEADS, nblk),
        in_specs=[pl.BlockSpec((t, HEAD_COLS), lambda h, i: (i, head_blk0 + h)),
                  pl.BlockSpec((1, HEAD_DIM), lambda h, i: (0, h)),
                  pl.BlockSpec((1, HEAD_DIM), lambda h, i: (0, h))] + [hbm] * nr,
        out_specs=[pl.BlockSpec((t, HEAD_DIM), lambda h, i: (i, h)),
                   pl.BlockSpec((t, HEAD_DIM), lambda h, i: (i, h)),
                   pl.BlockSpec((ncht, 1, HEAD_DIM, HEAD_DIM), lambda h, i: (i, h, 0, 0))] + [hbm] * nr,
        out_shape=[jax.ShapeDtypeStruct((s, WIDTH), BF16), jax.ShapeDtypeStruct((s, WIDTH), F32),
                   jax.ShapeDtypeStruct((s // CHUNK, HEADS, HEAD_DIM, HEAD_DIM), BF16)]
        + (rider.out_shape if rider else []),
        scratch_shapes=[pltpu.VMEM((HEAD_DIM, HEAD_DIM), F32)] + (rider.scratch if rider else []),
        compiler_params=_cparams(("arbitrary", "arbitrary")), name=name)(proj, lb, hn, *(rider.arrs if rider else []))


def hgrn_bwd(dya, proj, o_all, states, lb, hn, dproj, *, rider=None, name):
    s = proj.shape[0]
    t = _hgrn_tile(s)
    nblk = s // t
    ncht = t // CHUNK
    head_blk0 = GATE_COLS // HEAD_COLS
    nr = rider.n if rider else 0
    hbm = pl.BlockSpec(memory_space=pl.ANY)

    def body(*refs):
        dy_ref, p_ref, o_ref, st_ref, lb_ref, hn_ref = refs[:6]
        dp_ref, dhn_ref, dlb_ref = refs[7 + nr:10 + nr]
        dstate = refs[10 + 2 * nr]
        i = pl.program_id(1)
        if rider:
            total = HEADS * nblk
            rider.emit(pl.program_id(0) * nblk + i, total, _rider_mid_step(total),
                       refs[7:7 + nr], refs[10 + nr:10 + 2 * nr], refs[11 + 2 * nr:])

        @pl.when(i == 0)
        def _():
            dstate[...] = jnp.zeros_like(dstate)
            dhn_ref[...] = jnp.zeros_like(dhn_ref)
            dlb_ref[...] = jnp.zeros_like(dlb_ref)

        cums, causal, upper, row = _chunk_consts()
        lbv = lb_ref[...]
        hnv = hn_ref[...]

        dst1 = dstate[...]
        dhn_acc = jnp.zeros_like(hnv)
        dlb_acc = jnp.zeros_like(lbv)
        for ci in reversed(range(ncht)):
            rows = slice(ci * CHUNK, (ci + 1) * CHUNK)
            qa = p_ref[rows, 0:HEAD_DIM]
            fa = p_ref[rows, HEAD_DIM:2 * HEAD_DIM]
            vb = p_ref[rows, 2 * HEAD_DIM:3 * HEAD_DIM].astype(BF16)
            za = p_ref[rows, 3 * HEAD_DIM:4 * HEAD_DIM]
            o = o_ref[rows, :]
            dy = dy_ref[rows, :]
            st0b = st_ref[ci, 0]
            r = lax.rsqrt(_rowmean(o * o) + EPS)
            on = o * r
            sgz = _sigmoid(za)
            sz = za * sgz
            dza = dy * on * hnv * (sgz * (1.0 + za * (1.0 - sgz)))
            dhn_acc = dhn_acc + _colsum(dy * on * sz)
            don = dy * hnv * sz
            do = r * (don - on * _rowmean(don * on))
            dob = do.astype(BF16)
            pre = _hgrn_chunk(qa, fa, lbv, cums, row)
            q, k = pre["q"], pre["k"]
            a = jnp.where(causal, _dot_nt(pre["lhs"], pre["rhs"]), 0.0)
            dst1b = dst1.astype(BF16)
            dq_inter = _dot(dob, st0b) * pre["eb"]
            da = jnp.where(causal, _dot_nt(dob, vb), 0.0).astype(BF16)
            dv = _dot_tn(a.astype(BF16), dob) + _dot_nt(pre["kdec"].astype(BF16), dst1b)
            dk_state = _dot(vb, dst1b) * pre["edec"]
            dlhs = _dot(da, pre["rhs"])
            drhs = _dot_tn(da, pre["lhs"])
            dq_a = jnp.zeros_like(q)
            dk_a = jnp.zeros_like(k)
            db = q * dq_inter - k * dk_state
            for j in range(N_SUB):
                cols = slice(j * HEAD_DIM, (j + 1) * HEAD_DIM)
                dq_a = dq_a + pre["dms"][j] * dlhs[:, cols]
                dk_a = dk_a + jnp.where(row // SUB == j, drhs[:, cols], 0.0)
                db = db + (pre["lhs"][:, cols].astype(F32) * dlhs[:, cols]
                           - pre["rhs"][:, cols].astype(F32) * drhs[:, cols])
            dq = dq_inter + pre["ec"] * dq_a
            dk = dk_state + pre["enc"] * dk_a
            dbl = _colsum(k * dk_state) + pre["ebl"] * _colsum(dst1 * st0b.astype(F32))
            dlf = _dot_split(upper, db) + dbl
            dst1 = dst1 * pre["ebl"] + _dot_tn(dob, pre["qd"].astype(BF16))
            sq, sa, sna, oml = pre["sq"], pre["sa"], pre["sna"], pre["oml"]
            dqa = dq * (sq * (1.0 + qa * (1.0 - sq)))
            dlf_f = jnp.where(pre["f"] >= MIN_FORGET, dlf / pre["fc"], 0.0)
            dfa = (dlf_f - dk) * (oml * sa * sna)
            dlb_acc = dlb_acc + _colsum((dlf_f - dk) * sna)
            dp_ref[rows, :] = jnp.concatenate([dqa, dfa, dv, dza], axis=1).astype(dp_ref.dtype)
        dstate[...] = dst1
        dhn_ref[...] += dhn_acc
        dlb_ref[...] += dlb_acc

    rev = lambda h, i: (nblk - 1 - i, h)
    return pl.pallas_call(
        body, grid=(HEADS, nblk),
        in_specs=[pl.BlockSpec((t, HEAD_DIM), rev),
                  pl.BlockSpec((t, HEAD_COLS), lambda h, i: (nblk - 1 - i, head_blk0 + h)),
                  pl.BlockSpec((t, HEAD_DIM), rev),
                  pl.BlockSpec((ncht, 1, HEAD_DIM, HEAD_DIM), lambda h, i: (nblk - 1 - i, h, 0, 0)),
                  pl.BlockSpec((1, HEAD_DIM), lambda h, i: (0, h)),
                  pl.BlockSpec((1, HEAD_DIM), lambda h, i: (0, h)),
                  hbm] + [hbm] * nr,
        out_specs=[pl.BlockSpec((t, HEAD_COLS), lambda h, i: (nblk - 1 - i, head_blk0 + h)),
                   pl.BlockSpec((1, HEAD_DIM), lambda h, i: (0, h)),
                   pl.BlockSpec((1, HEAD_DIM), lambda h, i: (0, h))] + [hbm] * nr,
        out_shape=[jax.ShapeDtypeStruct(dproj.shape, dproj.dtype),
                   jax.ShapeDtypeStruct((1, WIDTH), F32), jax.ShapeDtypeStruct((1, WIDTH), F32)]
        + (rider.out_shape if rider else []),
        scratch_shapes=[pltpu.VMEM((HEAD_DIM, HEAD_DIM), F32)] + (rider.scratch if rider else []),
        input_output_aliases={6: 0},
        compiler_params=_cparams(("arbitrary", "arbitrary")),
        name=name)(dya, proj, o_all, states, lb, hn, dproj, *(rider.arrs if rider else []))


def _softmax_rows(lower):
    mx = jnp.max(lower, axis=0, keepdims=True)
    e = jnp.exp(lower - mx)
    return e / jnp.sum(e, axis=0, keepdims=True)


def lb_table(lower, *, name):
    depth, w = lower.shape

    def body(l_ref, o_ref):
        sm = _softmax_rows(l_ref[...])
        acc = jnp.zeros((1, w), F32)
        o_ref[0:1, :] = acc
        for l in range(1, depth):
            acc = acc + sm[l:l + 1, :]
            o_ref[l:l + 1, :] = acc

    return pl.pallas_call(body, out_shape=jax.ShapeDtypeStruct((depth, w), F32), name=name)(lower)


def lb_table_bwd(lower, dlb, *, name):
    depth, w = lower.shape

    def body(l_ref, d_ref, o_ref):
        sm = _softmax_rows(l_ref[...])
        dlbv = d_ref[...]
        dsm = [jnp.zeros((1, w), F32)]
        for i in range(1, depth):
            acc = jnp.zeros((1, w), F32)
            for l in range(i, depth):
                acc = acc + dlbv[l:l + 1, :]
            dsm.append(acc)
        inner = jnp.zeros((1, w), F32)
        for i in range(depth):
            inner = inner + sm[i:i + 1, :] * dsm[i]
        for i in range(depth):
            o_ref[i:i + 1, :] = sm[i:i + 1, :] * (dsm[i] - inner)

    return pl.pallas_call(body, out_shape=jax.ShapeDtypeStruct((depth, w), F32), name=name)(lower, dlb)


def w_ada_grad(c_all, dmod_cols, *, name):
    depth, _, cols = dmod_cols.shape
    d = c_all.shape[1]

    def body(c_ref, dm_ref, o_ref):
        cv = c_ref[...]
        ca = cv * _sigmoid(cv)
        o_ref[...] = _dot_tn(ca, dm_ref[...])

    return pl.pallas_call(
        body, grid=(depth,),
        in_specs=[pl.BlockSpec((N_DEV, d), lambda l: (0, 0)), pl.BlockSpec((None, N_DEV, cols), lambda l: (l, 0, 0))],
        out_specs=pl.BlockSpec((None, d, cols), lambda l: (l, 0, 0)),
        out_shape=jax.ShapeDtypeStruct((depth, d, cols), F32),
        compiler_params=_cparams(("arbitrary",)), name=name)(c_all, dmod_cols)


def sum_parts(parts, *, name):
    p, r, c = parts.shape

    def body(p_ref, o_ref):
        acc = p_ref[0]
        for j in range(1, p):
            acc = acc + p_ref[j]
        o_ref[...] = acc

    return pl.pallas_call(body, out_shape=jax.ShapeDtypeStruct((r, c), F32), name=name)(parts)


def _adam_rows(r, c):
    tr = r
    while tr * c * 4 > (1 << 20) and tr % 16 == 0:
        tr //= 2
    return tr


def _adam_update(w_ref, m_ref, v_ref, g_ref, go_ref, d_ref, mo_ref, vo_ref):
    g = g_ref[0].astype(F32)
    for j in range(1, g_ref.shape[0]):
        g = g + g_ref[j].astype(F32)
    mn = ADAM_B1 * m_ref[...] + (1.0 - ADAM_B1) * g
    vn = ADAM_B2 * v_ref[...] + (1.0 - ADAM_B2) * (g * g)
    m_hat = mn / (1.0 - ADAM_B1 ** ADAM_STEP)
    v_hat = vn / (1.0 - ADAM_B2 ** ADAM_STEP)
    go_ref[...] = g
    d_ref[...] = -ADAM_LR * (m_hat / (jnp.sqrt(v_hat) + ADAM_EPS) + ADAM_WD * w_ref[...])
    mo_ref[...] = mn
    vo_ref[...] = vn


def adamw(w, m, v, gparts, *, name):
    r, c = w.shape
    p = gparts.shape[0]
    tr = _adam_rows(r, c)
    spec = pl.BlockSpec((tr, c), lambda i: (i, 0))
    shp = jax.ShapeDtypeStruct((r, c), F32)
    return pl.pallas_call(
        functools.partial(_adam_update), grid=(r // tr,),
        in_specs=[spec, spec, spec, pl.BlockSpec((p, tr, c), lambda i: (0, i, 0))],
        out_specs=[spec] * 4, out_shape=[shp] * 4,
        compiler_params=_cparams(("arbitrary",)), name=name)(w, m, v, gparts)


def adamw_layers(w, m, v, gparts, *, name):
    depth, r, c = w.shape
    p = gparts[0].shape[0]
    tr = _adam_rows(r, c)

    def body(w_ref, m_ref, v_ref, *rest):
        g_refs, outs = rest[:depth], rest[depth:]
        layer = pl.program_id(0)
        for k in range(depth):
            @pl.when(layer == k)
            def _(k=k):
                _adam_update(w_ref, m_ref, v_ref, g_refs[k], *outs)

    spec = pl.BlockSpec((None, tr, c), lambda l, i: (l, i, 0))
    g_specs = [pl.BlockSpec((p, tr, c), functools.partial(lambda l, i, k: (0, jnp.where(l == k, i, 0), 0), k=k))
               for k in range(depth)]
    shp = jax.ShapeDtypeStruct((depth, r, c), F32)
    return pl.pallas_call(
        body, grid=(depth, r // tr),
        in_specs=[spec, spec, spec] + g_specs,
        out_specs=[spec] * 4, out_shape=[shp] * 4,
        compiler_params=_cparams(("arbitrary", "arbitrary")), name=name)(w, m, v, *gparts)


def _position():
    x, y, c = lax.axis_index("x"), lax.axis_index("y"), lax.axis_index("c")
    return x, y, c


def _dev_index(x, y, c):
    return 4 * x + 2 * y + c


def _gather_phases(ins, outs, send_sems, recv_sems, local_sems):
    n = len(ins)
    x, y, c = _position()
    me, sibling = (x, y, c), (x, y, 1 - c)
    chips = [(1 - x, y), (x, 1 - y), (1 - x, 1 - y)]

    def copy(a, k, block, to, own=False):
        slot = outs[a].at[_dev_index(*block)]
        return pltpu.make_async_remote_copy(
            src_ref=ins[a] if own else slot, dst_ref=slot,
            send_sem=send_sems.at[a * 7 + k], recv_sem=recv_sems.at[a * 7 + k],
            device_id=to, device_id_type=MESH)

    def mine(a):
        return pltpu.make_async_copy(ins[a], outs[a].at[_dev_index(*me)], local_sems.at[a])

    def first(a):
        return [copy(a, 0, me, sibling, True)] + [copy(a, 1 + j, me, (*chip, c), True) for j, chip in enumerate(chips)]

    def passed(a):
        return [copy(a, 4 + j, (*chip, c), sibling) for j, chip in enumerate(chips)]

    def start():
        for a in range(n):
            mine(a).start()
        for a in range(n):
            for cp in first(a):
                cp.start()

    def mid():
        for j, chip in enumerate(chips):
            for a in range(n):
                copy(a, 1 + j, (*chip, c), me).wait_recv()
                passed(a)[j].start()

    def finish():
        for a in range(n):
            copy(a, 0, sibling, me).wait_recv()
            for j, chip in enumerate(chips):
                copy(a, 4 + j, (*chip, 1 - c), me).wait_recv()
        for a in range(n):
            for cp in first(a) + passed(a):
                cp.wait_send()
            mine(a).wait()

    return start, mid, finish


def _scatter_phases(ins, outs, send_sems, recv_sems, local_sems):
    n = len(ins)
    x, y, c = _position()
    me = _dev_index(x, y, c)

    def peer(r):
        return (x ^ (r >> 2), y ^ ((r >> 1) & 1), c ^ (r & 1))

    def copy(a, r):
        to = peer(r)
        return pltpu.make_async_remote_copy(
            src_ref=ins[a].at[_dev_index(*to)], dst_ref=outs[a].at[me],
            send_sem=send_sems.at[a * 7 + r - 1], recv_sem=recv_sems.at[a * 7 + r - 1],
            device_id=to, device_id_type=MESH)

    def arrival(a, r):
        return pltpu.make_async_remote_copy(
            src_ref=ins[a].at[me], dst_ref=outs[a].at[_dev_index(*peer(r))],
            send_sem=send_sems.at[a * 7 + r - 1], recv_sem=recv_sems.at[a * 7 + r - 1],
            device_id=peer(r), device_id_type=MESH)

    def mine(a):
        return pltpu.make_async_copy(ins[a].at[me], outs[a].at[me], local_sems.at[a])

    def start():
        for a in range(n):
            mine(a).start()
        for r in range(1, N_DEV):
            for a in range(n):
                copy(a, r).start()

    def finish():
        for r in range(1, N_DEV):
            for a in range(n):
                arrival(a, r).wait_recv()
        for r in range(1, N_DEV):
            for a in range(n):
                copy(a, r).wait_send()
        for a in range(n):
            mine(a).wait()

    return start, None, finish


class Rider:
    def __init__(self, kind, arrs):
        self.kind, self.arrs, self.n = kind, list(arrs), len(arrs)
        lead = (N_DEV,) if kind == "gather" else ()
        self.out_shape = [jax.ShapeDtypeStruct(lead + a.shape, a.dtype) for a in self.arrs]
        self.scratch = [pltpu.SemaphoreType.DMA((7 * self.n,)), pltpu.SemaphoreType.DMA((7 * self.n,)),
                        pltpu.SemaphoreType.DMA((self.n,))]

    def phases(self, ins, outs, sems):
        make = _gather_phases if self.kind == "gather" else _scatter_phases
        return make(ins, outs, *sems)

    def emit(self, step, total, mid_step, ins, outs, sems):
        start, mid, finish = self.phases(ins, outs, sems)
        pl.when(step == 0)(start)
        if mid is not None:
            pl.when(step == mid_step)(mid)
        pl.when(step == total - 1)(finish)


def _standalone(rider, name):
    n = rider.n
    hbm = pl.BlockSpec(memory_space=pl.ANY)

    def body(*refs):
        start, mid, finish = rider.phases(refs[:n], refs[n:2 * n], refs[2 * n:])
        start()
        if mid is not None:
            mid()
        finish()

    return pl.pallas_call(body, out_shape=rider.out_shape, in_specs=[hbm] * n, out_specs=[hbm] * n,
                          scratch_shapes=rider.scratch, name=name)(*rider.arrs)


def all_gather(arrs, *, name):
    return _standalone(Rider("gather", arrs), name)


def scatter_parts(arrs, *, name):
    return _standalone(Rider("scatter", arrs), name)


def mod_exchange(c_all, w_ada, b_cols, *, name):
    depth, d, cols = w_ada.shape
    hbm = pl.BlockSpec(memory_space=pl.ANY)
    vmem = pl.BlockSpec(memory_space=pltpu.VMEM)

    def body(c_ref, w_ref, b_ref, out_ref, wbuf, sendbuf, send_sems, recv_sems, load_sem):
        x, y, c = _position()
        me = _dev_index(x, y, c)
        cv = c_ref[...]
        ca = cv * _sigmoid(cv)
        for l in range(depth):
            load = pltpu.make_async_copy(w_ref.at[l], wbuf, load_sem)
            load.start()
            load.wait()
            part = jnp.dot(ca, wbuf[...], preferred_element_type=F32,
                           precision=lax.Precision.HIGHEST) + b_ref[l:l + 1, :]
            for bi in range(N_DEV):
                sendbuf[bi, l:l + 1, :] = part[bi:bi + 1, :]

        def peer(r):
            return (x ^ (r >> 2), y ^ ((r >> 1) & 1), c ^ (r & 1))

        def copy(r):
            to = peer(r)
            return pltpu.make_async_remote_copy(
                src_ref=sendbuf.at[_dev_index(*to)], dst_ref=out_ref.at[me],
                send_sem=send_sems.at[r - 1], recv_sem=recv_sems.at[r - 1],
                device_id=to, device_id_type=MESH)

        def arrival(r):
            return pltpu.make_async_remote_copy(
                src_ref=sendbuf.at[me], dst_ref=out_ref.at[_dev_index(*peer(r))],
                send_sem=send_sems.at[r - 1], recv_sem=recv_sems.at[r - 1],
                device_id=peer(r), device_id_type=MESH)

        out_ref[me] = sendbuf[me]
        sends = [copy(r) for r in range(1, N_DEV)]
        for cp in sends:
            cp.start()
        for r in range(1, N_DEV):
            arrival(r).wait_recv()
        for cp in sends:
            cp.wait_send()

    return pl.pallas_call(
        body,
        out_shape=jax.ShapeDtypeStruct((N_DEV, depth, cols), F32),
        in_specs=[vmem, hbm, vmem], out_specs=vmem,
        scratch_shapes=[pltpu.VMEM((d, cols), F32), pltpu.VMEM((N_DEV, depth, cols), F32),
                        pltpu.SemaphoreType.DMA((7,)), pltpu.SemaphoreType.DMA((7,)), pltpu.SemaphoreType.DMA],
        compiler_params=pltpu.CompilerParams(vmem_limit_bytes=VMEM_LIMIT),
        name=name)(c_all, w_ada, b_cols)


def kernel(x, c, w_ada, b_ada, norm_pre, norm_post, w_in, lower_bounds, hgrn_norm, pool_w, pool_scale, w_proj_a, w_proj_b, w_out, loss_target, m_w_ada, m_b_ada, m_norm_pre, m_norm_post, m_w_in, m_lower_bounds, m_hgrn_norm, m_pool_w, m_pool_scale, m_w_proj_a, m_w_proj_b, m_w_out, v_w_ada, v_b_ada, v_norm_pre, v_norm_post, v_w_in, v_lower_bounds, v_hgrn_norm, v_pool_w, v_pool_scale, v_w_proj_a, v_w_proj_b, v_w_out):
    depth = w_in.shape[0]
    d = D_MODEL
    ada_cols = w_ada.shape[2]
    xi, yi, ci = _position()
    me = _dev_index(xi, yi, ci)
    xs = x[0]
    target = loss_target[0]
    ng = len(POOL_WINDOWS)

    def shards(l):
        return [w_in[l].astype(BF16), w_proj_a[l].astype(BF16), w_proj_b[l].astype(BF16),
                w_out[l].astype(BF16), pool_w[l].astype(BF16)]

    def other_weights(g_pa, g_pb, g_out, g_pool):
        return dict(
            pa=jnp.transpose(g_pa, (1, 0, 2)).reshape(WIDTH, d),
            pb=jnp.transpose(g_pb, (1, 0, 2)).reshape(WIDTH, d),
            w_out=g_out.reshape(d, d),
            pool=jnp.transpose(g_pool, (1, 0, 2, 3)).reshape(ng, POOL_GW, POOL_GW))

    (g_in0,) = all_gather(shards(0)[:1], name="gather_w_in")
    w_in_full = [permute_w_in(g_in0, name="permute_w_in")]
    others_full = []
    gathered = []

    (c_all,) = all_gather([c], name="gather_c")
    c_all = c_all.reshape(N_DEV, d)
    b_cols = lax.dynamic_slice_in_dim(b_ada, me * ada_cols, ada_cols, axis=1)
    mod_parts = mod_exchange(c_all, w_ada, b_cols, name="mod_exchange")
    mod = jnp.transpose(mod_parts, (1, 0, 2)).reshape(depth, 3 * d)
    lb_all = lb_table(lower_bounds, name="lb_table")

    saved = []
    cur = xs
    for l in range(depth):
        shift, scale, gate = mod[l:l + 1, :d], mod[l:l + 1, d:2 * d], mod[l:l + 1, 2 * d:]
        h = prenorm_fwd(cur, norm_pre[l:l + 1], shift, scale, name="prenorm_fwd")
        ride = (shards(0)[1:] if l == 0 else []) + (shards(l + 1)[:1] if l + 1 < depth else [])
        if ride:
            proj, *got = matmul_nn(h, w_in_full[l], tm=1024, tn=1024, out_dtype=F32,
                                   rider=Rider("gather", ride), name="mm_w_in_gather%d" % len(ride))
            if l == 0:
                others_full.append(other_weights(*got[:4]))
                got = got[4:]
            if got:
                w_in_full.append(permute_w_in(got[0], name="permute_w_in"))
        else:
            proj = matmul_nn(h, w_in_full[l], tm=1024, tn=1024, out_dtype=F32, name="mm_w_in")
        if l + 1 < depth:
            y_a, o_all, states, *got = hgrn_fwd(proj, lb_all[l:l + 1], hgrn_norm[l:l + 1],
                                                 rider=Rider("gather", shards(l + 1)[1:]), name="hgrn_fwd_gather")
            others_full.append(other_weights(*got))
        else:
            y_a, o_all, states = hgrn_fwd(proj, lb_all[l:l + 1], hgrn_norm[l:l + 1], name="hgrn_fwd")
        w = dict(w_in=w_in_full[l], **others_full[l])
        gathered.append(w)
        y_b, pooled, mixed = pool_fwd(proj, w["pool"], pool_scale[l:l + 1], name="pool_fwd")
        pa, pb, merged = proj_gate_fwd(y_a, y_b, w["pa"], w["pb"], proj, name="proj_gate_fwd")
        out = matmul_nn(merged, w["w_out"], tm=1024, tn=1024, out_dtype=F32, name="mm_w_out")
        nxt = postnorm_fwd(cur, out, norm_post[l:l + 1], gate, name="postnorm_fwd")
        saved.append(dict(x=cur, h=h, proj=proj, y_a=y_a, o=o_all, states=states, y_b=y_b, pooled=pooled,
                          mixed=mixed, pa=pa, pb=pb, merged=merged, out=out, scale=scale, gate=gate))
        cur = nxt

    g, loss_part = loss_head(cur, target, name="loss_head")
    loss = lax.psum(loss_part[0, 0], ("x", "y", "c"))

    small = [None] * depth
    big = [None] * depth
    pending = None
    for l in reversed(range(depth)):
        w, sv = gathered[l], saved[l]
        dout, dgate, dnpost = postnorm_bwd(g, sv["out"], norm_post[l:l + 1], sv["gate"], name="postnorm_bwd")
        dw_out = matmul_tn(sv["merged"], dout, tm=2048, tn=1024, out_dtype=BF16, name="mm_w_out_dw")
        dpa, dpb, dproj = gate_bwd(dout, w["w_out"], sv["proj"], sv["pa"], sv["pb"], name="gate_bwd")
        dya = matmul_nt(dpa, w["pa"], tm=1024, tn=2048, out_dtype=F32, name="mm_proj_a_dx")
        dyb = matmul_nt(dpb, w["pb"], tm=1024, tn=2048, out_dtype=F32, name="mm_proj_b_dx")
        dw_pa = matmul_tn(sv["y_a"], dpa, tm=2048, tn=2048, out_dtype=BF16, name="mm_proj_a_dw")
        dw_pb = matmul_tn(sv["y_b"], dpb, tm=2048, tn=2048, out_dtype=BF16, name="mm_proj_b_dw")
        dproj, dpool_w, dpool_scale = pool_bwd(dyb, sv["proj"], sv["pooled"], sv["mixed"], w["pool"],
                                               pool_scale[l:l + 1], dproj, name="pool_bwd")
        if pending is None:
            dproj, dhn, dlb = hgrn_bwd(dya, sv["proj"], sv["o"], sv["states"], lb_all[l:l + 1],
                                       hgrn_norm[l:l + 1], dproj, name="hgrn_bwd")
        else:
            dproj, dhn, dlb, *recv = hgrn_bwd(dya, sv["proj"], sv["o"], sv["states"], lb_all[l:l + 1],
                                              hgrn_norm[l:l + 1], dproj, rider=Rider("scatter", pending),
                                              name="hgrn_bwd_scatter")
            big[l + 1] = recv
        by_owner = lambda t: jnp.transpose(t.reshape(WIDTH, N_DEV, d // N_DEV), (1, 0, 2))
        others = [by_owner(dw_pa), by_owner(dw_pb), dw_out.reshape(N_DEV, d // N_DEV, d),
                  jnp.transpose(dpool_w.astype(BF16).reshape(ng, N_DEV, POOL_GW // N_DEV, POOL_GW), (1, 0, 2, 3))]
        if l > 0:
            dh = matmul_nt(dproj, w["w_in"], tm=512, tn=2048, out_dtype=F32, name="mm_w_in_dx")
            dw_in = matmul_tn(sv["h"], dproj, tm=2048, tn=1024, out_dtype=BF16, name="mm_w_in_dw")
            pending = [unpermute_w_in(dw_in, name="unpermute_w_in")] + others
        else:
            dw_in, *recv_others = matmul_tn(sv["h"], dproj, tm=2048, tn=1024, out_dtype=BF16,
                                            rider=Rider("scatter", others), name="mm_w_in_dw_scatter")
            dh, recv_in = matmul_nt(dproj, w["w_in"], tm=512, tn=2048, out_dtype=F32,
                                    rider=Rider("scatter", [unpermute_w_in(dw_in, name="unpermute_w_in")]),
                                    name="mm_w_in_dx_scatter")
            big[0] = [recv_in] + recv_others
        g, dshift, dscale, dnpre = prenorm_bwd(dh, sv["x"], norm_pre[l:l + 1], sv["scale"], g, name="prenorm_bwd")
        small[l] = jnp.concatenate([dshift, dscale, dgate, dnpre, dnpost, dlb, dhn, dpool_scale], axis=1)
    grad_x = g[None]

    small_mine = jnp.concatenate(small, axis=0)
    (small_all,) = all_gather([small_mine], name="gather_small")
    small_sum = sum_parts(small_all, name="sum_small")
    dmod_all = small_all[:, :, :3 * d]
    dmod_cols = jnp.transpose(lax.dynamic_slice_in_dim(dmod_all, me * ada_cols, ada_cols, axis=2), (1, 0, 2))
    g_w_ada = w_ada_grad(c_all, dmod_cols, name="w_ada_grad")
    off = 3 * d
    g_b_ada = small_sum[:, :off]
    g_npre = small_sum[:, off:off + d]
    g_npost = small_sum[:, off + d:off + 2 * d]
    g_lb_tab = small_sum[:, off + 2 * d:off + 2 * d + WIDTH]
    g_hn = small_sum[:, off + 2 * d + WIDTH:off + 2 * d + 2 * WIDTH]
    g_ps = small_sum[:, off + 2 * d + 2 * WIDTH:]
    g_lower = lb_table_bwd(lower_bounds, g_lb_tab, name="lb_table_bwd")

    def update(wt, mt, vt, gparts, shape2, name):
        outs = adamw(wt.reshape(shape2), mt.reshape(shape2), vt.reshape(shape2), gparts, name=name)
        return [o.reshape(wt.shape) for o in outs]

    def update_layers(wt, mt, vt, kind, name):
        shape3 = (depth, -1, wt.shape[-1])
        w3 = wt.reshape(shape3)
        gps = [big[l][kind].reshape((N_DEV,) + w3.shape[1:]) for l in range(depth)]
        outs = adamw_layers(w3, mt.reshape(shape3), vt.reshape(shape3), gps, name=name)
        return [o.reshape(wt.shape) for o in outs]

    def update_small(wt, mt, vt, gt, name):
        shape2 = (-1, wt.shape[-1])
        return update(wt, mt, vt, gt.reshape(shape2)[None], shape2, name)

    res = {
        "w_ada": update_small(w_ada, m_w_ada, v_w_ada, g_w_ada, "adamw_w_ada"),
        "b_ada": update_small(b_ada, m_b_ada, v_b_ada, g_b_ada, "adamw_b_ada"),
        "norm_pre": update_small(norm_pre, m_norm_pre, v_norm_pre, g_npre, "adamw_norm_pre"),
        "norm_post": update_small(norm_post, m_norm_post, v_norm_post, g_npost, "adamw_norm_post"),
        "w_in": update_layers(w_in, m_w_in, v_w_in, 0, "adamw_w_in"),
        "lower_bounds": update_small(lower_bounds, m_lower_bounds, v_lower_bounds, g_lower, "adamw_lower_bounds"),
        "hgrn_norm": update_small(hgrn_norm, m_hgrn_norm, v_hgrn_norm, g_hn, "adamw_hgrn_norm"),
        "pool_w": update_layers(pool_w, m_pool_w, v_pool_w, 4, "adamw_pool_w"),
        "pool_scale": update_small(pool_scale, m_pool_scale, v_pool_scale, g_ps, "adamw_pool_scale"),
        "w_proj_a": update_layers(w_proj_a, m_w_proj_a, v_w_proj_a, 1, "adamw_w_proj_a"),
        "w_proj_b": update_layers(w_proj_b, m_w_proj_b, v_w_proj_b, 2, "adamw_w_proj_b"),
        "w_out": update_layers(w_out, m_w_out, v_w_out, 3, "adamw_w_out"),
    }
    order = ["w_ada", "b_ada", "norm_pre", "norm_post", "w_in", "lower_bounds", "hgrn_norm", "pool_w",
             "pool_scale", "w_proj_a", "w_proj_b", "w_out"]
    outs = [loss, grad_x]
    for k in range(4):
        outs += [res[nm][k] for nm in order]
    return tuple(outs)
```

```python
import functools

import jax
import jax.numpy as jnp
from jax import lax
from jax.experimental import pallas as pl
from jax.experimental.pallas import tpu as pltpu

F32 = jnp.float32
BF16 = jnp.bfloat16
MESH = pl.DeviceIdType.MESH

N_DEV = 8
EPS = 1e-6
MIN_FORGET = 1e-30
D_MODEL = 2048
HEADS = 8
HEAD_DIM = 128
CHUNK = 64
SUB = 16
N_SUB = CHUNK // SUB
WIDTH = 1024
POOL_WINDOWS = (2, 4, 8, 16)
POOL_GW = 256
HALO = 16
IN_COLS = 10240
LANE = 128
N_COLBLK = IN_COLS // LANE
GATE_COLS = 4096
HEAD_COLS = 4 * HEAD_DIM
POOL_COLS = 2 * POOL_GW
MAX_EXP = 80.0

ADAM_LR = 0.001
ADAM_B1 = 0.9
ADAM_B2 = 0.999
ADAM_EPS = 1e-08
ADAM_WD = 0.01
ADAM_STEP = 10

VMEM_LIMIT = 56 * 1024 * 1024


def _cparams(sem=None):
    return pltpu.CompilerParams(dimension_semantics=sem, vmem_limit_bytes=VMEM_LIMIT)


def _sigmoid(v):
    return 1.0 / (1.0 + jnp.exp(-v))


def _dot(a, b):
    return jnp.dot(a, b, preferred_element_type=F32)


def _dot_nt(a, b):
    return lax.dot_general(a, b, (((1,), (1,)), ((), ())), preferred_element_type=F32)


def _dot_tn(a, b):
    return lax.dot_general(a, b, (((0,), (0,)), ((), ())), preferred_element_type=F32)


def _colsum(v):
    return jnp.sum(v, axis=0, keepdims=True)


def _rowmean(v):
    return jnp.mean(v, axis=-1, keepdims=True)


def _orig_block_static(n):
    if n < 32:
        return n + 48
    if n < 64:
        m = n - 32
        return 8 * (m % 4) + m // 4
    m = n - 64
    t = m % 4
    return 32 + 2 * (m // 4) + (t % 2) + 8 * (t // 2)


def _accumulate(step, steps, prod, o_ref, acc_ref):
    if steps == 1:
        o_ref[...] = prod.astype(o_ref.dtype)
        return

    @pl.when(step == 0)
    def _():
        acc_ref[...] = prod

    @pl.when(step > 0)
    def _():
        acc_ref[...] += prod

    @pl.when(step == steps - 1)
    def _():
        o_ref[...] = acc_ref[...].astype(o_ref.dtype)


def _matmul_call(dot, a, b, *, grid, in_specs, out_spec, out_shape, acc_shape, steps, rider, name):
    nr = rider.n if rider else 0
    hbm = pl.BlockSpec(memory_space=pl.ANY)
    has_acc = steps > 1

    def body(*refs):
        a_ref, b_ref = refs[:2]
        o_ref = refs[2 + nr]
        scratch = refs[3 + 2 * nr:]
        if rider:
            total = grid[0] * grid[1]
            rider.emit(pl.program_id(0) * grid[1] + pl.program_id(1), total, _rider_mid_step(total),
                       refs[2:2 + nr], refs[3 + nr:3 + 2 * nr], scratch[1 if has_acc else 0:])
        _accumulate(pl.program_id(1), steps, dot(a_ref[...], b_ref[...]), o_ref, scratch[0] if has_acc else None)

    outs = pl.pallas_call(
        body, grid=grid,
        in_specs=in_specs + [hbm] * nr, out_specs=[out_spec] + [hbm] * nr,
        out_shape=[out_shape] + (rider.out_shape if rider else []),
        scratch_shapes=([pltpu.VMEM(acc_shape, F32)] if has_acc else []) + (rider.scratch if rider else []),
        compiler_params=_cparams(("arbitrary", "arbitrary")),
        name=name)(a, b, *(rider.arrs if rider else []))
    return outs if rider else outs[0]


def matmul_nn(a, b, *, tm, tn, out_dtype, name, rider=None):
    m, k = a.shape
    n = b.shape[1]
    tm = min(tm, m)
    return _matmul_call(
        _dot, a, b, grid=(n // tn, m // tm),
        in_specs=[pl.BlockSpec((tm, k), lambda j, i: (i, 0)), pl.BlockSpec((k, tn), lambda j, i: (0, j))],
        out_spec=pl.BlockSpec((tm, tn), lambda j, i: (i, j)),
        out_shape=jax.ShapeDtypeStruct((m, n), out_dtype), acc_shape=None, steps=1, rider=rider, name=name)


def matmul_nt(a, b, *, tm, tn, out_dtype, name, rider=None):
    m, n = a.shape
    k = b.shape[0]
    tm = min(tm, m)
    return _matmul_call(
        _dot_nt, a, b, grid=(m // tm, n // tn),
        in_specs=[pl.BlockSpec((tm, tn), lambda i, j: (i, j)), pl.BlockSpec((k, tn), lambda i, j: (0, j))],
        out_spec=pl.BlockSpec((tm, k), lambda i, j: (i, 0)),
        out_shape=jax.ShapeDtypeStruct((m, k), out_dtype), acc_shape=(tm, k), steps=n // tn, rider=rider, name=name)


def matmul_tn(a, b, *, tm, tn, out_dtype, name, rider=None):
    m, k = a.shape
    n = b.shape[1]
    tm = min(tm, m)
    return _matmul_call(
        _dot_tn, a, b, grid=(n // tn, m // tm),
        in_specs=[pl.BlockSpec((tm, k), lambda j, i: (i, 0)), pl.BlockSpec((tm, tn), lambda j, i: (i, j))],
        out_spec=pl.BlockSpec((k, tn), lambda j, i: (0, j)),
        out_shape=jax.ShapeDtypeStruct((k, n), out_dtype), acc_shape=(k, tn), steps=m // tm, rider=rider, name=name)


def permute_w_in(staged, *, name):
    k = staged.shape[1]
    own = IN_COLS // N_DEV
    tr = min(256, k)

    def body(i_ref, o_ref):
        for nb in range(N_COLBLK):
            dev, col = divmod(_orig_block_static(nb) * LANE, own)
            o_ref[:, nb * LANE:(nb + 1) * LANE] = i_ref[dev, :, col:col + LANE]

    return pl.pallas_call(
        body, grid=(k // tr,),
        in_specs=[pl.BlockSpec((N_DEV, tr, own), lambda i: (0, i, 0))],
        out_specs=pl.BlockSpec((tr, IN_COLS), lambda i: (i, 0)),
        out_shape=jax.ShapeDtypeStruct((k, IN_COLS), staged.dtype),
        compiler_params=_cparams(("arbitrary",)), name=name)(staged)


def unpermute_w_in(dw, *, name):
    k = dw.shape[0]
    own = IN_COLS // N_DEV
    tr = min(256, k)

    def body(i_ref, o_ref):
        for nb in range(N_COLBLK):
            dev, col = divmod(_orig_block_static(nb) * LANE, own)
            o_ref[dev, :, col:col + LANE] = i_ref[:, nb * LANE:(nb + 1) * LANE]

    return pl.pallas_call(
        body, grid=(k // tr,),
        in_specs=[pl.BlockSpec((tr, IN_COLS), lambda i: (i, 0))],
        out_specs=pl.BlockSpec((N_DEV, tr, own), lambda i: (0, i, 0)),
        out_shape=jax.ShapeDtypeStruct((N_DEV, k, own), dw.dtype),
        compiler_params=_cparams(("arbitrary",)), name=name)(dw)


def _row_tile(s):
    return min(256, s)


def _row_spec(t, w, col=0):
    return pl.BlockSpec((t, w), lambda i: (i, col))


def _vec_spec(w):
    return pl.BlockSpec((1, w), lambda i: (0, 0))


def prenorm_fwd(x, gain, shift, scale, *, name):
    s, d = x.shape
    t = _row_tile(s)

    def body(x_ref, g_ref, sh_ref, sc_ref, h_ref):
        xv = x_ref[...]
        r = lax.rsqrt(_rowmean(xv * xv) + EPS)
        h_ref[...] = ((xv * r) * g_ref[...] * (1.0 + sc_ref[...]) + sh_ref[...]).astype(h_ref.dtype)

    return pl.pallas_call(
        body, grid=(s // t,),
        in_specs=[_row_spec(t, d), _vec_spec(d), _vec_spec(d), _vec_spec(d)],
        out_specs=_row_spec(t, d), out_shape=jax.ShapeDtypeStruct((s, d), BF16),
        compiler_params=_cparams(("arbitrary",)), name=name)(x, gain, shift, scale)


def prenorm_bwd(dh, x, gain, scale, g_res, *, name):
    s, d = x.shape
    t = _row_tile(s)

    def body(dh_ref, x_ref, g_ref, sc_ref, gr_ref, dx_ref, dsh_ref, dsc_ref, dg_ref):
        i = pl.program_id(0)
        xv = x_ref[...]
        dhv = dh_ref[...]
        r = lax.rsqrt(_rowmean(xv * xv) + EPS)
        xn = xv * r
        gain_v = g_ref[...]
        one_sc = 1.0 + sc_ref[...]
        dyn = dhv * one_sc
        dxn = dyn * gain_v
        dx_ref[...] = r * (dxn - xn * _rowmean(dxn * xn)) + gr_ref[...]
        p_sh = _colsum(dhv)
        p_sc = _colsum(dhv * (xn * gain_v))
        p_g = _colsum(dyn * xn)

        @pl.when(i == 0)
        def _():
            dsh_ref[...] = p_sh
            dsc_ref[...] = p_sc
            dg_ref[...] = p_g

        @pl.when(i > 0)
        def _():
            dsh_ref[...] += p_sh
            dsc_ref[...] += p_sc
            dg_ref[...] += p_g

    vec = jax.ShapeDtypeStruct((1, d), F32)
    return pl.pallas_call(
        body, grid=(s // t,),
        in_specs=[_row_spec(t, d), _row_spec(t, d), _vec_spec(d), _vec_spec(d), _row_spec(t, d)],
        out_specs=[_row_spec(t, d), _vec_spec(d), _vec_spec(d), _vec_spec(d)],
        out_shape=[jax.ShapeDtypeStruct((s, d), F32), vec, vec, vec],
        compiler_params=_cparams(("arbitrary",)), name=name)(dh, x, gain, scale, g_res)


def postnorm_fwd(x, out, gain, gate, *, name):
    s, d = x.shape
    t = _row_tile(s)

    def body(x_ref, o_ref, g_ref, gt_ref, y_ref):
        ov = o_ref[...]
        r = lax.rsqrt(_rowmean(ov * ov) + EPS)
        y_ref[...] = x_ref[...] + gt_ref[...] * ((ov * r) * g_ref[...])

    return pl.pallas_call(
        body, grid=(s // t,),
        in_specs=[_row_spec(t, d), _row_spec(t, d), _vec_spec(d), _vec_spec(d)],
        out_specs=_row_spec(t, d), out_shape=jax.ShapeDtypeStruct((s, d), F32),
        compiler_params=_cparams(("arbitrary",)), name=name)(x, out, gain, gate)


def postnorm_bwd(g, out, gain, gate, *, name):
    s, d = out.shape
    t = _row_tile(s)

    def body(g_ref, o_ref, gn_ref, gt_ref, do_ref, dgt_ref, dgn_ref):
        i = pl.program_id(0)
        ov = o_ref[...]
        gv = g_ref[...]
        r = lax.rsqrt(_rowmean(ov * ov) + EPS)
        on = ov * r
        gain_v = gn_ref[...]
        gate_v = gt_ref[...]
        dn = gv * gate_v
        don = dn * gain_v
        do_ref[...] = (r * (don - on * _rowmean(don * on))).astype(do_ref.dtype)
        p_gt = _colsum(gv * (on * gain_v))
        p_gn = _colsum(dn * on)

        @pl.when(i == 0)
        def _():
            dgt_ref[...] = p_gt
            dgn_ref[...] = p_gn

        @pl.when(i > 0)
        def _():
            dgt_ref[...] += p_gt
            dgn_ref[...] += p_gn

    vec = jax.ShapeDtypeStruct((1, d), F32)
    return pl.pallas_call(
        body, grid=(s // t,),
        in_specs=[_row_spec(t, d), _row_spec(t, d), _vec_spec(d), _vec_spec(d)],
        out_specs=[_row_spec(t, d), _vec_spec(d), _vec_spec(d)],
        out_shape=[jax.ShapeDtypeStruct((s, d), BF16), vec, vec],
        compiler_params=_cparams(("arbitrary",)), name=name)(g, out, gain, gate)


def loss_head(y, target, *, name):
    s, d = y.shape
    t = _row_tile(s)
    steps = s // t

    def body(y_ref, t_ref, dy_ref, loss_ref, acc_ref):
        i = pl.program_id(0)
        err = y_ref[...] - t_ref[...]
        dy_ref[...] = err * (1.0 / d)
        part = _colsum(err * err)

        @pl.when(i == 0)
        def _():
            acc_ref[...] = part

        @pl.when(i > 0)
        def _():
            acc_ref[...] += part

        @pl.when(i == steps - 1)
        def _():
            loss_ref[...] = jnp.sum(acc_ref[...], axis=1, keepdims=True) * (0.5 / d)

    return pl.pallas_call(
        body, grid=(steps,),
        in_specs=[_row_spec(t, d), _row_spec(t, d)],
        out_specs=[_row_spec(t, d), pl.BlockSpec((1, 1), lambda i: (0, 0))],
        out_shape=[jax.ShapeDtypeStruct((s, d), F32), jax.ShapeDtypeStruct((1, 1), F32)],
        scratch_shapes=[pltpu.VMEM((1, d), F32)],
        compiler_params=_cparams(("arbitrary",)), name=name)(y, target)


def _full_spec(shape):
    return pl.BlockSpec(shape, lambda i: (0,) * len(shape))


def proj_gate_fwd(y_a, y_b, w_pa, w_pb, proj, *, name):
    s, width = y_a.shape
    d = w_pa.shape[1]
    t = _row_tile(s)

    def body(ya_ref, yb_ref, wa_ref, wb_ref, ga_ref, gb_ref, pa_ref, pb_ref, m_ref):
        pa = _dot(ya_ref[...], wa_ref[...])
        pb = _dot(yb_ref[...], wb_ref[...])
        pa_ref[...] = pa.astype(pa_ref.dtype)
        pb_ref[...] = pb.astype(pb_ref.dtype)
        m_ref[...] = (_sigmoid(ga_ref[...]) * pa + _sigmoid(gb_ref[...]) * pb).astype(m_ref.dtype)

    out = jax.ShapeDtypeStruct((s, d), BF16)
    return pl.pallas_call(
        body, grid=(s // t,),
        in_specs=[_row_spec(t, width), _row_spec(t, width), _full_spec((width, d)), _full_spec((width, d)),
                  _row_spec(t, d, 0), _row_spec(t, d, 1)],
        out_specs=[_row_spec(t, d)] * 3, out_shape=[out] * 3,
        compiler_params=_cparams(("arbitrary",)), name=name)(y_a, y_b, w_pa, w_pb, proj, proj)


def gate_bwd(dout, w_out, proj, pa, pb, *, name):
    s, d = pa.shape
    t = _row_tile(s)

    def body(do_ref, w_ref, ga_ref, gb_ref, pa_ref, pb_ref, dpa_ref, dpb_ref, dp_ref):
        dm = _dot_nt(do_ref[...], w_ref[...])
        sa = _sigmoid(ga_ref[...])
        sb = _sigmoid(gb_ref[...])
        dpa_ref[...] = (dm * sa).astype(dpa_ref.dtype)
        dpb_ref[...] = (dm * sb).astype(dpb_ref.dtype)
        dp_ref[:, :d] = (dm * pa_ref[...].astype(F32) * sa * (1.0 - sa)).astype(dp_ref.dtype)
        dp_ref[:, d:] = (dm * pb_ref[...].astype(F32) * sb * (1.0 - sb)).astype(dp_ref.dtype)

    return pl.pallas_call(
        body, grid=(s // t,),
        in_specs=[_row_spec(t, d), _full_spec((d, d)), _row_spec(t, d, 0), _row_spec(t, d, 1),
                  _row_spec(t, d), _row_spec(t, d)],
        out_specs=[_row_spec(t, d), _row_spec(t, d), _row_spec(t, 2 * d, 0)],
        out_shape=[jax.ShapeDtypeStruct((s, d), BF16), jax.ShapeDtypeStruct((s, d), BF16),
                   jax.ShapeDtypeStruct((s, IN_COLS), BF16)],
        compiler_params=_cparams(("arbitrary",)), name=name)(dout, w_out, proj, proj, pa, pb)


def _pool_tile(s):
    return min(256, s)


def pool_fwd(proj, pw, ps, *, name):
    s = proj.shape[0]
    t = _pool_tile(s)
    pool_blk = (GATE_COLS + HEADS * HEAD_COLS) // (len(POOL_WINDOWS) * POOL_COLS)

    def body(p_ref, halo_ref, pw_ref, ps_ref, yb_ref, pooled_ref, mixed_ref):
        i = pl.program_id(0)
        halo = jnp.where(i == 0, 0.0, halo_ref[...])
        row = i * t + lax.broadcasted_iota(jnp.int32, (t, 1), 0)
        for g, w in enumerate(POOL_WINDOWS):
            vb = p_ref[:, g * POOL_COLS:g * POOL_COLS + POOL_GW]
            zb = p_ref[:, g * POOL_COLS + POOL_GW:(g + 1) * POOL_COLS]
            acc = jnp.concatenate([halo[:, g * POOL_COLS:g * POOL_COLS + POOL_GW], vb], axis=0)
            sh = 1
            while sh < w:
                acc = acc + pltpu.roll(acc, sh, axis=0)
                sh *= 2
            cnt = jnp.minimum(row + 1, w).astype(F32)
            pooled = acc[HALO:, :] / cnt - vb
            mixed = _dot(pooled.astype(BF16), pw_ref[g])
            cols = slice(g * POOL_GW, (g + 1) * POOL_GW)
            yb = mixed * ps_ref[:, cols] * (zb * _sigmoid(zb))
            yb_ref[:, cols] = yb.astype(yb_ref.dtype)
            pooled_ref[:, cols] = pooled.astype(pooled_ref.dtype)
            mixed_ref[:, cols] = mixed

    wide = len(POOL_WINDOWS) * POOL_COLS
    return pl.pallas_call(
        body, grid=(s // t,),
        in_specs=[pl.BlockSpec((t, wide), lambda i: (i, pool_blk)),
                  pl.BlockSpec((HALO, wide), lambda i: (jnp.maximum(i * (t // HALO) - 1, 0), pool_blk)),
                  pl.BlockSpec((len(POOL_WINDOWS), POOL_GW, POOL_GW), lambda i: (0, 0, 0)),
                  _vec_spec(WIDTH)],
        out_specs=[_row_spec(t, WIDTH)] * 3,
        out_shape=[jax.ShapeDtypeStruct((s, WIDTH), BF16), jax.ShapeDtypeStruct((s, WIDTH), BF16),
                   jax.ShapeDtypeStruct((s, WIDTH), F32)],
        compiler_params=_cparams(("arbitrary",)), name=name)(proj, proj, pw, ps)


def pool_bwd(dyb, proj, pooled, mixed, pw, ps, dproj, *, name):
    s = proj.shape[0]
    t = _pool_tile(s)
    nblk = s // t
    ng = len(POOL_WINDOWS)
    wide = ng * POOL_COLS
    pool_blk = (GATE_COLS + HEADS * HEAD_COLS) // wide

    def body(dy_ref, p_ref, pooled_ref, mixed_ref, pw_ref, ps_ref, dp_any, dp_ref, dpw_ref, dps_ref, carry):
        del dp_any
        i = pl.program_id(0)
        ii = nblk - 1 - i

        @pl.when(i == 0)
        def _():
            carry[...] = jnp.zeros_like(carry)
            dpw_ref[...] = jnp.zeros_like(dpw_ref)
            dps_ref[...] = jnp.zeros_like(dps_ref)

        row = ii * t + lax.broadcasted_iota(jnp.int32, (t, 1), 0)
        for g, w in enumerate(POOL_WINDOWS):
            cols = slice(g * POOL_GW, (g + 1) * POOL_GW)
            zb = p_ref[:, g * POOL_COLS + POOL_GW:(g + 1) * POOL_COLS]
            dy = dy_ref[:, cols]
            mx = mixed_ref[:, cols]
            sc = ps_ref[:, cols]
            sg = _sigmoid(zb)
            dzb = dy * (mx * sc) * (sg * (1.0 + zb * (1.0 - sg)))
            dpm = dy * (zb * sg)
            dps_ref[:, cols] += _colsum(dpm * mx)
            dmixed = (dpm * sc).astype(BF16)
            dpooled = _dot_nt(dmixed, pw_ref[g])
            dpw_ref[g] += _dot_tn(pooled_ref[:, cols], dmixed)
            cnt = jnp.minimum(row + 1, w).astype(F32)
            u = dpooled / cnt
            acc = jnp.concatenate([u, carry[:, cols]], axis=0)
            sh = 1
            while sh < w:
                acc = acc + pltpu.roll(acc, t + HALO - sh, axis=0)
                sh *= 2
            carry[:, cols] = u[:HALO, :]
            dp_ref[:, g * POOL_COLS:g * POOL_COLS + POOL_GW] = (acc[:t, :] - dpooled).astype(dp_ref.dtype)
            dp_ref[:, g * POOL_COLS + POOL_GW:(g + 1) * POOL_COLS] = dzb.astype(dp_ref.dtype)

    rev = lambda i: (nblk - 1 - i, 0)
    return pl.pallas_call(
        body, grid=(nblk,),
        in_specs=[pl.BlockSpec((t, WIDTH), rev),
                  pl.BlockSpec((t, wide), lambda i: (nblk - 1 - i, pool_blk)),
                  pl.BlockSpec((t, WIDTH), rev), pl.BlockSpec((t, WIDTH), rev),
                  pl.BlockSpec((ng, POOL_GW, POOL_GW), lambda i: (0, 0, 0)),
                  _vec_spec(WIDTH),
                  pl.BlockSpec(memory_space=pl.ANY)],
        out_specs=[pl.BlockSpec((t, wide), lambda i: (nblk - 1 - i, pool_blk)),
                   pl.BlockSpec((ng, POOL_GW, POOL_GW), lambda i: (0, 0, 0)),
                   _vec_spec(WIDTH)],
        out_shape=[jax.ShapeDtypeStruct(dproj.shape, dproj.dtype),
                   jax.ShapeDtypeStruct((ng, POOL_GW, POOL_GW), F32),
                   jax.ShapeDtypeStruct((1, WIDTH), F32)],
        scratch_shapes=[pltpu.VMEM((HALO, WIDTH), F32)],
        input_output_aliases={6: 0},
        compiler_params=_cparams(("arbitrary",)), name=name)(dyb, proj, pooled, mixed, pw, ps, dproj)


def _hgrn_tile(s):
    return min(512, s)


def _chunk_consts():
    tt = lax.broadcasted_iota(jnp.int32, (CHUNK, CHUNK), 0)
    ss = lax.broadcasted_iota(jnp.int32, (CHUNK, CHUNK), 1)
    within = (ss <= tt) & (ss // SUB == tt // SUB)
    before = ss < (tt // SUB) * SUB
    cums = jnp.concatenate([within.astype(F32), before.astype(F32)], axis=0).astype(BF16)
    causal = ss <= tt
    upper = (ss >= tt).astype(F32).astype(BF16)
    row = lax.broadcasted_iota(jnp.int32, (CHUNK, 1), 0)
    return cums, causal, upper, row


def _dot_split(mat01, v):
    hi = v.astype(BF16)
    r1 = v - hi.astype(F32)
    mid = r1.astype(BF16)
    lo = (r1 - mid.astype(F32)).astype(BF16)
    return _dot(mat01, hi) + _dot(mat01, mid) + _dot(mat01, lo)


def _hgrn_chunks(qas, fas, lb, cums, row):
    gates = [_hgrn_gates(qa, fa, lb) for qa, fa in zip(qas, fas)]
    cbs = [_dot_split(cums, g["lf"]) for g in gates]
    return [_hgrn_decay(g, cb, row) for g, cb in zip(gates, cbs)]


def _hgrn_gates(qa, fa, lb):
    sq = _sigmoid(qa)
    sa = _sigmoid(fa)
    sna = 1.0 - sa
    oml = 1.0 - lb
    f = lb + oml * sa
    fc = jnp.maximum(f, MIN_FORGET)
    return dict(sq=sq, q=qa * sq, sa=sa, sna=sna, oml=oml, f=f, fc=fc, lf=jnp.log(fc), k=oml * sna)


def _hgrn_decay(g, cb, row):
    sq, q, sa, sna, oml, f, fc, k = (g[n] for n in ("sq", "q", "sa", "sna", "oml", "f", "fc", "k"))
    c = cb[:CHUNK]
    bt = cb[CHUNK:]
    ec = jnp.exp(c)
    enc = jnp.exp(jnp.minimum(-c, MAX_EXP))
    qt = q * ec
    kt = k * enc
    dms, lhs, rhs = [], [], []
    for j in range(N_SUB):
        bj = bt[j * SUB:j * SUB + 1, :]
        dm = jnp.where(row >= j * SUB, jnp.exp(jnp.minimum(bt - bj, 0.0)), 0.0)
        dms.append(dm)
        lhs.append(qt * dm)
        rhs.append(jnp.where(row // SUB == j, kt, 0.0))
    lhs = jnp.concatenate(lhs, axis=1).astype(BF16)
    rhs = jnp.concatenate(rhs, axis=1).astype(BF16)
    b = bt + c
    bl = b[CHUNK - 1:CHUNK, :]
    ebl = jnp.exp(bl)
    edec = jnp.exp(bl - b)
    eb = ec * dms[0]
    return dict(sq=sq, q=q, sa=sa, sna=sna, oml=oml, f=f, fc=fc, k=k, ec=ec, enc=enc, dms=dms,
                lhs=lhs, rhs=rhs, ebl=ebl, edec=edec, eb=eb, qd=q * eb, kdec=k * edec)


def _rider_mid_step(total):
    return total - max(1, total // 8)


def hgrn_fwd(proj, lb, hn, *, rider=None, name):
    s = proj.shape[0]
    t = _hgrn_tile(s)
    nblk = s // t
    ncht = t // CHUNK
    head_blk0 = GATE_COLS // HEAD_COLS
    nr = rider.n if rider else 0
    hbm = pl.BlockSpec(memory_space=pl.ANY)

    def body(*refs):
        p_ref, lb_ref, hn_ref = refs[:3]
        ya_ref, o_ref, st_ref = refs[3 + nr:6 + nr]
        state = refs[6 + 2 * nr]
        i = pl.program_id(1)
        if rider:
            total = HEADS * nblk
            rider.emit(pl.program_id(0) * nblk + i, total, _rider_mid_step(total),
                       refs[3:3 + nr], refs[6 + nr:6 + 2 * nr], refs[7 + 2 * nr:])

        @pl.when(i == 0)
        def _():
            state[...] = jnp.zeros_like(state)

        cums, causal, _, row = _chunk_consts()
        lbv = lb_ref[...]
        hnv = hn_ref[...]

        rows = [slice(ci * CHUNK, (ci + 1) * CHUNK) for ci in range(ncht)]
        pres = _hgrn_chunks([p_ref[r, 0:HEAD_DIM] for r in rows], [p_ref[r, HEAD_DIM:2 * HEAD_DIM] for r in rows],
                            lbv, cums, row)
        vas = [p_ref[r, 2 * HEAD_DIM:3 * HEAD_DIM].astype(BF16) for r in rows]
        scores = [jnp.where(causal, _dot_nt(pre["lhs"], pre["rhs"]), 0.0).astype(BF16) for pre in pres]
        intra = [_dot(a, va) for a, va in zip(scores, vas)]
        qds = [pre["qd"].astype(BF16) for pre in pres]
        kdecs = [pre["kdec"].astype(BF16) for pre in pres]
        st = state[...]
        outs = []
        for ci in range(ncht):
            stb = st.astype(BF16)
            st_ref[ci, 0] = stb
            outs.append(intra[ci] + _dot_nt(qds[ci], stb))
            st = st * pres[ci]["ebl"] + _dot_tn(vas[ci], kdecs[ci])
        state[...] = st
        for r, o in zip(rows, outs):
            za = p_ref[r, 3 * HEAD_DIM:4 * HEAD_DIM]
            o_ref[r, :] = o
            ya_ref[r, :] = ((o * lax.rsqrt(_rowmean(o * o) + EPS)) * hnv * (za * _sigmoid(za))).astype(ya_ref.dtype)

    return pl.pallas_call(
        body, grid=(HEADS, nblk),
        in_specs=[pl.BlockSpec((t, HEAD_COLS), lambda h, i: (i, head_blk0 + h)),
                  pl.BlockSpec((1, HEAD_DIM), lambda h, i: (0, h)),
                  pl.BlockSpec((1, HEAD_DIM), lambda h, i: (0, h))] + [hbm] * nr,
        out_specs=[pl.BlockSpec((t, HEAD_DIM), lambda h, i: (i, h)),
                   pl.BlockSpec((t, HEAD_DIM), lambda h, i: (i, h)),
                   pl.BlockSpec((ncht, 1, HEAD_DIM, HEAD_DIM), lambda h, i: (i, h, 0, 0))] + [hbm] * nr,
        out_shape=[jax.ShapeDtypeStruct((s, WIDTH), BF16), jax.ShapeDtypeStruct((s, WIDTH), F32),
                   jax.ShapeDtypeStruct((s // CHUNK, HEADS, HEAD_DIM, HEAD_DIM), BF16)]
        + (rider.out_shape if rider else []),
        scratch_shapes=[pltpu.VMEM((HEAD_DIM, HEAD_DIM), F32)] + (rider.scratch if rider else []),
        compiler_params=_cparams(("arbitrary", "arbitrary")), name=name)(proj, lb, hn, *(rider.arrs if rider else []))


def hgrn_bwd(dya, proj, o_all, states, lb, hn, dproj, *, rider=None, name):
    s = proj.shape[0]
    t = _hgrn_tile(s)
    nblk = s // t
    ncht = t // CHUNK
    head_blk0 = GATE_COLS // HEAD_COLS
    nr = rider.n if rider else 0
    hbm = pl.BlockSpec(memory_space=pl.ANY)

    def body(*refs):
        dy_ref, p_ref, o_ref, st_ref, lb_ref, hn_ref = refs[:6]
        dp_ref, dhn_ref, dlb_ref = refs[7 + nr:10 + nr]
        dstate = refs[10 + 2 * nr]
        i = pl.program_id(1)
        if rider:
            total = HEADS * nblk
            rider.emit(pl.program_id(0) * nblk + i, total, _rider_mid_step(total),
                       refs[7:7 + nr], refs[10 + nr:10 + 2 * nr], refs[11 + 2 * nr:])

        @pl.when(i == 0)
        def _():
            dstate[...] = jnp.zeros_like(dstate)
            dhn_ref[...] = jnp.zeros_like(dhn_ref)
            dlb_ref[...] = jnp.zeros_like(dlb_ref)

        cums, causal, upper, row = _chunk_consts()
        lbv = lb_ref[...]
        hnv = hn_ref[...]

        chunks = range(ncht)
        rows = [slice(ci * CHUNK, (ci + 1) * CHUNK) for ci in chunks]
        qas = [p_ref[r, 0:HEAD_DIM] for r in rows]
        vbs = [p_ref[r, 2 * HEAD_DIM:3 * HEAD_DIM].astype(BF16) for r in rows]
        st0s = [st_ref[ci, 0] for ci in chunks]
        dzas, dobs = [], []
        dhn_acc = jnp.zeros_like(hnv)
        for r in rows:
            za = p_ref[r, 3 * HEAD_DIM:4 * HEAD_DIM]
            o = o_ref[r, :]
            dy = dy_ref[r, :]
            rn = lax.rsqrt(_rowmean(o * o) + EPS)
            on = o * rn
            sgz = _sigmoid(za)
            sz = za * sgz
            dzas.append(dy * on * hnv * (sgz * (1.0 + za * (1.0 - sgz))))
            dhn_acc = dhn_acc + _colsum(dy * on * sz)
            don = dy * hnv * sz
            dobs.append((rn * (don - on * _rowmean(don * on))).astype(BF16))
        pres = _hgrn_chunks(qas, [p_ref[r, HEAD_DIM:2 * HEAD_DIM] for r in rows], lbv, cums, row)
        scores = [jnp.where(causal, _dot_nt(pre["lhs"], pre["rhs"]), 0.0).astype(BF16) for pre in pres]
        das = [jnp.where(causal, _dot_nt(dob, vb), 0.0).astype(BF16) for dob, vb in zip(dobs, vbs)]
        dlhss = [_dot(da, pre["rhs"]) for da, pre in zip(das, pres)]
        drhss = [_dot_tn(da, pre["lhs"]) for da, pre in zip(das, pres)]
        dv_intra = [_dot_tn(a, dob) for a, dob in zip(scores, dobs)]
        dq_inter = [_dot(dob, st0) * pre["eb"] for dob, st0, pre in zip(dobs, st0s, pres)]
        qds = [pre["qd"].astype(BF16) for pre in pres]
        kdecs = [pre["kdec"].astype(BF16) for pre in pres]
        dst1 = dstate[...]
        dvs, dk_states, dbl_states = [None] * ncht, [None] * ncht, [None] * ncht
        for ci in reversed(chunks):
            dst1b = dst1.astype(BF16)
            dvs[ci] = dv_intra[ci] + _dot_nt(kdecs[ci], dst1b)
            dk_states[ci] = _dot(vbs[ci], dst1b) * pres[ci]["edec"]
            dbl_states[ci] = pres[ci]["ebl"] * _colsum(dst1 * st0s[ci].astype(F32))
            dst1 = dst1 * pres[ci]["ebl"] + _dot_tn(dobs[ci], qds[ci])
        dstate[...] = dst1
        dqs, dks, dbs, dbls = [], [], [], []
        for ci in chunks:
            pre = pres[ci]
            q, k = pre["q"], pre["k"]
            dq_a = jnp.zeros_like(q)
            dk_a = jnp.zeros_like(k)
            db = q * dq_inter[ci] - k * dk_states[ci]
            for j in range(N_SUB):
                cols = slice(j * HEAD_DIM, (j + 1) * HEAD_DIM)
                dlhs, drhs = dlhss[ci][:, cols], drhss[ci][:, cols]
                dq_a = dq_a + pre["dms"][j] * dlhs
                dk_a = dk_a + jnp.where(row // SUB == j, drhs, 0.0)
                db = db + (pre["lhs"][:, cols].astype(F32) * dlhs - pre["rhs"][:, cols].astype(F32) * drhs)
            dqs.append(dq_inter[ci] + pre["ec"] * dq_a)
            dks.append(dk_states[ci] + pre["enc"] * dk_a)
            dbs.append(db)
            dbls.append(_colsum(k * dk_states[ci]) + dbl_states[ci])
        dlfs = [_dot_split(upper, db) + dbl for db, dbl in zip(dbs, dbls)]
        dlb_acc = jnp.zeros_like(lbv)
        for ci in chunks:
            pre = pres[ci]
            sq, sa, sna, oml = pre["sq"], pre["sa"], pre["sna"], pre["oml"]
            dqa = dqs[ci] * (sq * (1.0 + qas[ci] * (1.0 - sq)))
            diff = jnp.where(pre["f"] >= MIN_FORGET, dlfs[ci] / pre["fc"], 0.0) - dks[ci]
            dlb_acc = dlb_acc + _colsum(diff * sna)
            dfa = diff * (oml * sa * sna)
            dp_ref[rows[ci], :] = jnp.concatenate([dqa, dfa, dvs[ci], dzas[ci]], axis=1).astype(dp_ref.dtype)
        dhn_ref[...] += dhn_acc
        dlb_ref[...] += dlb_acc

    rev = lambda h, i: (nblk - 1 - i, h)
    return pl.pallas_call(
        body, grid=(HEADS, nblk),
        in_specs=[pl.BlockSpec((t, HEAD_DIM), rev),
                  pl.BlockSpec((t, HEAD_COLS), lambda h, i: (nblk - 1 - i, head_blk0 + h)),
                  pl.BlockSpec((t, HEAD_DIM), rev),
                  pl.BlockSpec((ncht, 1, HEAD_DIM, HEAD_DIM), lambda h, i: (nblk - 1 - i, h, 0, 0)),
                  pl.BlockSpec((1, HEAD_DIM), lambda h, i: (0, h)),
                  pl.BlockSpec((1, HEAD_DIM), lambda h, i: (0, h)),
                  hbm] + [hbm] * nr,
        out_specs=[pl.BlockSpec((t, HEAD_COLS), lambda h, i: (nblk - 1 - i, head_blk0 + h)),
                   pl.BlockSpec((1, HEAD_DIM), lambda h, i: (0, h)),
                   pl.BlockSpec((1, HEAD_DIM), lambda h, i: (0, h))] + [hbm] * nr,
        out_shape=[jax.ShapeDtypeStruct(dproj.shape, dproj.dtype),
                   jax.ShapeDtypeStruct((1, WIDTH), F32), jax.ShapeDtypeStruct((1, WIDTH), F32)]
        + (rider.out_shape if rider else []),
        scratch_shapes=[pltpu.VMEM((HEAD_DIM, HEAD_DIM), F32)] + (rider.scratch if rider else []),
        input_output_aliases={6: 0},
        compiler_params=_cparams(("arbitrary", "arbitrary")),
        name=name)(dya, proj, o_all, states, lb, hn, dproj, *(rider.arrs if rider else []))


def _softmax_rows(lower):
    mx = jnp.max(lower, axis=0, keepdims=True)
    e = jnp.exp(lower - mx)
    return e / jnp.sum(e, axis=0, keepdims=True)


def lb_table(lower, *, name):
    depth, w = lower.shape

    def body(l_ref, o_ref):
        sm = _softmax_rows(l_ref[...])
        acc = jnp.zeros((1, w), F32)
        o_ref[0:1, :] = acc
        for l in range(1, depth):
            acc = acc + sm[l:l + 1, :]
            o_ref[l:l + 1, :] = acc

    return pl.pallas_call(body, out_shape=jax.ShapeDtypeStruct((depth, w), F32), name=name)(lower)


def lb_table_bwd(lower, dlb, *, name):
    depth, w = lower.shape

    def body(l_ref, d_ref, o_ref):
        sm = _softmax_rows(l_ref[...])
        dlbv = d_ref[...]
        dsm = [jnp.zeros((1, w), F32)]
        for i in range(1, depth):
            acc = jnp.zeros((1, w), F32)
            for l in range(i, depth):
                acc = acc + dlbv[l:l + 1, :]
            dsm.append(acc)
        inner = jnp.zeros((1, w), F32)
        for i in range(depth):
            inner = inner + sm[i:i + 1, :] * dsm[i]
        for i in range(depth):
            o_ref[i:i + 1, :] = sm[i:i + 1, :] * (dsm[i] - inner)

    return pl.pallas_call(body, out_shape=jax.ShapeDtypeStruct((depth, w), F32), name=name)(lower, dlb)


def w_ada_grad(c_all, dmod_cols, *, name):
    depth, _, cols = dmod_cols.shape
    d = c_all.shape[1]

    def body(c_ref, dm_ref, o_ref):
        cv = c_ref[...]
        ca = cv * _sigmoid(cv)
        o_ref[...] = _dot_tn(ca, dm_ref[...])

    return pl.pallas_call(
        body, grid=(depth,),
        in_specs=[pl.BlockSpec((N_DEV, d), lambda l: (0, 0)), pl.BlockSpec((None, N_DEV, cols), lambda l: (l, 0, 0))],
        out_specs=pl.BlockSpec((None, d, cols), lambda l: (l, 0, 0)),
        out_shape=jax.ShapeDtypeStruct((depth, d, cols), F32),
        compiler_params=_cparams(("arbitrary",)), name=name)(c_all, dmod_cols)


def sum_parts(parts, *, name):
    p, r, c = parts.shape

    def body(p_ref, o_ref):
        acc = p_ref[0]
        for j in range(1, p):
            acc = acc + p_ref[j]
        o_ref[...] = acc

    return pl.pallas_call(body, out_shape=jax.ShapeDtypeStruct((r, c), F32), name=name)(parts)


def _adam_rows(r, c):
    tr = r
    while tr * c * 4 > (1 << 20) and tr % 16 == 0:
        tr //= 2
    return tr


def _adam_update(w_ref, m_ref, v_ref, g_ref, go_ref, d_ref, mo_ref, vo_ref):
    g = g_ref[0].astype(F32)
    for j in range(1, g_ref.shape[0]):
        g = g + g_ref[j].astype(F32)
    mn = ADAM_B1 * m_ref[...] + (1.0 - ADAM_B1) * g
    vn = ADAM_B2 * v_ref[...] + (1.0 - ADAM_B2) * (g * g)
    m_hat = mn / (1.0 - ADAM_B1 ** ADAM_STEP)
    v_hat = vn / (1.0 - ADAM_B2 ** ADAM_STEP)
    go_ref[...] = g
    d_ref[...] = -ADAM_LR * (m_hat / (jnp.sqrt(v_hat) + ADAM_EPS) + ADAM_WD * w_ref[...])
    mo_ref[...] = mn
    vo_ref[...] = vn


def adamw(w, m, v, gparts, *, name):
    r, c = w.shape
    p = gparts.shape[0]
    tr = _adam_rows(r, c)
    spec = pl.BlockSpec((tr, c), lambda i: (i, 0))
    shp = jax.ShapeDtypeStruct((r, c), F32)
    return pl.pallas_call(
        functools.partial(_adam_update), grid=(r // tr,),
        in_specs=[spec, spec, spec, pl.BlockSpec((p, tr, c), lambda i: (0, i, 0))],
        out_specs=[spec] * 4, out_shape=[shp] * 4,
        compiler_params=_cparams(("arbitrary",)), name=name)(w, m, v, gparts)


def adamw_layers(w, m, v, gparts, *, name):
    depth, r, c = w.shape
    p = gparts[0].shape[0]
    tr = _adam_rows(r, c)

    def body(w_ref, m_ref, v_ref, *rest):
        g_refs, outs = rest[:depth], rest[depth:]
        layer = pl.program_id(0)
        for k in range(depth):
            @pl.when(layer == k)
            def _(k=k):
                _adam_update(w_ref, m_ref, v_ref, g_refs[k], *outs)

    spec = pl.BlockSpec((None, tr, c), lambda l, i: (l, i, 0))
    g_specs = [pl.BlockSpec((p, tr, c), functools.partial(lambda l, i, k: (0, jnp.where(l == k, i, 0), 0), k=k))
               for k in range(depth)]
    shp = jax.ShapeDtypeStruct((depth, r, c), F32)
    return pl.pallas_call(
        body, grid=(depth, r // tr),
        in_specs=[spec, spec, spec] + g_specs,
        out_specs=[spec] * 4, out_shape=[shp] * 4,
        compiler_params=_cparams(("arbitrary", "arbitrary")), name=name)(w, m, v, *gparts)


def _position():
    x, y, c = lax.axis_index("x"), lax.axis_index("y"), lax.axis_index("c")
    return x, y, c


def _dev_index(x, y, c):
    return 4 * x + 2 * y + c


def _gather_phases(ins, outs, send_sems, recv_sems, local_sems):
    n = len(ins)
    x, y, c = _position()
    me, sibling = (x, y, c), (x, y, 1 - c)
    chips = [(1 - x, y), (x, 1 - y), (1 - x, 1 - y)]

    def copy(a, k, block, to, own=False):
        slot = outs[a].at[_dev_index(*block)]
        return pltpu.make_async_remote_copy(
            src_ref=ins[a] if own else slot, dst_ref=slot,
            send_sem=send_sems.at[a * 7 + k], recv_sem=recv_sems.at[a * 7 + k],
            device_id=to, device_id_type=MESH)

    def mine(a):
        return pltpu.make_async_copy(ins[a], outs[a].at[_dev_index(*me)], local_sems.at[a])

    def first(a):
        return [copy(a, 0, me, sibling, True)] + [copy(a, 1 + j, me, (*chip, c), True) for j, chip in enumerate(chips)]

    def passed(a):
        return [copy(a, 4 + j, (*chip, c), sibling) for j, chip in enumerate(chips)]

    def start():
        for a in range(n):
            mine(a).start()
        for a in range(n):
            for cp in first(a):
                cp.start()

    def mid():
        for j, chip in enumerate(chips):
            for a in range(n):
                copy(a, 1 + j, (*chip, c), me).wait_recv()
                passed(a)[j].start()

    def finish():
        for a in range(n):
            copy(a, 0, sibling, me).wait_recv()
            for j, chip in enumerate(chips):
                copy(a, 4 + j, (*chip, 1 - c), me).wait_recv()
        for a in range(n):
            for cp in first(a) + passed(a):
                cp.wait_send()
            mine(a).wait()

    return start, mid, finish


def _scatter_phases(ins, outs, send_sems, recv_sems, local_sems):
    n = len(ins)
    x, y, c = _position()
    me = _dev_index(x, y, c)

    def peer(r):
        return (x ^ (r >> 2), y ^ ((r >> 1) & 1), c ^ (r & 1))

    def copy(a, r):
        to = peer(r)
        return pltpu.make_async_remote_copy(
            src_ref=ins[a].at[_dev_index(*to)], dst_ref=outs[a].at[me],
            send_sem=send_sems.at[a * 7 + r - 1], recv_sem=recv_sems.at[a * 7 + r - 1],
            device_id=to, device_id_type=MESH)

    def arrival(a, r):
        return pltpu.make_async_remote_copy(
            src_ref=ins[a].at[me], dst_ref=outs[a].at[_dev_index(*peer(r))],
            send_sem=send_sems.at[a * 7 + r - 1], recv_sem=recv_sems.at[a * 7 + r - 1],
            device_id=peer(r), device_id_type=MESH)

    def mine(a):
        return pltpu.make_async_copy(ins[a].at[me], outs[a].at[me], local_sems.at[a])

    def start():
        for a in range(n):
            mine(a).start()
        for r in range(1, N_DEV):
            for a in range(n):
                copy(a, r).start()

    def finish():
        for r in range(1, N_DEV):
            for a in range(n):
                arrival(a, r).wait_recv()
        for r in range(1, N_DEV):
            for a in range(n):
                copy(a, r).wait_send()
        for a in range(n):
            mine(a).wait()

    return start, None, finish


class Rider:
    def __init__(self, kind, arrs):
        self.kind, self.arrs, self.n = kind, list(arrs), len(arrs)
        lead = (N_DEV,) if kind == "gather" else ()
        self.out_shape = [jax.ShapeDtypeStruct(lead + a.shape, a.dtype) for a in self.arrs]
        self.scratch = [pltpu.SemaphoreType.DMA((7 * self.n,)), pltpu.SemaphoreType.DMA((7 * self.n,)),
                        pltpu.SemaphoreType.DMA((self.n,))]

    def phases(self, ins, outs, sems):
        make = _gather_phases if self.kind == "gather" else _scatter_phases
        return make(ins, outs, *sems)

    def emit(self, step, total, mid_step, ins, outs, sems):
        start, mid, finish = self.phases(ins, outs, sems)
        pl.when(step == 0)(start)
        if mid is not None:
            pl.when(step == mid_step)(mid)
        pl.when(step == total - 1)(finish)


def _standalone(rider, name):
    n = rider.n
    hbm = pl.BlockSpec(memory_space=pl.ANY)

    def body(*refs):
        start, mid, finish = rider.phases(refs[:n], refs[n:2 * n], refs[2 * n:])
        start()
        if mid is not None:
            mid()
        finish()

    return pl.pallas_call(body, out_shape=rider.out_shape, in_specs=[hbm] * n, out_specs=[hbm] * n,
                          scratch_shapes=rider.scratch, name=name)(*rider.arrs)


def all_gather(arrs, *, name):
    return _standalone(Rider("gather", arrs), name)


def scatter_parts(arrs, *, name):
    return _standalone(Rider("scatter", arrs), name)


def mod_exchange(c_all, w_ada, b_cols, *, name):
    depth, d, cols = w_ada.shape
    hbm = pl.BlockSpec(memory_space=pl.ANY)
    vmem = pl.BlockSpec(memory_space=pltpu.VMEM)

    def body(c_ref, w_ref, b_ref, out_ref, wbuf, sendbuf, send_sems, recv_sems, load_sem):
        x, y, c = _position()
        me = _dev_index(x, y, c)
        cv = c_ref[...]
        ca = cv * _sigmoid(cv)
        for l in range(depth):
            load = pltpu.make_async_copy(w_ref.at[l], wbuf, load_sem)
            load.start()
            load.wait()
            part = jnp.dot(ca, wbuf[...], preferred_element_type=F32,
                           precision=lax.Precision.HIGHEST) + b_ref[l:l + 1, :]
            for bi in range(N_DEV):
                sendbuf[bi, l:l + 1, :] = part[bi:bi + 1, :]

        def peer(r):
            return (x ^ (r >> 2), y ^ ((r >> 1) & 1), c ^ (r & 1))

        def copy(r):
            to = peer(r)
            return pltpu.make_async_remote_copy(
                src_ref=sendbuf.at[_dev_index(*to)], dst_ref=out_ref.at[me],
                send_sem=send_sems.at[r - 1], recv_sem=recv_sems.at[r - 1],
                device_id=to, device_id_type=MESH)

        def arrival(r):
            return pltpu.make_async_remote_copy(
                src_ref=sendbuf.at[me], dst_ref=out_ref.at[_dev_index(*peer(r))],
                send_sem=send_sems.at[r - 1], recv_sem=recv_sems.at[r - 1],
                device_id=peer(r), device_id_type=MESH)

        out_ref[me] = sendbuf[me]
        sends = [copy(r) for r in range(1, N_DEV)]
        for cp in sends:
            cp.start()
        for r in range(1, N_DEV):
            arrival(r).wait_recv()
        for cp in sends:
            cp.wait_send()

    return pl.pallas_call(
        body,
        out_shape=jax.ShapeDtypeStruct((N_DEV, depth, cols), F32),
        in_specs=[vmem, hbm, vmem], out_specs=vmem,
        scratch_shapes=[pltpu.VMEM((d, cols), F32), pltpu.VMEM((N_DEV, depth, cols), F32),
                        pltpu.SemaphoreType.DMA((7,)), pltpu.SemaphoreType.DMA((7,)), pltpu.SemaphoreType.DMA],
        compiler_params=pltpu.CompilerParams(vmem_limit_bytes=VMEM_LIMIT),
        name=name)(c_all, w_ada, b_cols)


def kernel(x, c, w_ada, b_ada, norm_pre, norm_post, w_in, lower_bounds, hgrn_norm, pool_w, pool_scale, w_proj_a, w_proj_b, w_out, loss_target, m_w_ada, m_b_ada, m_norm_pre, m_norm_post, m_w_in, m_lower_bounds, m_hgrn_norm, m_pool_w, m_pool_scale, m_w_proj_a, m_w_proj_b, m_w_out, v_w_ada, v_b_ada, v_norm_pre, v_norm_post, v_w_in, v_lower_bounds, v_hgrn_norm, v_pool_w, v_pool_scale, v_w_proj_a, v_w_proj_b, v_w_out):
    depth = w_in.shape[0]
    d = D_MODEL
    ada_cols = w_ada.shape[2]
    xi, yi, ci = _position()
    me = _dev_index(xi, yi, ci)
    xs = x[0]
    target = loss_target[0]
    ng = len(POOL_WINDOWS)

    def shards(l):
        return [w_in[l].astype(BF16), w_proj_a[l].astype(BF16), w_proj_b[l].astype(BF16),
                w_out[l].astype(BF16), pool_w[l].astype(BF16)]

    def other_weights(g_pa, g_pb, g_out, g_pool):
        return dict(
            pa=jnp.transpose(g_pa, (1, 0, 2)).reshape(WIDTH, d),
            pb=jnp.transpose(g_pb, (1, 0, 2)).reshape(WIDTH, d),
            w_out=g_out.reshape(d, d),
            pool=jnp.transpose(g_pool, (1, 0, 2, 3)).reshape(ng, POOL_GW, POOL_GW))

    (g_in0,) = all_gather(shards(0)[:1], name="gather_w_in")
    w_in_full = [permute_w_in(g_in0, name="permute_w_in")]
    others_full = []
    gathered = []

    (c_all,) = all_gather([c], name="gather_c")
    c_all = c_all.reshape(N_DEV, d)
    b_cols = lax.dynamic_slice_in_dim(b_ada, me * ada_cols, ada_cols, axis=1)
    mod_parts = mod_exchange(c_all, w_ada, b_cols, name="mod_exchange")
    mod = jnp.transpose(mod_parts, (1, 0, 2)).reshape(depth, 3 * d)
    lb_all = lb_table(lower_bounds, name="lb_table")

    saved = []
    cur = xs
    for l in range(depth):
        shift, scale, gate = mod[l:l + 1, :d], mod[l:l + 1, d:2 * d], mod[l:l + 1, 2 * d:]
        h = prenorm_fwd(cur, norm_pre[l:l + 1], shift, scale, name="prenorm_fwd")
        ride = (shards(0)[1:] if l == 0 else []) + (shards(l + 1)[:1] if l + 1 < depth else [])
        if ride:
            proj, *got = matmul_nn(h, w_in_full[l], tm=1024, tn=1024, out_dtype=F32,
                                   rider=Rider("gather", ride), name="mm_w_in_gather%d" % len(ride))
            if l == 0:
                others_full.append(other_weights(*got[:4]))
                got = got[4:]
            if got:
                w_in_full.append(permute_w_in(got[0], name="permute_w_in"))
        else:
            proj = matmul_nn(h, w_in_full[l], tm=1024, tn=1024, out_dtype=F32, name="mm_w_in")
        if l + 1 < depth:
            y_a, o_all, states, *got = hgrn_fwd(proj, lb_all[l:l + 1], hgrn_norm[l:l + 1],
                                                 rider=Rider("gather", shards(l + 1)[1:]), name="hgrn_fwd_gather")
            others_full.append(other_weights(*got))
        else:
            y_a, o_all, states = hgrn_fwd(proj, lb_all[l:l + 1], hgrn_norm[l:l + 1], name="hgrn_fwd")
        w = dict(w_in=w_in_full[l], **others_full[l])
        gathered.append(w)
        y_b, pooled, mixed = pool_fwd(proj, w["pool"], pool_scale[l:l + 1], name="pool_fwd")
        pa, pb, merged = proj_gate_fwd(y_a, y_b, w["pa"], w["pb"], proj, name="proj_gate_fwd")
        out = matmul_nn(merged, w["w_out"], tm=1024, tn=1024, out_dtype=F32, name="mm_w_out")
        nxt = postnorm_fwd(cur, out, norm_post[l:l + 1], gate, name="postnorm_fwd")
        saved.append(dict(x=cur, h=h, proj=proj, y_a=y_a, o=o_all, states=states, y_b=y_b, pooled=pooled,
                          mixed=mixed, pa=pa, pb=pb, merged=merged, out=out, scale=scale, gate=gate))
        cur = nxt

    g, loss_part = loss_head(cur, target, name="loss_head")
    loss = lax.psum(loss_part[0, 0], ("x", "y", "c"))

    small = [None] * depth
    big = [None] * depth
    pending = None
    for l in reversed(range(depth)):
        w, sv = gathered[l], saved[l]
        dout, dgate, dnpost = postnorm_bwd(g, sv["out"], norm_post[l:l + 1], sv["gate"], name="postnorm_bwd")
        dw_out = matmul_tn(sv["merged"], dout, tm=2048, tn=1024, out_dtype=BF16, name="mm_w_out_dw")
        dpa, dpb, dproj = gate_bwd(dout, w["w_out"], sv["proj"], sv["pa"], sv["pb"], name="gate_bwd")
        dya = matmul_nt(dpa, w["pa"], tm=1024, tn=2048, out_dtype=F32, name="mm_proj_a_dx")
        dyb = matmul_nt(dpb, w["pb"], tm=1024, tn=2048, out_dtype=F32, name="mm_proj_b_dx")
        dw_pa = matmul_tn(sv["y_a"], dpa, tm=2048, tn=2048, out_dtype=BF16, name="mm_proj_a_dw")
        dw_pb = matmul_tn(sv["y_b"], dpb, tm=2048, tn=2048, out_dtype=BF16, name="mm_proj_b_dw")
        dproj, dpool_w, dpool_scale = pool_bwd(dyb, sv["proj"], sv["pooled"], sv["mixed"], w["pool"],
                                               pool_scale[l:l + 1], dproj, name="pool_bwd")
        if pending is None:
            dproj, dhn, dlb = hgrn_bwd(dya, sv["proj"], sv["o"], sv["states"], lb_all[l:l + 1],
                                       hgrn_norm[l:l + 1], dproj, name="hgrn_bwd")
        else:
            dproj, dhn, dlb, *recv = hgrn_bwd(dya, sv["proj"], sv["o"], sv["states"], lb_all[l:l + 1],
                                              hgrn_norm[l:l + 1], dproj, rider=Rider("scatter", pending),
                                              name="hgrn_bwd_scatter")
            big[l + 1] = recv
        by_owner = lambda t: jnp.transpose(t.reshape(WIDTH, N_DEV, d // N_DEV), (1, 0, 2))
        others = [by_owner(dw_pa), by_owner(dw_pb), dw_out.reshape(N_DEV, d // N_DEV, d),
                  jnp.transpose(dpool_w.astype(BF16).reshape(ng, N_DEV, POOL_GW // N_DEV, POOL_GW), (1, 0, 2, 3))]
        if l > 0:
            dh = matmul_nt(dproj, w["w_in"], tm=512, tn=2048, out_dtype=F32, name="mm_w_in_dx")
            dw_in = matmul_tn(sv["h"], dproj, tm=2048, tn=1024, out_dtype=BF16, name="mm_w_in_dw")
            pending = [unpermute_w_in(dw_in, name="unpermute_w_in")] + others
        else:
            dw_in, *recv_others = matmul_tn(sv["h"], dproj, tm=2048, tn=1024, out_dtype=BF16,
                                            rider=Rider("scatter", others), name="mm_w_in_dw_scatter")
            dh, recv_in = matmul_nt(dproj, w["w_in"], tm=512, tn=2048, out_dtype=F32,
                                    rider=Rider("scatter", [unpermute_w_in(dw_in, name="unpermute_w_in")]),
                                    name="mm_w_in_dx_scatter")
            big[0] = [recv_in] + recv_others
        g, dshift, dscale, dnpre = prenorm_bwd(dh, sv["x"], norm_pre[l:l + 1], sv["scale"], g, name="prenorm_bwd")
        small[l] = jnp.concatenate([dshift, dscale, dgate, dnpre, dnpost, dlb, dhn, dpool_scale], axis=1)
    grad_x = g[None]

    small_mine = jnp.concatenate(small, axis=0)
    (small_all,) = all_gather([small_mine], name="gather_small")
    small_sum = sum_parts(small_all, name="sum_small")
    dmod_all = small_all[:, :, :3 * d]
    dmod_cols = jnp.transpose(lax.dynamic_slice_in_dim(dmod_all, me * ada_cols, ada_cols, axis=2), (1, 0, 2))
    g_w_ada = w_ada_grad(c_all, dmod_cols, name="w_ada_grad")
    off = 3 * d
    g_b_ada = small_sum[:, :off]
    g_npre = small_sum[:, off:off + d]
    g_npost = small_sum[:, off + d:off + 2 * d]
    g_lb_tab = small_sum[:, off + 2 * d:off + 2 * d + WIDTH]
    g_hn = small_sum[:, off + 2 * d + WIDTH:off + 2 * d + 2 * WIDTH]
    g_ps = small_sum[:, off + 2 * d + 2 * WIDTH:]
    g_lower = lb_table_bwd(lower_bounds, g_lb_tab, name="lb_table_bwd")

    def update(wt, mt, vt, gparts, shape2, name):
        outs = adamw(wt.reshape(shape2), mt.reshape(shape2), vt.reshape(shape2), gparts, name=name)
        return [o.reshape(wt.shape) for o in outs]

    def update_layers(wt, mt, vt, kind, name):
        shape3 = (depth, -1, wt.shape[-1])
        w3 = wt.reshape(shape3)
        gps = [big[l][kind].reshape((N_DEV,) + w3.shape[1:]) for l in range(depth)]
        outs = adamw_layers(w3, mt.reshape(shape3), vt.reshape(shape3), gps, name=name)
        return [o.reshape(wt.shape) for o in outs]

    def update_small(wt, mt, vt, gt, name):
        shape2 = (-1, wt.shape[-1])
        return update(wt, mt, vt, gt.reshape(shape2)[None], shape2, name)

    res = {
        "w_ada": update_small(w_ada, m_w_ada, v_w_ada, g_w_ada, "adamw_w_ada"),
        "b_ada": update_small(b_ada, m_b_ada, v_b_ada, g_b_ada, "adamw_b_ada"),
        "norm_pre": update_small(norm_pre, m_norm_pre, v_norm_pre, g_npre, "adamw_norm_pre"),
        "norm_post": update_small(norm_post, m_norm_post, v_norm_post, g_npost, "adamw_norm_post"),
        "w_in": update_layers(w_in, m_w_in, v_w_in, 0, "adamw_w_in"),
        "lower_bounds": update_small(lower_bounds, m_lower_bounds, v_lower_bounds, g_lower, "adamw_lower_bounds"),
        "hgrn_norm": update_small(hgrn_norm, m_hgrn_norm, v_hgrn_norm, g_hn, "adamw_hgrn_norm"),
        "pool_w": update_layers(pool_w, m_pool_w, v_pool_w, 4, "adamw_pool_w"),
        "pool_scale": update_small(pool_scale, m_pool_scale, v_pool_scale, g_ps, "adamw_pool_scale"),
        "w_proj_a": update_layers(w_proj_a, m_w_proj_a, v_w_proj_a, 1, "adamw_w_proj_a"),
        "w_proj_b": update_layers(w_proj_b, m_w_proj_b, v_w_proj_b, 2, "adamw_w_proj_b"),
        "w_out": update_layers(w_out, m_w_out, v_w_out, 3, "adamw_w_out"),
    }
    order = ["w_ada", "b_ada", "norm_pre", "norm_post", "w_in", "lower_bounds", "hgrn_norm", "pool_w",
             "pool_scale", "w_proj_a", "w_proj_b", "w_out"]
    outs = [loss, grad_x]
    for k in range(4):
        outs += [res[nm][k] for nm in order]
    return tuple(outs)
```

```python
import functools

import jax
import jax.numpy as jnp
from jax import lax
from jax.experimental import pallas as pl
from jax.experimental.pallas import tpu as pltpu

F32 = jnp.float32
BF16 = jnp.bfloat16
MESH = pl.DeviceIdType.MESH

N_DEV = 8
EPS = 1e-6
MIN_FORGET = 1e-30
D_MODEL = 2048
HEADS = 8
HEAD_DIM = 128
CHUNK = 64
SUB = 16
N_SUB = CHUNK // SUB
WIDTH = 1024
POOL_WINDOWS = (2, 4, 8, 16)
POOL_GW = 256
HALO = 16
IN_COLS = 10240
LANE = 128
N_COLBLK = IN_COLS // LANE
GATE_COLS = 4096
HEAD_COLS = 4 * HEAD_DIM
POOL_COLS = 2 * POOL_GW
MAX_EXP = 80.0

ADAM_LR = 0.001
ADAM_B1 = 0.9
ADAM_B2 = 0.999
ADAM_EPS = 1e-08
ADAM_WD = 0.01
ADAM_STEP = 10

VMEM_LIMIT = 56 * 1024 * 1024


def _cparams(sem=None):
    return pltpu.CompilerParams(dimension_semantics=sem, vmem_limit_bytes=VMEM_LIMIT)


def _sigmoid(v):
    return 1.0 / (1.0 + jnp.exp(-v))


def _dot(a, b):
    return jnp.dot(a, b, preferred_element_type=F32)


def _dot_nt(a, b):
    return lax.dot_general(a, b, (((1,), (1,)), ((), ())), preferred_element_type=F32)


def _dot_tn(a, b):
    return lax.dot_general(a, b, (((0,), (0,)), ((), ())), preferred_element_type=F32)


def _colsum(v):
    return jnp.sum(v, axis=0, keepdims=True)


def _rowmean(v):
    return jnp.mean(v, axis=-1, keepdims=True)


def _orig_block_static(n):
    if n < 32:
        return n + 48
    if n < 64:
        m = n - 32
        return 8 * (m % 4) + m // 4
    m = n - 64
    t = m % 4
    return 32 + 2 * (m // 4) + (t % 2) + 8 * (t // 2)


def _accumulate(step, steps, prod, o_ref, acc_ref):
    if steps == 1:
        o_ref[...] = prod.astype(o_ref.dtype)
        return

    @pl.when(step == 0)
    def _():
        acc_ref[...] = prod

    @pl.when(step > 0)
    def _():
        acc_ref[...] += prod

    @pl.when(step == steps - 1)
    def _():
        o_ref[...] = acc_ref[...].astype(o_ref.dtype)


def _matmul_call(dot, a, b, *, grid, in_specs, out_spec, out_shape, acc_shape, steps, rider, name):
    nr = rider.n if rider else 0
    hbm = pl.BlockSpec(memory_space=pl.ANY)
    has_acc = steps > 1

    def body(*refs):
        a_ref, b_ref = refs[:2]
        o_ref = refs[2 + nr]
        scratch = refs[3 + 2 * nr:]
        if rider:
            total = grid[0] * grid[1]
            rider.emit(pl.program_id(0) * grid[1] + pl.program_id(1), total, _rider_mid_step(total),
                       refs[2:2 + nr], refs[3 + nr:3 + 2 * nr], scratch[1 if has_acc else 0:])
        _accumulate(pl.program_id(1), steps, dot(a_ref[...], b_ref[...]), o_ref, scratch[0] if has_acc else None)

    outs = pl.pallas_call(
        body, grid=grid,
        in_specs=in_specs + [hbm] * nr, out_specs=[out_spec] + [hbm] * nr,
        out_shape=[out_shape] + (rider.out_shape if rider else []),
        scratch_shapes=([pltpu.VMEM(acc_shape, F32)] if has_acc else []) + (rider.scratch if rider else []),
        compiler_params=_cparams(("arbitrary", "arbitrary")),
        name=name)(a, b, *(rider.arrs if rider else []))
    return outs if rider else outs[0]


def matmul_nn(a, b, *, tm, tn, out_dtype, name, rider=None):
    m, k = a.shape
    n = b.shape[1]
    tm = min(tm, m)
    return _matmul_call(
        _dot, a, b, grid=(n // tn, m // tm),
        in_specs=[pl.BlockSpec((tm, k), lambda j, i: (i, 0)), pl.BlockSpec((k, tn), lambda j, i: (0, j))],
        out_spec=pl.BlockSpec((tm, tn), lambda j, i: (i, j)),
        out_shape=jax.ShapeDtypeStruct((m, n), out_dtype), acc_shape=None, steps=1, rider=rider, name=name)


def matmul_nt(a, b, *, tm, tn, out_dtype, name, rider=None):
    m, n = a.shape
    k = b.shape[0]
    tm = min(tm, m)
    return _matmul_call(
        _dot_nt, a, b, grid=(m // tm, n // tn),
        in_specs=[pl.BlockSpec((tm, tn), lambda i, j: (i, j)), pl.BlockSpec((k, tn), lambda i, j: (0, j))],
        out_spec=pl.BlockSpec((tm, k), lambda i, j: (i, 0)),
        out_shape=jax.ShapeDtypeStruct((m, k), out_dtype), acc_shape=(tm, k), steps=n // tn, rider=rider, name=name)


def matmul_tn(a, b, *, tm, tn, out_dtype, name, rider=None):
    m, k = a.shape
    n = b.shape[1]
    tm = min(tm, m)
    return _matmul_call(
        _dot_tn, a, b, grid=(n // tn, m // tm),
        in_specs=[pl.BlockSpec((tm, k), lambda j, i: (i, 0)), pl.BlockSpec((tm, tn), lambda j, i: (i, j))],
        out_spec=pl.BlockSpec((k, tn), lambda j, i: (0, j)),
        out_shape=jax.ShapeDtypeStruct((k, n), out_dtype), acc_shape=(k, tn), steps=m // tm, rider=rider, name=name)


def permute_w_in(staged, *, name):
    k = staged.shape[1]
    own = IN_COLS // N_DEV
    tr = min(256, k)

    def body(i_ref, o_ref):
        for nb in range(N_COLBLK):
            dev, col = divmod(_orig_block_static(nb) * LANE, own)
            o_ref[:, nb * LANE:(nb + 1) * LANE] = i_ref[dev, :, col:col + LANE]

    return pl.pallas_call(
        body, grid=(k // tr,),
        in_specs=[pl.BlockSpec((N_DEV, tr, own), lambda i: (0, i, 0))],
        out_specs=pl.BlockSpec((tr, IN_COLS), lambda i: (i, 0)),
        out_shape=jax.ShapeDtypeStruct((k, IN_COLS), staged.dtype),
        compiler_params=_cparams(("arbitrary",)), name=name)(staged)


def unpermute_w_in(dw, *, name):
    k = dw.shape[0]
    own = IN_COLS // N_DEV
    tr = min(256, k)

    def body(i_ref, o_ref):
        for nb in range(N_COLBLK):
            dev, col = divmod(_orig_block_static(nb) * LANE, own)
            o_ref[dev, :, col:col + LANE] = i_ref[:, nb * LANE:(nb + 1) * LANE]

    return pl.pallas_call(
        body, grid=(k // tr,),
        in_specs=[pl.BlockSpec((tr, IN_COLS), lambda i: (i, 0))],
        out_specs=pl.BlockSpec((N_DEV, tr, own), lambda i: (0, i, 0)),
        out_shape=jax.ShapeDtypeStruct((N_DEV, k, own), dw.dtype),
        compiler_params=_cparams(("arbitrary",)), name=name)(dw)


def _row_tile(s):
    return min(256, s)


def _norm_tile(s):
    return min(512, s)


def _row_spec(t, w, col=0):
    return pl.BlockSpec((t, w), lambda i: (i, col))


def _vec_spec(w):
    return pl.BlockSpec((1, w), lambda i: (0, 0))


def prenorm_fwd(x, gain, shift, scale, *, name):
    s, d = x.shape
    t = _norm_tile(s)

    def body(x_ref, g_ref, sh_ref, sc_ref, h_ref):
        xv = x_ref[...]
        r = lax.rsqrt(_rowmean(xv * xv) + EPS)
        h_ref[...] = ((xv * r) * g_ref[...] * (1.0 + sc_ref[...]) + sh_ref[...]).astype(h_ref.dtype)

    return pl.pallas_call(
        body, grid=(s // t,),
        in_specs=[_row_spec(t, d), _vec_spec(d), _vec_spec(d), _vec_spec(d)],
        out_specs=_row_spec(t, d), out_shape=jax.ShapeDtypeStruct((s, d), BF16),
        compiler_params=_cparams(("arbitrary",)), name=name)(x, gain, shift, scale)


def prenorm_bwd(dh, x, gain, scale, g_res, *, name):
    s, d = x.shape
    t = _norm_tile(s)

    def body(dh_ref, x_ref, g_ref, sc_ref, gr_ref, dx_ref, dsh_ref, dsc_ref, dg_ref):
        i = pl.program_id(0)
        xv = x_ref[...]
        dhv = dh_ref[...]
        r = lax.rsqrt(_rowmean(xv * xv) + EPS)
        xn = xv * r
        gain_v = g_ref[...]
        one_sc = 1.0 + sc_ref[...]
        dyn = dhv * one_sc
        dxn = dyn * gain_v
        dx_ref[...] = r * (dxn - xn * _rowmean(dxn * xn)) + gr_ref[...]
        p_sh = _colsum(dhv)
        p_sc = _colsum(dhv * (xn * gain_v))
        p_g = _colsum(dyn * xn)

        @pl.when(i == 0)
        def _():
            dsh_ref[...] = p_sh
            dsc_ref[...] = p_sc
            dg_ref[...] = p_g

        @pl.when(i > 0)
        def _():
            dsh_ref[...] += p_sh
            dsc_ref[...] += p_sc
            dg_ref[...] += p_g

    vec = jax.ShapeDtypeStruct((1, d), F32)
    return pl.pallas_call(
        body, grid=(s // t,),
        in_specs=[_row_spec(t, d), _row_spec(t, d), _vec_spec(d), _vec_spec(d), _row_spec(t, d)],
        out_specs=[_row_spec(t, d), _vec_spec(d), _vec_spec(d), _vec_spec(d)],
        out_shape=[jax.ShapeDtypeStruct((s, d), F32), vec, vec, vec],
        compiler_params=_cparams(("arbitrary",)), name=name)(dh, x, gain, scale, g_res)


def postnorm_fwd(x, out, gain, gate, *, name):
    s, d = x.shape
    t = _norm_tile(s)

    def body(x_ref, o_ref, g_ref, gt_ref, y_ref):
        ov = o_ref[...]
        r = lax.rsqrt(_rowmean(ov * ov) + EPS)
        y_ref[...] = x_ref[...] + gt_ref[...] * ((ov * r) * g_ref[...])

    return pl.pallas_call(
        body, grid=(s // t,),
        in_specs=[_row_spec(t, d), _row_spec(t, d), _vec_spec(d), _vec_spec(d)],
        out_specs=_row_spec(t, d), out_shape=jax.ShapeDtypeStruct((s, d), F32),
        compiler_params=_cparams(("arbitrary",)), name=name)(x, out, gain, gate)


def postnorm_bwd(g, out, gain, gate, *, name):
    s, d = out.shape
    t = _norm_tile(s)

    def body(g_ref, o_ref, gn_ref, gt_ref, do_ref, dgt_ref, dgn_ref):
        i = pl.program_id(0)
        ov = o_ref[...]
        gv = g_ref[...]
        r = lax.rsqrt(_rowmean(ov * ov) + EPS)
        on = ov * r
        gain_v = gn_ref[...]
        gate_v = gt_ref[...]
        dn = gv * gate_v
        don = dn * gain_v
        do_ref[...] = (r * (don - on * _rowmean(don * on))).astype(do_ref.dtype)
        p_gt = _colsum(gv * (on * gain_v))
        p_gn = _colsum(dn * on)

        @pl.when(i == 0)
        def _():
            dgt_ref[...] = p_gt
            dgn_ref[...] = p_gn

        @pl.when(i > 0)
        def _():
            dgt_ref[...] += p_gt
            dgn_ref[...] += p_gn

    vec = jax.ShapeDtypeStruct((1, d), F32)
    return pl.pallas_call(
        body, grid=(s // t,),
        in_specs=[_row_spec(t, d), _row_spec(t, d), _vec_spec(d), _vec_spec(d)],
        out_specs=[_row_spec(t, d), _vec_spec(d), _vec_spec(d)],
        out_shape=[jax.ShapeDtypeStruct((s, d), BF16), vec, vec],
        compiler_params=_cparams(("arbitrary",)), name=name)(g, out, gain, gate)


def loss_head(y, target, *, name):
    s, d = y.shape
    t = _norm_tile(s)
    steps = s // t

    def body(y_ref, t_ref, dy_ref, loss_ref, acc_ref):
        i = pl.program_id(0)
        err = y_ref[...] - t_ref[...]
        dy_ref[...] = err * (1.0 / d)
        part = _colsum(err * err)

        @pl.when(i == 0)
        def _():
            acc_ref[...] = part

        @pl.when(i > 0)
        def _():
            acc_ref[...] += part

        @pl.when(i == steps - 1)
        def _():
            loss_ref[...] = jnp.sum(acc_ref[...], axis=1, keepdims=True) * (0.5 / d)

    return pl.pallas_call(
        body, grid=(steps,),
        in_specs=[_row_spec(t, d), _row_spec(t, d)],
        out_specs=[_row_spec(t, d), pl.BlockSpec((1, 1), lambda i: (0, 0))],
        out_shape=[jax.ShapeDtypeStruct((s, d), F32), jax.ShapeDtypeStruct((1, 1), F32)],
        scratch_shapes=[pltpu.VMEM((1, d), F32)],
        compiler_params=_cparams(("arbitrary",)), name=name)(y, target)


def _full_spec(shape):
    return pl.BlockSpec(shape, lambda i: (0,) * len(shape))


def proj_gate_fwd(y_a, y_b, w_pa, w_pb, proj, *, name):
    s, width = y_a.shape
    d = w_pa.shape[1]
    t = _row_tile(s)

    def body(ya_ref, yb_ref, wa_ref, wb_ref, ga_ref, gb_ref, pa_ref, pb_ref, m_ref):
        pa = _dot(ya_ref[...], wa_ref[...])
        pb = _dot(yb_ref[...], wb_ref[...])
        pa_ref[...] = pa.astype(pa_ref.dtype)
        pb_ref[...] = pb.astype(pb_ref.dtype)
        m_ref[...] = (_sigmoid(ga_ref[...]) * pa + _sigmoid(gb_ref[...]) * pb).astype(m_ref.dtype)

    out = jax.ShapeDtypeStruct((s, d), BF16)
    return pl.pallas_call(
        body, grid=(s // t,),
        in_specs=[_row_spec(t, width), _row_spec(t, width), _full_spec((width, d)), _full_spec((width, d)),
                  _row_spec(t, d, 0), _row_spec(t, d, 1)],
        out_specs=[_row_spec(t, d)] * 3, out_shape=[out] * 3,
        compiler_params=_cparams(("arbitrary",)), name=name)(y_a, y_b, w_pa, w_pb, proj, proj)


def gate_bwd(dout, w_out, proj, pa, pb, *, name):
    s, d = pa.shape
    t = _row_tile(s)

    def body(do_ref, w_ref, ga_ref, gb_ref, pa_ref, pb_ref, dpa_ref, dpb_ref, dp_ref):
        dm = _dot_nt(do_ref[...], w_ref[...])
        sa = _sigmoid(ga_ref[...])
        sb = _sigmoid(gb_ref[...])
        dpa_ref[...] = (dm * sa).astype(dpa_ref.dtype)
        dpb_ref[...] = (dm * sb).astype(dpb_ref.dtype)
        dp_ref[:, :d] = (dm * pa_ref[...].astype(F32) * sa * (1.0 - sa)).astype(dp_ref.dtype)
        dp_ref[:, d:] = (dm * pb_ref[...].astype(F32) * sb * (1.0 - sb)).astype(dp_ref.dtype)

    return pl.pallas_call(
        body, grid=(s // t,),
        in_specs=[_row_spec(t, d), _full_spec((d, d)), _row_spec(t, d, 0), _row_spec(t, d, 1),
                  _row_spec(t, d), _row_spec(t, d)],
        out_specs=[_row_spec(t, d), _row_spec(t, d), _row_spec(t, 2 * d, 0)],
        out_shape=[jax.ShapeDtypeStruct((s, d), BF16), jax.ShapeDtypeStruct((s, d), BF16),
                   jax.ShapeDtypeStruct((s, IN_COLS), BF16)],
        compiler_params=_cparams(("arbitrary",)), name=name)(dout, w_out, proj, proj, pa, pb)


def _pool_tile(s):
    return min(256, s)


def pool_fwd(proj, pw, ps, *, name):
    s = proj.shape[0]
    t = _pool_tile(s)
    pool_blk = (GATE_COLS + HEADS * HEAD_COLS) // (len(POOL_WINDOWS) * POOL_COLS)

    def body(p_ref, halo_ref, pw_ref, ps_ref, yb_ref, pooled_ref, mixed_ref):
        i = pl.program_id(0)
        halo = jnp.where(i == 0, 0.0, halo_ref[...])
        row = i * t + lax.broadcasted_iota(jnp.int32, (t, 1), 0)
        for g, w in enumerate(POOL_WINDOWS):
            vb = p_ref[:, g * POOL_COLS:g * POOL_COLS + POOL_GW]
            zb = p_ref[:, g * POOL_COLS + POOL_GW:(g + 1) * POOL_COLS]
            acc = jnp.concatenate([halo[:, g * POOL_COLS:g * POOL_COLS + POOL_GW], vb], axis=0)
            sh = 1
            while sh < w:
                acc = acc + pltpu.roll(acc, sh, axis=0)
                sh *= 2
            cnt = jnp.minimum(row + 1, w).astype(F32)
            pooled = acc[HALO:, :] / cnt - vb
            mixed = _dot(pooled.astype(BF16), pw_ref[g])
            cols = slice(g * POOL_GW, (g + 1) * POOL_GW)
            yb = mixed * ps_ref[:, cols] * (zb * _sigmoid(zb))
            yb_ref[:, cols] = yb.astype(yb_ref.dtype)
            pooled_ref[:, cols] = pooled.astype(pooled_ref.dtype)
            mixed_ref[:, cols] = mixed

    wide = len(POOL_WINDOWS) * POOL_COLS
    return pl.pallas_call(
        body, grid=(s // t,),
        in_specs=[pl.BlockSpec((t, wide), lambda i: (i, pool_blk)),
                  pl.BlockSpec((HALO, wide), lambda i: (jnp.maximum(i * (t // HALO) - 1, 0), pool_blk)),
                  pl.BlockSpec((len(POOL_WINDOWS), POOL_GW, POOL_GW), lambda i: (0, 0, 0)),
                  _vec_spec(WIDTH)],
        out_specs=[_row_spec(t, WIDTH)] * 3,
        out_shape=[jax.ShapeDtypeStruct((s, WIDTH), BF16), jax.ShapeDtypeStruct((s, WIDTH), BF16),
                   jax.ShapeDtypeStruct((s, WIDTH), F32)],
        compiler_params=_cparams(("arbitrary",)), name=name)(proj, proj, pw, ps)


def pool_bwd(dyb, proj, pooled, mixed, pw, ps, dproj, *, name):
    s = proj.shape[0]
    t = _pool_tile(s)
    nblk = s // t
    ng = len(POOL_WINDOWS)
    wide = ng * POOL_COLS
    pool_blk = (GATE_COLS + HEADS * HEAD_COLS) // wide

    def body(dy_ref, p_ref, pooled_ref, mixed_ref, pw_ref, ps_ref, dp_any, dp_ref, dpw_ref, dps_ref, carry):
        del dp_any
        i = pl.program_id(0)
        ii = nblk - 1 - i

        @pl.when(i == 0)
        def _():
            carry[...] = jnp.zeros_like(carry)
            dpw_ref[...] = jnp.zeros_like(dpw_ref)
            dps_ref[...] = jnp.zeros_like(dps_ref)

        row = ii * t + lax.broadcasted_iota(jnp.int32, (t, 1), 0)
        for g, w in enumerate(POOL_WINDOWS):
            cols = slice(g * POOL_GW, (g + 1) * POOL_GW)
            zb = p_ref[:, g * POOL_COLS + POOL_GW:(g + 1) * POOL_COLS]
            dy = dy_ref[:, cols]
            mx = mixed_ref[:, cols]
            sc = ps_ref[:, cols]
            sg = _sigmoid(zb)
            dzb = dy * (mx * sc) * (sg * (1.0 + zb * (1.0 - sg)))
            dpm = dy * (zb * sg)
            dps_ref[:, cols] += _colsum(dpm * mx)
            dmixed = (dpm * sc).astype(BF16)
            dpooled = _dot_nt(dmixed, pw_ref[g])
            dpw_ref[g] += _dot_tn(pooled_ref[:, cols], dmixed)
            cnt = jnp.minimum(row + 1, w).astype(F32)
            u = dpooled / cnt
            acc = jnp.concatenate([u, carry[:, cols]], axis=0)
            sh = 1
            while sh < w:
                acc = acc + pltpu.roll(acc, t + HALO - sh, axis=0)
                sh *= 2
            carry[:, cols] = u[:HALO, :]
            dp_ref[:, g * POOL_COLS:g * POOL_COLS + POOL_GW] = (acc[:t, :] - dpooled).astype(dp_ref.dtype)
            dp_ref[:, g * POOL_COLS + POOL_GW:(g + 1) * POOL_COLS] = dzb.astype(dp_ref.dtype)

    rev = lambda i: (nblk - 1 - i, 0)
    return pl.pallas_call(
        body, grid=(nblk,),
        in_specs=[pl.BlockSpec((t, WIDTH), rev),
                  pl.BlockSpec((t, wide), lambda i: (nblk - 1 - i, pool_blk)),
                  pl.BlockSpec((t, WIDTH), rev), pl.BlockSpec((t, WIDTH), rev),
                  pl.BlockSpec((ng, POOL_GW, POOL_GW), lambda i: (0, 0, 0)),
                  _vec_spec(WIDTH),
                  pl.BlockSpec(memory_space=pl.ANY)],
        out_specs=[pl.BlockSpec((t, wide), lambda i: (nblk - 1 - i, pool_blk)),
                   pl.BlockSpec((ng, POOL_GW, POOL_GW), lambda i: (0, 0, 0)),
                   _vec_spec(WIDTH)],
        out_shape=[jax.ShapeDtypeStruct(dproj.shape, dproj.dtype),
                   jax.ShapeDtypeStruct((ng, POOL_GW, POOL_GW), F32),
                   jax.ShapeDtypeStruct((1, WIDTH), F32)],
        scratch_shapes=[pltpu.VMEM((HALO, WIDTH), F32)],
        input_output_aliases={6: 0},
        compiler_params=_cparams(("arbitrary",)), name=name)(dyb, proj, pooled, mixed, pw, ps, dproj)


def _hgrn_tile(s):
    return min(1024, s)


def _chunk_consts():
    tt = lax.broadcasted_iota(jnp.int32, (CHUNK, CHUNK), 0)
    ss = lax.broadcasted_iota(jnp.int32, (CHUNK, CHUNK), 1)
    within = (ss <= tt) & (ss // SUB == tt // SUB)
    before = ss < (tt // SUB) * SUB
    cums = jnp.concatenate([within.astype(F32), before.astype(F32)], axis=0).astype(BF16)
    causal = ss <= tt
    upper = (ss >= tt).astype(F32).astype(BF16)
    row = lax.broadcasted_iota(jnp.int32, (CHUNK, 1), 0)
    return cums, causal, upper, row


def _dot_split(mat01, v):
    hi = v.astype(BF16)
    r1 = v - hi.astype(F32)
    mid = r1.astype(BF16)
    lo = (r1 - mid.astype(F32)).astype(BF16)
    return _dot(mat01, hi) + _dot(mat01, mid) + _dot(mat01, lo)


def _hgrn_chunks(qas, fas, lb, cums, row):
    gates = [_hgrn_gates(qa, fa, lb) for qa, fa in zip(qas, fas)]
    cbs = [_dot_split(cums, g["lf"]) for g in gates]
    return [_hgrn_decay(g, cb, row) for g, cb in zip(gates, cbs)]


def _hgrn_gates(qa, fa, lb):
    sq = _sigmoid(qa)
    sa = _sigmoid(fa)
    sna = 1.0 - sa
    oml = 1.0 - lb
    f = lb + oml * sa
    fc = jnp.maximum(f, MIN_FORGET)
    return dict(sq=sq, q=qa * sq, sa=sa, sna=sna, oml=oml, f=f, fc=fc, lf=jnp.log(fc), k=oml * sna)


def _hgrn_decay(g, cb, row):
    sq, q, sa, sna, oml, f, fc, k = (g[n] for n in ("sq", "q", "sa", "sna", "oml", "f", "fc", "k"))
    c = cb[:CHUNK]
    bt = cb[CHUNK:]
    ec = jnp.exp(c)
    enc = jnp.exp(jnp.minimum(-c, MAX_EXP))
    qt = q * ec
    kt = k * enc
    dms, lhs, rhs = [], [], []
    for j in range(N_SUB):
        bj = bt[j * SUB:j * SUB + 1, :]
        dm = jnp.where(row >= j * SUB, jnp.exp(jnp.minimum(bt - bj, 0.0)), 0.0)
        dms.append(dm)
        lhs.append(qt * dm)
        rhs.append(jnp.where(row // SUB == j, kt, 0.0))
    lhs = jnp.concatenate(lhs, axis=1).astype(BF16)
    rhs = jnp.concatenate(rhs, axis=1).astype(BF16)
    b = bt + c
    bl = b[CHUNK - 1:CHUNK, :]
    ebl = jnp.exp(bl)
    edec = jnp.exp(bl - b)
    eb = ec * dms[0]
    return dict(sq=sq, q=q, sa=sa, sna=sna, oml=oml, f=f, fc=fc, k=k, ec=ec, enc=enc, dms=dms,
                lhs=lhs, rhs=rhs, ebl=ebl, edec=edec, eb=eb, qd=q * eb, kdec=k * edec)


def _rider_mid_step(total):
    return total - max(1, total // 8)


def hgrn_fwd(proj, lb, hn, *, rider=None, name):
    s = proj.shape[0]
    t = _hgrn_tile(s)
    nblk = s // t
    ncht = t // CHUNK
    head_blk0 = GATE_COLS // HEAD_COLS
    nr = rider.n if rider else 0
    hbm = pl.BlockSpec(memory_space=pl.ANY)

    def body(*refs):
        p_ref, lb_ref, hn_ref = refs[:3]
        ya_ref, o_ref, st_ref = refs[3 + nr:6 + nr]
        state = refs[6 + 2 * nr]
        i = pl.program_id(1)
        if rider:
            total = HEADS * nblk
            rider.emit(pl.program_id(0) * nblk + i, total, _rider_mid_step(total),
                       refs[3:3 + nr], refs[6 + nr:6 + 2 * nr], refs[7 + 2 * nr:])

        @pl.when(i == 0)
        def _():
            state[...] = jnp.zeros_like(state)

        cums, causal, _, row = _chunk_consts()
        lbv = lb_ref[...]
        hnv = hn_ref[...]

        rows = [slice(ci * CHUNK, (ci + 1) * CHUNK) for ci in range(ncht)]
        pres = _hgrn_chunks([p_ref[r, 0:HEAD_DIM] for r in rows], [p_ref[r, HEAD_DIM:2 * HEAD_DIM] for r in rows],
                            lbv, cums, row)
        vas = [p_ref[r, 2 * HEAD_DIM:3 * HEAD_DIM].astype(BF16) for r in rows]
        scores = [jnp.where(causal, _dot_nt(pre["lhs"], pre["rhs"]), 0.0).astype(BF16) for pre in pres]
        intra = [_dot(a, va) for a, va in zip(scores, vas)]
        qds = [pre["qd"].astype(BF16) for pre in pres]
        kdecs = [pre["kdec"].astype(BF16) for pre in pres]
        st = state[...]
        outs = []
        for ci in range(ncht):
            stb = st.astype(BF16)
            st_ref[ci, 0] = stb
            outs.append(intra[ci] + _dot_nt(qds[ci], stb))
            st = st * pres[ci]["ebl"] + _dot_tn(vas[ci], kdecs[ci])
        state[...] = st
        for r, o in zip(rows, outs):
            za = p_ref[r, 3 * HEAD_DIM:4 * HEAD_DIM]
            o_ref[r, :] = o
            ya_ref[r, :] = ((o * lax.rsqrt(_rowmean(o * o) + EPS)) * hnv * (za * _sigmoid(za))).astype(ya_ref.dtype)

    return pl.pallas_call(
        body, grid=(HEADS, nblk),
        in_specs=[pl.BlockSpec((t, HEAD_COLS), lambda h, i: (i, head_blk0 + h)),
                  pl.BlockSpec((1, HEAD_DIM), lambda h, i: (0, h)),
                  pl.BlockSpec((1, HEAD_DIM), lambda h, i: (0, h))] + [hbm] * nr,
        out_specs=[pl.BlockSpec((t, HEAD_DIM), lambda h, i: (i, h)),
                   pl.BlockSpec((t, HEAD_DIM), lambda h, i: (i, h)),
                   pl.BlockSpec((ncht, 1, HEAD_DIM, HEAD_DIM), lambda h, i: (i, h, 0, 0))] + [hbm] * nr,
        out_shape=[jax.ShapeDtypeStruct((s, WIDTH), BF16), jax.ShapeDtypeStruct((s, WIDTH), F32),
                   jax.ShapeDtypeStruct((s // CHUNK, HEADS, HEAD_DIM, HEAD_DIM), BF16)]
        + (rider.out_shape if rider else []),
        scratch_shapes=[pltpu.VMEM((HEAD_DIM, HEAD_DIM), F32)] + (rider.scratch if rider else []),
        compiler_params=_cparams(("arbitrary", "arbitrary")), name=name)(proj, lb, hn, *(rider.arrs if rider else []))


def hgrn_bwd(dya, proj, o_all, states, lb, hn, dproj, *, rider=None, name):
    s = proj.shape[0]
    t = _hgrn_tile(s)
    nblk = s // t
    ncht = t // CHUNK
    head_blk0 = GATE_COLS // HEAD_COLS
    nr = rider.n if rider else 0
    hbm = pl.BlockSpec(memory_space=pl.ANY)

    def body(*refs):
        dy_ref, p_ref, o_ref, st_ref, lb_ref, hn_ref = refs[:6]
        dp_ref, dhn_ref, dlb_ref = refs[7 + nr:10 + nr]
        dstate = refs[10 + 2 * nr]
        i = pl.program_id(1)
        if rider:
            total = HEADS * nblk
            rider.emit(pl.program_id(0) * nblk + i, total, _rider_mid_step(total),
                       refs[7:7 + nr], refs[10 + nr:10 + 2 * nr], refs[11 + 2 * nr:])

        @pl.when(i == 0)
        def _():
            dstate[...] = jnp.zeros_like(dstate)
            dhn_ref[...] = jnp.zeros_like(dhn_ref)
            dlb_ref[...] = jnp.zeros_like(dlb_ref)

        cums, causal, upper, row = _chunk_consts()
        lbv = lb_ref[...]
        hnv = hn_ref[...]

        chunks = range(ncht)
        rows = [slice(ci * CHUNK, (ci + 1) * CHUNK) for ci in chunks]
        qas = [p_ref[r, 0:HEAD_DIM] for r in rows]
        vbs = [p_ref[r, 2 * HEAD_DIM:3 * HEAD_DIM].astype(BF16) for r in rows]
        st0s = [st_ref[ci, 0] for ci in chunks]
        dzas, dobs = [], []
        dhn_acc = jnp.zeros_like(hnv)
        for r in rows:
            za = p_ref[r, 3 * HEAD_DIM:4 * HEAD_DIM]
            o = o_ref[r, :]
            dy = dy_ref[r, :]
            rn = lax.rsqrt(_rowmean(o * o) + EPS)
            on = o * rn
            sgz = _sigmoid(za)
            sz = za * sgz
            dzas.append(dy * on * hnv * (sgz * (1.0 + za * (1.0 - sgz))))
            dhn_acc = dhn_acc + _colsum(dy * on * sz)
            don = dy * hnv * sz
            dobs.append((rn * (don - on * _rowmean(don * on))).astype(BF16))
        pres = _hgrn_chunks(qas, [p_ref[r, HEAD_DIM:2 * HEAD_DIM] for r in rows], lbv, cums, row)
        scores = [jnp.where(causal, _dot_nt(pre["lhs"], pre["rhs"]), 0.0).astype(BF16) for pre in pres]
        das = [jnp.where(causal, _dot_nt(dob, vb), 0.0).astype(BF16) for dob, vb in zip(dobs, vbs)]
        dlhss = [_dot(da, pre["rhs"]) for da, pre in zip(das, pres)]
        drhss = [_dot_tn(da, pre["lhs"]) for da, pre in zip(das, pres)]
        dv_intra = [_dot_tn(a, dob) for a, dob in zip(scores, dobs)]
        dq_inter = [_dot(dob, st0) * pre["eb"] for dob, st0, pre in zip(dobs, st0s, pres)]
        qds = [pre["qd"].astype(BF16) for pre in pres]
        kdecs = [pre["kdec"].astype(BF16) for pre in pres]
        dst1 = dstate[...]
        dvs, dk_states, dbl_states = [None] * ncht, [None] * ncht, [None] * ncht
        for ci in reversed(chunks):
            dst1b = dst1.astype(BF16)
            dvs[ci] = dv_intra[ci] + _dot_nt(kdecs[ci], dst1b)
            dk_states[ci] = _dot(vbs[ci], dst1b) * pres[ci]["edec"]
            dbl_states[ci] = pres[ci]["ebl"] * _colsum(dst1 * st0s[ci].astype(F32))
            dst1 = dst1 * pres[ci]["ebl"] + _dot_tn(dobs[ci], qds[ci])
        dstate[...] = dst1
        dqs, dks, dbs, dbls = [], [], [], []
        for ci in chunks:
            pre = pres[ci]
            q, k = pre["q"], pre["k"]
            dq_a = jnp.zeros_like(q)
            dk_a = jnp.zeros_like(k)
            db = q * dq_inter[ci] - k * dk_states[ci]
            for j in range(N_SUB):
                cols = slice(j * HEAD_DIM, (j + 1) * HEAD_DIM)
                dlhs, drhs = dlhss[ci][:, cols], drhss[ci][:, cols]
                dq_a = dq_a + pre["dms"][j] * dlhs
                dk_a = dk_a + jnp.where(row // SUB == j, drhs, 0.0)
                db = db + (pre["lhs"][:, cols].astype(F32) * dlhs - pre["rhs"][:, cols].astype(F32) * drhs)
            dqs.append(dq_inter[ci] + pre["ec"] * dq_a)
            dks.append(dk_states[ci] + pre["enc"] * dk_a)
            dbs.append(db)
            dbls.append(_colsum(k * dk_states[ci]) + dbl_states[ci])
        dlfs = [_dot_split(upper, db) + dbl for db, dbl in zip(dbs, dbls)]
        dlb_acc = jnp.zeros_like(lbv)
        for ci in chunks:
            pre = pres[ci]
            sq, sa, sna, oml = pre["sq"], pre["sa"], pre["sna"], pre["oml"]
            dqa = dqs[ci] * (sq * (1.0 + qas[ci] * (1.0 - sq)))
            diff = jnp.where(pre["f"] >= MIN_FORGET, dlfs[ci] / pre["fc"], 0.0) - dks[ci]
            dlb_acc = dlb_acc + _colsum(diff * sna)
            dfa = diff * (oml * sa * sna)
            dp_ref[rows[ci], :] = jnp.concatenate([dqa, dfa, dvs[ci], dzas[ci]], axis=1).astype(dp_ref.dtype)
        dhn_ref[...] += dhn_acc
        dlb_ref[...] += dlb_acc

    rev = lambda h, i: (nblk - 1 - i, h)
    return pl.pallas_call(
        body, grid=(HEADS, nblk),
        in_specs=[pl.BlockSpec((t, HEAD_DIM), rev),
                  pl.BlockSpec((t, HEAD_COLS), lambda h, i: (nblk - 1 - i, head_blk0 + h)),
                  pl.BlockSpec((t, HEAD_DIM), rev),
                  pl.BlockSpec((ncht, 1, HEAD_DIM, HEAD_DIM), lambda h, i: (nblk - 1 - i, h, 0, 0)),
                  pl.BlockSpec((1, HEAD_DIM), lambda h, i: (0, h)),
                  pl.BlockSpec((1, HEAD_DIM), lambda h, i: (0, h)),
                  hbm] + [hbm] * nr,
        out_specs=[pl.BlockSpec((t, HEAD_COLS), lambda h, i: (nblk - 1 - i, head_blk0 + h)),
                   pl.BlockSpec((1, HEAD_DIM), lambda h, i: (0, h)),
                   pl.BlockSpec((1, HEAD_DIM), lambda h, i: (0, h))] + [hbm] * nr,
        out_shape=[jax.ShapeDtypeStruct(dproj.shape, dproj.dtype),
                   jax.ShapeDtypeStruct((1, WIDTH), F32), jax.ShapeDtypeStruct((1, WIDTH), F32)]
        + (rider.out_shape if rider else []),
        scratch_shapes=[pltpu.VMEM((HEAD_DIM, HEAD_DIM), F32)] + (rider.scratch if rider else []),
        input_output_aliases={6: 0},
        compiler_params=_cparams(("arbitrary", "arbitrary")),
        name=name)(dya, proj, o_all, states, lb, hn, dproj, *(rider.arrs if rider else []))


def _softmax_rows(lower):
    mx = jnp.max(lower, axis=0, keepdims=True)
    e = jnp.exp(lower - mx)
    return e / jnp.sum(e, axis=0, keepdims=True)


def lb_table(lower, *, name):
    depth, w = lower.shape

    def body(l_ref, o_ref):
        sm = _softmax_rows(l_ref[...])
        acc = jnp.zeros((1, w), F32)
        o_ref[0:1, :] = acc
        for l in range(1, depth):
            acc = acc + sm[l:l + 1, :]
            o_ref[l:l + 1, :] = acc

    return pl.pallas_call(body, out_shape=jax.ShapeDtypeStruct((depth, w), F32), name=name)(lower)


def lb_table_bwd(lower, dlb, *, name):
    depth, w = lower.shape

    def body(l_ref, d_ref, o_ref):
        sm = _softmax_rows(l_ref[...])
        dlbv = d_ref[...]
        dsm = [jnp.zeros((1, w), F32)]
        for i in range(1, depth):
            acc = jnp.zeros((1, w), F32)
            for l in range(i, depth):
                acc = acc + dlbv[l:l + 1, :]
            dsm.append(acc)
        inner = jnp.zeros((1, w), F32)
        for i in range(depth):
            inner = inner + sm[i:i + 1, :] * dsm[i]
        for i in range(depth):
            o_ref[i:i + 1, :] = sm[i:i + 1, :] * (dsm[i] - inner)

    return pl.pallas_call(body, out_shape=jax.ShapeDtypeStruct((depth, w), F32), name=name)(lower, dlb)


def w_ada_grad(c_all, dmod_cols, *, name):
    depth, _, cols = dmod_cols.shape
    d = c_all.shape[1]

    def body(c_ref, dm_ref, o_ref):
        cv = c_ref[...]
        ca = cv * _sigmoid(cv)
        o_ref[...] = _dot_tn(ca, dm_ref[...])

    return pl.pallas_call(
        body, grid=(depth,),
        in_specs=[pl.BlockSpec((N_DEV, d), lambda l: (0, 0)), pl.BlockSpec((None, N_DEV, cols), lambda l: (l, 0, 0))],
        out_specs=pl.BlockSpec((None, d, cols), lambda l: (l, 0, 0)),
        out_shape=jax.ShapeDtypeStruct((depth, d, cols), F32),
        compiler_params=_cparams(("arbitrary",)), name=name)(c_all, dmod_cols)


def sum_parts(parts, *, name):
    p, r, c = parts.shape

    def body(p_ref, o_ref):
        acc = p_ref[0]
        for j in range(1, p):
            acc = acc + p_ref[j]
        o_ref[...] = acc

    return pl.pallas_call(body, out_shape=jax.ShapeDtypeStruct((r, c), F32), name=name)(parts)


def _adam_rows(r, c):
    tr = r
    while tr * c * 4 > (1 << 20) and tr % 16 == 0:
        tr //= 2
    return tr


def _adam_update(w_ref, m_ref, v_ref, g_ref, go_ref, d_ref, mo_ref, vo_ref):
    g = g_ref[0].astype(F32)
    for j in range(1, g_ref.shape[0]):
        g = g + g_ref[j].astype(F32)
    mn = ADAM_B1 * m_ref[...] + (1.0 - ADAM_B1) * g
    vn = ADAM_B2 * v_ref[...] + (1.0 - ADAM_B2) * (g * g)
    m_hat = mn / (1.0 - ADAM_B1 ** ADAM_STEP)
    v_hat = vn / (1.0 - ADAM_B2 ** ADAM_STEP)
    go_ref[...] = g
    d_ref[...] = -ADAM_LR * (m_hat / (jnp.sqrt(v_hat) + ADAM_EPS) + ADAM_WD * w_ref[...])
    mo_ref[...] = mn
    vo_ref[...] = vn


def adamw(w, m, v, gparts, *, name):
    r, c = w.shape
    p = gparts.shape[0]
    tr = _adam_rows(r, c)
    spec = pl.BlockSpec((tr, c), lambda i: (i, 0))
    shp = jax.ShapeDtypeStruct((r, c), F32)
    return pl.pallas_call(
        functools.partial(_adam_update), grid=(r // tr,),
        in_specs=[spec, spec, spec, pl.BlockSpec((p, tr, c), lambda i: (0, i, 0))],
        out_specs=[spec] * 4, out_shape=[shp] * 4,
        compiler_params=_cparams(("arbitrary",)), name=name)(w, m, v, gparts)


def adamw_layers(w, m, v, gparts, *, name):
    depth, r, c = w.shape
    p = gparts[0].shape[0]
    tr = _adam_rows(r, c)

    def body(w_ref, m_ref, v_ref, *rest):
        g_refs, outs = rest[:depth], rest[depth:]
        layer = pl.program_id(0)
        for k in range(depth):
            @pl.when(layer == k)
            def _(k=k):
                _adam_update(w_ref, m_ref, v_ref, g_refs[k], *outs)

    spec = pl.BlockSpec((None, tr, c), lambda l, i: (l, i, 0))
    g_specs = [pl.BlockSpec((p, tr, c), functools.partial(lambda l, i, k: (0, jnp.where(l == k, i, 0), 0), k=k))
               for k in range(depth)]
    shp = jax.ShapeDtypeStruct((depth, r, c), F32)
    return pl.pallas_call(
        body, grid=(depth, r // tr),
        in_specs=[spec, spec, spec] + g_specs,
        out_specs=[spec] * 4, out_shape=[shp] * 4,
        compiler_params=_cparams(("arbitrary", "arbitrary")), name=name)(w, m, v, *gparts)


def _position():
    x, y, c = lax.axis_index("x"), lax.axis_index("y"), lax.axis_index("c")
    return x, y, c


def _dev_index(x, y, c):
    return 4 * x + 2 * y + c


def _gather_phases(ins, outs, send_sems, recv_sems, local_sems):
    n = len(ins)
    x, y, c = _position()
    me, sibling = (x, y, c), (x, y, 1 - c)
    chips = [(1 - x, y), (x, 1 - y), (1 - x, 1 - y)]

    def copy(a, k, block, to, own=False):
        slot = outs[a].at[_dev_index(*block)]
        return pltpu.make_async_remote_copy(
            src_ref=ins[a] if own else slot, dst_ref=slot,
            send_sem=send_sems.at[a * 7 + k], recv_sem=recv_sems.at[a * 7 + k],
            device_id=to, device_id_type=MESH)

    def mine(a):
        return pltpu.make_async_copy(ins[a], outs[a].at[_dev_index(*me)], local_sems.at[a])

    def first(a):
        return [copy(a, 0, me, sibling, True)] + [copy(a, 1 + j, me, (*chip, c), True) for j, chip in enumerate(chips)]

    def passed(a):
        return [copy(a, 4 + j, (*chip, c), sibling) for j, chip in enumerate(chips)]

    def start():
        for a in range(n):
            mine(a).start()
        for a in range(n):
            for cp in first(a):
                cp.start()

    def mid():
        for j, chip in enumerate(chips):
            for a in range(n):
                copy(a, 1 + j, (*chip, c), me).wait_recv()
                passed(a)[j].start()

    def finish():
        for a in range(n):
            copy(a, 0, sibling, me).wait_recv()
            for j, chip in enumerate(chips):
                copy(a, 4 + j, (*chip, 1 - c), me).wait_recv()
        for a in range(n):
            for cp in first(a) + passed(a):
                cp.wait_send()
            mine(a).wait()

    return start, mid, finish


def _scatter_phases(ins, outs, send_sems, recv_sems, local_sems):
    n = len(ins)
    x, y, c = _position()
    me = _dev_index(x, y, c)

    def peer(r):
        return (x ^ (r >> 2), y ^ ((r >> 1) & 1), c ^ (r & 1))

    def copy(a, r):
        to = peer(r)
        return pltpu.make_async_remote_copy(
            src_ref=ins[a].at[_dev_index(*to)], dst_ref=outs[a].at[me],
            send_sem=send_sems.at[a * 7 + r - 1], recv_sem=recv_sems.at[a * 7 + r - 1],
            device_id=to, device_id_type=MESH)

    def arrival(a, r):
        return pltpu.make_async_remote_copy(
            src_ref=ins[a].at[me], dst_ref=outs[a].at[_dev_index(*peer(r))],
            send_sem=send_sems.at[a * 7 + r - 1], recv_sem=recv_sems.at[a * 7 + r - 1],
            device_id=peer(r), device_id_type=MESH)

    def mine(a):
        return pltpu.make_async_copy(ins[a].at[me], outs[a].at[me], local_sems.at[a])

    def start():
        for a in range(n):
            mine(a).start()
        for r in range(1, N_DEV):
            for a in range(n):
                copy(a, r).start()

    def finish():
        for r in range(1, N_DEV):
            for a in range(n):
                arrival(a, r).wait_recv()
        for r in range(1, N_DEV):
            for a in range(n):
                copy(a, r).wait_send()
        for a in range(n):
            mine(a).wait()

    return start, None, finish


class Rider:
    def __init__(self, kind, arrs):
        self.kind, self.arrs, self.n = kind, list(arrs), len(arrs)
        lead = (N_DEV,) if kind == "gather" else ()
        self.out_shape = [jax.ShapeDtypeStruct(lead + a.shape, a.dtype) for a in self.arrs]
        self.scratch = [pltpu.SemaphoreType.DMA((7 * self.n,)), pltpu.SemaphoreType.DMA((7 * self.n,)),
                        pltpu.SemaphoreType.DMA((self.n,))]

    def phases(self, ins, outs, sems):
        make = _gather_phases if self.kind == "gather" else _scatter_phases
        return make(ins, outs, *sems)

    def emit(self, step, total, mid_step, ins, outs, sems):
        start, mid, finish = self.phases(ins, outs, sems)
        pl.when(step == 0)(start)
        if mid is not None:
            pl.when(step == mid_step)(mid)
        pl.when(step == total - 1)(finish)


def _standalone(rider, name):
    n = rider.n
    hbm = pl.BlockSpec(memory_space=pl.ANY)

    def body(*refs):
        start, mid, finish = rider.phases(refs[:n], refs[n:2 * n], refs[2 * n:])
        start()
        if mid is not None:
            mid()
        finish()

    return pl.pallas_call(body, out_shape=rider.out_shape, in_specs=[hbm] * n, out_specs=[hbm] * n,
                          scratch_shapes=rider.scratch, name=name)(*rider.arrs)


def all_gather(arrs, *, name):
    return _standalone(Rider("gather", arrs), name)


def scatter_parts(arrs, *, name):
    return _standalone(Rider("scatter", arrs), name)


def mod_exchange(c_all, w_ada, b_cols, *, name):
    depth, d, cols = w_ada.shape
    hbm = pl.BlockSpec(memory_space=pl.ANY)
    vmem = pl.BlockSpec(memory_space=pltpu.VMEM)

    def body(c_ref, w_ref, b_ref, out_ref, wbuf, sendbuf, send_sems, recv_sems, load_sem):
        x, y, c = _position()
        me = _dev_index(x, y, c)
        cv = c_ref[...]
        ca = cv * _sigmoid(cv)
        for l in range(depth):
            load = pltpu.make_async_copy(w_ref.at[l], wbuf, load_sem)
            load.start()
            load.wait()
            part = jnp.dot(ca, wbuf[...], preferred_element_type=F32,
                           precision=lax.Precision.HIGHEST) + b_ref[l:l + 1, :]
            for bi in range(N_DEV):
                sendbuf[bi, l:l + 1, :] = part[bi:bi + 1, :]

        def peer(r):
            return (x ^ (r >> 2), y ^ ((r >> 1) & 1), c ^ (r & 1))

        def copy(r):
            to = peer(r)
            return pltpu.make_async_remote_copy(
                src_ref=sendbuf.at[_dev_index(*to)], dst_ref=out_ref.at[me],
                send_sem=send_sems.at[r - 1], recv_sem=recv_sems.at[r - 1],
                device_id=to, device_id_type=MESH)

        def arrival(r):
            return pltpu.make_async_remote_copy(
                src_ref=sendbuf.at[me], dst_ref=out_ref.at[_dev_index(*peer(r))],
                send_sem=send_sems.at[r - 1], recv_sem=recv_sems.at[r - 1],
                device_id=peer(r), device_id_type=MESH)

        out_ref[me] = sendbuf[me]
        sends = [copy(r) for r in range(1, N_DEV)]
        for cp in sends:
            cp.start()
        for r in range(1, N_DEV):
            arrival(r).wait_recv()
        for cp in sends:
            cp.wait_send()

    return pl.pallas_call(
        body,
        out_shape=jax.ShapeDtypeStruct((N_DEV, depth, cols), F32),
        in_specs=[vmem, hbm, vmem], out_specs=vmem,
        scratch_shapes=[pltpu.VMEM((d, cols), F32), pltpu.VMEM((N_DEV, depth, cols), F32),
                        pltpu.SemaphoreType.DMA((7,)), pltpu.SemaphoreType.DMA((7,)), pltpu.SemaphoreType.DMA],
        compiler_params=pltpu.CompilerParams(vmem_limit_bytes=VMEM_LIMIT),
        name=name)(c_all, w_ada, b_cols)


def kernel(x, c, w_ada, b_ada, norm_pre, norm_post, w_in, lower_bounds, hgrn_norm, pool_w, pool_scale, w_proj_a, w_proj_b, w_out, loss_target, m_w_ada, m_b_ada, m_norm_pre, m_norm_post, m_w_in, m_lower_bounds, m_hgrn_norm, m_pool_w, m_pool_scale, m_w_proj_a, m_w_proj_b, m_w_out, v_w_ada, v_b_ada, v_norm_pre, v_norm_post, v_w_in, v_lower_bounds, v_hgrn_norm, v_pool_w, v_pool_scale, v_w_proj_a, v_w_proj_b, v_w_out):
    depth = w_in.shape[0]
    d = D_MODEL
    ada_cols = w_ada.shape[2]
    xi, yi, ci = _position()
    me = _dev_index(xi, yi, ci)
    xs = x[0]
    target = loss_target[0]
    ng = len(POOL_WINDOWS)

    def shards(l):
        return [w_in[l].astype(BF16), w_proj_a[l].astype(BF16), w_proj_b[l].astype(BF16),
                w_out[l].astype(BF16), pool_w[l].astype(BF16)]

    def other_weights(g_pa, g_pb, g_out, g_pool):
        return dict(
            pa=jnp.transpose(g_pa, (1, 0, 2)).reshape(WIDTH, d),
            pb=jnp.transpose(g_pb, (1, 0, 2)).reshape(WIDTH, d),
            w_out=g_out.reshape(d, d),
            pool=jnp.transpose(g_pool, (1, 0, 2, 3)).reshape(ng, POOL_GW, POOL_GW))

    (g_in0,) = all_gather(shards(0)[:1], name="gather_w_in")
    w_in_full = [permute_w_in(g_in0, name="permute_w_in")]
    others_full = []
    gathered = []

    (c_all,) = all_gather([c], name="gather_c")
    c_all = c_all.reshape(N_DEV, d)
    b_cols = lax.dynamic_slice_in_dim(b_ada, me * ada_cols, ada_cols, axis=1)
    mod_parts = mod_exchange(c_all, w_ada, b_cols, name="mod_exchange")
    mod = jnp.transpose(mod_parts, (1, 0, 2)).reshape(depth, 3 * d)
    lb_all = lb_table(lower_bounds, name="lb_table")

    saved = []
    cur = xs
    for l in range(depth):
        shift, scale, gate = mod[l:l + 1, :d], mod[l:l + 1, d:2 * d], mod[l:l + 1, 2 * d:]
        h = prenorm_fwd(cur, norm_pre[l:l + 1], shift, scale, name="prenorm_fwd")
        ride = (shards(0)[1:] if l == 0 else []) + (shards(l + 1)[:1] if l + 1 < depth else [])
        if ride:
            proj, *got = matmul_nn(h, w_in_full[l], tm=1024, tn=1024, out_dtype=F32,
                                   rider=Rider("gather", ride), name="mm_w_in_gather%d" % len(ride))
            if l == 0:
                others_full.append(other_weights(*got[:4]))
                got = got[4:]
            if got:
                w_in_full.append(permute_w_in(got[0], name="permute_w_in"))
        else:
            proj = matmul_nn(h, w_in_full[l], tm=1024, tn=1024, out_dtype=F32, name="mm_w_in")
        if l + 1 < depth:
            y_a, o_all, states, *got = hgrn_fwd(proj, lb_all[l:l + 1], hgrn_norm[l:l + 1],
                                                 rider=Rider("gather", shards(l + 1)[1:]), name="hgrn_fwd_gather")
            others_full.append(other_weights(*got))
        else:
            y_a, o_all, states = hgrn_fwd(proj, lb_all[l:l + 1], hgrn_norm[l:l + 1], name="hgrn_fwd")
        w = dict(w_in=w_in_full[l], **others_full[l])
        gathered.append(w)
        y_b, pooled, mixed = pool_fwd(proj, w["pool"], pool_scale[l:l + 1], name="pool_fwd")
        pa, pb, merged = proj_gate_fwd(y_a, y_b, w["pa"], w["pb"], proj, name="proj_gate_fwd")
        out = matmul_nn(merged, w["w_out"], tm=1024, tn=1024, out_dtype=F32, name="mm_w_out")
        nxt = postnorm_fwd(cur, out, norm_post[l:l + 1], gate, name="postnorm_fwd")
        saved.append(dict(x=cur, h=h, proj=proj, y_a=y_a, o=o_all, states=states, y_b=y_b, pooled=pooled,
                          mixed=mixed, pa=pa, pb=pb, merged=merged, out=out, scale=scale, gate=gate))
        cur = nxt

    g, loss_part = loss_head(cur, target, name="loss_head")
    loss = lax.psum(loss_part[0, 0], ("x", "y", "c"))

    small = [None] * depth
    big_in = [None] * depth
    big_others = [None] * depth
    pend_in, pend_others = None, None
    for l in reversed(range(depth)):
        w, sv = gathered[l], saved[l]
        dout, dgate, dnpost = postnorm_bwd(g, sv["out"], norm_post[l:l + 1], sv["gate"], name="postnorm_bwd")
        dw_out = matmul_tn(sv["merged"], dout, tm=2048, tn=1024, out_dtype=BF16, name="mm_w_out_dw")
        dpa, dpb, dproj = gate_bwd(dout, w["w_out"], sv["proj"], sv["pa"], sv["pb"], name="gate_bwd")
        dya = matmul_nt(dpa, w["pa"], tm=1024, tn=2048, out_dtype=F32, name="mm_proj_a_dx")
        dyb = matmul_nt(dpb, w["pb"], tm=1024, tn=2048, out_dtype=F32, name="mm_proj_b_dx")
        dw_pa = matmul_tn(sv["y_a"], dpa, tm=2048, tn=2048, out_dtype=BF16, name="mm_proj_a_dw")
        dw_pb = matmul_tn(sv["y_b"], dpb, tm=2048, tn=2048, out_dtype=BF16, name="mm_proj_b_dw")
        dproj, dpool_w, dpool_scale = pool_bwd(dyb, sv["proj"], sv["pooled"], sv["mixed"], w["pool"],
                                               pool_scale[l:l + 1], dproj, name="pool_bwd")
        by_owner = lambda t: jnp.transpose(t.reshape(WIDTH, N_DEV, d // N_DEV), (1, 0, 2))
        others = [by_owner(dw_pa), by_owner(dw_pb), dw_out.reshape(N_DEV, d // N_DEV, d),
                  jnp.transpose(dpool_w.astype(BF16).reshape(ng, N_DEV, POOL_GW // N_DEV, POOL_GW), (1, 0, 2, 3))]
        ride = (pend_others or []) + (others if l == 0 else [])
        if ride:
            dproj, dhn, dlb, *recv = hgrn_bwd(dya, sv["proj"], sv["o"], sv["states"], lb_all[l:l + 1],
                                              hgrn_norm[l:l + 1], dproj, rider=Rider("scatter", ride),
                                              name="hgrn_bwd_scatter%d" % len(ride))
            if pend_others:
                big_others[l + 1], recv = recv[:len(pend_others)], recv[len(pend_others):]
            if l == 0:
                big_others[0] = recv
        else:
            dproj, dhn, dlb = hgrn_bwd(dya, sv["proj"], sv["o"], sv["states"], lb_all[l:l + 1],
                                       hgrn_norm[l:l + 1], dproj, name="hgrn_bwd")
        above = Rider("scatter", [pend_in]) if pend_in is not None else None
        if l > 0:
            if above:
                dh, big_in[l + 1] = matmul_nt(dproj, w["w_in"], tm=512, tn=2048, out_dtype=F32, rider=above,
                                              name="mm_w_in_dx_scatter")
            else:
                dh = matmul_nt(dproj, w["w_in"], tm=512, tn=2048, out_dtype=F32, name="mm_w_in_dx")
            dw_in = matmul_tn(sv["h"], dproj, tm=2048, tn=1024, out_dtype=BF16, name="mm_w_in_dw")
            pend_in, pend_others = unpermute_w_in(dw_in, name="unpermute_w_in"), others
        else:
            if above:
                dw_in, big_in[1] = matmul_tn(sv["h"], dproj, tm=2048, tn=1024, out_dtype=BF16, rider=above,
                                             name="mm_w_in_dw_scatter")
            else:
                dw_in = matmul_tn(sv["h"], dproj, tm=2048, tn=1024, out_dtype=BF16, name="mm_w_in_dw")
            dh, big_in[0] = matmul_nt(dproj, w["w_in"], tm=512, tn=2048, out_dtype=F32,
                                      rider=Rider("scatter", [unpermute_w_in(dw_in, name="unpermute_w_in")]),
                                      name="mm_w_in_dx_scatter")
        g, dshift, dscale, dnpre = prenorm_bwd(dh, sv["x"], norm_pre[l:l + 1], sv["scale"], g, name="prenorm_bwd")
        small[l] = jnp.concatenate([dshift, dscale, dgate, dnpre, dnpost, dlb, dhn, dpool_scale], axis=1)
    grad_x = g[None]
    big = [[big_in[l]] + list(big_others[l]) for l in range(depth)]

    small_mine = jnp.concatenate(small, axis=0)
    (small_all,) = all_gather([small_mine], name="gather_small")
    small_sum = sum_parts(small_all, name="sum_small")
    dmod_all = small_all[:, :, :3 * d]
    dmod_cols = jnp.transpose(lax.dynamic_slice_in_dim(dmod_all, me * ada_cols, ada_cols, axis=2), (1, 0, 2))
    g_w_ada = w_ada_grad(c_all, dmod_cols, name="w_ada_grad")
    off = 3 * d
    g_b_ada = small_sum[:, :off]
    g_npre = small_sum[:, off:off + d]
    g_npost = small_sum[:, off + d:off + 2 * d]
    g_lb_tab = small_sum[:, off + 2 * d:off + 2 * d + WIDTH]
    g_hn = small_sum[:, off + 2 * d + WIDTH:off + 2 * d + 2 * WIDTH]
    g_ps = small_sum[:, off + 2 * d + 2 * WIDTH:]
    g_lower = lb_table_bwd(lower_bounds, g_lb_tab, name="lb_table_bwd")

    def update(wt, mt, vt, gparts, shape2, name):
        outs = adamw(wt.reshape(shape2), mt.reshape(shape2), vt.reshape(shape2), gparts, name=name)
        return [o.reshape(wt.shape) for o in outs]

    def update_layers(wt, mt, vt, kind, name):
        shape3 = (depth, -1, wt.shape[-1])
        w3 = wt.reshape(shape3)
        gps = [big[l][kind].reshape((N_DEV,) + w3.shape[1:]) for l in range(depth)]
        outs = adamw_layers(w3, mt.reshape(shape3), vt.reshape(shape3), gps, name=name)
        return [o.reshape(wt.shape) for o in outs]

    def update_small(wt, mt, vt, gt, name):
        shape2 = (-1, wt.shape[-1])
        return update(wt, mt, vt, gt.reshape(shape2)[None], shape2, name)

    res = {
        "w_ada": update_small(w_ada, m_w_ada, v_w_ada, g_w_ada, "adamw_w_ada"),
        "b_ada": update_small(b_ada, m_b_ada, v_b_ada, g_b_ada, "adamw_b_ada"),
        "norm_pre": update_small(norm_pre, m_norm_pre, v_norm_pre, g_npre, "adamw_norm_pre"),
        "norm_post": update_small(norm_post, m_norm_post, v_norm_post, g_npost, "adamw_norm_post"),
        "w_in": update_layers(w_in, m_w_in, v_w_in, 0, "adamw_w_in"),
        "lower_bounds": update_small(lower_bounds, m_lower_bounds, v_lower_bounds, g_lower, "adamw_lower_bounds"),
        "hgrn_norm": update_small(hgrn_norm, m_hgrn_norm, v_hgrn_norm, g_hn, "adamw_hgrn_norm"),
        "pool_w": update_layers(pool_w, m_pool_w, v_pool_w, 4, "adamw_pool_w"),
        "pool_scale": update_small(pool_scale, m_pool_scale, v_pool_scale, g_ps, "adamw_pool_scale"),
        "w_proj_a": update_layers(w_proj_a, m_w_proj_a, v_w_proj_a, 1, "adamw_w_proj_a"),
        "w_proj_b": update_layers(w_proj_b, m_w_proj_b, v_w_proj_b, 2, "adamw_w_proj_b"),
        "w_out": update_layers(w_out, m_w_out, v_w_out, 3, "adamw_w_out"),
    }
    order = ["w_ada", "b_ada", "norm_pre", "norm_post", "w_in", "lower_bounds", "hgrn_norm", "pool_w",
             "pool_scale", "w_proj_a", "w_proj_b", "w_out"]
    outs = [loss, grad_x]
    for k in range(4):
        outs += [res[nm][k] for nm in order]
    return tuple(outs)
```

```python
import functools

import jax
import jax.numpy as jnp
from jax import lax
from jax.experimental import pallas as pl
from jax.experimental.pallas import tpu as pltpu

F32 = jnp.float32
BF16 = jnp.bfloat16
MESH = pl.DeviceIdType.MESH

N_DEV = 8
EPS = 1e-6
MIN_FORGET = 1e-30
D_MODEL = 2048
HEADS = 8
HEAD_DIM = 128
CHUNK = 64
SUB = 16
N_SUB = CHUNK // SUB
WIDTH = 1024
POOL_WINDOWS = (2, 4, 8, 16)
POOL_GW = 256
HALO = 16
IN_COLS = 10240
LANE = 128
N_COLBLK = IN_COLS // LANE
GATE_COLS = 4096
HEAD_COLS = 4 * HEAD_DIM
POOL_COLS = 2 * POOL_GW
GP_COLS = GATE_COLS + len(POOL_WINDOWS) * POOL_COLS
MAX_EXP = 80.0

ADAM_LR = 0.001
ADAM_B1 = 0.9
ADAM_B2 = 0.999
ADAM_EPS = 1e-08
ADAM_WD = 0.01
ADAM_STEP = 10

VMEM_LIMIT = 56 * 1024 * 1024


def _cparams(sem=None):
    return pltpu.CompilerParams(dimension_semantics=sem, vmem_limit_bytes=VMEM_LIMIT)


def _sigmoid(v):
    return 1.0 / (1.0 + jnp.exp(-v))


def _dot(a, b):
    return jnp.dot(a, b, preferred_element_type=F32)


def _dot_nt(a, b):
    return lax.dot_general(a, b, (((1,), (1,)), ((), ())), preferred_element_type=F32)


def _dot_tn(a, b):
    return lax.dot_general(a, b, (((0,), (0,)), ((), ())), preferred_element_type=F32)


def _colsum(v):
    return jnp.sum(v, axis=0, keepdims=True)


def _rowmean(v):
    return jnp.mean(v, axis=-1, keepdims=True)


def _orig_block_static(n):
    if n < 32:
        return n + 48
    if n < 48:
        m = n - 32
        t = m % 4
        return 32 + 2 * (m // 4) + (t % 2) + 8 * (t // 2)
    m = n - 48
    return 8 * (m % 4) + m // 4


def _accumulate(step, steps, prod, o_ref, acc_ref):
    if steps == 1:
        o_ref[...] = prod.astype(o_ref.dtype)
        return

    @pl.when(step == 0)
    def _():
        acc_ref[...] = prod

    @pl.when(step > 0)
    def _():
        acc_ref[...] += prod

    @pl.when(step == steps - 1)
    def _():
        o_ref[...] = acc_ref[...].astype(o_ref.dtype)


def _matmul_call(dot, a, b, *, grid, in_specs, out_spec, out_shape, acc_shape, steps, rider, name):
    nr = rider.n if rider else 0
    hbm = pl.BlockSpec(memory_space=pl.ANY)
    has_acc = steps > 1

    def body(*refs):
        a_ref, b_ref = refs[:2]
        o_ref = refs[2 + nr]
        scratch = refs[3 + 2 * nr:]
        if rider:
            total = grid[0] * grid[1]
            rider.emit(pl.program_id(0) * grid[1] + pl.program_id(1), total, _rider_mid_step(total),
                       refs[2:2 + nr], refs[3 + nr:3 + 2 * nr], scratch[1 if has_acc else 0:])
        _accumulate(pl.program_id(1), steps, dot(a_ref[...], b_ref[...]), o_ref, scratch[0] if has_acc else None)

    outs = pl.pallas_call(
        body, grid=grid,
        in_specs=in_specs + [hbm] * nr, out_specs=[out_spec] + [hbm] * nr,
        out_shape=[out_shape] + (rider.out_shape if rider else []),
        scratch_shapes=([pltpu.VMEM(acc_shape, F32)] if has_acc else []) + (rider.scratch if rider else []),
        compiler_params=_cparams(("arbitrary", "arbitrary")),
        name=name)(a, b, *(rider.arrs if rider else []))
    return outs if rider else outs[0]


def matmul_nn(a, b, *, tm, tn, out_dtype, name, rider=None, cols=None):
    m, k = a.shape
    col0, n = cols if cols else (0, b.shape[1])
    tm = min(tm, m)
    j0 = col0 // tn
    return _matmul_call(
        _dot, a, b, grid=(n // tn, m // tm),
        in_specs=[pl.BlockSpec((tm, k), lambda j, i: (i, 0)), pl.BlockSpec((k, tn), lambda j, i: (0, j0 + j))],
        out_spec=pl.BlockSpec((tm, tn), lambda j, i: (i, j)),
        out_shape=jax.ShapeDtypeStruct((m, n), out_dtype), acc_shape=None, steps=1, rider=rider, name=name)


def matmul_nt(a, b, *, tm, tn, out_dtype, name, rider=None):
    m, n = a.shape
    k = b.shape[0]
    tm = min(tm, m)
    return _matmul_call(
        _dot_nt, a, b, grid=(m // tm, n // tn),
        in_specs=[pl.BlockSpec((tm, tn), lambda i, j: (i, j)), pl.BlockSpec((k, tn), lambda i, j: (0, j))],
        out_spec=pl.BlockSpec((tm, k), lambda i, j: (i, 0)),
        out_shape=jax.ShapeDtypeStruct((m, k), out_dtype), acc_shape=(tm, k), steps=n // tn, rider=rider, name=name)


def matmul_tn(a, b, *, tm, tn, out_dtype, name, rider=None):
    m, k = a.shape
    n = b.shape[1]
    tm = min(tm, m)
    return _matmul_call(
        _dot_tn, a, b, grid=(n // tn, m // tm),
        in_specs=[pl.BlockSpec((tm, k), lambda j, i: (i, 0)), pl.BlockSpec((tm, tn), lambda j, i: (i, j))],
        out_spec=pl.BlockSpec((k, tn), lambda j, i: (0, j)),
        out_shape=jax.ShapeDtypeStruct((k, n), out_dtype), acc_shape=(k, tn), steps=m // tm, rider=rider, name=name)


def permute_w_in(staged, *, name):
    k = staged.shape[1]
    own = IN_COLS // N_DEV
    tr = min(256, k)

    def body(i_ref, o_ref):
        for nb in range(N_COLBLK):
            dev, col = divmod(_orig_block_static(nb) * LANE, own)
            o_ref[:, nb * LANE:(nb + 1) * LANE] = i_ref[dev, :, col:col + LANE]

    return pl.pallas_call(
        body, grid=(k // tr,),
        in_specs=[pl.BlockSpec((N_DEV, tr, own), lambda i: (0, i, 0))],
        out_specs=pl.BlockSpec((tr, IN_COLS), lambda i: (i, 0)),
        out_shape=jax.ShapeDtypeStruct((k, IN_COLS), staged.dtype),
        compiler_params=_cparams(("arbitrary",)), name=name)(staged)


def unpermute_w_in(dw, *, name):
    k = dw.shape[0]
    own = IN_COLS // N_DEV
    tr = min(256, k)

    def body(i_ref, o_ref):
        for nb in range(N_COLBLK):
            dev, col = divmod(_orig_block_static(nb) * LANE, own)
            o_ref[dev, :, col:col + LANE] = i_ref[:, nb * LANE:(nb + 1) * LANE]

    return pl.pallas_call(
        body, grid=(k // tr,),
        in_specs=[pl.BlockSpec((tr, IN_COLS), lambda i: (i, 0))],
        out_specs=pl.BlockSpec((N_DEV, tr, own), lambda i: (0, i, 0)),
        out_shape=jax.ShapeDtypeStruct((N_DEV, k, own), dw.dtype),
        compiler_params=_cparams(("arbitrary",)), name=name)(dw)


def _row_tile(s):
    return min(256, s)


def _norm_tile(s):
    return min(512, s)


def _row_spec(t, w, col=0):
    return pl.BlockSpec((t, w), lambda i: (i, col))


def _vec_spec(w):
    return pl.BlockSpec((1, w), lambda i: (0, 0))


def prenorm_fwd(x, gain, shift, scale, *, name):
    s, d = x.shape
    t = _norm_tile(s)

    def body(x_ref, g_ref, sh_ref, sc_ref, h_ref):
        xv = x_ref[...]
        r = lax.rsqrt(_rowmean(xv * xv) + EPS)
        h_ref[...] = ((xv * r) * g_ref[...] * (1.0 + sc_ref[...]) + sh_ref[...]).astype(h_ref.dtype)

    return pl.pallas_call(
        body, grid=(s // t,),
        in_specs=[_row_spec(t, d), _vec_spec(d), _vec_spec(d), _vec_spec(d)],
        out_specs=_row_spec(t, d), out_shape=jax.ShapeDtypeStruct((s, d), BF16),
        compiler_params=_cparams(("arbitrary",)), name=name)(x, gain, shift, scale)


def prenorm_bwd(dh, x, gain, scale, g_res, *, name):
    s, d = x.shape
    t = _norm_tile(s)

    def body(dh_ref, x_ref, g_ref, sc_ref, gr_ref, dx_ref, dsh_ref, dsc_ref, dg_ref):
        i = pl.program_id(0)
        xv = x_ref[...]
        dhv = dh_ref[...]
        r = lax.rsqrt(_rowmean(xv * xv) + EPS)
        xn = xv * r
        gain_v = g_ref[...]
        one_sc = 1.0 + sc_ref[...]
        dyn = dhv * one_sc
        dxn = dyn * gain_v
        dx_ref[...] = r * (dxn - xn * _rowmean(dxn * xn)) + gr_ref[...]
        p_sh = _colsum(dhv)
        p_sc = _colsum(dhv * (xn * gain_v))
        p_g = _colsum(dyn * xn)

        @pl.when(i == 0)
        def _():
            dsh_ref[...] = p_sh
            dsc_ref[...] = p_sc
            dg_ref[...] = p_g

        @pl.when(i > 0)
        def _():
            dsh_ref[...] += p_sh
            dsc_ref[...] += p_sc
            dg_ref[...] += p_g

    vec = jax.ShapeDtypeStruct((1, d), F32)
    return pl.pallas_call(
        body, grid=(s // t,),
        in_specs=[_row_spec(t, d), _row_spec(t, d), _vec_spec(d), _vec_spec(d), _row_spec(t, d)],
        out_specs=[_row_spec(t, d), _vec_spec(d), _vec_spec(d), _vec_spec(d)],
        out_shape=[jax.ShapeDtypeStruct((s, d), F32), vec, vec, vec],
        compiler_params=_cparams(("arbitrary",)), name=name)(dh, x, gain, scale, g_res)


def postnorm_fwd(x, out, gain, gate, *, name):
    s, d = x.shape
    t = _norm_tile(s)

    def body(x_ref, o_ref, g_ref, gt_ref, y_ref):
        ov = o_ref[...]
        r = lax.rsqrt(_rowmean(ov * ov) + EPS)
        y_ref[...] = x_ref[...] + gt_ref[...] * ((ov * r) * g_ref[...])

    return pl.pallas_call(
        body, grid=(s // t,),
        in_specs=[_row_spec(t, d), _row_spec(t, d), _vec_spec(d), _vec_spec(d)],
        out_specs=_row_spec(t, d), out_shape=jax.ShapeDtypeStruct((s, d), F32),
        compiler_params=_cparams(("arbitrary",)), name=name)(x, out, gain, gate)


def postnorm_bwd(g, out, gain, gate, *, name):
    s, d = out.shape
    t = _norm_tile(s)

    def body(g_ref, o_ref, gn_ref, gt_ref, do_ref, dgt_ref, dgn_ref):
        i = pl.program_id(0)
        ov = o_ref[...]
        gv = g_ref[...]
        r = lax.rsqrt(_rowmean(ov * ov) + EPS)
        on = ov * r
        gain_v = gn_ref[...]
        gate_v = gt_ref[...]
        dn = gv * gate_v
        don = dn * gain_v
        do_ref[...] = (r * (don - on * _rowmean(don * on))).astype(do_ref.dtype)
        p_gt = _colsum(gv * (on * gain_v))
        p_gn = _colsum(dn * on)

        @pl.when(i == 0)
        def _():
            dgt_ref[...] = p_gt
            dgn_ref[...] = p_gn

        @pl.when(i > 0)
        def _():
            dgt_ref[...] += p_gt
            dgn_ref[...] += p_gn

    vec = jax.ShapeDtypeStruct((1, d), F32)
    return pl.pallas_call(
        body, grid=(s // t,),
        in_specs=[_row_spec(t, d), _row_spec(t, d), _vec_spec(d), _vec_spec(d)],
        out_specs=[_row_spec(t, d), _vec_spec(d), _vec_spec(d)],
        out_shape=[jax.ShapeDtypeStruct((s, d), BF16), vec, vec],
        compiler_params=_cparams(("arbitrary",)), name=name)(g, out, gain, gate)


def loss_head(y, target, *, name):
    s, d = y.shape
    t = _norm_tile(s)
    steps = s // t

    def body(y_ref, t_ref, dy_ref, loss_ref, acc_ref):
        i = pl.program_id(0)
        err = y_ref[...] - t_ref[...]
        dy_ref[...] = err * (1.0 / d)
        part = _colsum(err * err)

        @pl.when(i == 0)
        def _():
            acc_ref[...] = part

        @pl.when(i > 0)
        def _():
            acc_ref[...] += part

        @pl.when(i == steps - 1)
        def _():
            loss_ref[...] = jnp.sum(acc_ref[...], axis=1, keepdims=True) * (0.5 / d)

    return pl.pallas_call(
        body, grid=(steps,),
        in_specs=[_row_spec(t, d), _row_spec(t, d)],
        out_specs=[_row_spec(t, d), pl.BlockSpec((1, 1), lambda i: (0, 0))],
        out_shape=[jax.ShapeDtypeStruct((s, d), F32), jax.ShapeDtypeStruct((1, 1), F32)],
        scratch_shapes=[pltpu.VMEM((1, d), F32)],
        compiler_params=_cparams(("arbitrary",)), name=name)(y, target)


def _full_spec(shape):
    return pl.BlockSpec(shape, lambda i: (0,) * len(shape))


def proj_gate_fwd(y_a, y_b, w_pa, w_pb, proj, *, name):
    s, width = y_a.shape
    d = w_pa.shape[1]
    t = _norm_tile(s)

    def body(ya_ref, yb_ref, wa_ref, wb_ref, ga_ref, gb_ref, pa_ref, pb_ref, m_ref):
        pa = _dot(ya_ref[...], wa_ref[...])
        pb = _dot(yb_ref[...], wb_ref[...])
        pa_ref[...] = pa.astype(pa_ref.dtype)
        pb_ref[...] = pb.astype(pb_ref.dtype)
        m_ref[...] = (_sigmoid(ga_ref[...].astype(F32)) * pa
                      + _sigmoid(gb_ref[...].astype(F32)) * pb).astype(m_ref.dtype)

    out = jax.ShapeDtypeStruct((s, d), BF16)
    return pl.pallas_call(
        body, grid=(s // t,),
        in_specs=[_row_spec(t, width), _row_spec(t, width), _full_spec((width, d)), _full_spec((width, d)),
                  _row_spec(t, d, 0), _row_spec(t, d, 1)],
        out_specs=[_row_spec(t, d)] * 3, out_shape=[out] * 3,
        compiler_params=_cparams(("arbitrary",)), name=name)(y_a, y_b, w_pa, w_pb, proj, proj)


def gate_bwd(dout, w_out, proj, pa, pb, *, name):
    s, d = pa.shape
    t = _norm_tile(s)

    def body(do_ref, w_ref, ga_ref, gb_ref, pa_ref, pb_ref, dpa_ref, dpb_ref, dp_ref):
        dm = _dot_nt(do_ref[...], w_ref[...])
        sa = _sigmoid(ga_ref[...].astype(F32))
        sb = _sigmoid(gb_ref[...].astype(F32))
        dpa_ref[...] = (dm * sa).astype(dpa_ref.dtype)
        dpb_ref[...] = (dm * sb).astype(dpb_ref.dtype)
        dp_ref[:, :d] = (dm * pa_ref[...].astype(F32) * sa * (1.0 - sa)).astype(dp_ref.dtype)
        dp_ref[:, d:] = (dm * pb_ref[...].astype(F32) * sb * (1.0 - sb)).astype(dp_ref.dtype)

    return pl.pallas_call(
        body, grid=(s // t,),
        in_specs=[_row_spec(t, d), _full_spec((d, d)), _row_spec(t, d, 0), _row_spec(t, d, 1),
                  _row_spec(t, d), _row_spec(t, d)],
        out_specs=[_row_spec(t, d), _row_spec(t, d), _row_spec(t, 2 * d, 0)],
        out_shape=[jax.ShapeDtypeStruct((s, d), BF16), jax.ShapeDtypeStruct((s, d), BF16),
                   jax.ShapeDtypeStruct((s, IN_COLS), BF16)],
        compiler_params=_cparams(("arbitrary",)), name=name)(dout, w_out, proj, proj, pa, pb)


def _pool_tile(s):
    return min(256, s)


def pool_fwd(proj, pw, ps, *, name):
    s = proj.shape[0]
    t = _pool_tile(s)
    pool_blk = GATE_COLS // (len(POOL_WINDOWS) * POOL_COLS)

    def body(p_ref, halo_ref, pw_ref, ps_ref, yb_ref, pooled_ref, mixed_ref):
        i = pl.program_id(0)
        halo = jnp.where(i == 0, 0.0, halo_ref[...].astype(F32))
        row = i * t + lax.broadcasted_iota(jnp.int32, (t, 1), 0)
        for g, w in enumerate(POOL_WINDOWS):
            vb = p_ref[:, g * POOL_COLS:g * POOL_COLS + POOL_GW].astype(F32)
            zb = p_ref[:, g * POOL_COLS + POOL_GW:(g + 1) * POOL_COLS].astype(F32)
            acc = jnp.concatenate([halo[:, g * POOL_COLS:g * POOL_COLS + POOL_GW], vb], axis=0)
            sh = 1
            while sh < w:
                acc = acc + pltpu.roll(acc, sh, axis=0)
                sh *= 2
            cnt = jnp.minimum(row + 1, w).astype(F32)
            pooled = acc[HALO:, :] / cnt - vb
            mixed = _dot(pooled.astype(BF16), pw_ref[g])
            cols = slice(g * POOL_GW, (g + 1) * POOL_GW)
            yb = mixed * ps_ref[:, cols] * (zb * _sigmoid(zb))
            yb_ref[:, cols] = yb.astype(yb_ref.dtype)
            pooled_ref[:, cols] = pooled.astype(pooled_ref.dtype)
            mixed_ref[:, cols] = mixed

    wide = len(POOL_WINDOWS) * POOL_COLS
    return pl.pallas_call(
        body, grid=(s // t,),
        in_specs=[pl.BlockSpec((t, wide), lambda i: (i, pool_blk)),
                  pl.BlockSpec((HALO, wide), lambda i: (jnp.maximum(i * (t // HALO) - 1, 0), pool_blk)),
                  pl.BlockSpec((len(POOL_WINDOWS), POOL_GW, POOL_GW), lambda i: (0, 0, 0)),
                  _vec_spec(WIDTH)],
        out_specs=[_row_spec(t, WIDTH)] * 3,
        out_shape=[jax.ShapeDtypeStruct((s, WIDTH), BF16), jax.ShapeDtypeStruct((s, WIDTH), BF16),
                   jax.ShapeDtypeStruct((s, WIDTH), F32)],
        compiler_params=_cparams(("arbitrary",)), name=name)(proj, proj, pw, ps)


def pool_bwd(dyb, proj, pooled, mixed, pw, ps, dproj, *, name):
    s = proj.shape[0]
    t = _pool_tile(s)
    nblk = s // t
    ng = len(POOL_WINDOWS)
    wide = ng * POOL_COLS
    pool_blk = GATE_COLS // wide

    def body(dy_ref, p_ref, pooled_ref, mixed_ref, pw_ref, ps_ref, dp_any, dp_ref, dpw_ref, dps_ref, carry):
        del dp_any
        i = pl.program_id(0)
        ii = nblk - 1 - i

        @pl.when(i == 0)
        def _():
            carry[...] = jnp.zeros_like(carry)
            dpw_ref[...] = jnp.zeros_like(dpw_ref)
            dps_ref[...] = jnp.zeros_like(dps_ref)

        row = ii * t + lax.broadcasted_iota(jnp.int32, (t, 1), 0)
        for g, w in enumerate(POOL_WINDOWS):
            cols = slice(g * POOL_GW, (g + 1) * POOL_GW)
            zb = p_ref[:, g * POOL_COLS + POOL_GW:(g + 1) * POOL_COLS].astype(F32)
            dy = dy_ref[:, cols]
            mx = mixed_ref[:, cols]
            sc = ps_ref[:, cols]
            sg = _sigmoid(zb)
            dzb = dy * (mx * sc) * (sg * (1.0 + zb * (1.0 - sg)))
            dpm = dy * (zb * sg)
            dps_ref[:, cols] += _colsum(dpm * mx)
            dmixed = (dpm * sc).astype(BF16)
            dpooled = _dot_nt(dmixed, pw_ref[g])
            dpw_ref[g] += _dot_tn(pooled_ref[:, cols], dmixed)
            cnt = jnp.minimum(row + 1, w).astype(F32)
            u = dpooled / cnt
            acc = jnp.concatenate([u, carry[:, cols]], axis=0)
            sh = 1
            while sh < w:
                acc = acc + pltpu.roll(acc, t + HALO - sh, axis=0)
                sh *= 2
            carry[:, cols] = u[:HALO, :]
            dp_ref[:, g * POOL_COLS:g * POOL_COLS + POOL_GW] = (acc[:t, :] - dpooled).astype(dp_ref.dtype)
            dp_ref[:, g * POOL_COLS + POOL_GW:(g + 1) * POOL_COLS] = dzb.astype(dp_ref.dtype)

    rev = lambda i: (nblk - 1 - i, 0)
    return pl.pallas_call(
        body, grid=(nblk,),
        in_specs=[pl.BlockSpec((t, WIDTH), rev),
                  pl.BlockSpec((t, wide), lambda i: (nblk - 1 - i, pool_blk)),
                  pl.BlockSpec((t, WIDTH), rev), pl.BlockSpec((t, WIDTH), rev),
                  pl.BlockSpec((ng, POOL_GW, POOL_GW), lambda i: (0, 0, 0)),
                  _vec_spec(WIDTH),
                  pl.BlockSpec(memory_space=pl.ANY)],
        out_specs=[pl.BlockSpec((t, wide), lambda i: (nblk - 1 - i, pool_blk)),
                   pl.BlockSpec((ng, POOL_GW, POOL_GW), lambda i: (0, 0, 0)),
                   _vec_spec(WIDTH)],
        out_shape=[jax.ShapeDtypeStruct(dproj.shape, dproj.dtype),
                   jax.ShapeDtypeStruct((ng, POOL_GW, POOL_GW), F32),
                   jax.ShapeDtypeStruct((1, WIDTH), F32)],
        scratch_shapes=[pltpu.VMEM((HALO, WIDTH), F32)],
        input_output_aliases={6: 0},
        compiler_params=_cparams(("arbitrary",)), name=name)(dyb, proj, pooled, mixed, pw, ps, dproj)


def _hgrn_tile(s):
    return min(1024, s)


def _chunk_consts():
    tt = lax.broadcasted_iota(jnp.int32, (CHUNK, CHUNK), 0)
    ss = lax.broadcasted_iota(jnp.int32, (CHUNK, CHUNK), 1)
    within = (ss <= tt) & (ss // SUB == tt // SUB)
    before = ss < (tt // SUB) * SUB
    cums = jnp.concatenate([within.astype(F32), before.astype(F32)], axis=0).astype(BF16)
    causal = ss <= tt
    upper = (ss >= tt).astype(F32).astype(BF16)
    row = lax.broadcasted_iota(jnp.int32, (CHUNK, 1), 0)
    return cums, causal, upper, row


def _dot_split(mat01, v):
    hi = v.astype(BF16)
    r1 = v - hi.astype(F32)
    mid = r1.astype(BF16)
    lo = (r1 - mid.astype(F32)).astype(BF16)
    return _dot(mat01, hi) + _dot(mat01, mid) + _dot(mat01, lo)


def _hgrn_chunks(qas, fas, lb, cums, row):
    gates = [_hgrn_gates(qa, fa, lb) for qa, fa in zip(qas, fas)]
    cbs = [_dot_split(cums, g["lf"]) for g in gates]
    return [_hgrn_decay(g, cb, row) for g, cb in zip(gates, cbs)]


def _hgrn_gates(qa, fa, lb):
    sq = _sigmoid(qa)
    sa = _sigmoid(fa)
    sna = 1.0 - sa
    oml = 1.0 - lb
    f = lb + oml * sa
    fc = jnp.maximum(f, MIN_FORGET)
    return dict(sq=sq, q=qa * sq, sa=sa, sna=sna, oml=oml, f=f, fc=fc, lf=jnp.log(fc), k=oml * sna)


def _hgrn_decay(g, cb, row):
    sq, q, sa, sna, oml, f, fc, k = (g[n] for n in ("sq", "q", "sa", "sna", "oml", "f", "fc", "k"))
    c = cb[:CHUNK]
    bt = cb[CHUNK:]
    ec = jnp.exp(c)
    enc = jnp.exp(jnp.minimum(-c, MAX_EXP))
    qt = q * ec
    kt = k * enc
    dms, lhs, rhs = [], [], []
    for j in range(N_SUB):
        bj = bt[j * SUB:j * SUB + 1, :]
        dm = jnp.where(row >= j * SUB, jnp.exp(jnp.minimum(bt - bj, 0.0)), 0.0)
        dms.append(dm)
        lhs.append(qt * dm)
        rhs.append(jnp.where(row // SUB == j, kt, 0.0))
    lhs = jnp.concatenate(lhs, axis=1).astype(BF16)
    rhs = jnp.concatenate(rhs, axis=1).astype(BF16)
    b = bt + c
    bl = b[CHUNK - 1:CHUNK, :]
    ebl = jnp.exp(bl)
    edec = jnp.exp(bl - b)
    eb = ec * dms[0]
    return dict(sq=sq, q=q, sa=sa, sna=sna, oml=oml, f=f, fc=fc, k=k, ec=ec, enc=enc, dms=dms,
                lhs=lhs, rhs=rhs, ebl=ebl, edec=edec, eb=eb, qd=q * eb, kdec=k * edec)


def _rider_mid_step(total):
    return total - max(1, total // 8)


def hgrn_fwd(proj, lb, hn, *, rider=None, name):
    s = proj.shape[0]
    t = _hgrn_tile(s)
    nblk = s // t
    ncht = t // CHUNK
    head_blk0 = GP_COLS // HEAD_COLS
    nr = rider.n if rider else 0
    hbm = pl.BlockSpec(memory_space=pl.ANY)

    def body(*refs):
        p_ref, lb_ref, hn_ref = refs[:3]
        ya_ref, o_ref, st_ref = refs[3 + nr:6 + nr]
        state = refs[6 + 2 * nr]
        i = pl.program_id(1)
        if rider:
            total = HEADS * nblk
            rider.emit(pl.program_id(0) * nblk + i, total, _rider_mid_step(total),
                       refs[3:3 + nr], refs[6 + nr:6 + 2 * nr], refs[7 + 2 * nr:])

        @pl.when(i == 0)
        def _():
            state[...] = jnp.zeros_like(state)

        cums, causal, _, row = _chunk_consts()
        lbv = lb_ref[...]
        hnv = hn_ref[...]

        rows = [slice(ci * CHUNK, (ci + 1) * CHUNK) for ci in range(ncht)]
        pres = _hgrn_chunks([p_ref[r, 0:HEAD_DIM] for r in rows], [p_ref[r, HEAD_DIM:2 * HEAD_DIM] for r in rows],
                            lbv, cums, row)
        vas = [p_ref[r, 2 * HEAD_DIM:3 * HEAD_DIM].astype(BF16) for r in rows]
        scores = [jnp.where(causal, _dot_nt(pre["lhs"], pre["rhs"]), 0.0).astype(BF16) for pre in pres]
        intra = [_dot(a, va) for a, va in zip(scores, vas)]
        qds = [pre["qd"].astype(BF16) for pre in pres]
        kdecs = [pre["kdec"].astype(BF16) for pre in pres]
        st = state[...]
        outs = []
        for ci in range(ncht):
            stb = st.astype(BF16)
            st_ref[ci, 0] = stb
            outs.append(intra[ci] + _dot_nt(qds[ci], stb))
            st = st * pres[ci]["ebl"] + _dot_tn(vas[ci], kdecs[ci])
        state[...] = st
        for r, o in zip(rows, outs):
            za = p_ref[r, 3 * HEAD_DIM:4 * HEAD_DIM]
            o_ref[r, :] = o
            ya_ref[r, :] = ((o * lax.rsqrt(_rowmean(o * o) + EPS)) * hnv * (za * _sigmoid(za))).astype(ya_ref.dtype)

    return pl.pallas_call(
        body, grid=(HEADS, nblk),
        in_specs=[pl.BlockSpec((t, HEAD_COLS), lambda h, i: (i, h)),
                  pl.BlockSpec((1, HEAD_DIM), lambda h, i: (0, h)),
                  pl.BlockSpec((1, HEAD_DIM), lambda h, i: (0, h))] + [hbm] * nr,
        out_specs=[pl.BlockSpec((t, HEAD_DIM), lambda h, i: (i, h)),
                   pl.BlockSpec((t, HEAD_DIM), lambda h, i: (i, h)),
                   pl.BlockSpec((ncht, 1, HEAD_DIM, HEAD_DIM), lambda h, i: (i, h, 0, 0))] + [hbm] * nr,
        out_shape=[jax.ShapeDtypeStruct((s, WIDTH), BF16), jax.ShapeDtypeStruct((s, WIDTH), F32),
                   jax.ShapeDtypeStruct((s // CHUNK, HEADS, HEAD_DIM, HEAD_DIM), BF16)]
        + (rider.out_shape if rider else []),
        scratch_shapes=[pltpu.VMEM((HEAD_DIM, HEAD_DIM), F32)] + (rider.scratch if rider else []),
        compiler_params=_cparams(("arbitrary", "arbitrary")), name=name)(proj, lb, hn, *(rider.arrs if rider else []))


def hgrn_bwd(dya, proj, o_all, states, lb, hn, dproj, *, rider=None, name):
    s = proj.shape[0]
    t = _hgrn_tile(s)
    nblk = s // t
    ncht = t // CHUNK
    head_blk0 = GP_COLS // HEAD_COLS
    nr = rider.n if rider else 0
    hbm = pl.BlockSpec(memory_space=pl.ANY)

    def body(*refs):
        dy_ref, p_ref, o_ref, st_ref, lb_ref, hn_ref = refs[:6]
        dp_ref, dhn_ref, dlb_ref = refs[7 + nr:10 + nr]
        dstate = refs[10 + 2 * nr]
        i = pl.program_id(1)
        if rider:
            total = HEADS * nblk
            rider.emit(pl.program_id(0) * nblk + i, total, _rider_mid_step(total),
                       refs[7:7 + nr], refs[10 + nr:10 + 2 * nr], refs[11 + 2 * nr:])

        @pl.when(i == 0)
        def _():
            dstate[...] = jnp.zeros_like(dstate)
            dhn_ref[...] = jnp.zeros_like(dhn_ref)
            dlb_ref[...] = jnp.zeros_like(dlb_ref)

        cums, causal, upper, row = _chunk_consts()
        lbv = lb_ref[...]
        hnv = hn_ref[...]

        chunks = range(ncht)
        rows = [slice(ci * CHUNK, (ci + 1) * CHUNK) for ci in chunks]
        qas = [p_ref[r, 0:HEAD_DIM] for r in rows]
        vbs = [p_ref[r, 2 * HEAD_DIM:3 * HEAD_DIM].astype(BF16) for r in rows]
        st0s = [st_ref[ci, 0] for ci in chunks]
        dzas, dobs = [], []
        dhn_acc = jnp.zeros_like(hnv)
        for r in rows:
            za = p_ref[r, 3 * HEAD_DIM:4 * HEAD_DIM]
            o = o_ref[r, :]
            dy = dy_ref[r, :]
            rn = lax.rsqrt(_rowmean(o * o) + EPS)
            on = o * rn
            sgz = _sigmoid(za)
            sz = za * sgz
            dzas.append(dy * on * hnv * (sgz * (1.0 + za * (1.0 - sgz))))
            dhn_acc = dhn_acc + _colsum(dy * on * sz)
            don = dy * hnv * sz
            dobs.append((rn * (don - on * _rowmean(don * on))).astype(BF16))
        pres = _hgrn_chunks(qas, [p_ref[r, HEAD_DIM:2 * HEAD_DIM] for r in rows], lbv, cums, row)
        scores = [jnp.where(causal, _dot_nt(pre["lhs"], pre["rhs"]), 0.0).astype(BF16) for pre in pres]
        das = [jnp.where(causal, _dot_nt(dob, vb), 0.0).astype(BF16) for dob, vb in zip(dobs, vbs)]
        dlhss = [_dot(da, pre["rhs"]) for da, pre in zip(das, pres)]
        drhss = [_dot_tn(da, pre["lhs"]) for da, pre in zip(das, pres)]
        dv_intra = [_dot_tn(a, dob) for a, dob in zip(scores, dobs)]
        dq_inter = [_dot(dob, st0) * pre["eb"] for dob, st0, pre in zip(dobs, st0s, pres)]
        qds = [pre["qd"].astype(BF16) for pre in pres]
        kdecs = [pre["kdec"].astype(BF16) for pre in pres]
        dst1 = dstate[...]
        dvs, dk_states, dbl_states = [None] * ncht, [None] * ncht, [None] * ncht
        for ci in reversed(chunks):
            dst1b = dst1.astype(BF16)
            dvs[ci] = dv_intra[ci] + _dot_nt(kdecs[ci], dst1b)
            dk_states[ci] = _dot(vbs[ci], dst1b) * pres[ci]["edec"]
            dbl_states[ci] = pres[ci]["ebl"] * _colsum(dst1 * st0s[ci].astype(F32))
            dst1 = dst1 * pres[ci]["ebl"] + _dot_tn(dobs[ci], qds[ci])
        dstate[...] = dst1
        dqs, dks, dbs, dbls = [], [], [], []
        for ci in chunks:
            pre = pres[ci]
            q, k = pre["q"], pre["k"]
            dq_a = jnp.zeros_like(q)
            dk_a = jnp.zeros_like(k)
            db = q * dq_inter[ci] - k * dk_states[ci]
            for j in range(N_SUB):
                cols = slice(j * HEAD_DIM, (j + 1) * HEAD_DIM)
                dlhs, drhs = dlhss[ci][:, cols], drhss[ci][:, cols]
                dq_a = dq_a + pre["dms"][j] * dlhs
                dk_a = dk_a + jnp.where(row // SUB == j, drhs, 0.0)
                db = db + (pre["lhs"][:, cols].astype(F32) * dlhs - pre["rhs"][:, cols].astype(F32) * drhs)
            dqs.append(dq_inter[ci] + pre["ec"] * dq_a)
            dks.append(dk_states[ci] + pre["enc"] * dk_a)
            dbs.append(db)
            dbls.append(_colsum(k * dk_states[ci]) + dbl_states[ci])
        dlfs = [_dot_split(upper, db) + dbl for db, dbl in zip(dbs, dbls)]
        dlb_acc = jnp.zeros_like(lbv)
        for ci in chunks:
            pre = pres[ci]
            sq, sa, sna, oml = pre["sq"], pre["sa"], pre["sna"], pre["oml"]
            dqa = dqs[ci] * (sq * (1.0 + qas[ci] * (1.0 - sq)))
            diff = jnp.where(pre["f"] >= MIN_FORGET, dlfs[ci] / pre["fc"], 0.0) - dks[ci]
            dlb_acc = dlb_acc + _colsum(diff * sna)
            dfa = diff * (oml * sa * sna)
            dp_ref[rows[ci], :] = jnp.concatenate([dqa, dfa, dvs[ci], dzas[ci]], axis=1).astype(dp_ref.dtype)
        dhn_ref[...] += dhn_acc
        dlb_ref[...] += dlb_acc

    rev = lambda h, i: (nblk - 1 - i, h)
    return pl.pallas_call(
        body, grid=(HEADS, nblk),
        in_specs=[pl.BlockSpec((t, HEAD_DIM), rev),
                  pl.BlockSpec((t, HEAD_COLS), lambda h, i: (nblk - 1 - i, h)),
                  pl.BlockSpec((t, HEAD_DIM), rev),
                  pl.BlockSpec((ncht, 1, HEAD_DIM, HEAD_DIM), lambda h, i: (nblk - 1 - i, h, 0, 0)),
                  pl.BlockSpec((1, HEAD_DIM), lambda h, i: (0, h)),
                  pl.BlockSpec((1, HEAD_DIM), lambda h, i: (0, h)),
                  hbm] + [hbm] * nr,
        out_specs=[pl.BlockSpec((t, HEAD_COLS), lambda h, i: (nblk - 1 - i, head_blk0 + h)),
                   pl.BlockSpec((1, HEAD_DIM), lambda h, i: (0, h)),
                   pl.BlockSpec((1, HEAD_DIM), lambda h, i: (0, h))] + [hbm] * nr,
        out_shape=[jax.ShapeDtypeStruct(dproj.shape, dproj.dtype),
                   jax.ShapeDtypeStruct((1, WIDTH), F32), jax.ShapeDtypeStruct((1, WIDTH), F32)]
        + (rider.out_shape if rider else []),
        scratch_shapes=[pltpu.VMEM((HEAD_DIM, HEAD_DIM), F32)] + (rider.scratch if rider else []),
        input_output_aliases={6: 0},
        compiler_params=_cparams(("arbitrary", "arbitrary")),
        name=name)(dya, proj, o_all, states, lb, hn, dproj, *(rider.arrs if rider else []))


def _softmax_rows(lower):
    mx = jnp.max(lower, axis=0, keepdims=True)
    e = jnp.exp(lower - mx)
    return e / jnp.sum(e, axis=0, keepdims=True)


def lb_table(lower, *, name):
    depth, w = lower.shape

    def body(l_ref, o_ref):
        sm = _softmax_rows(l_ref[...])
        acc = jnp.zeros((1, w), F32)
        o_ref[0:1, :] = acc
        for l in range(1, depth):
            acc = acc + sm[l:l + 1, :]
            o_ref[l:l + 1, :] = acc

    return pl.pallas_call(body, out_shape=jax.ShapeDtypeStruct((depth, w), F32), name=name)(lower)


def lb_table_bwd(lower, dlb, *, name):
    depth, w = lower.shape

    def body(l_ref, d_ref, o_ref):
        sm = _softmax_rows(l_ref[...])
        dlbv = d_ref[...]
        dsm = [jnp.zeros((1, w), F32)]
        for i in range(1, depth):
            acc = jnp.zeros((1, w), F32)
            for l in range(i, depth):
                acc = acc + dlbv[l:l + 1, :]
            dsm.append(acc)
        inner = jnp.zeros((1, w), F32)
        for i in range(depth):
            inner = inner + sm[i:i + 1, :] * dsm[i]
        for i in range(depth):
            o_ref[i:i + 1, :] = sm[i:i + 1, :] * (dsm[i] - inner)

    return pl.pallas_call(body, out_shape=jax.ShapeDtypeStruct((depth, w), F32), name=name)(lower, dlb)


def w_ada_grad(c_all, dmod_cols, *, name):
    depth, _, cols = dmod_cols.shape
    d = c_all.shape[1]

    def body(c_ref, dm_ref, o_ref):
        cv = c_ref[...]
        ca = cv * _sigmoid(cv)
        o_ref[...] = _dot_tn(ca, dm_ref[...])

    return pl.pallas_call(
        body, grid=(depth,),
        in_specs=[pl.BlockSpec((N_DEV, d), lambda l: (0, 0)), pl.BlockSpec((None, N_DEV, cols), lambda l: (l, 0, 0))],
        out_specs=pl.BlockSpec((None, d, cols), lambda l: (l, 0, 0)),
        out_shape=jax.ShapeDtypeStruct((depth, d, cols), F32),
        compiler_params=_cparams(("arbitrary",)), name=name)(c_all, dmod_cols)


def sum_parts(parts, *, name):
    p, r, c = parts.shape

    def body(p_ref, o_ref):
        acc = p_ref[0]
        for j in range(1, p):
            acc = acc + p_ref[j]
        o_ref[...] = acc

    return pl.pallas_call(body, out_shape=jax.ShapeDtypeStruct((r, c), F32), name=name)(parts)


def _adam_rows(r, c):
    tr = r
    while tr * c * 4 > (1 << 20) and tr % 16 == 0:
        tr //= 2
    return tr


def _adam_update(w_ref, m_ref, v_ref, g_ref, go_ref, d_ref, mo_ref, vo_ref):
    g = g_ref[0].astype(F32)
    for j in range(1, g_ref.shape[0]):
        g = g + g_ref[j].astype(F32)
    mn = ADAM_B1 * m_ref[...] + (1.0 - ADAM_B1) * g
    vn = ADAM_B2 * v_ref[...] + (1.0 - ADAM_B2) * (g * g)
    m_hat = mn / (1.0 - ADAM_B1 ** ADAM_STEP)
    v_hat = vn / (1.0 - ADAM_B2 ** ADAM_STEP)
    go_ref[...] = g
    d_ref[...] = -ADAM_LR * (m_hat / (jnp.sqrt(v_hat) + ADAM_EPS) + ADAM_WD * w_ref[...])
    mo_ref[...] = mn
    vo_ref[...] = vn


def adamw(w, m, v, gparts, *, name):
    r, c = w.shape
    p = gparts.shape[0]
    tr = _adam_rows(r, c)
    spec = pl.BlockSpec((tr, c), lambda i: (i, 0))
    shp = jax.ShapeDtypeStruct((r, c), F32)
    return pl.pallas_call(
        functools.partial(_adam_update), grid=(r // tr,),
        in_specs=[spec, spec, spec, pl.BlockSpec((p, tr, c), lambda i: (0, i, 0))],
        out_specs=[spec] * 4, out_shape=[shp] * 4,
        compiler_params=_cparams(("arbitrary",)), name=name)(w, m, v, gparts)


def adamw_layers(w, m, v, gparts, *, name):
    depth, r, c = w.shape
    p = gparts[0].shape[0]
    tr = _adam_rows(r, c)

    def body(w_ref, m_ref, v_ref, *rest):
        g_refs, outs = rest[:depth], rest[depth:]
        layer = pl.program_id(0)
        for k in range(depth):
            @pl.when(layer == k)
            def _(k=k):
                _adam_update(w_ref, m_ref, v_ref, g_refs[k], *outs)

    spec = pl.BlockSpec((None, tr, c), lambda l, i: (l, i, 0))
    g_specs = [pl.BlockSpec((p, tr, c), functools.partial(lambda l, i, k: (0, jnp.where(l == k, i, 0), 0), k=k))
               for k in range(depth)]
    shp = jax.ShapeDtypeStruct((depth, r, c), F32)
    return pl.pallas_call(
        body, grid=(depth, r // tr),
        in_specs=[spec, spec, spec] + g_specs,
        out_specs=[spec] * 4, out_shape=[shp] * 4,
        compiler_params=_cparams(("arbitrary", "arbitrary")), name=name)(w, m, v, *gparts)


def _position():
    x, y, c = lax.axis_index("x"), lax.axis_index("y"), lax.axis_index("c")
    return x, y, c


def _dev_index(x, y, c):
    return 4 * x + 2 * y + c


def _gather_phases(ins, outs, send_sems, recv_sems, local_sems):
    n = len(ins)
    x, y, c = _position()
    me, sibling = (x, y, c), (x, y, 1 - c)
    chips = [(1 - x, y), (x, 1 - y), (1 - x, 1 - y)]

    def copy(a, k, block, to, own=False):
        slot = outs[a].at[_dev_index(*block)]
        return pltpu.make_async_remote_copy(
            src_ref=ins[a] if own else slot, dst_ref=slot,
            send_sem=send_sems.at[a * 7 + k], recv_sem=recv_sems.at[a * 7 + k],
            device_id=to, device_id_type=MESH)

    def mine(a):
        return pltpu.make_async_copy(ins[a], outs[a].at[_dev_index(*me)], local_sems.at[a])

    def first(a):
        return [copy(a, 0, me, sibling, True)] + [copy(a, 1 + j, me, (*chip, c), True) for j, chip in enumerate(chips)]

    def passed(a):
        return [copy(a, 4 + j, (*chip, c), sibling) for j, chip in enumerate(chips)]

    def start():
        for a in range(n):
            mine(a).start()
        for a in range(n):
            for cp in first(a):
                cp.start()

    def mid():
        for j, chip in enumerate(chips):
            for a in range(n):
                copy(a, 1 + j, (*chip, c), me).wait_recv()
                passed(a)[j].start()

    def finish():
        for a in range(n):
            copy(a, 0, sibling, me).wait_recv()
            for j, chip in enumerate(chips):
                copy(a, 4 + j, (*chip, 1 - c), me).wait_recv()
        for a in range(n):
            for cp in first(a) + passed(a):
                cp.wait_send()
            mine(a).wait()

    return start, mid, finish


def _scatter_phases(ins, outs, send_sems, recv_sems, local_sems):
    n = len(ins)
    x, y, c = _position()
    me = _dev_index(x, y, c)

    def peer(r):
        return (x ^ (r >> 2), y ^ ((r >> 1) & 1), c ^ (r & 1))

    def copy(a, r):
        to = peer(r)
        return pltpu.make_async_remote_copy(
            src_ref=ins[a].at[_dev_index(*to)], dst_ref=outs[a].at[me],
            send_sem=send_sems.at[a * 7 + r - 1], recv_sem=recv_sems.at[a * 7 + r - 1],
            device_id=to, device_id_type=MESH)

    def arrival(a, r):
        return pltpu.make_async_remote_copy(
            src_ref=ins[a].at[me], dst_ref=outs[a].at[_dev_index(*peer(r))],
            send_sem=send_sems.at[a * 7 + r - 1], recv_sem=recv_sems.at[a * 7 + r - 1],
            device_id=peer(r), device_id_type=MESH)

    def mine(a):
        return pltpu.make_async_copy(ins[a].at[me], outs[a].at[me], local_sems.at[a])

    def start():
        for a in range(n):
            mine(a).start()
        for r in range(1, N_DEV):
            for a in range(n):
                copy(a, r).start()

    def finish():
        for r in range(1, N_DEV):
            for a in range(n):
                arrival(a, r).wait_recv()
        for r in range(1, N_DEV):
            for a in range(n):
                copy(a, r).wait_send()
        for a in range(n):
            mine(a).wait()

    return start, None, finish


class Rider:
    def __init__(self, kind, arrs):
        self.kind, self.arrs, self.n = kind, list(arrs), len(arrs)
        lead = (N_DEV,) if kind == "gather" else ()
        self.out_shape = [jax.ShapeDtypeStruct(lead + a.shape, a.dtype) for a in self.arrs]
        self.scratch = [pltpu.SemaphoreType.DMA((7 * self.n,)), pltpu.SemaphoreType.DMA((7 * self.n,)),
                        pltpu.SemaphoreType.DMA((self.n,))]

    def phases(self, ins, outs, sems):
        make = _gather_phases if self.kind == "gather" else _scatter_phases
        return make(ins, outs, *sems)

    def emit(self, step, total, mid_step, ins, outs, sems):
        start, mid, finish = self.phases(ins, outs, sems)
        pl.when(step == 0)(start)
        if mid is not None:
            pl.when(step == mid_step)(mid)
        pl.when(step == total - 1)(finish)


def _standalone(rider, name):
    n = rider.n
    hbm = pl.BlockSpec(memory_space=pl.ANY)

    def body(*refs):
        start, mid, finish = rider.phases(refs[:n], refs[n:2 * n], refs[2 * n:])
        start()
        if mid is not None:
            mid()
        finish()

    return pl.pallas_call(body, out_shape=rider.out_shape, in_specs=[hbm] * n, out_specs=[hbm] * n,
                          scratch_shapes=rider.scratch, name=name)(*rider.arrs)


def all_gather(arrs, *, name):
    return _standalone(Rider("gather", arrs), name)


def scatter_parts(arrs, *, name):
    return _standalone(Rider("scatter", arrs), name)


def mod_exchange(c_all, w_ada, b_cols, *, name):
    depth, d, cols = w_ada.shape
    hbm = pl.BlockSpec(memory_space=pl.ANY)
    vmem = pl.BlockSpec(memory_space=pltpu.VMEM)

    def body(c_ref, w_ref, b_ref, out_ref, wbuf, sendbuf, send_sems, recv_sems, load_sem):
        x, y, c = _position()
        me = _dev_index(x, y, c)
        cv = c_ref[...]
        ca = cv * _sigmoid(cv)
        for l in range(depth):
            load = pltpu.make_async_copy(w_ref.at[l], wbuf, load_sem)
            load.start()
            load.wait()
            part = jnp.dot(ca, wbuf[...], preferred_element_type=F32,
                           precision=lax.Precision.HIGHEST) + b_ref[l:l + 1, :]
            for bi in range(N_DEV):
                sendbuf[bi, l:l + 1, :] = part[bi:bi + 1, :]

        def peer(r):
            return (x ^ (r >> 2), y ^ ((r >> 1) & 1), c ^ (r & 1))

        def copy(r):
            to = peer(r)
            return pltpu.make_async_remote_copy(
                src_ref=sendbuf.at[_dev_index(*to)], dst_ref=out_ref.at[me],
                send_sem=send_sems.at[r - 1], recv_sem=recv_sems.at[r - 1],
                device_id=to, device_id_type=MESH)

        def arrival(r):
            return pltpu.make_async_remote_copy(
                src_ref=sendbuf.at[me], dst_ref=out_ref.at[_dev_index(*peer(r))],
                send_sem=send_sems.at[r - 1], recv_sem=recv_sems.at[r - 1],
                device_id=peer(r), device_id_type=MESH)

        out_ref[me] = sendbuf[me]
        sends = [copy(r) for r in range(1, N_DEV)]
        for cp in sends:
            cp.start()
        for r in range(1, N_DEV):
            arrival(r).wait_recv()
        for cp in sends:
            cp.wait_send()

    return pl.pallas_call(
        body,
        out_shape=jax.ShapeDtypeStruct((N_DEV, depth, cols), F32),
        in_specs=[vmem, hbm, vmem], out_specs=vmem,
        scratch_shapes=[pltpu.VMEM((d, cols), F32), pltpu.VMEM((N_DEV, depth, cols), F32),
                        pltpu.SemaphoreType.DMA((7,)), pltpu.SemaphoreType.DMA((7,)), pltpu.SemaphoreType.DMA],
        compiler_params=pltpu.CompilerParams(vmem_limit_bytes=VMEM_LIMIT),
        name=name)(c_all, w_ada, b_cols)


def kernel(x, c, w_ada, b_ada, norm_pre, norm_post, w_in, lower_bounds, hgrn_norm, pool_w, pool_scale, w_proj_a, w_proj_b, w_out, loss_target, m_w_ada, m_b_ada, m_norm_pre, m_norm_post, m_w_in, m_lower_bounds, m_hgrn_norm, m_pool_w, m_pool_scale, m_w_proj_a, m_w_proj_b, m_w_out, v_w_ada, v_b_ada, v_norm_pre, v_norm_post, v_w_in, v_lower_bounds, v_hgrn_norm, v_pool_w, v_pool_scale, v_w_proj_a, v_w_proj_b, v_w_out):
    depth = w_in.shape[0]
    d = D_MODEL
    ada_cols = w_ada.shape[2]
    xi, yi, ci = _position()
    me = _dev_index(xi, yi, ci)
    xs = x[0]
    target = loss_target[0]
    ng = len(POOL_WINDOWS)

    def shards(l):
        return [w_in[l].astype(BF16), w_proj_a[l].astype(BF16), w_proj_b[l].astype(BF16),
                w_out[l].astype(BF16), pool_w[l].astype(BF16)]

    def other_weights(g_pa, g_pb, g_out, g_pool):
        return dict(
            pa=jnp.transpose(g_pa, (1, 0, 2)).reshape(WIDTH, d),
            pb=jnp.transpose(g_pb, (1, 0, 2)).reshape(WIDTH, d),
            w_out=g_out.reshape(d, d),
            pool=jnp.transpose(g_pool, (1, 0, 2, 3)).reshape(ng, POOL_GW, POOL_GW))

    (g_in0,) = all_gather(shards(0)[:1], name="gather_w_in")
    w_in_full = [permute_w_in(g_in0, name="permute_w_in")]
    others_full = []
    gathered = []

    (c_all,) = all_gather([c], name="gather_c")
    c_all = c_all.reshape(N_DEV, d)
    b_cols = lax.dynamic_slice_in_dim(b_ada, me * ada_cols, ada_cols, axis=1)
    mod_parts = mod_exchange(c_all, w_ada, b_cols, name="mod_exchange")
    mod = jnp.transpose(mod_parts, (1, 0, 2)).reshape(depth, 3 * d)
    lb_all = lb_table(lower_bounds, name="lb_table")

    saved = []
    cur = xs
    for l in range(depth):
        shift, scale, gate = mod[l:l + 1, :d], mod[l:l + 1, d:2 * d], mod[l:l + 1, 2 * d:]
        h = prenorm_fwd(cur, norm_pre[l:l + 1], shift, scale, name="prenorm_fwd")
        if l + 1 < depth:
            proj_gp, nxt_in = matmul_nn(h, w_in_full[l], cols=(0, GP_COLS), tm=2048, tn=1024, out_dtype=BF16,
                                        rider=Rider("gather", shards(l + 1)[:1]), name="mm_w_in_gp_gather")
            w_in_full.append(permute_w_in(nxt_in, name="permute_w_in"))
        else:
            proj_gp = matmul_nn(h, w_in_full[l], cols=(0, GP_COLS), tm=2048, tn=1024, out_dtype=BF16,
                                name="mm_w_in_gp")
        if l == 0:
            proj_h, *got = matmul_nn(h, w_in_full[l], cols=(GP_COLS, IN_COLS - GP_COLS), tm=2048, tn=1024,
                                     out_dtype=F32, rider=Rider("gather", shards(0)[1:]),
                                     name="mm_w_in_heads_gather")
            others_full.append(other_weights(*got))
        else:
            proj_h = matmul_nn(h, w_in_full[l], cols=(GP_COLS, IN_COLS - GP_COLS), tm=2048, tn=1024,
                               out_dtype=F32, name="mm_w_in_heads")
        if l + 1 < depth:
            y_a, o_all, states, *got = hgrn_fwd(proj_h, lb_all[l:l + 1], hgrn_norm[l:l + 1],
                                                 rider=Rider("gather", shards(l + 1)[1:]), name="hgrn_fwd_gather")
            others_full.append(other_weights(*got))
        else:
            y_a, o_all, states = hgrn_fwd(proj_h, lb_all[l:l + 1], hgrn_norm[l:l + 1], name="hgrn_fwd")
        w = dict(w_in=w_in_full[l], **others_full[l])
        gathered.append(w)
        y_b, pooled, mixed = pool_fwd(proj_gp, w["pool"], pool_scale[l:l + 1], name="pool_fwd")
        pa, pb, merged = proj_gate_fwd(y_a, y_b, w["pa"], w["pb"], proj_gp, name="proj_gate_fwd")
        out = matmul_nn(merged, w["w_out"], tm=2048, tn=1024, out_dtype=F32, name="mm_w_out")
        nxt = postnorm_fwd(cur, out, norm_post[l:l + 1], gate, name="postnorm_fwd")
        saved.append(dict(x=cur, h=h, proj_gp=proj_gp, proj_h=proj_h, y_a=y_a, o=o_all, states=states, y_b=y_b,
                          pooled=pooled, mixed=mixed, pa=pa, pb=pb, merged=merged, out=out, scale=scale, gate=gate))
        cur = nxt

    g, loss_part = loss_head(cur, target, name="loss_head")
    loss = lax.psum(loss_part[0, 0], ("x", "y", "c"))

    small = [None] * depth
    big_in = [None] * depth
    big_others = [None] * depth
    pend_in, pend_others = None, None
    for l in reversed(range(depth)):
        w, sv = gathered[l], saved[l]
        dout, dgate, dnpost = postnorm_bwd(g, sv["out"], norm_post[l:l + 1], sv["gate"], name="postnorm_bwd")
        dw_out = matmul_tn(sv["merged"], dout, tm=2048, tn=1024, out_dtype=BF16, name="mm_w_out_dw")
        dpa, dpb, dproj = gate_bwd(dout, w["w_out"], sv["proj_gp"], sv["pa"], sv["pb"], name="gate_bwd")
        dya = matmul_nt(dpa, w["pa"], tm=1024, tn=2048, out_dtype=F32, name="mm_proj_a_dx")
        dyb = matmul_nt(dpb, w["pb"], tm=1024, tn=2048, out_dtype=F32, name="mm_proj_b_dx")
        dw_pa = matmul_tn(sv["y_a"], dpa, tm=2048, tn=2048, out_dtype=BF16, name="mm_proj_a_dw")
        dw_pb = matmul_tn(sv["y_b"], dpb, tm=2048, tn=2048, out_dtype=BF16, name="mm_proj_b_dw")
        dproj, dpool_w, dpool_scale = pool_bwd(dyb, sv["proj_gp"], sv["pooled"], sv["mixed"], w["pool"],
                                               pool_scale[l:l + 1], dproj, name="pool_bwd")
        by_owner = lambda t: jnp.transpose(t.reshape(WIDTH, N_DEV, d // N_DEV), (1, 0, 2))
        others = [by_owner(dw_pa), by_owner(dw_pb), dw_out.reshape(N_DEV, d // N_DEV, d),
                  jnp.transpose(dpool_w.astype(BF16).reshape(ng, N_DEV, POOL_GW // N_DEV, POOL_GW), (1, 0, 2, 3))]
        ride = (pend_others or []) + (others if l == 0 else [])
        if ride:
            dproj, dhn, dlb, *recv = hgrn_bwd(dya, sv["proj_h"], sv["o"], sv["states"], lb_all[l:l + 1],
                                              hgrn_norm[l:l + 1], dproj, rider=Rider("scatter", ride),
                                              name="hgrn_bwd_scatter%d" % len(ride))
            if pend_others:
                big_others[l + 1], recv = recv[:len(pend_others)], recv[len(pend_others):]
            if l == 0:
                big_others[0] = recv
        else:
            dproj, dhn, dlb = hgrn_bwd(dya, sv["proj_h"], sv["o"], sv["states"], lb_all[l:l + 1],
                                       hgrn_norm[l:l + 1], dproj, name="hgrn_bwd")
        above = Rider("scatter", [pend_in]) if pend_in is not None else None
        if l > 0:
            if above:
                dh, big_in[l + 1] = matmul_nt(dproj, w["w_in"], tm=512, tn=2048, out_dtype=F32, rider=above,
                                              name="mm_w_in_dx_scatter")
            else:
                dh = matmul_nt(dproj, w["w_in"], tm=512, tn=2048, out_dtype=F32, name="mm_w_in_dx")
            dw_in = matmul_tn(sv["h"], dproj, tm=2048, tn=1024, out_dtype=BF16, name="mm_w_in_dw")
            pend_in, pend_others = unpermute_w_in(dw_in, name="unpermute_w_in"), others
        else:
            if above:
                dw_in, big_in[1] = matmul_tn(sv["h"], dproj, tm=2048, tn=1024, out_dtype=BF16, rider=above,
                                             name="mm_w_in_dw_scatter")
            else:
                dw_in = matmul_tn(sv["h"], dproj, tm=2048, tn=1024, out_dtype=BF16, name="mm_w_in_dw")
            dh, big_in[0] = matmul_nt(dproj, w["w_in"], tm=512, tn=2048, out_dtype=F32,
                                      rider=Rider("scatter", [unpermute_w_in(dw_in, name="unpermute_w_in")]),
                                      name="mm_w_in_dx_scatter")
        g, dshift, dscale, dnpre = prenorm_bwd(dh, sv["x"], norm_pre[l:l + 1], sv["scale"], g, name="prenorm_bwd")
        small[l] = jnp.concatenate([dshift, dscale, dgate, dnpre, dnpost, dlb, dhn, dpool_scale], axis=1)
    grad_x = g[None]
    big = [[big_in[l]] + list(big_others[l]) for l in range(depth)]

    small_mine = jnp.concatenate(small, axis=0)
    (small_all,) = all_gather([small_mine], name="gather_small")
    small_sum = sum_parts(small_all, name="sum_small")
    dmod_all = small_all[:, :, :3 * d]
    dmod_cols = jnp.transpose(lax.dynamic_slice_in_dim(dmod_all, me * ada_cols, ada_cols, axis=2), (1, 0, 2))
    g_w_ada = w_ada_grad(c_all, dmod_cols, name="w_ada_grad")
    off = 3 * d
    g_b_ada = small_sum[:, :off]
    g_npre = small_sum[:, off:off + d]
    g_npost = small_sum[:, off + d:off + 2 * d]
    g_lb_tab = small_sum[:, off + 2 * d:off + 2 * d + WIDTH]
    g_hn = small_sum[:, off + 2 * d + WIDTH:off + 2 * d + 2 * WIDTH]
    g_ps = small_sum[:, off + 2 * d + 2 * WIDTH:]
    g_lower = lb_table_bwd(lower_bounds, g_lb_tab, name="lb_table_bwd")

    def update(wt, mt, vt, gparts, shape2, name):
        outs = adamw(wt.reshape(shape2), mt.reshape(shape2), vt.reshape(shape2), gparts, name=name)
        return [o.reshape(wt.shape) for o in outs]

    def update_layers(wt, mt, vt, kind, name):
        shape3 = (depth, -1, wt.shape[-1])
        w3 = wt.reshape(shape3)
        gps = [big[l][kind].reshape((N_DEV,) + w3.shape[1:]) for l in range(depth)]
        outs = adamw_layers(w3, mt.reshape(shape3), vt.reshape(shape3), gps, name=name)
        return [o.reshape(wt.shape) for o in outs]

    def update_small(wt, mt, vt, gt, name):
        shape2 = (-1, wt.shape[-1])
        return update(wt, mt, vt, gt.reshape(shape2)[None], shape2, name)

    res = {
        "w_ada": update_small(w_ada, m_w_ada, v_w_ada, g_w_ada, "adamw_w_ada"),
        "b_ada": update_small(b_ada, m_b_ada, v_b_ada, g_b_ada, "adamw_b_ada"),
        "norm_pre": update_small(norm_pre, m_norm_pre, v_norm_pre, g_npre, "adamw_norm_pre"),
        "norm_post": update_small(norm_post, m_norm_post, v_norm_post, g_npost, "adamw_norm_post"),
        "w_in": update_layers(w_in, m_w_in, v_w_in, 0, "adamw_w_in"),
        "lower_bounds": update_small(lower_bounds, m_lower_bounds, v_lower_bounds, g_lower, "adamw_lower_bounds"),
        "hgrn_norm": update_small(hgrn_norm, m_hgrn_norm, v_hgrn_norm, g_hn, "adamw_hgrn_norm"),
        "pool_w": update_layers(pool_w, m_pool_w, v_pool_w, 4, "adamw_pool_w"),
        "pool_scale": update_small(pool_scale, m_pool_scale, v_pool_scale, g_ps, "adamw_pool_scale"),
        "w_proj_a": update_layers(w_proj_a, m_w_proj_a, v_w_proj_a, 1, "adamw_w_proj_a"),
        "w_proj_b": update_layers(w_proj_b, m_w_proj_b, v_w_proj_b, 2, "adamw_w_proj_b"),
        "w_out": update_layers(w_out, m_w_out, v_w_out, 3, "adamw_w_out"),
    }
    order = ["w_ada", "b_ada", "norm_pre", "norm_post", "w_in", "lower_bounds", "hgrn_norm", "pool_w",
             "pool_scale", "w_proj_a", "w_proj_b", "w_out"]
    outs = [loss, grad_x]
    for k in range(4):
        outs += [res[nm][k] for nm in order]
    return tuple(outs)
```

```python
import functools

import jax
import jax.numpy as jnp
from jax import lax
from jax.experimental import pallas as pl
from jax.experimental.pallas import tpu as pltpu

F32 = jnp.float32
BF16 = jnp.bfloat16
MESH = pl.DeviceIdType.MESH

N_DEV = 8
EPS = 1e-6
MIN_FORGET = 1e-30
D_MODEL = 2048
HEADS = 8
HEAD_DIM = 128
CHUNK = 64
SUB = 16
N_SUB = CHUNK // SUB
WIDTH = 1024
POOL_WINDOWS = (2, 4, 8, 16)
POOL_GW = 256
HALO = 16
IN_COLS = 10240
LANE = 128
N_COLBLK = IN_COLS // LANE
GATE_COLS = 4096
HEAD_COLS = 4 * HEAD_DIM
POOL_COLS = 2 * POOL_GW
GP_COLS = GATE_COLS + len(POOL_WINDOWS) * POOL_COLS
MAX_EXP = 80.0

ADAM_LR = 0.001
ADAM_B1 = 0.9
ADAM_B2 = 0.999
ADAM_EPS = 1e-08
ADAM_WD = 0.01
ADAM_STEP = 10

VMEM_LIMIT = 62 * 1024 * 1024


def _cparams(sem=None):
    return pltpu.CompilerParams(dimension_semantics=sem, vmem_limit_bytes=VMEM_LIMIT)


def _sigmoid(v):
    return 1.0 / (1.0 + jnp.exp(-v))


def _dot(a, b):
    return jnp.dot(a, b, preferred_element_type=F32)


def _dot_nt(a, b):
    return lax.dot_general(a, b, (((1,), (1,)), ((), ())), preferred_element_type=F32)


def _dot_tn(a, b):
    return lax.dot_general(a, b, (((0,), (0,)), ((), ())), preferred_element_type=F32)


def _colsum(v):
    return jnp.sum(v, axis=0, keepdims=True)


def _rowmean(v):
    return jnp.mean(v, axis=-1, keepdims=True)


def _orig_block_static(n):
    if n < 32:
        return n + 48
    if n < 48:
        m = n - 32
        t = m % 4
        return 32 + 2 * (m // 4) + (t % 2) + 8 * (t // 2)
    m = n - 48
    return 8 * (m % 4) + m // 4


def _accumulate(step, steps, prod, o_ref, acc_ref):
    if steps == 1:
        o_ref[...] = prod.astype(o_ref.dtype)
        return

    @pl.when(step == 0)
    def _():
        acc_ref[...] = prod

    @pl.when(step > 0)
    def _():
        acc_ref[...] += prod

    @pl.when(step == steps - 1)
    def _():
        o_ref[...] = acc_ref[...].astype(o_ref.dtype)


def _matmul_call(dot, a, b, *, grid, in_specs, out_spec, out_shape, acc_shape, steps, rider, name):
    nr = rider.n if rider else 0
    hbm = pl.BlockSpec(memory_space=pl.ANY)
    has_acc = steps > 1

    def body(*refs):
        a_ref, b_ref = refs[:2]
        o_ref = refs[2 + nr]
        scratch = refs[3 + 2 * nr:]
        if rider:
            total = grid[0] * grid[1]
            rider.emit(pl.program_id(0) * grid[1] + pl.program_id(1), total, _rider_mid_step(total),
                       refs[2:2 + nr], refs[3 + nr:3 + 2 * nr], scratch[1 if has_acc else 0:])
        _accumulate(pl.program_id(1), steps, dot(a_ref[...], b_ref[...]), o_ref, scratch[0] if has_acc else None)

    outs = pl.pallas_call(
        body, grid=grid,
        in_specs=in_specs + [hbm] * nr, out_specs=[out_spec] + [hbm] * nr,
        out_shape=[out_shape] + (rider.out_shape if rider else []),
        scratch_shapes=([pltpu.VMEM(acc_shape, F32)] if has_acc else []) + (rider.scratch if rider else []),
        compiler_params=_cparams(("arbitrary", "arbitrary")),
        name=name)(a, b, *(rider.arrs if rider else []))
    return outs if rider else outs[0]


def matmul_nn(a, b, *, tm, tn, out_dtype, name, rider=None, cols=None):
    m, k = a.shape
    col0, n = cols if cols else (0, b.shape[1])
    tm = min(tm, m)
    j0 = col0 // tn
    return _matmul_call(
        _dot, a, b, grid=(n // tn, m // tm),
        in_specs=[pl.BlockSpec((tm, k), lambda j, i: (i, 0)), pl.BlockSpec((k, tn), lambda j, i: (0, j0 + j))],
        out_spec=pl.BlockSpec((tm, tn), lambda j, i: (i, j)),
        out_shape=jax.ShapeDtypeStruct((m, n), out_dtype), acc_shape=None, steps=1, rider=rider, name=name)


def matmul_nt(a, b, *, tm, tn, out_dtype, name, rider=None):
    m, n = a.shape
    k = b.shape[0]
    tm = min(tm, m)
    return _matmul_call(
        _dot_nt, a, b, grid=(m // tm, n // tn),
        in_specs=[pl.BlockSpec((tm, tn), lambda i, j: (i, j)), pl.BlockSpec((k, tn), lambda i, j: (0, j))],
        out_spec=pl.BlockSpec((tm, k), lambda i, j: (i, 0)),
        out_shape=jax.ShapeDtypeStruct((m, k), out_dtype), acc_shape=(tm, k), steps=n // tn, rider=rider, name=name)


def matmul_tn(a, b, *, tm, tn, out_dtype, name, rider=None):
    m, k = a.shape
    n = b.shape[1]
    tm = min(tm, m)
    return _matmul_call(
        _dot_tn, a, b, grid=(n // tn, m // tm),
        in_specs=[pl.BlockSpec((tm, k), lambda j, i: (i, 0)), pl.BlockSpec((tm, tn), lambda j, i: (i, j))],
        out_spec=pl.BlockSpec((k, tn), lambda j, i: (0, j)),
        out_shape=jax.ShapeDtypeStruct((k, n), out_dtype), acc_shape=(k, tn), steps=m // tm, rider=rider, name=name)


def permute_w_in(staged_lo, staged_hi, *, name):
    k = staged_lo.shape[1]
    own = IN_COLS // N_DEV
    half = own // 2
    tr = min(256, k)

    def body(lo_ref, hi_ref, o_ref):
        for nb in range(N_COLBLK):
            dev, col = divmod(_orig_block_static(nb) * LANE, own)
            src = lo_ref[dev, :, col:col + LANE] if col < half else hi_ref[dev, :, col - half:col - half + LANE]
            o_ref[:, nb * LANE:(nb + 1) * LANE] = src

    spec = pl.BlockSpec((N_DEV, tr, half), lambda i: (0, i, 0))
    return pl.pallas_call(
        body, grid=(k // tr,),
        in_specs=[spec, spec],
        out_specs=pl.BlockSpec((tr, IN_COLS), lambda i: (i, 0)),
        out_shape=jax.ShapeDtypeStruct((k, IN_COLS), staged_lo.dtype),
        compiler_params=_cparams(("arbitrary",)), name=name)(staged_lo, staged_hi)


def unpermute_w_in(dw, *, name):
    k = dw.shape[0]
    own = IN_COLS // N_DEV
    tr = min(256, k)

    def body(i_ref, o_ref):
        for nb in range(N_COLBLK):
            dev, col = divmod(_orig_block_static(nb) * LANE, own)
            o_ref[dev, :, col:col + LANE] = i_ref[:, nb * LANE:(nb + 1) * LANE]

    return pl.pallas_call(
        body, grid=(k // tr,),
        in_specs=[pl.BlockSpec((tr, IN_COLS), lambda i: (i, 0))],
        out_specs=pl.BlockSpec((N_DEV, tr, own), lambda i: (0, i, 0)),
        out_shape=jax.ShapeDtypeStruct((N_DEV, k, own), dw.dtype),
        compiler_params=_cparams(("arbitrary",)), name=name)(dw)


def _row_tile(s):
    return min(256, s)


def _norm_tile(s):
    return min(512, s)


def _row_spec(t, w, col=0):
    return pl.BlockSpec((t, w), lambda i: (i, col))


def _vec_spec(w):
    return pl.BlockSpec((1, w), lambda i: (0, 0))


def prenorm_fwd(x, gain, shift, scale, *, name):
    s, d = x.shape
    t = _norm_tile(s)

    def body(x_ref, g_ref, sh_ref, sc_ref, h_ref):
        xv = x_ref[...]
        r = lax.rsqrt(_rowmean(xv * xv) + EPS)
        h_ref[...] = ((xv * r) * g_ref[...] * (1.0 + sc_ref[...]) + sh_ref[...]).astype(h_ref.dtype)

    return pl.pallas_call(
        body, grid=(s // t,),
        in_specs=[_row_spec(t, d), _vec_spec(d), _vec_spec(d), _vec_spec(d)],
        out_specs=_row_spec(t, d), out_shape=jax.ShapeDtypeStruct((s, d), BF16),
        compiler_params=_cparams(("arbitrary",)), name=name)(x, gain, shift, scale)


def w_in_dx_prenorm_bwd(dproj, w_in, x, gain, scale, g_res, *, rider=None, name):
    s, n = dproj.shape
    d = w_in.shape[0]
    tm = min(512, s)
    tn = 2048
    steps = n // tn
    nrow = s // tm
    nr = rider.n if rider else 0
    hbm = pl.BlockSpec(memory_space=pl.ANY)

    def body(*refs):
        dp_ref, w_ref, x_ref, g_ref, sc_ref, gr_ref = refs[:6]
        dx_ref, dsh_ref, dsc_ref, dg_ref = refs[6 + nr:10 + nr]
        acc_ref = refs[10 + 2 * nr]
        i, j = pl.program_id(0), pl.program_id(1)
        if rider:
            total = nrow * steps
            rider.emit(i * steps + j, total, _rider_mid_step(total),
                       refs[6:6 + nr], refs[10 + nr:10 + 2 * nr], refs[11 + 2 * nr:])
        prod = _dot_nt(dp_ref[...], w_ref[...])

        @pl.when(j == 0)
        def _():
            acc_ref[...] = prod

        @pl.when(j > 0)
        def _():
            acc_ref[...] += prod

        @pl.when(j == steps - 1)
        def _():
            @pl.when(i == 0)
            def _():
                dsh_ref[...] = jnp.zeros_like(dsh_ref)
                dsc_ref[...] = jnp.zeros_like(dsc_ref)
                dg_ref[...] = jnp.zeros_like(dg_ref)

            gain_v = g_ref[...]
            one_sc = 1.0 + sc_ref[...]
            sub = min(128, tm)

            def piece(k, carry):
                rows = pl.ds(pl.multiple_of(k * sub, sub), sub)
                xv = x_ref[rows, :]
                dhv = acc_ref[rows, :]
                r = lax.rsqrt(_rowmean(xv * xv) + EPS)
                xn = xv * r
                dyn = dhv * one_sc
                dxn = dyn * gain_v
                dx_ref[rows, :] = r * (dxn - xn * _rowmean(dxn * xn)) + gr_ref[rows, :]
                dsh_ref[...] += _colsum(dhv)
                dsc_ref[...] += _colsum(dhv * (xn * gain_v))
                dg_ref[...] += _colsum(dyn * xn)
                return carry

            lax.fori_loop(0, tm // sub, piece, 0)

    rows = pl.BlockSpec((tm, d), lambda i, j: (i, 0))
    vec_in = pl.BlockSpec((1, d), lambda i, j: (0, 0))
    vec = jax.ShapeDtypeStruct((1, d), F32)
    outs = pl.pallas_call(
        body, grid=(nrow, steps),
        in_specs=[pl.BlockSpec((tm, tn), lambda i, j: (i, j)), pl.BlockSpec((d, tn), lambda i, j: (0, j)),
                  rows, vec_in, vec_in, rows] + [hbm] * nr,
        out_specs=[rows, vec_in, vec_in, vec_in] + [hbm] * nr,
        out_shape=[jax.ShapeDtypeStruct((s, d), F32), vec, vec, vec] + (rider.out_shape if rider else []),
        scratch_shapes=[pltpu.VMEM((tm, d), F32)] + (rider.scratch if rider else []),
        compiler_params=_cparams(("arbitrary", "arbitrary")),
        name=name)(dproj, w_in, x, gain, scale, g_res, *(rider.arrs if rider else []))
    return outs


def w_out_postnorm_fwd(merged, w_out, x, gain, gate, *, name):
    s, d = x.shape
    t = _norm_tile(s)

    def body(m_ref, w_ref, x_ref, g_ref, gt_ref, o_ref, y_ref):
        ov = _dot(m_ref[...], w_ref[...])
        o_ref[...] = ov
        r = lax.rsqrt(_rowmean(ov * ov) + EPS)
        y_ref[...] = x_ref[...] + gt_ref[...] * ((ov * r) * g_ref[...])

    shp = jax.ShapeDtypeStruct((s, d), F32)
    return pl.pallas_call(
        body, grid=(s // t,),
        in_specs=[_row_spec(t, d), _full_spec((d, d)), _row_spec(t, d), _vec_spec(d), _vec_spec(d)],
        out_specs=[_row_spec(t, d)] * 2, out_shape=[shp] * 2,
        compiler_params=_cparams(("arbitrary",)), name=name)(merged, w_out, x, gain, gate)


def postnorm_bwd(g, out, gain, gate, *, name):
    s, d = out.shape
    t = _norm_tile(s)

    def body(g_ref, o_ref, gn_ref, gt_ref, do_ref, dgt_ref, dgn_ref):
        i = pl.program_id(0)
        ov = o_ref[...]
        gv = g_ref[...]
        r = lax.rsqrt(_rowmean(ov * ov) + EPS)
        on = ov * r
        gain_v = gn_ref[...]
        gate_v = gt_ref[...]
        dn = gv * gate_v
        don = dn * gain_v
        do_ref[...] = (r * (don - on * _rowmean(don * on))).astype(do_ref.dtype)
        p_gt = _colsum(gv * (on * gain_v))
        p_gn = _colsum(dn * on)

        @pl.when(i == 0)
        def _():
            dgt_ref[...] = p_gt
            dgn_ref[...] = p_gn

        @pl.when(i > 0)
        def _():
            dgt_ref[...] += p_gt
            dgn_ref[...] += p_gn

    vec = jax.ShapeDtypeStruct((1, d), F32)
    return pl.pallas_call(
        body, grid=(s // t,),
        in_specs=[_row_spec(t, d), _row_spec(t, d), _vec_spec(d), _vec_spec(d)],
        out_specs=[_row_spec(t, d), _vec_spec(d), _vec_spec(d)],
        out_shape=[jax.ShapeDtypeStruct((s, d), BF16), vec, vec],
        compiler_params=_cparams(("arbitrary",)), name=name)(g, out, gain, gate)


def loss_head(y, target, *, name):
    s, d = y.shape
    t = _norm_tile(s)
    steps = s // t

    def body(y_ref, t_ref, dy_ref, loss_ref, acc_ref):
        i = pl.program_id(0)
        err = y_ref[...] - t_ref[...]
        dy_ref[...] = err * (1.0 / d)
        part = _colsum(err * err)

        @pl.when(i == 0)
        def _():
            acc_ref[...] = part

        @pl.when(i > 0)
        def _():
            acc_ref[...] += part

        @pl.when(i == steps - 1)
        def _():
            loss_ref[...] = jnp.sum(acc_ref[...], axis=1, keepdims=True) * (0.5 / d)

    return pl.pallas_call(
        body, grid=(steps,),
        in_specs=[_row_spec(t, d), _row_spec(t, d)],
        out_specs=[_row_spec(t, d), pl.BlockSpec((1, 1), lambda i: (0, 0))],
        out_shape=[jax.ShapeDtypeStruct((s, d), F32), jax.ShapeDtypeStruct((1, 1), F32)],
        scratch_shapes=[pltpu.VMEM((1, d), F32)],
        compiler_params=_cparams(("arbitrary",)), name=name)(y, target)


def _full_spec(shape):
    return pl.BlockSpec(shape, lambda i: (0,) * len(shape))


def proj_gate_fwd(y_a, y_b, w_pa, w_pb, proj, *, name):
    s, width = y_a.shape
    d = w_pa.shape[1]
    t = _norm_tile(s)

    def body(ya_ref, yb_ref, wa_ref, wb_ref, ga_ref, gb_ref, pa_ref, pb_ref, m_ref):
        pa = _dot(ya_ref[...], wa_ref[...])
        pb = _dot(yb_ref[...], wb_ref[...])
        pa_ref[...] = pa.astype(pa_ref.dtype)
        pb_ref[...] = pb.astype(pb_ref.dtype)
        m_ref[...] = (_sigmoid(ga_ref[...].astype(F32)) * pa
                      + _sigmoid(gb_ref[...].astype(F32)) * pb).astype(m_ref.dtype)

    out = jax.ShapeDtypeStruct((s, d), BF16)
    return pl.pallas_call(
        body, grid=(s // t,),
        in_specs=[_row_spec(t, width), _row_spec(t, width), _full_spec((width, d)), _full_spec((width, d)),
                  _row_spec(t, d, 0), _row_spec(t, d, 1)],
        out_specs=[_row_spec(t, d)] * 3, out_shape=[out] * 3,
        compiler_params=_cparams(("arbitrary",)), name=name)(y_a, y_b, w_pa, w_pb, proj, proj)


def gate_bwd(dout, w_out, proj, pa, pb, *, name):
    s, d = pa.shape
    t = _norm_tile(s)

    def body(do_ref, w_ref, ga_ref, gb_ref, pa_ref, pb_ref, dpa_ref, dpb_ref, dp_ref):
        dm = _dot_nt(do_ref[...], w_ref[...])
        sa = _sigmoid(ga_ref[...].astype(F32))
        sb = _sigmoid(gb_ref[...].astype(F32))
        dpa_ref[...] = (dm * sa).astype(dpa_ref.dtype)
        dpb_ref[...] = (dm * sb).astype(dpb_ref.dtype)
        dp_ref[:, :d] = (dm * pa_ref[...].astype(F32) * sa * (1.0 - sa)).astype(dp_ref.dtype)
        dp_ref[:, d:] = (dm * pb_ref[...].astype(F32) * sb * (1.0 - sb)).astype(dp_ref.dtype)

    return pl.pallas_call(
        body, grid=(s // t,),
        in_specs=[_row_spec(t, d), _full_spec((d, d)), _row_spec(t, d, 0), _row_spec(t, d, 1),
                  _row_spec(t, d), _row_spec(t, d)],
        out_specs=[_row_spec(t, d), _row_spec(t, d), _row_spec(t, 2 * d, 0)],
        out_shape=[jax.ShapeDtypeStruct((s, d), BF16), jax.ShapeDtypeStruct((s, d), BF16),
                   jax.ShapeDtypeStruct((s, IN_COLS), BF16)],
        compiler_params=_cparams(("arbitrary",)), name=name)(dout, w_out, proj, proj, pa, pb)


def _pool_tile(s):
    return min(256, s)


def pool_fwd(proj, pw, ps, *, name):
    s = proj.shape[0]
    t = _pool_tile(s)
    pool_blk = GATE_COLS // (len(POOL_WINDOWS) * POOL_COLS)

    def body(p_ref, halo_ref, pw_ref, ps_ref, yb_ref, pooled_ref, mixed_ref):
        i = pl.program_id(0)
        halo = jnp.where(i == 0, 0.0, halo_ref[...].astype(F32))
        row = i * t + lax.broadcasted_iota(jnp.int32, (t, 1), 0)
        for g, w in enumerate(POOL_WINDOWS):
            vb = p_ref[:, g * POOL_COLS:g * POOL_COLS + POOL_GW].astype(F32)
            zb = p_ref[:, g * POOL_COLS + POOL_GW:(g + 1) * POOL_COLS].astype(F32)
            acc = jnp.concatenate([halo[:, g * POOL_COLS:g * POOL_COLS + POOL_GW], vb], axis=0)
            sh = 1
            while sh < w:
                acc = acc + pltpu.roll(acc, sh, axis=0)
                sh *= 2
            cnt = jnp.minimum(row + 1, w).astype(F32)
            pooled = acc[HALO:, :] / cnt - vb
            mixed = _dot(pooled.astype(BF16), pw_ref[g])
            cols = slice(g * POOL_GW, (g + 1) * POOL_GW)
            yb = mixed * ps_ref[:, cols] * (zb * _sigmoid(zb))
            yb_ref[:, cols] = yb.astype(yb_ref.dtype)
            pooled_ref[:, cols] = pooled.astype(pooled_ref.dtype)
            mixed_ref[:, cols] = mixed

    wide = len(POOL_WINDOWS) * POOL_COLS
    return pl.pallas_call(
        body, grid=(s // t,),
        in_specs=[pl.BlockSpec((t, wide), lambda i: (i, pool_blk)),
                  pl.BlockSpec((HALO, wide), lambda i: (jnp.maximum(i * (t // HALO) - 1, 0), pool_blk)),
                  pl.BlockSpec((len(POOL_WINDOWS), POOL_GW, POOL_GW), lambda i: (0, 0, 0)),
                  _vec_spec(WIDTH)],
        out_specs=[_row_spec(t, WIDTH)] * 3,
        out_shape=[jax.ShapeDtypeStruct((s, WIDTH), BF16), jax.ShapeDtypeStruct((s, WIDTH), BF16),
                   jax.ShapeDtypeStruct((s, WIDTH), F32)],
        compiler_params=_cparams(("arbitrary",)), name=name)(proj, proj, pw, ps)


def pool_bwd(dyb, proj, pooled, mixed, pw, ps, dproj, *, name):
    s = proj.shape[0]
    t = _pool_tile(s)
    nblk = s // t
    ng = len(POOL_WINDOWS)
    wide = ng * POOL_COLS
    pool_blk = GATE_COLS // wide

    def body(dy_ref, p_ref, pooled_ref, mixed_ref, pw_ref, ps_ref, dp_any, dp_ref, dpw_ref, dps_ref, carry):
        del dp_any
        i = pl.program_id(0)
        ii = nblk - 1 - i

        @pl.when(i == 0)
        def _():
            carry[...] = jnp.zeros_like(carry)
            dpw_ref[...] = jnp.zeros_like(dpw_ref)
            dps_ref[...] = jnp.zeros_like(dps_ref)

        row = ii * t + lax.broadcasted_iota(jnp.int32, (t, 1), 0)
        for g, w in enumerate(POOL_WINDOWS):
            cols = slice(g * POOL_GW, (g + 1) * POOL_GW)
            zb = p_ref[:, g * POOL_COLS + POOL_GW:(g + 1) * POOL_COLS].astype(F32)
            dy = dy_ref[:, cols]
            mx = mixed_ref[:, cols]
            sc = ps_ref[:, cols]
            sg = _sigmoid(zb)
            dzb = dy * (mx * sc) * (sg * (1.0 + zb * (1.0 - sg)))
            dpm = dy * (zb * sg)
            dps_ref[:, cols] += _colsum(dpm * mx)
            dmixed = (dpm * sc).astype(BF16)
            dpooled = _dot_nt(dmixed, pw_ref[g])
            dpw_ref[g] += _dot_tn(pooled_ref[:, cols], dmixed)
            cnt = jnp.minimum(row + 1, w).astype(F32)
            u = dpooled / cnt
            acc = jnp.concatenate([u, carry[:, cols]], axis=0)
            sh = 1
            while sh < w:
                acc = acc + pltpu.roll(acc, t + HALO - sh, axis=0)
                sh *= 2
            carry[:, cols] = u[:HALO, :]
            dp_ref[:, g * POOL_COLS:g * POOL_COLS + POOL_GW] = (acc[:t, :] - dpooled).astype(dp_ref.dtype)
            dp_ref[:, g * POOL_COLS + POOL_GW:(g + 1) * POOL_COLS] = dzb.astype(dp_ref.dtype)

    rev = lambda i: (nblk - 1 - i, 0)
    return pl.pallas_call(
        body, grid=(nblk,),
        in_specs=[pl.BlockSpec((t, WIDTH), rev),
                  pl.BlockSpec((t, wide), lambda i: (nblk - 1 - i, pool_blk)),
                  pl.BlockSpec((t, WIDTH), rev), pl.BlockSpec((t, WIDTH), rev),
                  pl.BlockSpec((ng, POOL_GW, POOL_GW), lambda i: (0, 0, 0)),
                  _vec_spec(WIDTH),
                  pl.BlockSpec(memory_space=pl.ANY)],
        out_specs=[pl.BlockSpec((t, wide), lambda i: (nblk - 1 - i, pool_blk)),
                   pl.BlockSpec((ng, POOL_GW, POOL_GW), lambda i: (0, 0, 0)),
                   _vec_spec(WIDTH)],
        out_shape=[jax.ShapeDtypeStruct(dproj.shape, dproj.dtype),
                   jax.ShapeDtypeStruct((ng, POOL_GW, POOL_GW), F32),
                   jax.ShapeDtypeStruct((1, WIDTH), F32)],
        scratch_shapes=[pltpu.VMEM((HALO, WIDTH), F32)],
        input_output_aliases={6: 0},
        compiler_params=_cparams(("arbitrary",)), name=name)(dyb, proj, pooled, mixed, pw, ps, dproj)


def _hgrn_tile(s):
    return min(1024, s)


def _chunk_consts():
    tt = lax.broadcasted_iota(jnp.int32, (CHUNK, CHUNK), 0)
    ss = lax.broadcasted_iota(jnp.int32, (CHUNK, CHUNK), 1)
    within = (ss <= tt) & (ss // SUB == tt // SUB)
    before = ss < (tt // SUB) * SUB
    cums = jnp.concatenate([within.astype(F32), before.astype(F32)], axis=0).astype(BF16)
    causal = ss <= tt
    upper = (ss >= tt).astype(F32).astype(BF16)
    row = lax.broadcasted_iota(jnp.int32, (CHUNK, 1), 0)
    return cums, causal, upper, row


def _dot_split(mat01, v):
    hi = v.astype(BF16)
    r1 = v - hi.astype(F32)
    mid = r1.astype(BF16)
    lo = (r1 - mid.astype(F32)).astype(BF16)
    return _dot(mat01, hi) + _dot(mat01, mid) + _dot(mat01, lo)


def _hgrn_chunks(qas, fas, lb, cums, row):
    gates = [_hgrn_gates(qa, fa, lb) for qa, fa in zip(qas, fas)]
    cbs = [_dot_split(cums, g["lf"]) for g in gates]
    return [_hgrn_decay(g, cb, row) for g, cb in zip(gates, cbs)]


def _hgrn_gates(qa, fa, lb):
    sq = _sigmoid(qa)
    sa = _sigmoid(fa)
    sna = 1.0 - sa
    oml = 1.0 - lb
    f = lb + oml * sa
    fc = jnp.maximum(f, MIN_FORGET)
    return dict(sq=sq, q=qa * sq, sa=sa, sna=sna, oml=oml, f=f, fc=fc, lf=jnp.log(fc), k=oml * sna)


def _hgrn_decay(g, cb, row):
    sq, q, sa, sna, oml, f, fc, k = (g[n] for n in ("sq", "q", "sa", "sna", "oml", "f", "fc", "k"))
    c = cb[:CHUNK]
    bt = cb[CHUNK:]
    ec = jnp.exp(c)
    enc = jnp.exp(jnp.minimum(-c, MAX_EXP))
    qt = q * ec
    kt = k * enc
    dms, lhs, rhs = [], [], []
    for j in range(N_SUB):
        bj = bt[j * SUB:j * SUB + 1, :]
        dm = jnp.where(row >= j * SUB, jnp.exp(jnp.minimum(bt - bj, 0.0)), 0.0)
        dms.append(dm)
        lhs.append(qt * dm)
        rhs.append(jnp.where(row // SUB == j, kt, 0.0))
    lhs = jnp.concatenate(lhs, axis=1).astype(BF16)
    rhs = jnp.concatenate(rhs, axis=1).astype(BF16)
    b = bt + c
    bl = b[CHUNK - 1:CHUNK, :]
    ebl = jnp.exp(bl)
    edec = jnp.exp(bl - b)
    eb = ec * dms[0]
    return dict(sq=sq, q=q, sa=sa, sna=sna, oml=oml, f=f, fc=fc, k=k, ec=ec, enc=enc, dms=dms,
                lhs=lhs, rhs=rhs, ebl=ebl, edec=edec, eb=eb, qd=q * eb, kdec=k * edec)


def _rider_mid_step(total):
    return total - max(1, total // 8)


def hgrn_fwd(proj, lb, hn, *, rider=None, name):
    s = proj.shape[0]
    t = _hgrn_tile(s)
    nblk = s // t
    ncht = t // CHUNK
    head_blk0 = GP_COLS // HEAD_COLS
    nr = rider.n if rider else 0
    hbm = pl.BlockSpec(memory_space=pl.ANY)

    def body(*refs):
        p_ref, lb_ref, hn_ref = refs[:3]
        ya_ref, o_ref, st_ref = refs[3 + nr:6 + nr]
        state = refs[6 + 2 * nr]
        i = pl.program_id(1)
        if rider:
            total = HEADS * nblk
            rider.emit(pl.program_id(0) * nblk + i, total, _rider_mid_step(total),
                       refs[3:3 + nr], refs[6 + nr:6 + 2 * nr], refs[7 + 2 * nr:])

        @pl.when(i == 0)
        def _():
            state[...] = jnp.zeros_like(state)

        cums, causal, _, row = _chunk_consts()
        lbv = lb_ref[...]
        hnv = hn_ref[...]

        rows = [slice(ci * CHUNK, (ci + 1) * CHUNK) for ci in range(ncht)]
        pres = _hgrn_chunks([p_ref[r, 0:HEAD_DIM] for r in rows], [p_ref[r, HEAD_DIM:2 * HEAD_DIM] for r in rows],
                            lbv, cums, row)
        vas = [p_ref[r, 2 * HEAD_DIM:3 * HEAD_DIM].astype(BF16) for r in rows]
        scores = [jnp.where(causal, _dot_nt(pre["lhs"], pre["rhs"]), 0.0).astype(BF16) for pre in pres]
        intra = [_dot(a, va) for a, va in zip(scores, vas)]
        qds = [pre["qd"].astype(BF16) for pre in pres]
        kdecs = [pre["kdec"].astype(BF16) for pre in pres]
        st = state[...]
        outs = []
        for ci in range(ncht):
            stb = st.astype(BF16)
            st_ref[ci, 0] = stb
            outs.append(intra[ci] + _dot_nt(qds[ci], stb))
            st = st * pres[ci]["ebl"] + _dot_tn(vas[ci], kdecs[ci])
        state[...] = st
        for r, o in zip(rows, outs):
            za = p_ref[r, 3 * HEAD_DIM:4 * HEAD_DIM]
            o_ref[r, :] = o
            ya_ref[r, :] = ((o * lax.rsqrt(_rowmean(o * o) + EPS)) * hnv * (za * _sigmoid(za))).astype(ya_ref.dtype)

    return pl.pallas_call(
        body, grid=(HEADS, nblk),
        in_specs=[pl.BlockSpec((t, HEAD_COLS), lambda h, i: (i, h)),
                  pl.BlockSpec((1, HEAD_DIM), lambda h, i: (0, h)),
                  pl.BlockSpec((1, HEAD_DIM), lambda h, i: (0, h))] + [hbm] * nr,
        out_specs=[pl.BlockSpec((t, HEAD_DIM), lambda h, i: (i, h)),
                   pl.BlockSpec((t, HEAD_DIM), lambda h, i: (i, h)),
                   pl.BlockSpec((ncht, 1, HEAD_DIM, HEAD_DIM), lambda h, i: (i, h, 0, 0))] + [hbm] * nr,
        out_shape=[jax.ShapeDtypeStruct((s, WIDTH), BF16), jax.ShapeDtypeStruct((s, WIDTH), F32),
                   jax.ShapeDtypeStruct((s // CHUNK, HEADS, HEAD_DIM, HEAD_DIM), BF16)]
        + (rider.out_shape if rider else []),
        scratch_shapes=[pltpu.VMEM((HEAD_DIM, HEAD_DIM), F32)] + (rider.scratch if rider else []),
        compiler_params=_cparams(("arbitrary", "arbitrary")), name=name)(proj, lb, hn, *(rider.arrs if rider else []))


def hgrn_bwd(dya, proj, o_all, states, lb, hn, dproj, *, rider=None, name):
    s = proj.shape[0]
    t = _hgrn_tile(s)
    nblk = s // t
    ncht = t // CHUNK
    head_blk0 = GP_COLS // HEAD_COLS
    nr = rider.n if rider else 0
    hbm = pl.BlockSpec(memory_space=pl.ANY)

    def body(*refs):
        dy_ref, p_ref, o_ref, st_ref, lb_ref, hn_ref = refs[:6]
        dp_ref, dhn_ref, dlb_ref = refs[7 + nr:10 + nr]
        dstate = refs[10 + 2 * nr]
        i = pl.program_id(1)
        if rider:
            total = HEADS * nblk
            rider.emit(pl.program_id(0) * nblk + i, total, _rider_mid_step(total),
                       refs[7:7 + nr], refs[10 + nr:10 + 2 * nr], refs[11 + 2 * nr:])

        @pl.when(i == 0)
        def _():
            dstate[...] = jnp.zeros_like(dstate)
            dhn_ref[...] = jnp.zeros_like(dhn_ref)
            dlb_ref[...] = jnp.zeros_like(dlb_ref)

        cums, causal, upper, row = _chunk_consts()
        lbv = lb_ref[...]
        hnv = hn_ref[...]

        chunks = range(ncht)
        rows = [slice(ci * CHUNK, (ci + 1) * CHUNK) for ci in chunks]
        qas = [p_ref[r, 0:HEAD_DIM] for r in rows]
        vbs = [p_ref[r, 2 * HEAD_DIM:3 * HEAD_DIM].astype(BF16) for r in rows]
        st0s = [st_ref[ci, 0] for ci in chunks]
        dzas, dobs = [], []
        dhn_acc = jnp.zeros_like(hnv)
        for r in rows:
            za = p_ref[r, 3 * HEAD_DIM:4 * HEAD_DIM]
            o = o_ref[r, :]
            dy = dy_ref[r, :]
            rn = lax.rsqrt(_rowmean(o * o) + EPS)
            on = o * rn
            sgz = _sigmoid(za)
            sz = za * sgz
            dzas.append(dy * on * hnv * (sgz * (1.0 + za * (1.0 - sgz))))
            dhn_acc = dhn_acc + _colsum(dy * on * sz)
            don = dy * hnv * sz
            dobs.append((rn * (don - on * _rowmean(don * on))).astype(BF16))
        pres = _hgrn_chunks(qas, [p_ref[r, HEAD_DIM:2 * HEAD_DIM] for r in rows], lbv, cums, row)
        scores = [jnp.where(causal, _dot_nt(pre["lhs"], pre["rhs"]), 0.0).astype(BF16) for pre in pres]
        das = [jnp.where(causal, _dot_nt(dob, vb), 0.0).astype(BF16) for dob, vb in zip(dobs, vbs)]
        dlhss = [_dot(da, pre["rhs"]) for da, pre in zip(das, pres)]
        drhss = [_dot_tn(da, pre["lhs"]) for da, pre in zip(das, pres)]
        dv_intra = [_dot_tn(a, dob) for a, dob in zip(scores, dobs)]
        dq_inter = [_dot(dob, st0) * pre["eb"] for dob, st0, pre in zip(dobs, st0s, pres)]
        qds = [pre["qd"].astype(BF16) for pre in pres]
        kdecs = [pre["kdec"].astype(BF16) for pre in pres]
        dst1 = dstate[...]
        dvs, dk_states, dbl_states = [None] * ncht, [None] * ncht, [None] * ncht
        for ci in reversed(chunks):
            dst1b = dst1.astype(BF16)
            dvs[ci] = dv_intra[ci] + _dot_nt(kdecs[ci], dst1b)
            dk_states[ci] = _dot(vbs[ci], dst1b) * pres[ci]["edec"]
            dbl_states[ci] = pres[ci]["ebl"] * _colsum(dst1 * st0s[ci].astype(F32))
            dst1 = dst1 * pres[ci]["ebl"] + _dot_tn(dobs[ci], qds[ci])
        dstate[...] = dst1
        dqs, dks, dbs, dbls = [], [], [], []
        for ci in chunks:
            pre = pres[ci]
            q, k = pre["q"], pre["k"]
            dq_a = jnp.zeros_like(q)
            dk_a = jnp.zeros_like(k)
            db = q * dq_inter[ci] - k * dk_states[ci]
            for j in range(N_SUB):
                cols = slice(j * HEAD_DIM, (j + 1) * HEAD_DIM)
                dlhs, drhs = dlhss[ci][:, cols], drhss[ci][:, cols]
                dq_a = dq_a + pre["dms"][j] * dlhs
                dk_a = dk_a + jnp.where(row // SUB == j, drhs, 0.0)
                db = db + (pre["lhs"][:, cols].astype(F32) * dlhs - pre["rhs"][:, cols].astype(F32) * drhs)
            dqs.append(dq_inter[ci] + pre["ec"] * dq_a)
            dks.append(dk_states[ci] + pre["enc"] * dk_a)
            dbs.append(db)
            dbls.append(_colsum(k * dk_states[ci]) + dbl_states[ci])
        dlfs = [_dot_split(upper, db) + dbl for db, dbl in zip(dbs, dbls)]
        dlb_acc = jnp.zeros_like(lbv)
        for ci in chunks:
            pre = pres[ci]
            sq, sa, sna, oml = pre["sq"], pre["sa"], pre["sna"], pre["oml"]
            dqa = dqs[ci] * (sq * (1.0 + qas[ci] * (1.0 - sq)))
            diff = jnp.where(pre["f"] >= MIN_FORGET, dlfs[ci] / pre["fc"], 0.0) - dks[ci]
            dlb_acc = dlb_acc + _colsum(diff * sna)
            dfa = diff * (oml * sa * sna)
            dp_ref[rows[ci], :] = jnp.concatenate([dqa, dfa, dvs[ci], dzas[ci]], axis=1).astype(dp_ref.dtype)
        dhn_ref[...] += dhn_acc
        dlb_ref[...] += dlb_acc

    rev = lambda h, i: (nblk - 1 - i, h)
    return pl.pallas_call(
        body, grid=(HEADS, nblk),
        in_specs=[pl.BlockSpec((t, HEAD_DIM), rev),
                  pl.BlockSpec((t, HEAD_COLS), lambda h, i: (nblk - 1 - i, h)),
                  pl.BlockSpec((t, HEAD_DIM), rev),
                  pl.BlockSpec((ncht, 1, HEAD_DIM, HEAD_DIM), lambda h, i: (nblk - 1 - i, h, 0, 0)),
                  pl.BlockSpec((1, HEAD_DIM), lambda h, i: (0, h)),
                  pl.BlockSpec((1, HEAD_DIM), lambda h, i: (0, h)),
                  hbm] + [hbm] * nr,
        out_specs=[pl.BlockSpec((t, HEAD_COLS), lambda h, i: (nblk - 1 - i, head_blk0 + h)),
                   pl.BlockSpec((1, HEAD_DIM), lambda h, i: (0, h)),
                   pl.BlockSpec((1, HEAD_DIM), lambda h, i: (0, h))] + [hbm] * nr,
        out_shape=[jax.ShapeDtypeStruct(dproj.shape, dproj.dtype),
                   jax.ShapeDtypeStruct((1, WIDTH), F32), jax.ShapeDtypeStruct((1, WIDTH), F32)]
        + (rider.out_shape if rider else []),
        scratch_shapes=[pltpu.VMEM((HEAD_DIM, HEAD_DIM), F32)] + (rider.scratch if rider else []),
        input_output_aliases={6: 0},
        compiler_params=_cparams(("arbitrary", "arbitrary")),
        name=name)(dya, proj, o_all, states, lb, hn, dproj, *(rider.arrs if rider else []))


def _softmax_rows(lower):
    mx = jnp.max(lower, axis=0, keepdims=True)
    e = jnp.exp(lower - mx)
    return e / jnp.sum(e, axis=0, keepdims=True)


def lb_table(lower, *, name):
    depth, w = lower.shape

    def body(l_ref, o_ref):
        sm = _softmax_rows(l_ref[...])
        acc = jnp.zeros((1, w), F32)
        o_ref[0:1, :] = acc
        for l in range(1, depth):
            acc = acc + sm[l:l + 1, :]
            o_ref[l:l + 1, :] = acc

    return pl.pallas_call(body, out_shape=jax.ShapeDtypeStruct((depth, w), F32), name=name)(lower)


def lb_table_bwd(lower, dlb, *, name):
    depth, w = lower.shape

    def body(l_ref, d_ref, o_ref):
        sm = _softmax_rows(l_ref[...])
        dlbv = d_ref[...]
        dsm = [jnp.zeros((1, w), F32)]
        for i in range(1, depth):
            acc = jnp.zeros((1, w), F32)
            for l in range(i, depth):
                acc = acc + dlbv[l:l + 1, :]
            dsm.append(acc)
        inner = jnp.zeros((1, w), F32)
        for i in range(depth):
            inner = inner + sm[i:i + 1, :] * dsm[i]
        for i in range(depth):
            o_ref[i:i + 1, :] = sm[i:i + 1, :] * (dsm[i] - inner)

    return pl.pallas_call(body, out_shape=jax.ShapeDtypeStruct((depth, w), F32), name=name)(lower, dlb)


def w_ada_grad(c_all, dmod_cols, *, name):
    depth, _, cols = dmod_cols.shape
    d = c_all.shape[1]

    def body(c_ref, dm_ref, o_ref):
        cv = c_ref[...]
        ca = cv * _sigmoid(cv)
        o_ref[...] = _dot_tn(ca, dm_ref[...])

    return pl.pallas_call(
        body, grid=(depth,),
        in_specs=[pl.BlockSpec((N_DEV, d), lambda l: (0, 0)), pl.BlockSpec((None, N_DEV, cols), lambda l: (l, 0, 0))],
        out_specs=pl.BlockSpec((None, d, cols), lambda l: (l, 0, 0)),
        out_shape=jax.ShapeDtypeStruct((depth, d, cols), F32),
        compiler_params=_cparams(("arbitrary",)), name=name)(c_all, dmod_cols)


def sum_parts(parts, *, name):
    p, r, c = parts.shape

    def body(p_ref, o_ref):
        acc = p_ref[0]
        for j in range(1, p):
            acc = acc + p_ref[j]
        o_ref[...] = acc

    return pl.pallas_call(body, out_shape=jax.ShapeDtypeStruct((r, c), F32), name=name)(parts)


def _adam_rows(r, c):
    tr = r
    while tr * c * 4 > (1 << 20) and tr % 16 == 0:
        tr //= 2
    return tr


def _adam_update(w_ref, m_ref, v_ref, g_ref, go_ref, d_ref, mo_ref, vo_ref):
    g = g_ref[0].astype(F32)
    for j in range(1, g_ref.shape[0]):
        g = g + g_ref[j].astype(F32)
    mn = ADAM_B1 * m_ref[...] + (1.0 - ADAM_B1) * g
    vn = ADAM_B2 * v_ref[...] + (1.0 - ADAM_B2) * (g * g)
    m_hat = mn / (1.0 - ADAM_B1 ** ADAM_STEP)
    v_hat = vn / (1.0 - ADAM_B2 ** ADAM_STEP)
    go_ref[...] = g
    d_ref[...] = -ADAM_LR * (m_hat / (jnp.sqrt(v_hat) + ADAM_EPS) + ADAM_WD * w_ref[...])
    mo_ref[...] = mn
    vo_ref[...] = vn


def adamw(w, m, v, gparts, *, name):
    r, c = w.shape
    p = gparts.shape[0]
    tr = _adam_rows(r, c)
    spec = pl.BlockSpec((tr, c), lambda i: (i, 0))
    shp = jax.ShapeDtypeStruct((r, c), F32)
    return pl.pallas_call(
        functools.partial(_adam_update), grid=(r // tr,),
        in_specs=[spec, spec, spec, pl.BlockSpec((p, tr, c), lambda i: (0, i, 0))],
        out_specs=[spec] * 4, out_shape=[shp] * 4,
        compiler_params=_cparams(("arbitrary",)), name=name)(w, m, v, gparts)


def adamw_layers(w, m, v, gparts, *, name):
    depth, r, c = w.shape
    p = gparts[0].shape[0]
    tr = _adam_rows(r, c)

    def body(w_ref, m_ref, v_ref, *rest):
        g_refs, outs = rest[:depth], rest[depth:]
        layer = pl.program_id(0)
        for k in range(depth):
            @pl.when(layer == k)
            def _(k=k):
                _adam_update(w_ref, m_ref, v_ref, g_refs[k], *outs)

    spec = pl.BlockSpec((None, tr, c), lambda l, i: (l, i, 0))
    g_specs = [pl.BlockSpec((p, tr, c), functools.partial(lambda l, i, k: (0, jnp.where(l == k, i, 0), 0), k=k))
               for k in range(depth)]
    shp = jax.ShapeDtypeStruct((depth, r, c), F32)
    return pl.pallas_call(
        body, grid=(depth, r // tr),
        in_specs=[spec, spec, spec] + g_specs,
        out_specs=[spec] * 4, out_shape=[shp] * 4,
        compiler_params=_cparams(("arbitrary", "arbitrary")), name=name)(w, m, v, *gparts)


def _position():
    x, y, c = lax.axis_index("x"), lax.axis_index("y"), lax.axis_index("c")
    return x, y, c


def _dev_index(x, y, c):
    return 4 * x + 2 * y + c


def _gather_phases(ins, outs, send_sems, recv_sems, local_sems):
    n = len(ins)
    x, y, c = _position()
    me, sibling = (x, y, c), (x, y, 1 - c)
    chips = [(1 - x, y), (x, 1 - y), (1 - x, 1 - y)]

    def copy(a, k, block, to, own=False):
        slot = outs[a].at[_dev_index(*block)]
        return pltpu.make_async_remote_copy(
            src_ref=ins[a] if own else slot, dst_ref=slot,
            send_sem=send_sems.at[a * 7 + k], recv_sem=recv_sems.at[a * 7 + k],
            device_id=to, device_id_type=MESH)

    def mine(a):
        return pltpu.make_async_copy(ins[a], outs[a].at[_dev_index(*me)], local_sems.at[a])

    def first(a):
        return [copy(a, 0, me, sibling, True)] + [copy(a, 1 + j, me, (*chip, c), True) for j, chip in enumerate(chips)]

    def passed(a):
        return [copy(a, 4 + j, (*chip, c), sibling) for j, chip in enumerate(chips)]

    def start():
        for a in range(n):
            mine(a).start()
        for a in range(n):
            for cp in first(a):
                cp.start()

    def mid():
        for j, chip in enumerate(chips):
            for a in range(n):
                copy(a, 1 + j, (*chip, c), me).wait_recv()
                passed(a)[j].start()

    def finish():
        for a in range(n):
            copy(a, 0, sibling, me).wait_recv()
            for j, chip in enumerate(chips):
                copy(a, 4 + j, (*chip, 1 - c), me).wait_recv()
        for a in range(n):
            for cp in first(a) + passed(a):
                cp.wait_send()
            mine(a).wait()

    return start, mid, finish


def _scatter_phases(ins, outs, send_sems, recv_sems, local_sems):
    n = len(ins)
    x, y, c = _position()
    me = _dev_index(x, y, c)

    def peer(r):
        return (x ^ (r >> 2), y ^ ((r >> 1) & 1), c ^ (r & 1))

    def copy(a, r):
        to = peer(r)
        return pltpu.make_async_remote_copy(
            src_ref=ins[a].at[_dev_index(*to)], dst_ref=outs[a].at[me],
            send_sem=send_sems.at[a * 7 + r - 1], recv_sem=recv_sems.at[a * 7 + r - 1],
            device_id=to, device_id_type=MESH)

    def arrival(a, r):
        return pltpu.make_async_remote_copy(
            src_ref=ins[a].at[me], dst_ref=outs[a].at[_dev_index(*peer(r))],
            send_sem=send_sems.at[a * 7 + r - 1], recv_sem=recv_sems.at[a * 7 + r - 1],
            device_id=peer(r), device_id_type=MESH)

    def mine(a):
        return pltpu.make_async_copy(ins[a].at[me], outs[a].at[me], local_sems.at[a])

    def start():
        for a in range(n):
            mine(a).start()
        for r in range(1, N_DEV):
            for a in range(n):
                copy(a, r).start()

    def finish():
        for r in range(1, N_DEV):
            for a in range(n):
                arrival(a, r).wait_recv()
        for r in range(1, N_DEV):
            for a in range(n):
                copy(a, r).wait_send()
        for a in range(n):
            mine(a).wait()

    return start, None, finish


class Rider:
    def __init__(self, kind, arrs):
        self.kind, self.arrs, self.n = kind, list(arrs), len(arrs)
        lead = (N_DEV,) if kind == "gather" else ()
        self.out_shape = [jax.ShapeDtypeStruct(lead + a.shape, a.dtype) for a in self.arrs]
        self.scratch = [pltpu.SemaphoreType.DMA((7 * self.n,)), pltpu.SemaphoreType.DMA((7 * self.n,)),
                        pltpu.SemaphoreType.DMA((self.n,))]

    def phases(self, ins, outs, sems):
        make = _gather_phases if self.kind == "gather" else _scatter_phases
        return make(ins, outs, *sems)

    def emit(self, step, total, mid_step, ins, outs, sems):
        start, mid, finish = self.phases(ins, outs, sems)
        pl.when(step == 0)(start)
        if mid is not None:
            pl.when(step == mid_step)(mid)
        pl.when(step == total - 1)(finish)


def _standalone(rider, name):
    n = rider.n
    hbm = pl.BlockSpec(memory_space=pl.ANY)

    def body(*refs):
        start, mid, finish = rider.phases(refs[:n], refs[n:2 * n], refs[2 * n:])
        start()
        if mid is not None:
            mid()
        finish()

    return pl.pallas_call(body, out_shape=rider.out_shape, in_specs=[hbm] * n, out_specs=[hbm] * n,
                          scratch_shapes=rider.scratch, name=name)(*rider.arrs)


def all_gather(arrs, *, name):
    return _standalone(Rider("gather", arrs), name)


def scatter_parts(arrs, *, name):
    return _standalone(Rider("scatter", arrs), name)


def mod_exchange(c_all, w_ada, b_cols, *, name):
    depth, d, cols = w_ada.shape
    hbm = pl.BlockSpec(memory_space=pl.ANY)
    vmem = pl.BlockSpec(memory_space=pltpu.VMEM)

    def body(c_ref, w_ref, b_ref, out_ref, wbuf, sendbuf, send_sems, recv_sems, load_sem):
        x, y, c = _position()
        me = _dev_index(x, y, c)
        cv = c_ref[...]
        ca = cv * _sigmoid(cv)
        for l in range(depth):
            load = pltpu.make_async_copy(w_ref.at[l], wbuf, load_sem)
            load.start()
            load.wait()
            part = jnp.dot(ca, wbuf[...], preferred_element_type=F32,
                           precision=lax.Precision.HIGHEST) + b_ref[l:l + 1, :]
            for bi in range(N_DEV):
                sendbuf[bi, l:l + 1, :] = part[bi:bi + 1, :]

        def peer(r):
            return (x ^ (r >> 2), y ^ ((r >> 1) & 1), c ^ (r & 1))

        def copy(r):
            to = peer(r)
            return pltpu.make_async_remote_copy(
                src_ref=sendbuf.at[_dev_index(*to)], dst_ref=out_ref.at[me],
                send_sem=send_sems.at[r - 1], recv_sem=recv_sems.at[r - 1],
                device_id=to, device_id_type=MESH)

        def arrival(r):
            return pltpu.make_async_remote_copy(
                src_ref=sendbuf.at[me], dst_ref=out_ref.at[_dev_index(*peer(r))],
                send_sem=send_sems.at[r - 1], recv_sem=recv_sems.at[r - 1],
                device_id=peer(r), device_id_type=MESH)

        out_ref[me] = sendbuf[me]
        sends = [copy(r) for r in range(1, N_DEV)]
        for cp in sends:
            cp.start()
        for r in range(1, N_DEV):
            arrival(r).wait_recv()
        for cp in sends:
            cp.wait_send()

    return pl.pallas_call(
        body,
        out_shape=jax.ShapeDtypeStruct((N_DEV, depth, cols), F32),
        in_specs=[vmem, hbm, vmem], out_specs=vmem,
        scratch_shapes=[pltpu.VMEM((d, cols), F32), pltpu.VMEM((N_DEV, depth, cols), F32),
                        pltpu.SemaphoreType.DMA((7,)), pltpu.SemaphoreType.DMA((7,)), pltpu.SemaphoreType.DMA],
        compiler_params=pltpu.CompilerParams(vmem_limit_bytes=VMEM_LIMIT),
        name=name)(c_all, w_ada, b_cols)


def kernel(x, c, w_ada, b_ada, norm_pre, norm_post, w_in, lower_bounds, hgrn_norm, pool_w, pool_scale, w_proj_a, w_proj_b, w_out, loss_target, m_w_ada, m_b_ada, m_norm_pre, m_norm_post, m_w_in, m_lower_bounds, m_hgrn_norm, m_pool_w, m_pool_scale, m_w_proj_a, m_w_proj_b, m_w_out, v_w_ada, v_b_ada, v_norm_pre, v_norm_post, v_w_in, v_lower_bounds, v_hgrn_norm, v_pool_w, v_pool_scale, v_w_proj_a, v_w_proj_b, v_w_out):
    depth = w_in.shape[0]
    d = D_MODEL
    ada_cols = w_ada.shape[2]
    xi, yi, ci = _position()
    me = _dev_index(xi, yi, ci)
    xs = x[0]
    target = loss_target[0]
    ng = len(POOL_WINDOWS)

    def shards(l):
        w_in_l = w_in[l].astype(BF16)
        half = w_in_l.shape[1] // 2
        return [w_in_l[:, :half], w_in_l[:, half:], w_proj_a[l].astype(BF16), w_proj_b[l].astype(BF16),
                w_out[l].astype(BF16), pool_w[l].astype(BF16)]

    def other_weights(g_pa, g_pb, g_out, g_pool):
        return dict(
            pa=jnp.transpose(g_pa, (1, 0, 2)).reshape(WIDTH, d),
            pb=jnp.transpose(g_pb, (1, 0, 2)).reshape(WIDTH, d),
            w_out=g_out.reshape(d, d),
            pool=jnp.transpose(g_pool, (1, 0, 2, 3)).reshape(ng, POOL_GW, POOL_GW))

    w_in_full = [permute_w_in(*all_gather(shards(0)[:2], name="gather_w_in"), name="permute_w_in")]
    others_full = []
    gathered = []

    (c_all,) = all_gather([c], name="gather_c")
    c_all = c_all.reshape(N_DEV, d)
    b_cols = lax.dynamic_slice_in_dim(b_ada, me * ada_cols, ada_cols, axis=1)
    mod_parts = mod_exchange(c_all, w_ada, b_cols, name="mod_exchange")
    mod = jnp.transpose(mod_parts, (1, 0, 2)).reshape(depth, 3 * d)
    lb_all = lb_table(lower_bounds, name="lb_table")

    saved = []
    cur = xs
    for l in range(depth):
        shift, scale, gate = mod[l:l + 1, :d], mod[l:l + 1, d:2 * d], mod[l:l + 1, 2 * d:]
        h = prenorm_fwd(cur, norm_pre[l:l + 1], shift, scale, name="prenorm_fwd")
        head_cols = (GP_COLS, IN_COLS - GP_COLS)
        if l + 1 < depth:
            proj_gp, nxt_lo = matmul_nn(h, w_in_full[l], cols=(0, GP_COLS), tm=2048, tn=1024, out_dtype=BF16,
                                        rider=Rider("gather", shards(l + 1)[:1]), name="mm_w_in_gp_gather")
            proj_h, nxt_hi = matmul_nn(h, w_in_full[l], cols=head_cols, tm=2048, tn=1024, out_dtype=F32,
                                       rider=Rider("gather", shards(l + 1)[1:2]), name="mm_w_in_heads_gather")
            w_in_full.append(permute_w_in(nxt_lo, nxt_hi, name="permute_w_in"))
        else:
            proj_gp = matmul_nn(h, w_in_full[l], cols=(0, GP_COLS), tm=2048, tn=1024, out_dtype=BF16,
                                name="mm_w_in_gp")
            proj_h = matmul_nn(h, w_in_full[l], cols=head_cols, tm=2048, tn=1024, out_dtype=F32,
                               name="mm_w_in_heads")
        ride = (shards(0)[2:] if l == 0 else []) + (shards(l + 1)[2:] if l + 1 < depth else [])
        if ride:
            y_a, o_all, states, *got = hgrn_fwd(proj_h, lb_all[l:l + 1], hgrn_norm[l:l + 1],
                                                 rider=Rider("gather", ride), name="hgrn_fwd_gather%d" % len(ride))
            for k in range(0, len(got), 4):
                others_full.append(other_weights(*got[k:k + 4]))
        else:
            y_a, o_all, states = hgrn_fwd(proj_h, lb_all[l:l + 1], hgrn_norm[l:l + 1], name="hgrn_fwd")
        w = dict(w_in=w_in_full[l], **others_full[l])
        gathered.append(w)
        y_b, pooled, mixed = pool_fwd(proj_gp, w["pool"], pool_scale[l:l + 1], name="pool_fwd")
        pa, pb, merged = proj_gate_fwd(y_a, y_b, w["pa"], w["pb"], proj_gp, name="proj_gate_fwd")
        out, nxt = w_out_postnorm_fwd(merged, w["w_out"], cur, norm_post[l:l + 1], gate, name="w_out_postnorm_fwd")
        saved.append(dict(x=cur, h=h, proj_gp=proj_gp, proj_h=proj_h, y_a=y_a, o=o_all, states=states, y_b=y_b,
                          pooled=pooled, mixed=mixed, pa=pa, pb=pb, merged=merged, out=out, scale=scale, gate=gate))
        cur = nxt

    g, loss_part = loss_head(cur, target, name="loss_head")
    loss = lax.psum(loss_part[0, 0], ("x", "y", "c"))

    small = [None] * depth
    big_in = [None] * depth
    big_others = [None] * depth
    pend_in, pend_others = None, None
    for l in reversed(range(depth)):
        w, sv = gathered[l], saved[l]
        dout, dgate, dnpost = postnorm_bwd(g, sv["out"], norm_post[l:l + 1], sv["gate"], name="postnorm_bwd")
        dw_out = matmul_tn(sv["merged"], dout, tm=2048, tn=1024, out_dtype=BF16, name="mm_w_out_dw")
        dpa, dpb, dproj = gate_bwd(dout, w["w_out"], sv["proj_gp"], sv["pa"], sv["pb"], name="gate_bwd")
        dya = matmul_nt(dpa, w["pa"], tm=1024, tn=2048, out_dtype=F32, name="mm_proj_a_dx")
        dyb = matmul_nt(dpb, w["pb"], tm=1024, tn=2048, out_dtype=F32, name="mm_proj_b_dx")
        dw_pa = matmul_tn(sv["y_a"], dpa, tm=2048, tn=2048, out_dtype=BF16, name="mm_proj_a_dw")
        dw_pb = matmul_tn(sv["y_b"], dpb, tm=2048, tn=2048, out_dtype=BF16, name="mm_proj_b_dw")
        dproj, dpool_w, dpool_scale = pool_bwd(dyb, sv["proj_gp"], sv["pooled"], sv["mixed"], w["pool"],
                                               pool_scale[l:l + 1], dproj, name="pool_bwd")
        by_owner = lambda t: jnp.transpose(t.reshape(WIDTH, N_DEV, d // N_DEV), (1, 0, 2))
        others = [by_owner(dw_pa), by_owner(dw_pb), dw_out.reshape(N_DEV, d // N_DEV, d),
                  jnp.transpose(dpool_w.astype(BF16).reshape(ng, N_DEV, POOL_GW // N_DEV, POOL_GW), (1, 0, 2, 3))]
        ride = (pend_others or []) + (others if l == 0 else [])
        if ride:
            dproj, dhn, dlb, *recv = hgrn_bwd(dya, sv["proj_h"], sv["o"], sv["states"], lb_all[l:l + 1],
                                              hgrn_norm[l:l + 1], dproj, rider=Rider("scatter", ride),
                                              name="hgrn_bwd_scatter%d" % len(ride))
            if pend_others:
                big_others[l + 1], recv = recv[:len(pend_others)], recv[len(pend_others):]
            if l == 0:
                big_others[0] = recv
        else:
            dproj, dhn, dlb = hgrn_bwd(dya, sv["proj_h"], sv["o"], sv["states"], lb_all[l:l + 1],
                                       hgrn_norm[l:l + 1], dproj, name="hgrn_bwd")
        above = Rider("scatter", [pend_in]) if pend_in is not None else None
        dx_args = (dproj, w["w_in"], sv["x"], norm_pre[l:l + 1], sv["scale"], g)
        if l > 0:
            if above:
                g, dshift, dscale, dnpre, big_in[l + 1] = w_in_dx_prenorm_bwd(*dx_args, rider=above,
                                                                              name="w_in_dx_prenorm_bwd_scatter")
            else:
                g, dshift, dscale, dnpre = w_in_dx_prenorm_bwd(*dx_args, name="w_in_dx_prenorm_bwd")
            dw_in = matmul_tn(sv["h"], dproj, tm=2048, tn=1024, out_dtype=BF16, name="mm_w_in_dw")
            pend_in, pend_others = unpermute_w_in(dw_in, name="unpermute_w_in"), others
        else:
            if above:
                dw_in, big_in[1] = matmul_tn(sv["h"], dproj, tm=2048, tn=1024, out_dtype=BF16, rider=above,
                                             name="mm_w_in_dw_scatter")
            else:
                dw_in = matmul_tn(sv["h"], dproj, tm=2048, tn=1024, out_dtype=BF16, name="mm_w_in_dw")
            g, dshift, dscale, dnpre, big_in[0] = w_in_dx_prenorm_bwd(
                *dx_args, rider=Rider("scatter", [unpermute_w_in(dw_in, name="unpermute_w_in")]),
                name="w_in_dx_prenorm_bwd_scatter")
        small[l] = jnp.concatenate([dshift, dscale, dgate, dnpre, dnpost, dlb, dhn, dpool_scale], axis=1)
    grad_x = g[None]
    big = [[big_in[l]] + list(big_others[l]) for l in range(depth)]

    small_mine = jnp.concatenate(small, axis=0)
    (small_all,) = all_gather([small_mine], name="gather_small")
    small_sum = sum_parts(small_all, name="sum_small")
    dmod_all = small_all[:, :, :3 * d]
    dmod_cols = jnp.transpose(lax.dynamic_slice_in_dim(dmod_all, me * ada_cols, ada_cols, axis=2), (1, 0, 2))
    g_w_ada = w_ada_grad(c_all, dmod_cols, name="w_ada_grad")
    off = 3 * d
    g_b_ada = small_sum[:, :off]
    g_npre = small_sum[:, off:off + d]
    g_npost = small_sum[:, off + d:off + 2 * d]
    g_lb_tab = small_sum[:, off + 2 * d:off + 2 * d + WIDTH]
    g_hn = small_sum[:, off + 2 * d + WIDTH:off + 2 * d + 2 * WIDTH]
    g_ps = small_sum[:, off + 2 * d + 2 * WIDTH:]
    g_lower = lb_table_bwd(lower_bounds, g_lb_tab, name="lb_table_bwd")

    def update(wt, mt, vt, gparts, shape2, name):
        outs = adamw(wt.reshape(shape2), mt.reshape(shape2), vt.reshape(shape2), gparts, name=name)
        return [o.reshape(wt.shape) for o in outs]

    def update_layers(wt, mt, vt, kind, name):
        shape3 = (depth, -1, wt.shape[-1])
        w3 = wt.reshape(shape3)
        gps = [big[l][kind].reshape((N_DEV,) + w3.shape[1:]) for l in range(depth)]
        outs = adamw_layers(w3, mt.reshape(shape3), vt.reshape(shape3), gps, name=name)
        return [o.reshape(wt.shape) for o in outs]

    def update_small(wt, mt, vt, gt, name):
        shape2 = (-1, wt.shape[-1])
        return update(wt, mt, vt, gt.reshape(shape2)[None], shape2, name)

    res = {
        "w_ada": update_small(w_ada, m_w_ada, v_w_ada, g_w_ada, "adamw_w_ada"),
        "b_ada": update_small(b_ada, m_b_ada, v_b_ada, g_b_ada, "adamw_b_ada"),
        "norm_pre": update_small(norm_pre, m_norm_pre, v_norm_pre, g_npre, "adamw_norm_pre"),
        "norm_post": update_small(norm_post, m_norm_post, v_norm_post, g_npost, "adamw_norm_post"),
        "w_in": update_layers(w_in, m_w_in, v_w_in, 0, "adamw_w_in"),
        "lower_bounds": update_small(lower_bounds, m_lower_bounds, v_lower_bounds, g_lower, "adamw_lower_bounds"),
        "hgrn_norm": update_small(hgrn_norm, m_hgrn_norm, v_hgrn_norm, g_hn, "adamw_hgrn_norm"),
        "pool_w": update_layers(pool_w, m_pool_w, v_pool_w, 4, "adamw_pool_w"),
        "pool_scale": update_small(pool_scale, m_pool_scale, v_pool_scale, g_ps, "adamw_pool_scale"),
        "w_proj_a": update_layers(w_proj_a, m_w_proj_a, v_w_proj_a, 1, "adamw_w_proj_a"),
        "w_proj_b": update_layers(w_proj_b, m_w_proj_b, v_w_proj_b, 2, "adamw_w_proj_b"),
        "w_out": update_layers(w_out, m_w_out, v_w_out, 3, "adamw_w_out"),
    }
    order = ["w_ada", "b_ada", "norm_pre", "norm_post", "w_in", "lower_bounds", "hgrn_norm", "pool_w",
             "pool_scale", "w_proj_a", "w_proj_b", "w_out"]
    outs = [loss, grad_x]
    for k in range(4):
        outs += [res[nm][k] for nm in order]
    return tuple(outs)
```

```python
import functools

import jax
import jax.numpy as jnp
from jax import lax
from jax.experimental import pallas as pl
from jax.experimental.pallas import tpu as pltpu

F32 = jnp.float32
BF16 = jnp.bfloat16
MESH = pl.DeviceIdType.MESH

N_DEV = 8
EPS = 1e-6
MIN_FORGET = 1e-30
D_MODEL = 2048
HEADS = 8
HEAD_DIM = 128
CHUNK = 64
SUB = 16
N_SUB = CHUNK // SUB
WIDTH = 1024
POOL_WINDOWS = (2, 4, 8, 16)
POOL_GW = 256
HALO = 16
IN_COLS = 10240
LANE = 128
N_COLBLK = IN_COLS // LANE
GATE_COLS = 4096
HEAD_COLS = 4 * HEAD_DIM
POOL_COLS = 2 * POOL_GW
GP_COLS = GATE_COLS + len(POOL_WINDOWS) * POOL_COLS
MAX_EXP = 80.0

ADAM_LR = 0.001
ADAM_B1 = 0.9
ADAM_B2 = 0.999
ADAM_EPS = 1e-08
ADAM_WD = 0.01
ADAM_STEP = 10

VMEM_LIMIT = 62 * 1024 * 1024


def _cparams(sem=None):
    return pltpu.CompilerParams(dimension_semantics=sem, vmem_limit_bytes=VMEM_LIMIT)


def _sigmoid(v):
    return 1.0 / (1.0 + jnp.exp(-v))


def _dot(a, b):
    return jnp.dot(a, b, preferred_element_type=F32)


def _dot_nt(a, b):
    return lax.dot_general(a, b, (((1,), (1,)), ((), ())), preferred_element_type=F32)


def _dot_tn(a, b):
    return lax.dot_general(a, b, (((0,), (0,)), ((), ())), preferred_element_type=F32)


def _colsum(v):
    return jnp.sum(v, axis=0, keepdims=True)


def _rowmean(v):
    return jnp.mean(v, axis=-1, keepdims=True)


def _orig_block_static(n):
    if n < 32:
        return n + 48
    if n < 48:
        m = n - 32
        t = m % 4
        return 32 + 2 * (m // 4) + (t % 2) + 8 * (t // 2)
    m = n - 48
    return 8 * (m % 4) + m // 4


def _accumulate(step, steps, prod, o_ref, acc_ref):
    if steps == 1:
        o_ref[...] = prod.astype(o_ref.dtype)
        return

    @pl.when(step == 0)
    def _():
        acc_ref[...] = prod

    @pl.when(step > 0)
    def _():
        acc_ref[...] += prod

    @pl.when(step == steps - 1)
    def _():
        o_ref[...] = acc_ref[...].astype(o_ref.dtype)


def _matmul_call(dot, a, b, *, grid, in_specs, out_spec, out_shape, acc_shape, steps, rider, name):
    nr = rider.n if rider else 0
    hbm = pl.BlockSpec(memory_space=pl.ANY)
    has_acc = steps > 1

    def body(*refs):
        a_ref, b_ref = refs[:2]
        o_ref = refs[2 + nr]
        scratch = refs[3 + 2 * nr:]
        if rider:
            total = grid[0] * grid[1]
            rider.emit(pl.program_id(0) * grid[1] + pl.program_id(1), total, _rider_mid_step(total),
                       refs[2:2 + nr], refs[3 + nr:3 + 2 * nr], scratch[1 if has_acc else 0:])
        _accumulate(pl.program_id(1), steps, dot(a_ref[...], b_ref[...]), o_ref, scratch[0] if has_acc else None)

    outs = pl.pallas_call(
        body, grid=grid,
        in_specs=in_specs + [hbm] * nr, out_specs=[out_spec] + [hbm] * nr,
        out_shape=[out_shape] + (rider.out_shape if rider else []),
        scratch_shapes=([pltpu.VMEM(acc_shape, F32)] if has_acc else []) + (rider.scratch if rider else []),
        compiler_params=_cparams(("arbitrary", "arbitrary")),
        name=name)(a, b, *(rider.arrs if rider else []))
    return outs if rider else outs[0]


def matmul_nn(a, b, *, tm, tn, out_dtype, name, rider=None, cols=None):
    m, k = a.shape
    col0, n = cols if cols else (0, b.shape[1])
    tm = min(tm, m)
    j0 = col0 // tn
    return _matmul_call(
        _dot, a, b, grid=(n // tn, m // tm),
        in_specs=[pl.BlockSpec((tm, k), lambda j, i: (i, 0)), pl.BlockSpec((k, tn), lambda j, i: (0, j0 + j))],
        out_spec=pl.BlockSpec((tm, tn), lambda j, i: (i, j)),
        out_shape=jax.ShapeDtypeStruct((m, n), out_dtype), acc_shape=None, steps=1, rider=rider, name=name)


def matmul_nt(a, b, *, tm, tn, out_dtype, name, rider=None):
    m, n = a.shape
    k = b.shape[0]
    tm = min(tm, m)
    return _matmul_call(
        _dot_nt, a, b, grid=(m // tm, n // tn),
        in_specs=[pl.BlockSpec((tm, tn), lambda i, j: (i, j)), pl.BlockSpec((k, tn), lambda i, j: (0, j))],
        out_spec=pl.BlockSpec((tm, k), lambda i, j: (i, 0)),
        out_shape=jax.ShapeDtypeStruct((m, k), out_dtype), acc_shape=(tm, k), steps=n // tn, rider=rider, name=name)


def matmul_tn(a, b, *, tm, tn, out_dtype, name, rider=None):
    m, k = a.shape
    n = b.shape[1]
    tm = min(tm, m)
    return _matmul_call(
        _dot_tn, a, b, grid=(n // tn, m // tm),
        in_specs=[pl.BlockSpec((tm, k), lambda j, i: (i, 0)), pl.BlockSpec((tm, tn), lambda j, i: (i, j))],
        out_spec=pl.BlockSpec((k, tn), lambda j, i: (0, j)),
        out_shape=jax.ShapeDtypeStruct((k, n), out_dtype), acc_shape=(k, tn), steps=m // tm, rider=rider, name=name)


def permute_w_in(staged_lo, staged_hi, *, name):
    k = staged_lo.shape[1]
    own = IN_COLS // N_DEV
    half = own // 2
    tr = min(256, k)

    def body(lo_ref, hi_ref, o_ref):
        for nb in range(N_COLBLK):
            dev, col = divmod(_orig_block_static(nb) * LANE, own)
            src = lo_ref[dev, :, col:col + LANE] if col < half else hi_ref[dev, :, col - half:col - half + LANE]
            o_ref[:, nb * LANE:(nb + 1) * LANE] = src

    spec = pl.BlockSpec((N_DEV, tr, half), lambda i: (0, i, 0))
    return pl.pallas_call(
        body, grid=(k // tr,),
        in_specs=[spec, spec],
        out_specs=pl.BlockSpec((tr, IN_COLS), lambda i: (i, 0)),
        out_shape=jax.ShapeDtypeStruct((k, IN_COLS), staged_lo.dtype),
        compiler_params=_cparams(("arbitrary",)), name=name)(staged_lo, staged_hi)


def unpermute_w_in(dw, *, name):
    k = dw.shape[0]
    own = IN_COLS // N_DEV
    tr = min(256, k)

    def body(i_ref, o_ref):
        for nb in range(N_COLBLK):
            dev, col = divmod(_orig_block_static(nb) * LANE, own)
            o_ref[dev, :, col:col + LANE] = i_ref[:, nb * LANE:(nb + 1) * LANE]

    return pl.pallas_call(
        body, grid=(k // tr,),
        in_specs=[pl.BlockSpec((tr, IN_COLS), lambda i: (i, 0))],
        out_specs=pl.BlockSpec((N_DEV, tr, own), lambda i: (0, i, 0)),
        out_shape=jax.ShapeDtypeStruct((N_DEV, k, own), dw.dtype),
        compiler_params=_cparams(("arbitrary",)), name=name)(dw)


def _row_tile(s):
    return min(256, s)


def _norm_tile(s):
    return min(512, s)


def _row_spec(t, w, col=0):
    return pl.BlockSpec((t, w), lambda i: (i, col))


def _vec_spec(w):
    return pl.BlockSpec((1, w), lambda i: (0, 0))


def prenorm_fwd(x, gain, shift, scale, *, name):
    s, d = x.shape
    t = _norm_tile(s)

    def body(x_ref, g_ref, sh_ref, sc_ref, h_ref):
        xv = x_ref[...]
        r = lax.rsqrt(_rowmean(xv * xv) + EPS)
        h_ref[...] = ((xv * r) * g_ref[...] * (1.0 + sc_ref[...]) + sh_ref[...]).astype(h_ref.dtype)

    return pl.pallas_call(
        body, grid=(s // t,),
        in_specs=[_row_spec(t, d), _vec_spec(d), _vec_spec(d), _vec_spec(d)],
        out_specs=_row_spec(t, d), out_shape=jax.ShapeDtypeStruct((s, d), BF16),
        compiler_params=_cparams(("arbitrary",)), name=name)(x, gain, shift, scale)


def w_in_dx_prenorm_bwd(dproj, w_in, x, gain, scale, g_res, *, rider=None, name):
    s, n = dproj.shape
    d = w_in.shape[0]
    tm = min(512, s)
    tn = 2048
    steps = n // tn
    nrow = s // tm
    nr = rider.n if rider else 0
    hbm = pl.BlockSpec(memory_space=pl.ANY)

    def body(*refs):
        dp_ref, w_ref, x_ref, g_ref, sc_ref, gr_ref = refs[:6]
        dx_ref, dsh_ref, dsc_ref, dg_ref = refs[6 + nr:10 + nr]
        acc_ref = refs[10 + 2 * nr]
        i, j = pl.program_id(0), pl.program_id(1)
        if rider:
            total = nrow * steps
            rider.emit(i * steps + j, total, _rider_mid_step(total),
                       refs[6:6 + nr], refs[10 + nr:10 + 2 * nr], refs[11 + 2 * nr:])
        prod = _dot_nt(dp_ref[...], w_ref[...])

        @pl.when(j == 0)
        def _():
            acc_ref[...] = prod

        @pl.when(j > 0)
        def _():
            acc_ref[...] += prod

        @pl.when(j == steps - 1)
        def _():
            @pl.when(i == 0)
            def _():
                dsh_ref[...] = jnp.zeros_like(dsh_ref)
                dsc_ref[...] = jnp.zeros_like(dsc_ref)
                dg_ref[...] = jnp.zeros_like(dg_ref)

            gain_v = g_ref[...]
            one_sc = 1.0 + sc_ref[...]
            sub = min(256, tm)

            def piece(k, carry):
                rows = pl.ds(pl.multiple_of(k * sub, sub), sub)
                xv = x_ref[rows, :]
                dhv = acc_ref[rows, :]
                r = lax.rsqrt(_rowmean(xv * xv) + EPS)
                xn = xv * r
                dyn = dhv * one_sc
                dxn = dyn * gain_v
                dx_ref[rows, :] = r * (dxn - xn * _rowmean(dxn * xn)) + gr_ref[rows, :]
                dsh_ref[...] += _colsum(dhv)
                dsc_ref[...] += _colsum(dhv * (xn * gain_v))
                dg_ref[...] += _colsum(dyn * xn)
                return carry

            lax.fori_loop(0, tm // sub, piece, 0)

    rows = pl.BlockSpec((tm, d), lambda i, j: (i, 0))
    vec_in = pl.BlockSpec((1, d), lambda i, j: (0, 0))
    vec = jax.ShapeDtypeStruct((1, d), F32)
    outs = pl.pallas_call(
        body, grid=(nrow, steps),
        in_specs=[pl.BlockSpec((tm, tn), lambda i, j: (i, j)), pl.BlockSpec((d, tn), lambda i, j: (0, j)),
                  rows, vec_in, vec_in, rows] + [hbm] * nr,
        out_specs=[rows, vec_in, vec_in, vec_in] + [hbm] * nr,
        out_shape=[jax.ShapeDtypeStruct((s, d), F32), vec, vec, vec] + (rider.out_shape if rider else []),
        scratch_shapes=[pltpu.VMEM((tm, d), F32)] + (rider.scratch if rider else []),
        compiler_params=_cparams(("arbitrary", "arbitrary")),
        name=name)(dproj, w_in, x, gain, scale, g_res, *(rider.arrs if rider else []))
    return outs


def w_out_postnorm_fwd(merged, w_out, x, gain, gate, *, name):
    s, d = x.shape
    t = _norm_tile(s)

    def body(m_ref, w_ref, x_ref, g_ref, gt_ref, o_ref, y_ref):
        ov = _dot(m_ref[...], w_ref[...])
        o_ref[...] = ov
        r = lax.rsqrt(_rowmean(ov * ov) + EPS)
        y_ref[...] = x_ref[...] + gt_ref[...] * ((ov * r) * g_ref[...])

    shp = jax.ShapeDtypeStruct((s, d), F32)
    return pl.pallas_call(
        body, grid=(s // t,),
        in_specs=[_row_spec(t, d), _full_spec((d, d)), _row_spec(t, d), _vec_spec(d), _vec_spec(d)],
        out_specs=[_row_spec(t, d)] * 2, out_shape=[shp] * 2,
        compiler_params=_cparams(("arbitrary",)), name=name)(merged, w_out, x, gain, gate)


def loss_head(y, target, *, name):
    s, d = y.shape
    t = _norm_tile(s)
    steps = s // t

    def body(y_ref, t_ref, dy_ref, loss_ref, acc_ref):
        i = pl.program_id(0)
        err = y_ref[...] - t_ref[...]
        dy_ref[...] = err * (1.0 / d)
        part = _colsum(err * err)

        @pl.when(i == 0)
        def _():
            acc_ref[...] = part

        @pl.when(i > 0)
        def _():
            acc_ref[...] += part

        @pl.when(i == steps - 1)
        def _():
            loss_ref[...] = jnp.sum(acc_ref[...], axis=1, keepdims=True) * (0.5 / d)

    return pl.pallas_call(
        body, grid=(steps,),
        in_specs=[_row_spec(t, d), _row_spec(t, d)],
        out_specs=[_row_spec(t, d), pl.BlockSpec((1, 1), lambda i: (0, 0))],
        out_shape=[jax.ShapeDtypeStruct((s, d), F32), jax.ShapeDtypeStruct((1, 1), F32)],
        scratch_shapes=[pltpu.VMEM((1, d), F32)],
        compiler_params=_cparams(("arbitrary",)), name=name)(y, target)


def _full_spec(shape):
    return pl.BlockSpec(shape, lambda i: (0,) * len(shape))


def proj_gate_fwd(y_a, y_b, w_pa, w_pb, proj, *, name):
    s, width = y_a.shape
    d = w_pa.shape[1]
    t = _norm_tile(s)

    def body(ya_ref, yb_ref, wa_ref, wb_ref, ga_ref, gb_ref, pa_ref, pb_ref, m_ref):
        pa = _dot(ya_ref[...], wa_ref[...])
        pb = _dot(yb_ref[...], wb_ref[...])
        pa_ref[...] = pa.astype(pa_ref.dtype)
        pb_ref[...] = pb.astype(pb_ref.dtype)
        m_ref[...] = (_sigmoid(ga_ref[...].astype(F32)) * pa
                      + _sigmoid(gb_ref[...].astype(F32)) * pb).astype(m_ref.dtype)

    out = jax.ShapeDtypeStruct((s, d), BF16)
    return pl.pallas_call(
        body, grid=(s // t,),
        in_specs=[_row_spec(t, width), _row_spec(t, width), _full_spec((width, d)), _full_spec((width, d)),
                  _row_spec(t, d, 0), _row_spec(t, d, 1)],
        out_specs=[_row_spec(t, d)] * 3, out_shape=[out] * 3,
        compiler_params=_cparams(("arbitrary",)), name=name)(y_a, y_b, w_pa, w_pb, proj, proj)


def postnorm_gate_bwd(g, out, gain, gate, w_out, proj, pa, pb, *, name):
    s, d = pa.shape
    t = _row_tile(s)

    def body(g_ref, o_ref, gn_ref, gt_ref, w_ref, ga_ref, gb_ref, pa_ref, pb_ref,
             do_ref, dpa_ref, dpb_ref, dp_ref, dgt_ref, dgn_ref):
        i = pl.program_id(0)
        ov = o_ref[...]
        gv = g_ref[...]
        r = lax.rsqrt(_rowmean(ov * ov) + EPS)
        on = ov * r
        gain_v = gn_ref[...]
        dn = gv * gt_ref[...]
        don = dn * gain_v
        dout = (r * (don - on * _rowmean(don * on))).astype(BF16)
        do_ref[...] = dout
        p_gt = _colsum(gv * (on * gain_v))
        p_gn = _colsum(dn * on)

        @pl.when(i == 0)
        def _():
            dgt_ref[...] = p_gt
            dgn_ref[...] = p_gn

        @pl.when(i > 0)
        def _():
            dgt_ref[...] += p_gt
            dgn_ref[...] += p_gn

        dm = _dot_nt(dout, w_ref[...])
        sa = _sigmoid(ga_ref[...].astype(F32))
        sb = _sigmoid(gb_ref[...].astype(F32))
        dpa_ref[...] = (dm * sa).astype(dpa_ref.dtype)
        dpb_ref[...] = (dm * sb).astype(dpb_ref.dtype)
        dp_ref[:, :d] = (dm * pa_ref[...].astype(F32) * sa * (1.0 - sa)).astype(dp_ref.dtype)
        dp_ref[:, d:] = (dm * pb_ref[...].astype(F32) * sb * (1.0 - sb)).astype(dp_ref.dtype)

    act = jax.ShapeDtypeStruct((s, d), BF16)
    vec = jax.ShapeDtypeStruct((1, d), F32)
    return pl.pallas_call(
        body, grid=(s // t,),
        in_specs=[_row_spec(t, d), _row_spec(t, d), _vec_spec(d), _vec_spec(d), _full_spec((d, d)),
                  _row_spec(t, d, 0), _row_spec(t, d, 1), _row_spec(t, d), _row_spec(t, d)],
        out_specs=[_row_spec(t, d), _row_spec(t, d), _row_spec(t, d), _row_spec(t, 2 * d, 0),
                   _vec_spec(d), _vec_spec(d)],
        out_shape=[act, act, act, jax.ShapeDtypeStruct((s, IN_COLS), BF16), vec, vec],
        compiler_params=_cparams(("arbitrary",)), name=name)(g, out, gain, gate, w_out, proj, proj, pa, pb)


def _pool_tile(s):
    return min(512, s)


def pool_fwd(proj, pw, ps, *, name):
    s = proj.shape[0]
    t = _pool_tile(s)
    pool_blk = GATE_COLS // (len(POOL_WINDOWS) * POOL_COLS)

    def body(p_ref, halo_ref, pw_ref, ps_ref, yb_ref, pooled_ref, mixed_ref):
        i = pl.program_id(0)
        halo = jnp.where(i == 0, 0.0, halo_ref[...].astype(F32))
        row = i * t + lax.broadcasted_iota(jnp.int32, (t, 1), 0)
        for g, w in enumerate(POOL_WINDOWS):
            vb = p_ref[:, g * POOL_COLS:g * POOL_COLS + POOL_GW].astype(F32)
            zb = p_ref[:, g * POOL_COLS + POOL_GW:(g + 1) * POOL_COLS].astype(F32)
            acc = jnp.concatenate([halo[:, g * POOL_COLS:g * POOL_COLS + POOL_GW], vb], axis=0)
            sh = 1
            while sh < w:
                acc = acc + pltpu.roll(acc, sh, axis=0)
                sh *= 2
            cnt = jnp.minimum(row + 1, w).astype(F32)
            pooled = acc[HALO:, :] / cnt - vb
            mixed = _dot(pooled.astype(BF16), pw_ref[g])
            cols = slice(g * POOL_GW, (g + 1) * POOL_GW)
            yb = mixed * ps_ref[:, cols] * (zb * _sigmoid(zb))
            yb_ref[:, cols] = yb.astype(yb_ref.dtype)
            pooled_ref[:, cols] = pooled.astype(pooled_ref.dtype)
            mixed_ref[:, cols] = mixed

    wide = len(POOL_WINDOWS) * POOL_COLS
    return pl.pallas_call(
        body, grid=(s // t,),
        in_specs=[pl.BlockSpec((t, wide), lambda i: (i, pool_blk)),
                  pl.BlockSpec((HALO, wide), lambda i: (jnp.maximum(i * (t // HALO) - 1, 0), pool_blk)),
                  pl.BlockSpec((len(POOL_WINDOWS), POOL_GW, POOL_GW), lambda i: (0, 0, 0)),
                  _vec_spec(WIDTH)],
        out_specs=[_row_spec(t, WIDTH)] * 3,
        out_shape=[jax.ShapeDtypeStruct((s, WIDTH), BF16), jax.ShapeDtypeStruct((s, WIDTH), BF16),
                   jax.ShapeDtypeStruct((s, WIDTH), F32)],
        compiler_params=_cparams(("arbitrary",)), name=name)(proj, proj, pw, ps)


def pool_bwd(dyb, proj, pooled, mixed, pw, ps, dproj, *, name):
    s = proj.shape[0]
    t = _pool_tile(s)
    nblk = s // t
    ng = len(POOL_WINDOWS)
    wide = ng * POOL_COLS
    pool_blk = GATE_COLS // wide

    def body(dy_ref, p_ref, pooled_ref, mixed_ref, pw_ref, ps_ref, dp_any, dp_ref, dpw_ref, dps_ref, carry):
        del dp_any
        i = pl.program_id(0)
        ii = nblk - 1 - i

        @pl.when(i == 0)
        def _():
            carry[...] = jnp.zeros_like(carry)
            dpw_ref[...] = jnp.zeros_like(dpw_ref)
            dps_ref[...] = jnp.zeros_like(dps_ref)

        row = ii * t + lax.broadcasted_iota(jnp.int32, (t, 1), 0)
        for g, w in enumerate(POOL_WINDOWS):
            cols = slice(g * POOL_GW, (g + 1) * POOL_GW)
            zb = p_ref[:, g * POOL_COLS + POOL_GW:(g + 1) * POOL_COLS].astype(F32)
            dy = dy_ref[:, cols]
            mx = mixed_ref[:, cols]
            sc = ps_ref[:, cols]
            sg = _sigmoid(zb)
            dzb = dy * (mx * sc) * (sg * (1.0 + zb * (1.0 - sg)))
            dpm = dy * (zb * sg)
            dps_ref[:, cols] += _colsum(dpm * mx)
            dmixed = (dpm * sc).astype(BF16)
            dpooled = _dot_nt(dmixed, pw_ref[g])
            dpw_ref[g] += _dot_tn(pooled_ref[:, cols], dmixed)
            cnt = jnp.minimum(row + 1, w).astype(F32)
            u = dpooled / cnt
            acc = jnp.concatenate([u, carry[:, cols]], axis=0)
            sh = 1
            while sh < w:
                acc = acc + pltpu.roll(acc, t + HALO - sh, axis=0)
                sh *= 2
            carry[:, cols] = u[:HALO, :]
            dp_ref[:, g * POOL_COLS:g * POOL_COLS + POOL_GW] = (acc[:t, :] - dpooled).astype(dp_ref.dtype)
            dp_ref[:, g * POOL_COLS + POOL_GW:(g + 1) * POOL_COLS] = dzb.astype(dp_ref.dtype)

    rev = lambda i: (nblk - 1 - i, 0)
    return pl.pallas_call(
        body, grid=(nblk,),
        in_specs=[pl.BlockSpec((t, WIDTH), rev),
                  pl.BlockSpec((t, wide), lambda i: (nblk - 1 - i, pool_blk)),
                  pl.BlockSpec((t, WIDTH), rev), pl.BlockSpec((t, WIDTH), rev),
                  pl.BlockSpec((ng, POOL_GW, POOL_GW), lambda i: (0, 0, 0)),
                  _vec_spec(WIDTH),
                  pl.BlockSpec(memory_space=pl.ANY)],
        out_specs=[pl.BlockSpec((t, wide), lambda i: (nblk - 1 - i, pool_blk)),
                   pl.BlockSpec((ng, POOL_GW, POOL_GW), lambda i: (0, 0, 0)),
                   _vec_spec(WIDTH)],
        out_shape=[jax.ShapeDtypeStruct(dproj.shape, dproj.dtype),
                   jax.ShapeDtypeStruct((ng, POOL_GW, POOL_GW), F32),
                   jax.ShapeDtypeStruct((1, WIDTH), F32)],
        scratch_shapes=[pltpu.VMEM((HALO, WIDTH), F32)],
        input_output_aliases={6: 0},
        compiler_params=_cparams(("arbitrary",)), name=name)(dyb, proj, pooled, mixed, pw, ps, dproj)


def _hgrn_tile(s):
    return min(1024, s)


def _chunk_consts():
    tt = lax.broadcasted_iota(jnp.int32, (CHUNK, CHUNK), 0)
    ss = lax.broadcasted_iota(jnp.int32, (CHUNK, CHUNK), 1)
    within = (ss <= tt) & (ss // SUB == tt // SUB)
    before = ss < (tt // SUB) * SUB
    cums = jnp.concatenate([within.astype(F32), before.astype(F32)], axis=0).astype(BF16)
    causal = ss <= tt
    upper = (ss >= tt).astype(F32).astype(BF16)
    row = lax.broadcasted_iota(jnp.int32, (CHUNK, 1), 0)
    return cums, causal, upper, row


def _dot_split(mat01, v):
    hi = v.astype(BF16)
    r1 = v - hi.astype(F32)
    mid = r1.astype(BF16)
    lo = (r1 - mid.astype(F32)).astype(BF16)
    return _dot(mat01, hi) + _dot(mat01, mid) + _dot(mat01, lo)


def _hgrn_chunks(qas, fas, lb, cums, row):
    gates = [_hgrn_gates(qa, fa, lb) for qa, fa in zip(qas, fas)]
    cbs = [_dot_split(cums, g["lf"]) for g in gates]
    return [_hgrn_decay(g, cb, row) for g, cb in zip(gates, cbs)]


def _hgrn_gates(qa, fa, lb):
    sq = _sigmoid(qa)
    sa = _sigmoid(fa)
    sna = 1.0 - sa
    oml = 1.0 - lb
    f = lb + oml * sa
    fc = jnp.maximum(f, MIN_FORGET)
    return dict(sq=sq, q=qa * sq, sa=sa, sna=sna, oml=oml, f=f, fc=fc, lf=jnp.log(fc), k=oml * sna)


def _hgrn_decay(g, cb, row):
    sq, q, sa, sna, oml, f, fc, k = (g[n] for n in ("sq", "q", "sa", "sna", "oml", "f", "fc", "k"))
    c = cb[:CHUNK]
    bt = cb[CHUNK:]
    ec = jnp.exp(c)
    enc = jnp.exp(jnp.minimum(-c, MAX_EXP))
    qt = q * ec
    kt = k * enc
    dms, lhs, rhs = [], [], []
    for j in range(N_SUB):
        bj = bt[j * SUB:j * SUB + 1, :]
        dm = jnp.where(row >= j * SUB, jnp.exp(jnp.minimum(bt - bj, 0.0)), 0.0)
        dms.append(dm)
        lhs.append(qt * dm)
        rhs.append(jnp.where(row // SUB == j, kt, 0.0))
    lhs = jnp.concatenate(lhs, axis=1).astype(BF16)
    rhs = jnp.concatenate(rhs, axis=1).astype(BF16)
    b = bt + c
    bl = b[CHUNK - 1:CHUNK, :]
    ebl = jnp.exp(bl)
    edec = jnp.exp(bl - b)
    eb = ec * dms[0]
    return dict(sq=sq, q=q, sa=sa, sna=sna, oml=oml, f=f, fc=fc, k=k, ec=ec, enc=enc, dms=dms,
                lhs=lhs, rhs=rhs, ebl=ebl, edec=edec, eb=eb, qd=q * eb, kdec=k * edec)


def _rider_mid_step(total):
    return total - max(1, total // 8)


def hgrn_fwd(proj, lb, hn, *, rider=None, name):
    s = proj.shape[0]
    t = _hgrn_tile(s)
    nblk = s // t
    ncht = t // CHUNK
    head_blk0 = GP_COLS // HEAD_COLS
    nr = rider.n if rider else 0
    hbm = pl.BlockSpec(memory_space=pl.ANY)

    def body(*refs):
        p_ref, lb_ref, hn_ref = refs[:3]
        ya_ref, o_ref, st_ref = refs[3 + nr:6 + nr]
        state = refs[6 + 2 * nr]
        i = pl.program_id(1)
        if rider:
            total = HEADS * nblk
            rider.emit(pl.program_id(0) * nblk + i, total, _rider_mid_step(total),
                       refs[3:3 + nr], refs[6 + nr:6 + 2 * nr], refs[7 + 2 * nr:])

        @pl.when(i == 0)
        def _():
            state[...] = jnp.zeros_like(state)

        cums, causal, _, row = _chunk_consts()
        lbv = lb_ref[...]
        hnv = hn_ref[...]

        rows = [slice(ci * CHUNK, (ci + 1) * CHUNK) for ci in range(ncht)]
        pres = _hgrn_chunks([p_ref[r, 0:HEAD_DIM] for r in rows], [p_ref[r, HEAD_DIM:2 * HEAD_DIM] for r in rows],
                            lbv, cums, row)
        vas = [p_ref[r, 2 * HEAD_DIM:3 * HEAD_DIM].astype(BF16) for r in rows]
        scores = [jnp.where(causal, _dot_nt(pre["lhs"], pre["rhs"]), 0.0).astype(BF16) for pre in pres]
        intra = [_dot(a, va) for a, va in zip(scores, vas)]
        qds = [pre["qd"].astype(BF16) for pre in pres]
        kdecs = [pre["kdec"].astype(BF16) for pre in pres]
        st = state[...]
        outs = []
        for ci in range(ncht):
            stb = st.astype(BF16)
            st_ref[ci, 0] = stb
            outs.append(intra[ci] + _dot_nt(qds[ci], stb))
            st = st * pres[ci]["ebl"] + _dot_tn(vas[ci], kdecs[ci])
        state[...] = st
        for r, o in zip(rows, outs):
            za = p_ref[r, 3 * HEAD_DIM:4 * HEAD_DIM]
            o_ref[r, :] = o
            ya_ref[r, :] = ((o * lax.rsqrt(_rowmean(o * o) + EPS)) * hnv * (za * _sigmoid(za))).astype(ya_ref.dtype)

    return pl.pallas_call(
        body, grid=(HEADS, nblk),
        in_specs=[pl.BlockSpec((t, HEAD_COLS), lambda h, i: (i, h)),
                  pl.BlockSpec((1, HEAD_DIM), lambda h, i: (0, h)),
                  pl.BlockSpec((1, HEAD_DIM), lambda h, i: (0, h))] + [hbm] * nr,
        out_specs=[pl.BlockSpec((t, HEAD_DIM), lambda h, i: (i, h)),
                   pl.BlockSpec((t, HEAD_DIM), lambda h, i: (i, h)),
                   pl.BlockSpec((ncht, 1, HEAD_DIM, HEAD_DIM), lambda h, i: (i, h, 0, 0))] + [hbm] * nr,
        out_shape=[jax.ShapeDtypeStruct((s, WIDTH), BF16), jax.ShapeDtypeStruct((s, WIDTH), F32),
                   jax.ShapeDtypeStruct((s // CHUNK, HEADS, HEAD_DIM, HEAD_DIM), BF16)]
        + (rider.out_shape if rider else []),
        scratch_shapes=[pltpu.VMEM((HEAD_DIM, HEAD_DIM), F32)] + (rider.scratch if rider else []),
        compiler_params=_cparams(("arbitrary", "arbitrary")), name=name)(proj, lb, hn, *(rider.arrs if rider else []))


def hgrn_bwd(dya, proj, o_all, states, lb, hn, dproj, *, rider=None, name):
    s = proj.shape[0]
    t = _hgrn_tile(s)
    nblk = s // t
    ncht = t // CHUNK
    head_blk0 = GP_COLS // HEAD_COLS
    nr = rider.n if rider else 0
    hbm = pl.BlockSpec(memory_space=pl.ANY)

    def body(*refs):
        dy_ref, p_ref, o_ref, st_ref, lb_ref, hn_ref = refs[:6]
        dp_ref, dhn_ref, dlb_ref = refs[7 + nr:10 + nr]
        dstate = refs[10 + 2 * nr]
        i = pl.program_id(1)
        if rider:
            total = HEADS * nblk
            rider.emit(pl.program_id(0) * nblk + i, total, _rider_mid_step(total),
                       refs[7:7 + nr], refs[10 + nr:10 + 2 * nr], refs[11 + 2 * nr:])

        @pl.when(i == 0)
        def _():
            dstate[...] = jnp.zeros_like(dstate)
            dhn_ref[...] = jnp.zeros_like(dhn_ref)
            dlb_ref[...] = jnp.zeros_like(dlb_ref)

        cums, causal, upper, row = _chunk_consts()
        lbv = lb_ref[...]
        hnv = hn_ref[...]

        chunks = range(ncht)
        rows = [slice(ci * CHUNK, (ci + 1) * CHUNK) for ci in chunks]
        qas = [p_ref[r, 0:HEAD_DIM] for r in rows]
        vbs = [p_ref[r, 2 * HEAD_DIM:3 * HEAD_DIM].astype(BF16) for r in rows]
        st0s = [st_ref[ci, 0] for ci in chunks]
        dzas, dobs = [], []
        dhn_acc = jnp.zeros_like(hnv)
        for r in rows:
            za = p_ref[r, 3 * HEAD_DIM:4 * HEAD_DIM]
            o = o_ref[r, :]
            dy = dy_ref[r, :]
            rn = lax.rsqrt(_rowmean(o * o) + EPS)
            on = o * rn
            sgz = _sigmoid(za)
            sz = za * sgz
            dzas.append(dy * on * hnv * (sgz * (1.0 + za * (1.0 - sgz))))
            dhn_acc = dhn_acc + _colsum(dy * on * sz)
            don = dy * hnv * sz
            dobs.append((rn * (don - on * _rowmean(don * on))).astype(BF16))
        pres = _hgrn_chunks(qas, [p_ref[r, HEAD_DIM:2 * HEAD_DIM] for r in rows], lbv, cums, row)
        scores = [jnp.where(causal, _dot_nt(pre["lhs"], pre["rhs"]), 0.0).astype(BF16) for pre in pres]
        das = [jnp.where(causal, _dot_nt(dob, vb), 0.0).astype(BF16) for dob, vb in zip(dobs, vbs)]
        dlhss = [_dot(da, pre["rhs"]) for da, pre in zip(das, pres)]
        drhss = [_dot_tn(da, pre["lhs"]) for da, pre in zip(das, pres)]
        dv_intra = [_dot_tn(a, dob) for a, dob in zip(scores, dobs)]
        dq_inter = [_dot(dob, st0) * pre["eb"] for dob, st0, pre in zip(dobs, st0s, pres)]
        qds = [pre["qd"].astype(BF16) for pre in pres]
        kdecs = [pre["kdec"].astype(BF16) for pre in pres]
        dst1 = dstate[...]
        dvs, dk_states, dbl_states = [None] * ncht, [None] * ncht, [None] * ncht
        for ci in reversed(chunks):
            dst1b = dst1.astype(BF16)
            dvs[ci] = dv_intra[ci] + _dot_nt(kdecs[ci], dst1b)
            dk_states[ci] = _dot(vbs[ci], dst1b) * pres[ci]["edec"]
            dbl_states[ci] = pres[ci]["ebl"] * _colsum(dst1 * st0s[ci].astype(F32))
            dst1 = dst1 * pres[ci]["ebl"] + _dot_tn(dobs[ci], qds[ci])
        dstate[...] = dst1
        dqs, dks, dbs, dbls = [], [], [], []
        for ci in chunks:
            pre = pres[ci]
            q, k = pre["q"], pre["k"]
            dq_a = jnp.zeros_like(q)
            dk_a = jnp.zeros_like(k)
            db = q * dq_inter[ci] - k * dk_states[ci]
            for j in range(N_SUB):
                cols = slice(j * HEAD_DIM, (j + 1) * HEAD_DIM)
                dlhs, drhs = dlhss[ci][:, cols], drhss[ci][:, cols]
                dq_a = dq_a + pre["dms"][j] * dlhs
                dk_a = dk_a + jnp.where(row // SUB == j, drhs, 0.0)
                db = db + (pre["lhs"][:, cols].astype(F32) * dlhs - pre["rhs"][:, cols].astype(F32) * drhs)
            dqs.append(dq_inter[ci] + pre["ec"] * dq_a)
            dks.append(dk_states[ci] + pre["enc"] * dk_a)
            dbs.append(db)
            dbls.append(_colsum(k * dk_states[ci]) + dbl_states[ci])
        dlfs = [_dot_split(upper, db) + dbl for db, dbl in zip(dbs, dbls)]
        dlb_acc = jnp.zeros_like(lbv)
        for ci in chunks:
            pre = pres[ci]
            sq, sa, sna, oml = pre["sq"], pre["sa"], pre["sna"], pre["oml"]
            dqa = dqs[ci] * (sq * (1.0 + qas[ci] * (1.0 - sq)))
            diff = jnp.where(pre["f"] >= MIN_FORGET, dlfs[ci] / pre["fc"], 0.0) - dks[ci]
            dlb_acc = dlb_acc + _colsum(diff * sna)
            dfa = diff * (oml * sa * sna)
            dp_ref[rows[ci], :] = jnp.concatenate([dqa, dfa, dvs[ci], dzas[ci]], axis=1).astype(dp_ref.dtype)
        dhn_ref[...] += dhn_acc
        dlb_ref[...] += dlb_acc

    rev = lambda h, i: (nblk - 1 - i, h)
    return pl.pallas_call(
        body, grid=(HEADS, nblk),
        in_specs=[pl.BlockSpec((t, HEAD_DIM), rev),
                  pl.BlockSpec((t, HEAD_COLS), lambda h, i: (nblk - 1 - i, h)),
                  pl.BlockSpec((t, HEAD_DIM), rev),
                  pl.BlockSpec((ncht, 1, HEAD_DIM, HEAD_DIM), lambda h, i: (nblk - 1 - i, h, 0, 0)),
                  pl.BlockSpec((1, HEAD_DIM), lambda h, i: (0, h)),
                  pl.BlockSpec((1, HEAD_DIM), lambda h, i: (0, h)),
                  hbm] + [hbm] * nr,
        out_specs=[pl.BlockSpec((t, HEAD_COLS), lambda h, i: (nblk - 1 - i, head_blk0 + h)),
                   pl.BlockSpec((1, HEAD_DIM), lambda h, i: (0, h)),
                   pl.BlockSpec((1, HEAD_DIM), lambda h, i: (0, h))] + [hbm] * nr,
        out_shape=[jax.ShapeDtypeStruct(dproj.shape, dproj.dtype),
                   jax.ShapeDtypeStruct((1, WIDTH), F32), jax.ShapeDtypeStruct((1, WIDTH), F32)]
        + (rider.out_shape if rider else []),
        scratch_shapes=[pltpu.VMEM((HEAD_DIM, HEAD_DIM), F32)] + (rider.scratch if rider else []),
        input_output_aliases={6: 0},
        compiler_params=_cparams(("arbitrary", "arbitrary")),
        name=name)(dya, proj, o_all, states, lb, hn, dproj, *(rider.arrs if rider else []))


def _softmax_rows(lower):
    mx = jnp.max(lower, axis=0, keepdims=True)
    e = jnp.exp(lower - mx)
    return e / jnp.sum(e, axis=0, keepdims=True)


def lb_table(lower, *, name):
    depth, w = lower.shape

    def body(l_ref, o_ref):
        sm = _softmax_rows(l_ref[...])
        acc = jnp.zeros((1, w), F32)
        o_ref[0:1, :] = acc
        for l in range(1, depth):
            acc = acc + sm[l:l + 1, :]
            o_ref[l:l + 1, :] = acc

    return pl.pallas_call(body, out_shape=jax.ShapeDtypeStruct((depth, w), F32), name=name)(lower)


def lb_table_bwd(lower, dlb, *, name):
    depth, w = lower.shape

    def body(l_ref, d_ref, o_ref):
        sm = _softmax_rows(l_ref[...])
        dlbv = d_ref[...]
        dsm = [jnp.zeros((1, w), F32)]
        for i in range(1, depth):
            acc = jnp.zeros((1, w), F32)
            for l in range(i, depth):
                acc = acc + dlbv[l:l + 1, :]
            dsm.append(acc)
        inner = jnp.zeros((1, w), F32)
        for i in range(depth):
            inner = inner + sm[i:i + 1, :] * dsm[i]
        for i in range(depth):
            o_ref[i:i + 1, :] = sm[i:i + 1, :] * (dsm[i] - inner)

    return pl.pallas_call(body, out_shape=jax.ShapeDtypeStruct((depth, w), F32), name=name)(lower, dlb)


def w_ada_grad(c_all, dmod_cols, *, name):
    depth, _, cols = dmod_cols.shape
    d = c_all.shape[1]

    def body(c_ref, dm_ref, o_ref):
        cv = c_ref[...]
        ca = cv * _sigmoid(cv)
        o_ref[...] = _dot_tn(ca, dm_ref[...])

    return pl.pallas_call(
        body, grid=(depth,),
        in_specs=[pl.BlockSpec((N_DEV, d), lambda l: (0, 0)), pl.BlockSpec((None, N_DEV, cols), lambda l: (l, 0, 0))],
        out_specs=pl.BlockSpec((None, d, cols), lambda l: (l, 0, 0)),
        out_shape=jax.ShapeDtypeStruct((depth, d, cols), F32),
        compiler_params=_cparams(("arbitrary",)), name=name)(c_all, dmod_cols)


def sum_parts(parts, *, name):
    p, r, c = parts.shape

    def body(p_ref, o_ref):
        acc = p_ref[0]
        for j in range(1, p):
            acc = acc + p_ref[j]
        o_ref[...] = acc

    return pl.pallas_call(body, out_shape=jax.ShapeDtypeStruct((r, c), F32), name=name)(parts)


def _adam_rows(r, c):
    tr = r
    while tr * c * 4 > (1 << 20) and tr % 16 == 0:
        tr //= 2
    return tr


def _adam_update(w_ref, m_ref, v_ref, g_ref, go_ref, d_ref, mo_ref, vo_ref):
    g = g_ref[0].astype(F32)
    for j in range(1, g_ref.shape[0]):
        g = g + g_ref[j].astype(F32)
    mn = ADAM_B1 * m_ref[...] + (1.0 - ADAM_B1) * g
    vn = ADAM_B2 * v_ref[...] + (1.0 - ADAM_B2) * (g * g)
    m_hat = mn / (1.0 - ADAM_B1 ** ADAM_STEP)
    v_hat = vn / (1.0 - ADAM_B2 ** ADAM_STEP)
    go_ref[...] = g
    d_ref[...] = -ADAM_LR * (m_hat / (jnp.sqrt(v_hat) + ADAM_EPS) + ADAM_WD * w_ref[...])
    mo_ref[...] = mn
    vo_ref[...] = vn


def adamw(w, m, v, gparts, *, name):
    r, c = w.shape
    p = gparts.shape[0]
    tr = _adam_rows(r, c)
    spec = pl.BlockSpec((tr, c), lambda i: (i, 0))
    shp = jax.ShapeDtypeStruct((r, c), F32)
    return pl.pallas_call(
        functools.partial(_adam_update), grid=(r // tr,),
        in_specs=[spec, spec, spec, pl.BlockSpec((p, tr, c), lambda i: (0, i, 0))],
        out_specs=[spec] * 4, out_shape=[shp] * 4,
        compiler_params=_cparams(("arbitrary",)), name=name)(w, m, v, gparts)


def adamw_layers(w, m, v, gparts, *, name):
    depth, r, c = w.shape
    p = gparts[0].shape[0]
    tr = _adam_rows(r, c)

    def body(w_ref, m_ref, v_ref, *rest):
        g_refs, outs = rest[:depth], rest[depth:]
        layer = pl.program_id(0)
        for k in range(depth):
            @pl.when(layer == k)
            def _(k=k):
                _adam_update(w_ref, m_ref, v_ref, g_refs[k], *outs)

    spec = pl.BlockSpec((None, tr, c), lambda l, i: (l, i, 0))
    g_specs = [pl.BlockSpec((p, tr, c), functools.partial(lambda l, i, k: (0, jnp.where(l == k, i, 0), 0), k=k))
               for k in range(depth)]
    shp = jax.ShapeDtypeStruct((depth, r, c), F32)
    return pl.pallas_call(
        body, grid=(depth, r // tr),
        in_specs=[spec, spec, spec] + g_specs,
        out_specs=[spec] * 4, out_shape=[shp] * 4,
        compiler_params=_cparams(("arbitrary", "arbitrary")), name=name)(w, m, v, *gparts)


def _position():
    x, y, c = lax.axis_index("x"), lax.axis_index("y"), lax.axis_index("c")
    return x, y, c


def _dev_index(x, y, c):
    return 4 * x + 2 * y + c


def _gather_phases(ins, outs, send_sems, recv_sems, local_sems):
    n = len(ins)
    x, y, c = _position()
    me, sibling = (x, y, c), (x, y, 1 - c)
    chips = [(1 - x, y), (x, 1 - y), (1 - x, 1 - y)]

    def copy(a, k, block, to, own=False):
        slot = outs[a].at[_dev_index(*block)]
        return pltpu.make_async_remote_copy(
            src_ref=ins[a] if own else slot, dst_ref=slot,
            send_sem=send_sems.at[a * 7 + k], recv_sem=recv_sems.at[a * 7 + k],
            device_id=to, device_id_type=MESH)

    def mine(a):
        return pltpu.make_async_copy(ins[a], outs[a].at[_dev_index(*me)], local_sems.at[a])

    def first(a):
        return [copy(a, 0, me, sibling, True)] + [copy(a, 1 + j, me, (*chip, c), True) for j, chip in enumerate(chips)]

    def passed(a):
        return [copy(a, 4 + j, (*chip, c), sibling) for j, chip in enumerate(chips)]

    def start():
        for a in range(n):
            mine(a).start()
        for a in range(n):
            for cp in first(a):
                cp.start()

    def mid():
        for j, chip in enumerate(chips):
            for a in range(n):
                copy(a, 1 + j, (*chip, c), me).wait_recv()
                passed(a)[j].start()

    def finish():
        for a in range(n):
            copy(a, 0, sibling, me).wait_recv()
            for j, chip in enumerate(chips):
                copy(a, 4 + j, (*chip, 1 - c), me).wait_recv()
        for a in range(n):
            for cp in first(a) + passed(a):
                cp.wait_send()
            mine(a).wait()

    return start, mid, finish


def _scatter_phases(ins, outs, send_sems, recv_sems, local_sems):
    n = len(ins)
    x, y, c = _position()
    me = _dev_index(x, y, c)

    def peer(r):
        return (x ^ (r >> 2), y ^ ((r >> 1) & 1), c ^ (r & 1))

    def copy(a, r):
        to = peer(r)
        return pltpu.make_async_remote_copy(
            src_ref=ins[a].at[_dev_index(*to)], dst_ref=outs[a].at[me],
            send_sem=send_sems.at[a * 7 + r - 1], recv_sem=recv_sems.at[a * 7 + r - 1],
            device_id=to, device_id_type=MESH)

    def arrival(a, r):
        return pltpu.make_async_remote_copy(
            src_ref=ins[a].at[me], dst_ref=outs[a].at[_dev_index(*peer(r))],
            send_sem=send_sems.at[a * 7 + r - 1], recv_sem=recv_sems.at[a * 7 + r - 1],
            device_id=peer(r), device_id_type=MESH)

    def mine(a):
        return pltpu.make_async_copy(ins[a].at[me], outs[a].at[me], local_sems.at[a])

    def start():
        for a in range(n):
            mine(a).start()
        for r in range(1, N_DEV):
            for a in range(n):
                copy(a, r).start()

    def finish():
        for r in range(1, N_DEV):
            for a in range(n):
                arrival(a, r).wait_recv()
        for r in range(1, N_DEV):
            for a in range(n):
                copy(a, r).wait_send()
        for a in range(n):
            mine(a).wait()

    return start, None, finish


class Rider:
    def __init__(self, kind, arrs):
        self.kind, self.arrs, self.n = kind, list(arrs), len(arrs)
        lead = (N_DEV,) if kind == "gather" else ()
        self.out_shape = [jax.ShapeDtypeStruct(lead + a.shape, a.dtype) for a in self.arrs]
        self.scratch = [pltpu.SemaphoreType.DMA((7 * self.n,)), pltpu.SemaphoreType.DMA((7 * self.n,)),
                        pltpu.SemaphoreType.DMA((self.n,))]

    def phases(self, ins, outs, sems):
        make = _gather_phases if self.kind == "gather" else _scatter_phases
        return make(ins, outs, *sems)

    def emit(self, step, total, mid_step, ins, outs, sems):
        start, mid, finish = self.phases(ins, outs, sems)
        pl.when(step == 0)(start)
        if mid is not None:
            pl.when(step == mid_step)(mid)
        pl.when(step == total - 1)(finish)


def _standalone(rider, name):
    n = rider.n
    hbm = pl.BlockSpec(memory_space=pl.ANY)

    def body(*refs):
        start, mid, finish = rider.phases(refs[:n], refs[n:2 * n], refs[2 * n:])
        start()
        if mid is not None:
            mid()
        finish()

    return pl.pallas_call(body, out_shape=rider.out_shape, in_specs=[hbm] * n, out_specs=[hbm] * n,
                          scratch_shapes=rider.scratch, name=name)(*rider.arrs)


def all_gather(arrs, *, name):
    return _standalone(Rider("gather", arrs), name)


def scatter_parts(arrs, *, name):
    return _standalone(Rider("scatter", arrs), name)


def mod_exchange(c_all, w_ada, b_cols, *, name):
    depth, d, cols = w_ada.shape
    hbm = pl.BlockSpec(memory_space=pl.ANY)
    vmem = pl.BlockSpec(memory_space=pltpu.VMEM)

    def body(c_ref, w_ref, b_ref, out_ref, wbuf, sendbuf, send_sems, recv_sems, load_sem):
        x, y, c = _position()
        me = _dev_index(x, y, c)
        cv = c_ref[...]
        ca = cv * _sigmoid(cv)
        for l in range(depth):
            load = pltpu.make_async_copy(w_ref.at[l], wbuf, load_sem)
            load.start()
            load.wait()
            part = jnp.dot(ca, wbuf[...], preferred_element_type=F32,
                           precision=lax.Precision.HIGHEST) + b_ref[l:l + 1, :]
            for bi in range(N_DEV):
                sendbuf[bi, l:l + 1, :] = part[bi:bi + 1, :]

        def peer(r):
            return (x ^ (r >> 2), y ^ ((r >> 1) & 1), c ^ (r & 1))

        def copy(r):
            to = peer(r)
            return pltpu.make_async_remote_copy(
                src_ref=sendbuf.at[_dev_index(*to)], dst_ref=out_ref.at[me],
                send_sem=send_sems.at[r - 1], recv_sem=recv_sems.at[r - 1],
                device_id=to, device_id_type=MESH)

        def arrival(r):
            return pltpu.make_async_remote_copy(
                src_ref=sendbuf.at[me], dst_ref=out_ref.at[_dev_index(*peer(r))],
                send_sem=send_sems.at[r - 1], recv_sem=recv_sems.at[r - 1],
                device_id=peer(r), device_id_type=MESH)

        out_ref[me] = sendbuf[me]
        sends = [copy(r) for r in range(1, N_DEV)]
        for cp in sends:
            cp.start()
        for r in range(1, N_DEV):
            arrival(r).wait_recv()
        for cp in sends:
            cp.wait_send()

    return pl.pallas_call(
        body,
        out_shape=jax.ShapeDtypeStruct((N_DEV, depth, cols), F32),
        in_specs=[vmem, hbm, vmem], out_specs=vmem,
        scratch_shapes=[pltpu.VMEM((d, cols), F32), pltpu.VMEM((N_DEV, depth, cols), F32),
                        pltpu.SemaphoreType.DMA((7,)), pltpu.SemaphoreType.DMA((7,)), pltpu.SemaphoreType.DMA],
        compiler_params=pltpu.CompilerParams(vmem_limit_bytes=VMEM_LIMIT),
        name=name)(c_all, w_ada, b_cols)


def kernel(x, c, w_ada, b_ada, norm_pre, norm_post, w_in, lower_bounds, hgrn_norm, pool_w, pool_scale, w_proj_a, w_proj_b, w_out, loss_target, m_w_ada, m_b_ada, m_norm_pre, m_norm_post, m_w_in, m_lower_bounds, m_hgrn_norm, m_pool_w, m_pool_scale, m_w_proj_a, m_w_proj_b, m_w_out, v_w_ada, v_b_ada, v_norm_pre, v_norm_post, v_w_in, v_lower_bounds, v_hgrn_norm, v_pool_w, v_pool_scale, v_w_proj_a, v_w_proj_b, v_w_out):
    depth = w_in.shape[0]
    d = D_MODEL
    ada_cols = w_ada.shape[2]
    xi, yi, ci = _position()
    me = _dev_index(xi, yi, ci)
    xs = x[0]
    target = loss_target[0]
    ng = len(POOL_WINDOWS)

    def shards(l):
        w_in_l = w_in[l].astype(BF16)
        half = w_in_l.shape[1] // 2
        return [w_in_l[:, :half], w_in_l[:, half:], w_proj_a[l].astype(BF16), w_proj_b[l].astype(BF16),
                w_out[l].astype(BF16), pool_w[l].astype(BF16)]

    def other_weights(g_pa, g_pb, g_out, g_pool):
        return dict(
            pa=jnp.transpose(g_pa, (1, 0, 2)).reshape(WIDTH, d),
            pb=jnp.transpose(g_pb, (1, 0, 2)).reshape(WIDTH, d),
            w_out=g_out.reshape(d, d),
            pool=jnp.transpose(g_pool, (1, 0, 2, 3)).reshape(ng, POOL_GW, POOL_GW))

    w_in_full = [permute_w_in(*all_gather(shards(0)[:2], name="gather_w_in"), name="permute_w_in")]
    others_full = []
    gathered = []

    (c_all,) = all_gather([c], name="gather_c")
    c_all = c_all.reshape(N_DEV, d)
    b_cols = lax.dynamic_slice_in_dim(b_ada, me * ada_cols, ada_cols, axis=1)
    mod_parts = mod_exchange(c_all, w_ada, b_cols, name="mod_exchange")
    mod = jnp.transpose(mod_parts, (1, 0, 2)).reshape(depth, 3 * d)
    lb_all = lb_table(lower_bounds, name="lb_table")

    saved = []
    cur = xs
    for l in range(depth):
        shift, scale, gate = mod[l:l + 1, :d], mod[l:l + 1, d:2 * d], mod[l:l + 1, 2 * d:]
        h = prenorm_fwd(cur, norm_pre[l:l + 1], shift, scale, name="prenorm_fwd")
        head_cols = (GP_COLS, IN_COLS - GP_COLS)
        if l + 1 < depth:
            proj_gp, nxt_lo = matmul_nn(h, w_in_full[l], cols=(0, GP_COLS), tm=2048, tn=1024, out_dtype=BF16,
                                        rider=Rider("gather", shards(l + 1)[:1]), name="mm_w_in_gp_gather")
            proj_h, nxt_hi = matmul_nn(h, w_in_full[l], cols=head_cols, tm=2048, tn=1024, out_dtype=F32,
                                       rider=Rider("gather", shards(l + 1)[1:2]), name="mm_w_in_heads_gather")
            w_in_full.append(permute_w_in(nxt_lo, nxt_hi, name="permute_w_in"))
        else:
            proj_gp = matmul_nn(h, w_in_full[l], cols=(0, GP_COLS), tm=2048, tn=1024, out_dtype=BF16,
                                name="mm_w_in_gp")
            proj_h = matmul_nn(h, w_in_full[l], cols=head_cols, tm=2048, tn=1024, out_dtype=F32,
                               name="mm_w_in_heads")
        ride = (shards(0)[2:] if l == 0 else []) + (shards(l + 1)[2:] if l + 1 < depth else [])
        if ride:
            y_a, o_all, states, *got = hgrn_fwd(proj_h, lb_all[l:l + 1], hgrn_norm[l:l + 1],
                                                 rider=Rider("gather", ride), name="hgrn_fwd_gather%d" % len(ride))
            for k in range(0, len(got), 4):
                others_full.append(other_weights(*got[k:k + 4]))
        else:
            y_a, o_all, states = hgrn_fwd(proj_h, lb_all[l:l + 1], hgrn_norm[l:l + 1], name="hgrn_fwd")
        w = dict(w_in=w_in_full[l], **others_full[l])
        gathered.append(w)
        y_b, pooled, mixed = pool_fwd(proj_gp, w["pool"], pool_scale[l:l + 1], name="pool_fwd")
        pa, pb, merged = proj_gate_fwd(y_a, y_b, w["pa"], w["pb"], proj_gp, name="proj_gate_fwd")
        out, nxt = w_out_postnorm_fwd(merged, w["w_out"], cur, norm_post[l:l + 1], gate, name="w_out_postnorm_fwd")
        saved.append(dict(x=cur, h=h, proj_gp=proj_gp, proj_h=proj_h, y_a=y_a, o=o_all, states=states, y_b=y_b,
                          pooled=pooled, mixed=mixed, pa=pa, pb=pb, merged=merged, out=out, scale=scale, gate=gate))
        cur = nxt

    g, loss_part = loss_head(cur, target, name="loss_head")
    loss = lax.psum(loss_part[0, 0], ("x", "y", "c"))

    small = [None] * depth
    big_in = [None] * depth
    big_others = [None] * depth
    pend_in, pend_others = None, None
    for l in reversed(range(depth)):
        w, sv = gathered[l], saved[l]
        dout, dpa, dpb, dproj, dgate, dnpost = postnorm_gate_bwd(
            g, sv["out"], norm_post[l:l + 1], sv["gate"], w["w_out"], sv["proj_gp"], sv["pa"], sv["pb"],
            name="postnorm_gate_bwd")
        dw_out = matmul_tn(sv["merged"], dout, tm=2048, tn=1024, out_dtype=BF16, name="mm_w_out_dw")
        dya = matmul_nt(dpa, w["pa"], tm=1024, tn=2048, out_dtype=F32, name="mm_proj_a_dx")
        dyb = matmul_nt(dpb, w["pb"], tm=1024, tn=2048, out_dtype=F32, name="mm_proj_b_dx")
        dw_pa = matmul_tn(sv["y_a"], dpa, tm=2048, tn=2048, out_dtype=BF16, name="mm_proj_a_dw")
        dw_pb = matmul_tn(sv["y_b"], dpb, tm=2048, tn=2048, out_dtype=BF16, name="mm_proj_b_dw")
        dproj, dpool_w, dpool_scale = pool_bwd(dyb, sv["proj_gp"], sv["pooled"], sv["mixed"], w["pool"],
                                               pool_scale[l:l + 1], dproj, name="pool_bwd")
        by_owner = lambda t: jnp.transpose(t.reshape(WIDTH, N_DEV, d // N_DEV), (1, 0, 2))
        others = [by_owner(dw_pa), by_owner(dw_pb), dw_out.reshape(N_DEV, d // N_DEV, d),
                  jnp.transpose(dpool_w.astype(BF16).reshape(ng, N_DEV, POOL_GW // N_DEV, POOL_GW), (1, 0, 2, 3))]
        ride = (pend_others or []) + (others if l == 0 else [])
        if ride:
            dproj, dhn, dlb, *recv = hgrn_bwd(dya, sv["proj_h"], sv["o"], sv["states"], lb_all[l:l + 1],
                                              hgrn_norm[l:l + 1], dproj, rider=Rider("scatter", ride),
                                              name="hgrn_bwd_scatter%d" % len(ride))
            if pend_others:
                big_others[l + 1], recv = recv[:len(pend_others)], recv[len(pend_others):]
            if l == 0:
                big_others[0] = recv
        else:
            dproj, dhn, dlb = hgrn_bwd(dya, sv["proj_h"], sv["o"], sv["states"], lb_all[l:l + 1],
                                       hgrn_norm[l:l + 1], dproj, name="hgrn_bwd")
        above = Rider("scatter", [pend_in]) if pend_in is not None else None
        dx_args = (dproj, w["w_in"], sv["x"], norm_pre[l:l + 1], sv["scale"], g)
        if l > 0:
            if above:
                g, dshift, dscale, dnpre, big_in[l + 1] = w_in_dx_prenorm_bwd(*dx_args, rider=above,
                                                                              name="w_in_dx_prenorm_bwd_scatter")
            else:
                g, dshift, dscale, dnpre = w_in_dx_prenorm_bwd(*dx_args, name="w_in_dx_prenorm_bwd")
            dw_in = matmul_tn(sv["h"], dproj, tm=2048, tn=1024, out_dtype=BF16, name="mm_w_in_dw")
            pend_in, pend_others = unpermute_w_in(dw_in, name="unpermute_w_in"), others
        else:
            if above:
                dw_in, big_in[1] = matmul_tn(sv["h"], dproj, tm=2048, tn=1024, out_dtype=BF16, rider=above,
                                             name="mm_w_in_dw_scatter")
            else:
                dw_in = matmul_tn(sv["h"], dproj, tm=2048, tn=1024, out_dtype=BF16, name="mm_w_in_dw")
            g, dshift, dscale, dnpre, big_in[0] = w_in_dx_prenorm_bwd(
                *dx_args, rider=Rider("scatter", [unpermute_w_in(dw_in, name="unpermute_w_in")]),
                name="w_in_dx_prenorm_bwd_scatter")
        small[l] = jnp.concatenate([dshift, dscale, dgate, dnpre, dnpost, dlb, dhn, dpool_scale], axis=1)
    grad_x = g[None]
    big = [[big_in[l]] + list(big_others[l]) for l in range(depth)]

    small_mine = jnp.concatenate(small, axis=0)
    (small_all,) = all_gather([small_mine], name="gather_small")
    small_sum = sum_parts(small_all, name="sum_small")
    dmod_all = small_all[:, :, :3 * d]
    dmod_cols = jnp.transpose(lax.dynamic_slice_in_dim(dmod_all, me * ada_cols, ada_cols, axis=2), (1, 0, 2))
    g_w_ada = w_ada_grad(c_all, dmod_cols, name="w_ada_grad")
    off = 3 * d
    g_b_ada = small_sum[:, :off]
    g_npre = small_sum[:, off:off + d]
    g_npost = small_sum[:, off + d:off + 2 * d]
    g_lb_tab = small_sum[:, off + 2 * d:off + 2 * d + WIDTH]
    g_hn = small_sum[:, off + 2 * d + WIDTH:off + 2 * d + 2 * WIDTH]
    g_ps = small_sum[:, off + 2 * d + 2 * WIDTH:]
    g_lower = lb_table_bwd(lower_bounds, g_lb_tab, name="lb_table_bwd")

    def update(wt, mt, vt, gparts, shape2, name):
        outs = adamw(wt.reshape(shape2), mt.reshape(shape2), vt.reshape(shape2), gparts, name=name)
        return [o.reshape(wt.shape) for o in outs]

    def update_layers(wt, mt, vt, kind, name):
        shape3 = (depth, -1, wt.shape[-1])
        w3 = wt.reshape(shape3)
        gps = [big[l][kind].reshape((N_DEV,) + w3.shape[1:]) for l in range(depth)]
        outs = adamw_layers(w3, mt.reshape(shape3), vt.reshape(shape3), gps, name=name)
        return [o.reshape(wt.shape) for o in outs]

    def update_small(wt, mt, vt, gt, name):
        shape2 = (-1, wt.shape[-1])
        return update(wt, mt, vt, gt.reshape(shape2)[None], shape2, name)

    res = {
        "w_ada": update_small(w_ada, m_w_ada, v_w_ada, g_w_ada, "adamw_w_ada"),
        "b_ada": update_small(b_ada, m_b_ada, v_b_ada, g_b_ada, "adamw_b_ada"),
        "norm_pre": update_small(norm_pre, m_norm_pre, v_norm_pre, g_npre, "adamw_norm_pre"),
        "norm_post": update_small(norm_post, m_norm_post, v_norm_post, g_npost, "adamw_norm_post"),
        "w_in": update_layers(w_in, m_w_in, v_w_in, 0, "adamw_w_in"),
        "lower_bounds": update_small(lower_bounds, m_lower_bounds, v_lower_bounds, g_lower, "adamw_lower_bounds"),
        "hgrn_norm": update_small(hgrn_norm, m_hgrn_norm, v_hgrn_norm, g_hn, "adamw_hgrn_norm"),
        "pool_w": update_layers(pool_w, m_pool_w, v_pool_w, 4, "adamw_pool_w"),
        "pool_scale": update_small(pool_scale, m_pool_scale, v_pool_scale, g_ps, "adamw_pool_scale"),
        "w_proj_a": update_layers(w_proj_a, m_w_proj_a, v_w_proj_a, 1, "adamw_w_proj_a"),
        "w_proj_b": update_layers(w_proj_b, m_w_proj_b, v_w_proj_b, 2, "adamw_w_proj_b"),
        "w_out": update_layers(w_out, m_w_out, v_w_out, 3, "adamw_w_out"),
    }
    order = ["w_ada", "b_ada", "norm_pre", "norm_post", "w_in", "lower_bounds", "hgrn_norm", "pool_w",
             "pool_scale", "w_proj_a", "w_proj_b", "w_out"]
    outs = [loss, grad_x]
    for k in range(4):
        outs += [res[nm][k] for nm in order]
    return tuple(outs)
```

```python
import functools

import jax
import jax.numpy as jnp
from jax import lax
from jax.experimental import pallas as pl
from jax.experimental.pallas import tpu as pltpu

F32 = jnp.float32
BF16 = jnp.bfloat16
MESH = pl.DeviceIdType.MESH

N_DEV = 8
EPS = 1e-6
MIN_FORGET = 1e-30
D_MODEL = 2048
HEADS = 8
HEAD_DIM = 128
CHUNK = 64
SUB = 16
N_SUB = CHUNK // SUB
WIDTH = 1024
POOL_WINDOWS = (2, 4, 8, 16)
POOL_GW = 256
HALO = 16
IN_COLS = 10240
LANE = 128
N_COLBLK = IN_COLS // LANE
GATE_COLS = 4096
HEAD_COLS = 4 * HEAD_DIM
POOL_COLS = 2 * POOL_GW
GP_COLS = GATE_COLS + len(POOL_WINDOWS) * POOL_COLS
MAX_EXP = 80.0

ADAM_LR = 0.001
ADAM_B1 = 0.9
ADAM_B2 = 0.999
ADAM_EPS = 1e-08
ADAM_WD = 0.01
ADAM_STEP = 10

VMEM_LIMIT = 62 * 1024 * 1024


def _cparams(sem=None):
    return pltpu.CompilerParams(dimension_semantics=sem, vmem_limit_bytes=VMEM_LIMIT)


def _sigmoid(v):
    return 1.0 / (1.0 + jnp.exp(-v))


def _dot(a, b):
    return jnp.dot(a, b, preferred_element_type=F32)


def _dot_nt(a, b):
    return lax.dot_general(a, b, (((1,), (1,)), ((), ())), preferred_element_type=F32)


def _dot_tn(a, b):
    return lax.dot_general(a, b, (((0,), (0,)), ((), ())), preferred_element_type=F32)


def _colsum(v):
    return jnp.sum(v, axis=0, keepdims=True)


def _rowmean(v):
    return jnp.mean(v, axis=-1, keepdims=True)


def _orig_block_static(n):
    if n < 32:
        return n + 48
    if n < 48:
        m = n - 32
        t = m % 4
        return 32 + 2 * (m // 4) + (t % 2) + 8 * (t // 2)
    m = n - 48
    return 8 * (m % 4) + m // 4


def _accumulate(step, steps, prod, o_ref, acc_ref):
    if steps == 1:
        o_ref[...] = prod.astype(o_ref.dtype)
        return

    @pl.when(step == 0)
    def _():
        acc_ref[...] = prod

    @pl.when(step > 0)
    def _():
        acc_ref[...] += prod

    @pl.when(step == steps - 1)
    def _():
        o_ref[...] = acc_ref[...].astype(o_ref.dtype)


def _matmul_call(dot, a, b, *, grid, in_specs, out_spec, out_shape, acc_shape, steps, rider, name):
    nr = rider.n if rider else 0
    hbm = pl.BlockSpec(memory_space=pl.ANY)
    has_acc = steps > 1

    def body(*refs):
        a_ref, b_ref = refs[:2]
        o_ref = refs[2 + nr]
        scratch = refs[3 + 2 * nr:]
        if rider:
            total = grid[0] * grid[1]
            rider.emit(pl.program_id(0) * grid[1] + pl.program_id(1), total, _rider_mid_step(total),
                       refs[2:2 + nr], refs[3 + nr:3 + 2 * nr], scratch[1 if has_acc else 0:])
        _accumulate(pl.program_id(1), steps, dot(a_ref[...], b_ref[...]), o_ref, scratch[0] if has_acc else None)

    outs = pl.pallas_call(
        body, grid=grid,
        in_specs=in_specs + [hbm] * nr, out_specs=[out_spec] + [hbm] * nr,
        out_shape=[out_shape] + (rider.out_shape if rider else []),
        scratch_shapes=([pltpu.VMEM(acc_shape, F32)] if has_acc else []) + (rider.scratch if rider else []),
        compiler_params=_cparams(("arbitrary", "arbitrary")),
        name=name)(a, b, *(rider.arrs if rider else []))
    return outs if rider else outs[0]


def matmul_nn(a, b, *, tm, tn, out_dtype, name, rider=None, cols=None):
    m, k = a.shape
    col0, n = cols if cols else (0, b.shape[1])
    tm = min(tm, m)
    j0 = col0 // tn
    return _matmul_call(
        _dot, a, b, grid=(n // tn, m // tm),
        in_specs=[pl.BlockSpec((tm, k), lambda j, i: (i, 0)), pl.BlockSpec((k, tn), lambda j, i: (0, j0 + j))],
        out_spec=pl.BlockSpec((tm, tn), lambda j, i: (i, j)),
        out_shape=jax.ShapeDtypeStruct((m, n), out_dtype), acc_shape=None, steps=1, rider=rider, name=name)


def matmul_nt(a, b, *, tm, tn, out_dtype, name, rider=None):
    m, n = a.shape
    k = b.shape[0]
    tm = min(tm, m)
    return _matmul_call(
        _dot_nt, a, b, grid=(m // tm, n // tn),
        in_specs=[pl.BlockSpec((tm, tn), lambda i, j: (i, j)), pl.BlockSpec((k, tn), lambda i, j: (0, j))],
        out_spec=pl.BlockSpec((tm, k), lambda i, j: (i, 0)),
        out_shape=jax.ShapeDtypeStruct((m, k), out_dtype), acc_shape=(tm, k), steps=n // tn, rider=rider, name=name)


def matmul_tn(a, b, *, tm, tn, out_dtype, name, rider=None):
    m, k = a.shape
    n = b.shape[1]
    tm = min(tm, m)
    return _matmul_call(
        _dot_tn, a, b, grid=(n // tn, m // tm),
        in_specs=[pl.BlockSpec((tm, k), lambda j, i: (i, 0)), pl.BlockSpec((tm, tn), lambda j, i: (i, j))],
        out_spec=pl.BlockSpec((k, tn), lambda j, i: (0, j)),
        out_shape=jax.ShapeDtypeStruct((k, n), out_dtype), acc_shape=(k, tn), steps=m // tm, rider=rider, name=name)


def permute_w_in(staged_lo, staged_hi, *, name):
    k = staged_lo.shape[1]
    own = IN_COLS // N_DEV
    half = own // 2
    tr = min(256, k)

    def body(lo_ref, hi_ref, o_ref):
        for nb in range(N_COLBLK):
            dev, col = divmod(_orig_block_static(nb) * LANE, own)
            src = lo_ref[dev, :, col:col + LANE] if col < half else hi_ref[dev, :, col - half:col - half + LANE]
            o_ref[:, nb * LANE:(nb + 1) * LANE] = src

    spec = pl.BlockSpec((N_DEV, tr, half), lambda i: (0, i, 0))
    return pl.pallas_call(
        body, grid=(k // tr,),
        in_specs=[spec, spec],
        out_specs=pl.BlockSpec((tr, IN_COLS), lambda i: (i, 0)),
        out_shape=jax.ShapeDtypeStruct((k, IN_COLS), staged_lo.dtype),
        compiler_params=_cparams(("arbitrary",)), name=name)(staged_lo, staged_hi)


def unpermute_w_in(dw, *, name):
    k = dw.shape[0]
    own = IN_COLS // N_DEV
    tr = min(256, k)

    def body(i_ref, o_ref):
        for nb in range(N_COLBLK):
            dev, col = divmod(_orig_block_static(nb) * LANE, own)
            o_ref[dev, :, col:col + LANE] = i_ref[:, nb * LANE:(nb + 1) * LANE]

    return pl.pallas_call(
        body, grid=(k // tr,),
        in_specs=[pl.BlockSpec((tr, IN_COLS), lambda i: (i, 0))],
        out_specs=pl.BlockSpec((N_DEV, tr, own), lambda i: (0, i, 0)),
        out_shape=jax.ShapeDtypeStruct((N_DEV, k, own), dw.dtype),
        compiler_params=_cparams(("arbitrary",)), name=name)(dw)


def _row_tile(s):
    return min(256, s)


def _norm_tile(s):
    return min(512, s)


def _row_spec(t, w, col=0):
    return pl.BlockSpec((t, w), lambda i: (i, col))


def _vec_spec(w):
    return pl.BlockSpec((1, w), lambda i: (0, 0))


def prenorm_fwd(x, gain, shift, scale, *, name):
    s, d = x.shape
    t = _norm_tile(s)

    def body(x_ref, g_ref, sh_ref, sc_ref, h_ref):
        xv = x_ref[...]
        r = lax.rsqrt(_rowmean(xv * xv) + EPS)
        h_ref[...] = ((xv * r) * g_ref[...] * (1.0 + sc_ref[...]) + sh_ref[...]).astype(h_ref.dtype)

    return pl.pallas_call(
        body, grid=(s // t,),
        in_specs=[_row_spec(t, d), _vec_spec(d), _vec_spec(d), _vec_spec(d)],
        out_specs=_row_spec(t, d), out_shape=jax.ShapeDtypeStruct((s, d), BF16),
        compiler_params=_cparams(("arbitrary",)), name=name)(x, gain, shift, scale)


def w_in_dx_prenorm_bwd(dproj, w_in, x, gain, scale, g_res, *, rider=None, name):
    s, n = dproj.shape
    d = w_in.shape[0]
    tm = min(512, s)
    tn = 2048
    steps = n // tn
    nrow = s // tm
    nr = rider.n if rider else 0
    hbm = pl.BlockSpec(memory_space=pl.ANY)

    def body(*refs):
        dp_ref, w_ref, x_ref, g_ref, sc_ref, gr_ref = refs[:6]
        dx_ref, dsh_ref, dsc_ref, dg_ref = refs[6 + nr:10 + nr]
        acc_ref = refs[10 + 2 * nr]
        i, j = pl.program_id(0), pl.program_id(1)
        if rider:
            total = nrow * steps
            rider.emit(i * steps + j, total, _rider_mid_step(total),
                       refs[6:6 + nr], refs[10 + nr:10 + 2 * nr], refs[11 + 2 * nr:])
        prod = _dot_nt(dp_ref[...], w_ref[...])

        @pl.when(j == 0)
        def _():
            acc_ref[...] = prod

        @pl.when(j > 0)
        def _():
            acc_ref[...] += prod

        @pl.when(j == steps - 1)
        def _():
            @pl.when(i == 0)
            def _():
                dsh_ref[...] = jnp.zeros_like(dsh_ref)
                dsc_ref[...] = jnp.zeros_like(dsc_ref)
                dg_ref[...] = jnp.zeros_like(dg_ref)

            gain_v = g_ref[...]
            one_sc = 1.0 + sc_ref[...]
            sub = min(256, tm)

            def piece(k, carry):
                rows = pl.ds(pl.multiple_of(k * sub, sub), sub)
                xv = x_ref[rows, :]
                dhv = acc_ref[rows, :]
                r = lax.rsqrt(_rowmean(xv * xv) + EPS)
                xn = xv * r
                dyn = dhv * one_sc
                dxn = dyn * gain_v
                dx_ref[rows, :] = r * (dxn - xn * _rowmean(dxn * xn)) + gr_ref[rows, :]
                dsh_ref[...] += _colsum(dhv)
                dsc_ref[...] += _colsum(dhv * (xn * gain_v))
                dg_ref[...] += _colsum(dyn * xn)
                return carry

            lax.fori_loop(0, tm // sub, piece, 0)

    rows = pl.BlockSpec((tm, d), lambda i, j: (i, 0))
    vec_in = pl.BlockSpec((1, d), lambda i, j: (0, 0))
    vec = jax.ShapeDtypeStruct((1, d), F32)
    outs = pl.pallas_call(
        body, grid=(nrow, steps),
        in_specs=[pl.BlockSpec((tm, tn), lambda i, j: (i, j)), pl.BlockSpec((d, tn), lambda i, j: (0, j)),
                  rows, vec_in, vec_in, rows] + [hbm] * nr,
        out_specs=[rows, vec_in, vec_in, vec_in] + [hbm] * nr,
        out_shape=[jax.ShapeDtypeStruct((s, d), F32), vec, vec, vec] + (rider.out_shape if rider else []),
        scratch_shapes=[pltpu.VMEM((tm, d), F32)] + (rider.scratch if rider else []),
        compiler_params=_cparams(("arbitrary", "arbitrary")),
        name=name)(dproj, w_in, x, gain, scale, g_res, *(rider.arrs if rider else []))
    return outs


def w_out_postnorm_fwd(merged, w_out, x, gain, gate, *, name):
    s, d = x.shape
    t = _norm_tile(s)

    def body(m_ref, w_ref, x_ref, g_ref, gt_ref, o_ref, y_ref):
        ov = _dot(m_ref[...], w_ref[...])
        o_ref[...] = ov
        r = lax.rsqrt(_rowmean(ov * ov) + EPS)
        y_ref[...] = x_ref[...] + gt_ref[...] * ((ov * r) * g_ref[...])

    shp = jax.ShapeDtypeStruct((s, d), F32)
    return pl.pallas_call(
        body, grid=(s // t,),
        in_specs=[_row_spec(t, d), _full_spec((d, d)), _row_spec(t, d), _vec_spec(d), _vec_spec(d)],
        out_specs=[_row_spec(t, d)] * 2, out_shape=[shp] * 2,
        compiler_params=_cparams(("arbitrary",)), name=name)(merged, w_out, x, gain, gate)


def loss_head(y, target, *, name):
    s, d = y.shape
    t = _norm_tile(s)
    steps = s // t

    def body(y_ref, t_ref, dy_ref, loss_ref, acc_ref):
        i = pl.program_id(0)
        err = y_ref[...] - t_ref[...]
        dy_ref[...] = err * (1.0 / d)
        part = _colsum(err * err)

        @pl.when(i == 0)
        def _():
            acc_ref[...] = part

        @pl.when(i > 0)
        def _():
            acc_ref[...] += part

        @pl.when(i == steps - 1)
        def _():
            loss_ref[...] = jnp.sum(acc_ref[...], axis=1, keepdims=True) * (0.5 / d)

    return pl.pallas_call(
        body, grid=(steps,),
        in_specs=[_row_spec(t, d), _row_spec(t, d)],
        out_specs=[_row_spec(t, d), pl.BlockSpec((1, 1), lambda i: (0, 0))],
        out_shape=[jax.ShapeDtypeStruct((s, d), F32), jax.ShapeDtypeStruct((1, 1), F32)],
        scratch_shapes=[pltpu.VMEM((1, d), F32)],
        compiler_params=_cparams(("arbitrary",)), name=name)(y, target)


def _full_spec(shape):
    return pl.BlockSpec(shape, lambda i: (0,) * len(shape))


def proj_gate_fwd(y_a, y_b, w_pa, w_pb, proj, *, name):
    s, width = y_a.shape
    d = w_pa.shape[1]
    t = _norm_tile(s)

    def body(ya_ref, yb_ref, wa_ref, wb_ref, ga_ref, gb_ref, pa_ref, pb_ref, m_ref):
        pa = _dot(ya_ref[...], wa_ref[...])
        pb = _dot(yb_ref[...], wb_ref[...])
        pa_ref[...] = pa.astype(pa_ref.dtype)
        pb_ref[...] = pb.astype(pb_ref.dtype)
        m_ref[...] = (_sigmoid(ga_ref[...].astype(F32)) * pa
                      + _sigmoid(gb_ref[...].astype(F32)) * pb).astype(m_ref.dtype)

    out = jax.ShapeDtypeStruct((s, d), BF16)
    return pl.pallas_call(
        body, grid=(s // t,),
        in_specs=[_row_spec(t, width), _row_spec(t, width), _full_spec((width, d)), _full_spec((width, d)),
                  _row_spec(t, d, 0), _row_spec(t, d, 1)],
        out_specs=[_row_spec(t, d)] * 3, out_shape=[out] * 3,
        compiler_params=_cparams(("arbitrary",)), name=name)(y_a, y_b, w_pa, w_pb, proj, proj)


def postnorm_gate_bwd(g, out, gain, gate, w_out, proj, pa, pb, *, name):
    s, d = pa.shape
    t = _row_tile(s)

    def body(g_ref, o_ref, gn_ref, gt_ref, w_ref, ga_ref, gb_ref, pa_ref, pb_ref,
             do_ref, dpa_ref, dpb_ref, dp_ref, dgt_ref, dgn_ref):
        i = pl.program_id(0)
        ov = o_ref[...]
        gv = g_ref[...]
        r = lax.rsqrt(_rowmean(ov * ov) + EPS)
        on = ov * r
        gain_v = gn_ref[...]
        dn = gv * gt_ref[...]
        don = dn * gain_v
        dout = (r * (don - on * _rowmean(don * on))).astype(BF16)
        do_ref[...] = dout
        p_gt = _colsum(gv * (on * gain_v))
        p_gn = _colsum(dn * on)

        @pl.when(i == 0)
        def _():
            dgt_ref[...] = p_gt
            dgn_ref[...] = p_gn

        @pl.when(i > 0)
        def _():
            dgt_ref[...] += p_gt
            dgn_ref[...] += p_gn

        dm = _dot_nt(dout, w_ref[...])
        sa = _sigmoid(ga_ref[...].astype(F32))
        sb = _sigmoid(gb_ref[...].astype(F32))
        dpa_ref[...] = (dm * sa).astype(dpa_ref.dtype)
        dpb_ref[...] = (dm * sb).astype(dpb_ref.dtype)
        dp_ref[:, :d] = (dm * pa_ref[...].astype(F32) * sa * (1.0 - sa)).astype(dp_ref.dtype)
        dp_ref[:, d:] = (dm * pb_ref[...].astype(F32) * sb * (1.0 - sb)).astype(dp_ref.dtype)

    act = jax.ShapeDtypeStruct((s, d), BF16)
    vec = jax.ShapeDtypeStruct((1, d), F32)
    return pl.pallas_call(
        body, grid=(s // t,),
        in_specs=[_row_spec(t, d), _row_spec(t, d), _vec_spec(d), _vec_spec(d), _full_spec((d, d)),
                  _row_spec(t, d, 0), _row_spec(t, d, 1), _row_spec(t, d), _row_spec(t, d)],
        out_specs=[_row_spec(t, d), _row_spec(t, d), _row_spec(t, d), _row_spec(t, 2 * d, 0),
                   _vec_spec(d), _vec_spec(d)],
        out_shape=[act, act, act, jax.ShapeDtypeStruct((s, IN_COLS), BF16), vec, vec],
        compiler_params=_cparams(("arbitrary",)), name=name)(g, out, gain, gate, w_out, proj, proj, pa, pb)


def _pool_tile(s):
    return min(512, s)


def pool_fwd(proj, pw, ps, *, name):
    s = proj.shape[0]
    t = _pool_tile(s)
    pool_blk = GATE_COLS // (len(POOL_WINDOWS) * POOL_COLS)

    def body(p_ref, halo_ref, pw_ref, ps_ref, yb_ref, pooled_ref, mixed_ref):
        i = pl.program_id(0)
        halo = jnp.where(i == 0, 0.0, halo_ref[...].astype(F32))
        row = i * t + lax.broadcasted_iota(jnp.int32, (t, 1), 0)
        for g, w in enumerate(POOL_WINDOWS):
            vb = p_ref[:, g * POOL_COLS:g * POOL_COLS + POOL_GW].astype(F32)
            zb = p_ref[:, g * POOL_COLS + POOL_GW:(g + 1) * POOL_COLS].astype(F32)
            acc = jnp.concatenate([halo[:, g * POOL_COLS:g * POOL_COLS + POOL_GW], vb], axis=0)
            sh = 1
            while sh < w:
                acc = acc + pltpu.roll(acc, sh, axis=0)
                sh *= 2
            cnt = jnp.minimum(row + 1, w).astype(F32)
            pooled = acc[HALO:, :] / cnt - vb
            mixed = _dot(pooled.astype(BF16), pw_ref[g])
            cols = slice(g * POOL_GW, (g + 1) * POOL_GW)
            yb = mixed * ps_ref[:, cols] * (zb * _sigmoid(zb))
            yb_ref[:, cols] = yb.astype(yb_ref.dtype)
            pooled_ref[:, cols] = pooled.astype(pooled_ref.dtype)
            mixed_ref[:, cols] = mixed

    wide = len(POOL_WINDOWS) * POOL_COLS
    return pl.pallas_call(
        body, grid=(s // t,),
        in_specs=[pl.BlockSpec((t, wide), lambda i: (i, pool_blk)),
                  pl.BlockSpec((HALO, wide), lambda i: (jnp.maximum(i * (t // HALO) - 1, 0), pool_blk)),
                  pl.BlockSpec((len(POOL_WINDOWS), POOL_GW, POOL_GW), lambda i: (0, 0, 0)),
                  _vec_spec(WIDTH)],
        out_specs=[_row_spec(t, WIDTH)] * 3,
        out_shape=[jax.ShapeDtypeStruct((s, WIDTH), BF16), jax.ShapeDtypeStruct((s, WIDTH), BF16),
                   jax.ShapeDtypeStruct((s, WIDTH), F32)],
        compiler_params=_cparams(("arbitrary",)), name=name)(proj, proj, pw, ps)


def pool_bwd(dyb, proj, pooled, mixed, pw, ps, dproj, *, name):
    s = proj.shape[0]
    t = _pool_tile(s)
    nblk = s // t
    ng = len(POOL_WINDOWS)
    wide = ng * POOL_COLS
    pool_blk = GATE_COLS // wide

    def body(dy_ref, p_ref, pooled_ref, mixed_ref, pw_ref, ps_ref, dp_any, dp_ref, dpw_ref, dps_ref, carry):
        del dp_any
        i = pl.program_id(0)
        ii = nblk - 1 - i

        @pl.when(i == 0)
        def _():
            carry[...] = jnp.zeros_like(carry)
            dpw_ref[...] = jnp.zeros_like(dpw_ref)
            dps_ref[...] = jnp.zeros_like(dps_ref)

        row = ii * t + lax.broadcasted_iota(jnp.int32, (t, 1), 0)
        for g, w in enumerate(POOL_WINDOWS):
            cols = slice(g * POOL_GW, (g + 1) * POOL_GW)
            zb = p_ref[:, g * POOL_COLS + POOL_GW:(g + 1) * POOL_COLS].astype(F32)
            dy = dy_ref[:, cols]
            mx = mixed_ref[:, cols]
            sc = ps_ref[:, cols]
            sg = _sigmoid(zb)
            dzb = dy * (mx * sc) * (sg * (1.0 + zb * (1.0 - sg)))
            dpm = dy * (zb * sg)
            dps_ref[:, cols] += _colsum(dpm * mx)
            dmixed = (dpm * sc).astype(BF16)
            dpooled = _dot_nt(dmixed, pw_ref[g])
            dpw_ref[g] += _dot_tn(pooled_ref[:, cols], dmixed)
            cnt = jnp.minimum(row + 1, w).astype(F32)
            u = dpooled / cnt
            acc = jnp.concatenate([u, carry[:, cols]], axis=0)
            sh = 1
            while sh < w:
                acc = acc + pltpu.roll(acc, t + HALO - sh, axis=0)
                sh *= 2
            carry[:, cols] = u[:HALO, :]
            dp_ref[:, g * POOL_COLS:g * POOL_COLS + POOL_GW] = (acc[:t, :] - dpooled).astype(dp_ref.dtype)
            dp_ref[:, g * POOL_COLS + POOL_GW:(g + 1) * POOL_COLS] = dzb.astype(dp_ref.dtype)

    rev = lambda i: (nblk - 1 - i, 0)
    return pl.pallas_call(
        body, grid=(nblk,),
        in_specs=[pl.BlockSpec((t, WIDTH), rev),
                  pl.BlockSpec((t, wide), lambda i: (nblk - 1 - i, pool_blk)),
                  pl.BlockSpec((t, WIDTH), rev), pl.BlockSpec((t, WIDTH), rev),
                  pl.BlockSpec((ng, POOL_GW, POOL_GW), lambda i: (0, 0, 0)),
                  _vec_spec(WIDTH),
                  pl.BlockSpec(memory_space=pl.ANY)],
        out_specs=[pl.BlockSpec((t, wide), lambda i: (nblk - 1 - i, pool_blk)),
                   pl.BlockSpec((ng, POOL_GW, POOL_GW), lambda i: (0, 0, 0)),
                   _vec_spec(WIDTH)],
        out_shape=[jax.ShapeDtypeStruct(dproj.shape, dproj.dtype),
                   jax.ShapeDtypeStruct((ng, POOL_GW, POOL_GW), F32),
                   jax.ShapeDtypeStruct((1, WIDTH), F32)],
        scratch_shapes=[pltpu.VMEM((HALO, WIDTH), F32)],
        input_output_aliases={6: 0},
        compiler_params=_cparams(("arbitrary",)), name=name)(dyb, proj, pooled, mixed, pw, ps, dproj)


def _hgrn_tile(s):
    return min(1024, s)


def _chunk_consts():
    tt = lax.broadcasted_iota(jnp.int32, (CHUNK, CHUNK), 0)
    ss = lax.broadcasted_iota(jnp.int32, (CHUNK, CHUNK), 1)
    within = (ss <= tt) & (ss // SUB == tt // SUB)
    before = ss < (tt // SUB) * SUB
    cums = jnp.concatenate([within.astype(F32), before.astype(F32)], axis=0).astype(BF16)
    causal = ss <= tt
    upper = (ss >= tt).astype(F32).astype(BF16)
    row = lax.broadcasted_iota(jnp.int32, (CHUNK, 1), 0)
    return cums, causal, upper, row


def _dot_split(mat01, v):
    hi = v.astype(BF16)
    r1 = v - hi.astype(F32)
    mid = r1.astype(BF16)
    lo = (r1 - mid.astype(F32)).astype(BF16)
    return _dot(mat01, hi) + _dot(mat01, mid) + _dot(mat01, lo)


def _hgrn_chunks(qas, fas, lb, cums, row):
    gates = [_hgrn_gates(qa, fa, lb) for qa, fa in zip(qas, fas)]
    cbs = [_dot_split(cums, g["lf"]) for g in gates]
    return [_hgrn_decay(g, cb, row) for g, cb in zip(gates, cbs)]


def _hgrn_gates(qa, fa, lb):
    sq = _sigmoid(qa)
    sa = _sigmoid(fa)
    sna = 1.0 - sa
    oml = 1.0 - lb
    f = lb + oml * sa
    fc = jnp.maximum(f, MIN_FORGET)
    return dict(sq=sq, q=qa * sq, sa=sa, sna=sna, oml=oml, f=f, fc=fc, lf=jnp.log(fc), k=oml * sna)


def _hgrn_decay(g, cb, row):
    sq, q, sa, sna, oml, f, fc, k = (g[n] for n in ("sq", "q", "sa", "sna", "oml", "f", "fc", "k"))
    c = cb[:CHUNK]
    bt = cb[CHUNK:]
    ec = jnp.exp(c)
    enc = jnp.exp(jnp.minimum(-c, MAX_EXP))
    qt = q * ec
    kt = k * enc
    dms, lhs, rhs = [], [], []
    for j in range(N_SUB):
        bj = bt[j * SUB:j * SUB + 1, :]
        dm = jnp.where(row >= j * SUB, jnp.exp(jnp.minimum(bt - bj, 0.0)), 0.0)
        dms.append(dm)
        lhs.append(qt * dm)
        rhs.append(jnp.where(row // SUB == j, kt, 0.0))
    lhs = jnp.concatenate(lhs, axis=1).astype(BF16)
    rhs = jnp.concatenate(rhs, axis=1).astype(BF16)
    b = bt + c
    bl = b[CHUNK - 1:CHUNK, :]
    ebl = jnp.exp(bl)
    edec = jnp.exp(bl - b)
    eb = ec * dms[0]
    return dict(sq=sq, q=q, sa=sa, sna=sna, oml=oml, f=f, fc=fc, k=k, ec=ec, enc=enc, dms=dms,
                lhs=lhs, rhs=rhs, ebl=ebl, edec=edec, eb=eb, qd=q * eb, kdec=k * edec)


def _rider_mid_step(total):
    return total - max(1, total // 8)


def hgrn_fwd(proj, lb, hn, *, rider=None, name):
    s = proj.shape[0]
    t = _hgrn_tile(s)
    nblk = s // t
    ncht = t // CHUNK
    nr = rider.n if rider else 0
    hbm = pl.BlockSpec(memory_space=pl.ANY)

    def body(*refs):
        p_ref, lb_ref, hn_ref = refs[:3]
        ya_ref, o_ref, st_ref = refs[3 + nr:6 + nr]
        state = refs[6 + 2 * nr]
        i = pl.program_id(1)
        if rider:
            total = HEADS * nblk
            rider.emit(pl.program_id(0) * nblk + i, total, _rider_mid_step(total),
                       refs[3:3 + nr], refs[6 + nr:6 + 2 * nr], refs[7 + 2 * nr:])

        @pl.when(i == 0)
        def _():
            state[...] = jnp.zeros_like(state)

        cums, causal, _, row = _chunk_consts()
        lbv = lb_ref[...]
        hnv = hn_ref[...]

        rows = [slice(ci * CHUNK, (ci + 1) * CHUNK) for ci in range(ncht)]
        pres = _hgrn_chunks([p_ref[r, 0:HEAD_DIM] for r in rows], [p_ref[r, HEAD_DIM:2 * HEAD_DIM] for r in rows],
                            lbv, cums, row)
        vas = [p_ref[r, 2 * HEAD_DIM:3 * HEAD_DIM].astype(BF16) for r in rows]
        scores = [jnp.where(causal, _dot_nt(pre["lhs"], pre["rhs"]), 0.0).astype(BF16) for pre in pres]
        intra = [_dot(a, va) for a, va in zip(scores, vas)]
        qds = [pre["qd"].astype(BF16) for pre in pres]
        kdecs = [pre["kdec"].astype(BF16) for pre in pres]
        st = state[...]
        outs = []
        for ci in range(ncht):
            stb = st.astype(BF16)
            st_ref[ci, 0] = stb
            outs.append(intra[ci] + _dot_nt(qds[ci], stb))
            st = st * pres[ci]["ebl"] + _dot_tn(vas[ci], kdecs[ci])
        state[...] = st
        for r, o in zip(rows, outs):
            za = p_ref[r, 3 * HEAD_DIM:4 * HEAD_DIM]
            o_ref[r, :] = o
            ya_ref[r, :] = ((o * lax.rsqrt(_rowmean(o * o) + EPS)) * hnv * (za * _sigmoid(za))).astype(ya_ref.dtype)

    return pl.pallas_call(
        body, grid=(HEADS, nblk),
        in_specs=[pl.BlockSpec((t, HEAD_COLS), lambda h, i: (i, h)),
                  pl.BlockSpec((1, HEAD_DIM), lambda h, i: (0, h)),
                  pl.BlockSpec((1, HEAD_DIM), lambda h, i: (0, h))] + [hbm] * nr,
        out_specs=[pl.BlockSpec((t, HEAD_DIM), lambda h, i: (i, h)),
                   pl.BlockSpec((t, HEAD_DIM), lambda h, i: (i, h)),
                   pl.BlockSpec((ncht, 1, HEAD_DIM, HEAD_DIM), lambda h, i: (i, h, 0, 0))] + [hbm] * nr,
        out_shape=[jax.ShapeDtypeStruct((s, WIDTH), BF16), jax.ShapeDtypeStruct((s, WIDTH), F32),
                   jax.ShapeDtypeStruct((s // CHUNK, HEADS, HEAD_DIM, HEAD_DIM), BF16)]
        + (rider.out_shape if rider else []),
        scratch_shapes=[pltpu.VMEM((HEAD_DIM, HEAD_DIM), F32)] + (rider.scratch if rider else []),
        compiler_params=_cparams(("arbitrary", "arbitrary")), name=name)(proj, lb, hn, *(rider.arrs if rider else []))


def hgrn_bwd(dya, proj, o_all, states, lb, hn, dproj, *, rider=None, name):
    s = proj.shape[0]
    t = _hgrn_tile(s)
    nblk = s // t
    ncht = t // CHUNK
    head_blk0 = GP_COLS // HEAD_COLS
    nr = rider.n if rider else 0
    hbm = pl.BlockSpec(memory_space=pl.ANY)

    def body(*refs):
        dy_ref, p_ref, o_ref, st_ref, lb_ref, hn_ref = refs[:6]
        dp_ref, dhn_ref, dlb_ref = refs[7 + nr:10 + nr]
        dstate = refs[10 + 2 * nr]
        i = pl.program_id(1)
        if rider:
            total = HEADS * nblk
            rider.emit(pl.program_id(0) * nblk + i, total, _rider_mid_step(total),
                       refs[7:7 + nr], refs[10 + nr:10 + 2 * nr], refs[11 + 2 * nr:])

        @pl.when(i == 0)
        def _():
            dstate[...] = jnp.zeros_like(dstate)
            dhn_ref[...] = jnp.zeros_like(dhn_ref)
            dlb_ref[...] = jnp.zeros_like(dlb_ref)

        cums, causal, upper, row = _chunk_consts()
        lbv = lb_ref[...]
        hnv = hn_ref[...]

        chunks = range(ncht)
        rows = [slice(ci * CHUNK, (ci + 1) * CHUNK) for ci in chunks]
        qas = [p_ref[r, 0:HEAD_DIM] for r in rows]
        vbs = [p_ref[r, 2 * HEAD_DIM:3 * HEAD_DIM].astype(BF16) for r in rows]
        st0s = [st_ref[ci, 0] for ci in chunks]
        dzas, dobs = [], []
        dhn_acc = jnp.zeros_like(hnv)
        for r in rows:
            za = p_ref[r, 3 * HEAD_DIM:4 * HEAD_DIM]
            o = o_ref[r, :]
            dy = dy_ref[r, :]
            rn = lax.rsqrt(_rowmean(o * o) + EPS)
            on = o * rn
            sgz = _sigmoid(za)
            sz = za * sgz
            dzas.append(dy * on * hnv * (sgz * (1.0 + za * (1.0 - sgz))))
            dhn_acc = dhn_acc + _colsum(dy * on * sz)
            don = dy * hnv * sz
            dobs.append((rn * (don - on * _rowmean(don * on))).astype(BF16))
        pres = _hgrn_chunks(qas, [p_ref[r, HEAD_DIM:2 * HEAD_DIM] for r in rows], lbv, cums, row)
        scores = [jnp.where(causal, _dot_nt(pre["lhs"], pre["rhs"]), 0.0).astype(BF16) for pre in pres]
        das = [jnp.where(causal, _dot_nt(dob, vb), 0.0).astype(BF16) for dob, vb in zip(dobs, vbs)]
        dlhss = [_dot(da, pre["rhs"]) for da, pre in zip(das, pres)]
        drhss = [_dot_tn(da, pre["lhs"]) for da, pre in zip(das, pres)]
        dv_intra = [_dot_tn(a, dob) for a, dob in zip(scores, dobs)]
        dq_inter = [_dot(dob, st0) * pre["eb"] for dob, st0, pre in zip(dobs, st0s, pres)]
        qds = [pre["qd"].astype(BF16) for pre in pres]
        kdecs = [pre["kdec"].astype(BF16) for pre in pres]
        dst1 = dstate[...]
        dvs, dk_states, dbl_states = [None] * ncht, [None] * ncht, [None] * ncht
        for ci in reversed(chunks):
            dst1b = dst1.astype(BF16)
            dvs[ci] = dv_intra[ci] + _dot_nt(kdecs[ci], dst1b)
            dk_states[ci] = _dot(vbs[ci], dst1b) * pres[ci]["edec"]
            dbl_states[ci] = pres[ci]["ebl"] * _colsum(dst1 * st0s[ci].astype(F32))
            dst1 = dst1 * pres[ci]["ebl"] + _dot_tn(dobs[ci], qds[ci])
        dstate[...] = dst1
        dqs, dks, dbs, dbls = [], [], [], []
        for ci in chunks:
            pre = pres[ci]
            q, k = pre["q"], pre["k"]
            dq_a = jnp.zeros_like(q)
            dk_a = jnp.zeros_like(k)
            db = q * dq_inter[ci] - k * dk_states[ci]
            for j in range(N_SUB):
                cols = slice(j * HEAD_DIM, (j + 1) * HEAD_DIM)
                dlhs, drhs = dlhss[ci][:, cols], drhss[ci][:, cols]
                dq_a = dq_a + pre["dms"][j] * dlhs
                dk_a = dk_a + jnp.where(row // SUB == j, drhs, 0.0)
                db = db + (pre["lhs"][:, cols].astype(F32) * dlhs - pre["rhs"][:, cols].astype(F32) * drhs)
            dqs.append(dq_inter[ci] + pre["ec"] * dq_a)
            dks.append(dk_states[ci] + pre["enc"] * dk_a)
            dbs.append(db)
            dbls.append(_colsum(k * dk_states[ci]) + dbl_states[ci])
        dlfs = [_dot_split(upper, db) + dbl for db, dbl in zip(dbs, dbls)]
        dlb_acc = jnp.zeros_like(lbv)
        for ci in chunks:
            pre = pres[ci]
            sq, sa, sna, oml = pre["sq"], pre["sa"], pre["sna"], pre["oml"]
            dqa = dqs[ci] * (sq * (1.0 + qas[ci] * (1.0 - sq)))
            diff = jnp.where(pre["f"] >= MIN_FORGET, dlfs[ci] / pre["fc"], 0.0) - dks[ci]
            dlb_acc = dlb_acc + _colsum(diff * sna)
            dfa = diff * (oml * sa * sna)
            dp_ref[rows[ci], :] = jnp.concatenate([dqa, dfa, dvs[ci], dzas[ci]], axis=1).astype(dp_ref.dtype)
        dhn_ref[...] += dhn_acc
        dlb_ref[...] += dlb_acc

    rev = lambda h, i: (nblk - 1 - i, h)
    return pl.pallas_call(
        body, grid=(HEADS, nblk),
        in_specs=[pl.BlockSpec((t, HEAD_DIM), rev),
                  pl.BlockSpec((t, HEAD_COLS), lambda h, i: (nblk - 1 - i, h)),
                  pl.BlockSpec((t, HEAD_DIM), rev),
                  pl.BlockSpec((ncht, 1, HEAD_DIM, HEAD_DIM), lambda h, i: (nblk - 1 - i, h, 0, 0)),
                  pl.BlockSpec((1, HEAD_DIM), lambda h, i: (0, h)),
                  pl.BlockSpec((1, HEAD_DIM), lambda h, i: (0, h)),
                  hbm] + [hbm] * nr,
        out_specs=[pl.BlockSpec((t, HEAD_COLS), lambda h, i: (nblk - 1 - i, head_blk0 + h)),
                   pl.BlockSpec((1, HEAD_DIM), lambda h, i: (0, h)),
                   pl.BlockSpec((1, HEAD_DIM), lambda h, i: (0, h))] + [hbm] * nr,
        out_shape=[jax.ShapeDtypeStruct(dproj.shape, dproj.dtype),
                   jax.ShapeDtypeStruct((1, WIDTH), F32), jax.ShapeDtypeStruct((1, WIDTH), F32)]
        + (rider.out_shape if rider else []),
        scratch_shapes=[pltpu.VMEM((HEAD_DIM, HEAD_DIM), F32)] + (rider.scratch if rider else []),
        input_output_aliases={6: 0},
        compiler_params=_cparams(("arbitrary", "arbitrary")),
        name=name)(dya, proj, o_all, states, lb, hn, dproj, *(rider.arrs if rider else []))


def _softmax_rows(lower):
    mx = jnp.max(lower, axis=0, keepdims=True)
    e = jnp.exp(lower - mx)
    return e / jnp.sum(e, axis=0, keepdims=True)


def lb_table(lower, *, name):
    depth, w = lower.shape

    def body(l_ref, o_ref):
        sm = _softmax_rows(l_ref[...])
        acc = jnp.zeros((1, w), F32)
        o_ref[0:1, :] = acc
        for l in range(1, depth):
            acc = acc + sm[l:l + 1, :]
            o_ref[l:l + 1, :] = acc

    return pl.pallas_call(body, out_shape=jax.ShapeDtypeStruct((depth, w), F32), name=name)(lower)


def lb_table_bwd(lower, dlb, *, name):
    depth, w = lower.shape

    def body(l_ref, d_ref, o_ref):
        sm = _softmax_rows(l_ref[...])
        dlbv = d_ref[...]
        dsm = [jnp.zeros((1, w), F32)]
        for i in range(1, depth):
            acc = jnp.zeros((1, w), F32)
            for l in range(i, depth):
                acc = acc + dlbv[l:l + 1, :]
            dsm.append(acc)
        inner = jnp.zeros((1, w), F32)
        for i in range(depth):
            inner = inner + sm[i:i + 1, :] * dsm[i]
        for i in range(depth):
            o_ref[i:i + 1, :] = sm[i:i + 1, :] * (dsm[i] - inner)

    return pl.pallas_call(body, out_shape=jax.ShapeDtypeStruct((depth, w), F32), name=name)(lower, dlb)


def w_ada_grad(c_all, dmod_cols, *, name):
    depth, _, cols = dmod_cols.shape
    d = c_all.shape[1]

    def body(c_ref, dm_ref, o_ref):
        cv = c_ref[...]
        ca = cv * _sigmoid(cv)
        o_ref[...] = _dot_tn(ca, dm_ref[...])

    return pl.pallas_call(
        body, grid=(depth,),
        in_specs=[pl.BlockSpec((N_DEV, d), lambda l: (0, 0)), pl.BlockSpec((None, N_DEV, cols), lambda l: (l, 0, 0))],
        out_specs=pl.BlockSpec((None, d, cols), lambda l: (l, 0, 0)),
        out_shape=jax.ShapeDtypeStruct((depth, d, cols), F32),
        compiler_params=_cparams(("arbitrary",)), name=name)(c_all, dmod_cols)


def sum_parts(parts, *, name):
    p, r, c = parts.shape

    def body(p_ref, o_ref):
        acc = p_ref[0]
        for j in range(1, p):
            acc = acc + p_ref[j]
        o_ref[...] = acc

    return pl.pallas_call(body, out_shape=jax.ShapeDtypeStruct((r, c), F32), name=name)(parts)


def _adam_rows(r, c):
    tr = r
    while tr * c * 4 > (1 << 20) and tr % 16 == 0:
        tr //= 2
    return tr


def _adam_update(w_ref, m_ref, v_ref, g_ref, go_ref, d_ref, mo_ref, vo_ref):
    g = g_ref[0].astype(F32)
    for j in range(1, g_ref.shape[0]):
        g = g + g_ref[j].astype(F32)
    mn = ADAM_B1 * m_ref[...] + (1.0 - ADAM_B1) * g
    vn = ADAM_B2 * v_ref[...] + (1.0 - ADAM_B2) * (g * g)
    m_hat = mn / (1.0 - ADAM_B1 ** ADAM_STEP)
    v_hat = vn / (1.0 - ADAM_B2 ** ADAM_STEP)
    go_ref[...] = g
    d_ref[...] = -ADAM_LR * (m_hat / (jnp.sqrt(v_hat) + ADAM_EPS) + ADAM_WD * w_ref[...])
    mo_ref[...] = mn
    vo_ref[...] = vn


def adamw(w, m, v, gparts, *, name):
    r, c = w.shape
    p = gparts.shape[0]
    tr = _adam_rows(r, c)
    spec = pl.BlockSpec((tr, c), lambda i: (i, 0))
    shp = jax.ShapeDtypeStruct((r, c), F32)
    return pl.pallas_call(
        functools.partial(_adam_update), grid=(r // tr,),
        in_specs=[spec, spec, spec, pl.BlockSpec((p, tr, c), lambda i: (0, i, 0))],
        out_specs=[spec] * 4, out_shape=[shp] * 4,
        compiler_params=_cparams(("arbitrary",)), name=name)(w, m, v, gparts)


def adamw_layers(w, m, v, gparts, *, name):
    depth, r, c = w.shape
    p = gparts[0].shape[0]
    tr = _adam_rows(r, c)

    def body(w_ref, m_ref, v_ref, *rest):
        g_refs, outs = rest[:depth], rest[depth:]
        layer = pl.program_id(0)
        for k in range(depth):
            @pl.when(layer == k)
            def _(k=k):
                _adam_update(w_ref, m_ref, v_ref, g_refs[k], *outs)

    spec = pl.BlockSpec((None, tr, c), lambda l, i: (l, i, 0))
    g_specs = [pl.BlockSpec((p, tr, c), functools.partial(lambda l, i, k: (0, jnp.where(l == k, i, 0), 0), k=k))
               for k in range(depth)]
    shp = jax.ShapeDtypeStruct((depth, r, c), F32)
    return pl.pallas_call(
        body, grid=(depth, r // tr),
        in_specs=[spec, spec, spec] + g_specs,
        out_specs=[spec] * 4, out_shape=[shp] * 4,
        compiler_params=_cparams(("arbitrary", "arbitrary")), name=name)(w, m, v, *gparts)


def _position():
    x, y, c = lax.axis_index("x"), lax.axis_index("y"), lax.axis_index("c")
    return x, y, c


def _dev_index(x, y, c):
    return 4 * x + 2 * y + c


def _gather_phases(ins, outs, send_sems, recv_sems, local_sems):
    n = len(ins)
    x, y, c = _position()
    me, sibling = (x, y, c), (x, y, 1 - c)
    chips = [(1 - x, y), (x, 1 - y), (1 - x, 1 - y)]

    def copy(a, k, block, to, own=False):
        slot = outs[a].at[_dev_index(*block)]
        return pltpu.make_async_remote_copy(
            src_ref=ins[a] if own else slot, dst_ref=slot,
            send_sem=send_sems.at[a * 7 + k], recv_sem=recv_sems.at[a * 7 + k],
            device_id=to, device_id_type=MESH)

    def mine(a):
        return pltpu.make_async_copy(ins[a], outs[a].at[_dev_index(*me)], local_sems.at[a])

    def first(a):
        return [copy(a, 0, me, sibling, True)] + [copy(a, 1 + j, me, (*chip, c), True) for j, chip in enumerate(chips)]

    def passed(a):
        return [copy(a, 4 + j, (*chip, c), sibling) for j, chip in enumerate(chips)]

    def start():
        for a in range(n):
            mine(a).start()
        for a in range(n):
            for cp in first(a):
                cp.start()

    def mid():
        for j, chip in enumerate(chips):
            for a in range(n):
                copy(a, 1 + j, (*chip, c), me).wait_recv()
                passed(a)[j].start()

    def finish():
        for a in range(n):
            copy(a, 0, sibling, me).wait_recv()
            for j, chip in enumerate(chips):
                copy(a, 4 + j, (*chip, 1 - c), me).wait_recv()
        for a in range(n):
            for cp in first(a) + passed(a):
                cp.wait_send()
            mine(a).wait()

    return start, mid, finish


def _scatter_phases(ins, outs, send_sems, recv_sems, local_sems):
    n = len(ins)
    x, y, c = _position()
    me = _dev_index(x, y, c)

    def peer(r):
        return (x ^ (r >> 2), y ^ ((r >> 1) & 1), c ^ (r & 1))

    def copy(a, r):
        to = peer(r)
        return pltpu.make_async_remote_copy(
            src_ref=ins[a].at[_dev_index(*to)], dst_ref=outs[a].at[me],
            send_sem=send_sems.at[a * 7 + r - 1], recv_sem=recv_sems.at[a * 7 + r - 1],
            device_id=to, device_id_type=MESH)

    def arrival(a, r):
        return pltpu.make_async_remote_copy(
            src_ref=ins[a].at[me], dst_ref=outs[a].at[_dev_index(*peer(r))],
            send_sem=send_sems.at[a * 7 + r - 1], recv_sem=recv_sems.at[a * 7 + r - 1],
            device_id=peer(r), device_id_type=MESH)

    def mine(a):
        return pltpu.make_async_copy(ins[a].at[me], outs[a].at[me], local_sems.at[a])

    def start():
        for a in range(n):
            mine(a).start()
        for r in range(1, N_DEV):
            for a in range(n):
                copy(a, r).start()

    def finish():
        for r in range(1, N_DEV):
            for a in range(n):
                arrival(a, r).wait_recv()
        for r in range(1, N_DEV):
            for a in range(n):
                copy(a, r).wait_send()
        for a in range(n):
            mine(a).wait()

    return start, None, finish


class Rider:
    def __init__(self, kind, arrs):
        self.kind, self.arrs, self.n = kind, list(arrs), len(arrs)
        lead = (N_DEV,) if kind == "gather" else ()
        self.out_shape = [jax.ShapeDtypeStruct(lead + a.shape, a.dtype) for a in self.arrs]
        self.scratch = [pltpu.SemaphoreType.DMA((7 * self.n,)), pltpu.SemaphoreType.DMA((7 * self.n,)),
                        pltpu.SemaphoreType.DMA((self.n,))]

    def phases(self, ins, outs, sems):
        make = _gather_phases if self.kind == "gather" else _scatter_phases
        return make(ins, outs, *sems)

    def emit(self, step, total, mid_step, ins, outs, sems):
        start, mid, finish = self.phases(ins, outs, sems)
        pl.when(step == 0)(start)
        if mid is not None:
            pl.when(step == mid_step)(mid)
        pl.when(step == total - 1)(finish)


def _standalone(rider, name):
    n = rider.n
    hbm = pl.BlockSpec(memory_space=pl.ANY)

    def body(*refs):
        start, mid, finish = rider.phases(refs[:n], refs[n:2 * n], refs[2 * n:])
        start()
        if mid is not None:
            mid()
        finish()

    return pl.pallas_call(body, out_shape=rider.out_shape, in_specs=[hbm] * n, out_specs=[hbm] * n,
                          scratch_shapes=rider.scratch, name=name)(*rider.arrs)


def all_gather(arrs, *, name):
    return _standalone(Rider("gather", arrs), name)


def scatter_parts(arrs, *, name):
    return _standalone(Rider("scatter", arrs), name)


def mod_exchange(c_all, w_ada, b_cols, *, name):
    depth, d, cols = w_ada.shape
    hbm = pl.BlockSpec(memory_space=pl.ANY)
    vmem = pl.BlockSpec(memory_space=pltpu.VMEM)

    def body(c_ref, w_ref, b_ref, out_ref, wbuf, sendbuf, send_sems, recv_sems, load_sem):
        x, y, c = _position()
        me = _dev_index(x, y, c)
        cv = c_ref[...]
        ca = cv * _sigmoid(cv)
        for l in range(depth):
            load = pltpu.make_async_copy(w_ref.at[l], wbuf, load_sem)
            load.start()
            load.wait()
            part = jnp.dot(ca, wbuf[...], preferred_element_type=F32,
                           precision=lax.Precision.HIGHEST) + b_ref[l:l + 1, :]
            for bi in range(N_DEV):
                sendbuf[bi, l:l + 1, :] = part[bi:bi + 1, :]

        def peer(r):
            return (x ^ (r >> 2), y ^ ((r >> 1) & 1), c ^ (r & 1))

        def copy(r):
            to = peer(r)
            return pltpu.make_async_remote_copy(
                src_ref=sendbuf.at[_dev_index(*to)], dst_ref=out_ref.at[me],
                send_sem=send_sems.at[r - 1], recv_sem=recv_sems.at[r - 1],
                device_id=to, device_id_type=MESH)

        def arrival(r):
            return pltpu.make_async_remote_copy(
                src_ref=sendbuf.at[me], dst_ref=out_ref.at[_dev_index(*peer(r))],
                send_sem=send_sems.at[r - 1], recv_sem=recv_sems.at[r - 1],
                device_id=peer(r), device_id_type=MESH)

        out_ref[me] = sendbuf[me]
        sends = [copy(r) for r in range(1, N_DEV)]
        for cp in sends:
            cp.start()
        for r in range(1, N_DEV):
            arrival(r).wait_recv()
        for cp in sends:
            cp.wait_send()

    return pl.pallas_call(
        body,
        out_shape=jax.ShapeDtypeStruct((N_DEV, depth, cols), F32),
        in_specs=[vmem, hbm, vmem], out_specs=vmem,
        scratch_shapes=[pltpu.VMEM((d, cols), F32), pltpu.VMEM((N_DEV, depth, cols), F32),
                        pltpu.SemaphoreType.DMA((7,)), pltpu.SemaphoreType.DMA((7,)), pltpu.SemaphoreType.DMA],
        compiler_params=pltpu.CompilerParams(vmem_limit_bytes=VMEM_LIMIT),
        name=name)(c_all, w_ada, b_cols)


def kernel(x, c, w_ada, b_ada, norm_pre, norm_post, w_in, lower_bounds, hgrn_norm, pool_w, pool_scale, w_proj_a, w_proj_b, w_out, loss_target, m_w_ada, m_b_ada, m_norm_pre, m_norm_post, m_w_in, m_lower_bounds, m_hgrn_norm, m_pool_w, m_pool_scale, m_w_proj_a, m_w_proj_b, m_w_out, v_w_ada, v_b_ada, v_norm_pre, v_norm_post, v_w_in, v_lower_bounds, v_hgrn_norm, v_pool_w, v_pool_scale, v_w_proj_a, v_w_proj_b, v_w_out):
    depth = w_in.shape[0]
    d = D_MODEL
    ada_cols = w_ada.shape[2]
    xi, yi, ci = _position()
    me = _dev_index(xi, yi, ci)
    xs = x[0]
    target = loss_target[0]
    ng = len(POOL_WINDOWS)

    def shards(l):
        w_in_l = w_in[l].astype(BF16)
        half = w_in_l.shape[1] // 2
        return [w_in_l[:, :half], w_in_l[:, half:], w_proj_a[l].astype(BF16), w_proj_b[l].astype(BF16),
                w_out[l].astype(BF16), pool_w[l].astype(BF16)]

    def other_weights(g_pa, g_pb, g_out, g_pool):
        return dict(
            pa=jnp.transpose(g_pa, (1, 0, 2)).reshape(WIDTH, d),
            pb=jnp.transpose(g_pb, (1, 0, 2)).reshape(WIDTH, d),
            w_out=g_out.reshape(d, d),
            pool=jnp.transpose(g_pool, (1, 0, 2, 3)).reshape(ng, POOL_GW, POOL_GW))

    w_in_full = [permute_w_in(*all_gather(shards(0)[:2], name="gather_w_in"), name="permute_w_in")]
    others_full = []
    gathered = []

    (c_all,) = all_gather([c], name="gather_c")
    c_all = c_all.reshape(N_DEV, d)
    b_cols = lax.dynamic_slice_in_dim(b_ada, me * ada_cols, ada_cols, axis=1)
    mod_parts = mod_exchange(c_all, w_ada, b_cols, name="mod_exchange")
    mod = jnp.transpose(mod_parts, (1, 0, 2)).reshape(depth, 3 * d)
    lb_all = lb_table(lower_bounds, name="lb_table")

    saved = []
    cur = xs
    for l in range(depth):
        shift, scale, gate = mod[l:l + 1, :d], mod[l:l + 1, d:2 * d], mod[l:l + 1, 2 * d:]
        h = prenorm_fwd(cur, norm_pre[l:l + 1], shift, scale, name="prenorm_fwd")
        head_cols = (GP_COLS, IN_COLS - GP_COLS)
        if l + 1 < depth:
            proj_gp, nxt_lo = matmul_nn(h, w_in_full[l], cols=(0, GP_COLS), tm=2048, tn=1024, out_dtype=BF16,
                                        rider=Rider("gather", shards(l + 1)[:1]), name="mm_w_in_gp_gather")
            proj_h, nxt_hi = matmul_nn(h, w_in_full[l], cols=head_cols, tm=2048, tn=1024, out_dtype=F32,
                                       rider=Rider("gather", shards(l + 1)[1:2]), name="mm_w_in_heads_gather")
            w_in_full.append(permute_w_in(nxt_lo, nxt_hi, name="permute_w_in"))
        else:
            proj_gp = matmul_nn(h, w_in_full[l], cols=(0, GP_COLS), tm=2048, tn=1024, out_dtype=BF16,
                                name="mm_w_in_gp")
            proj_h = matmul_nn(h, w_in_full[l], cols=head_cols, tm=2048, tn=1024, out_dtype=F32,
                               name="mm_w_in_heads")
        ride = (shards(0)[2:] if l == 0 else []) + (shards(l + 1)[2:] if l + 1 < depth else [])
        if ride:
            y_a, o_all, states, *got = hgrn_fwd(proj_h, lb_all[l:l + 1], hgrn_norm[l:l + 1],
                                                 rider=Rider("gather", ride), name="hgrn_fwd_gather%d" % len(ride))
            for k in range(0, len(got), 4):
                others_full.append(other_weights(*got[k:k + 4]))
        else:
            y_a, o_all, states = hgrn_fwd(proj_h, lb_all[l:l + 1], hgrn_norm[l:l + 1], name="hgrn_fwd")
        w = dict(w_in=w_in_full[l], **others_full[l])
        gathered.append(w)
        y_b, pooled, mixed = pool_fwd(proj_gp, w["pool"], pool_scale[l:l + 1], name="pool_fwd")
        pa, pb, merged = proj_gate_fwd(y_a, y_b, w["pa"], w["pb"], proj_gp, name="proj_gate_fwd")
        out, nxt = w_out_postnorm_fwd(merged, w["w_out"], cur, norm_post[l:l + 1], gate, name="w_out_postnorm_fwd")
        saved.append(dict(x=cur, h=h, proj_gp=proj_gp, proj_h=proj_h, y_a=y_a, o=o_all, states=states, y_b=y_b,
                          pooled=pooled, mixed=mixed, pa=pa, pb=pb, merged=merged, out=out, scale=scale, gate=gate))
        cur = nxt

    g, loss_part = loss_head(cur, target, name="loss_head")
    loss = lax.psum(loss_part[0, 0], ("x", "y", "c"))

    small = [None] * depth
    big_in = [None] * depth
    big_others = [None] * depth
    pend_in, pend_others = None, None
    for l in reversed(range(depth)):
        w, sv = gathered[l], saved[l]
        dout, dpa, dpb, dproj, dgate, dnpost = postnorm_gate_bwd(
            g, sv["out"], norm_post[l:l + 1], sv["gate"], w["w_out"], sv["proj_gp"], sv["pa"], sv["pb"],
            name="postnorm_gate_bwd")
        dw_out = matmul_tn(sv["merged"], dout, tm=2048, tn=1024, out_dtype=BF16, name="mm_w_out_dw")
        dya = matmul_nt(dpa, w["pa"], tm=2048, tn=2048, out_dtype=F32, name="mm_proj_a_dx")
        dyb = matmul_nt(dpb, w["pb"], tm=2048, tn=2048, out_dtype=F32, name="mm_proj_b_dx")
        dw_pa = matmul_tn(sv["y_a"], dpa, tm=2048, tn=2048, out_dtype=BF16, name="mm_proj_a_dw")
        dw_pb = matmul_tn(sv["y_b"], dpb, tm=2048, tn=2048, out_dtype=BF16, name="mm_proj_b_dw")
        dproj, dpool_w, dpool_scale = pool_bwd(dyb, sv["proj_gp"], sv["pooled"], sv["mixed"], w["pool"],
                                               pool_scale[l:l + 1], dproj, name="pool_bwd")
        by_owner = lambda t: jnp.transpose(t.reshape(WIDTH, N_DEV, d // N_DEV), (1, 0, 2))
        others = [by_owner(dw_pa), by_owner(dw_pb), dw_out.reshape(N_DEV, d // N_DEV, d),
                  jnp.transpose(dpool_w.astype(BF16).reshape(ng, N_DEV, POOL_GW // N_DEV, POOL_GW), (1, 0, 2, 3))]
        ride = (pend_others or []) + (others if l == 0 else [])
        if ride:
            dproj, dhn, dlb, *recv = hgrn_bwd(dya, sv["proj_h"], sv["o"], sv["states"], lb_all[l:l + 1],
                                              hgrn_norm[l:l + 1], dproj, rider=Rider("scatter", ride),
                                              name="hgrn_bwd_scatter%d" % len(ride))
            if pend_others:
                big_others[l + 1], recv = recv[:len(pend_others)], recv[len(pend_others):]
            if l == 0:
                big_others[0] = recv
        else:
            dproj, dhn, dlb = hgrn_bwd(dya, sv["proj_h"], sv["o"], sv["states"], lb_all[l:l + 1],
                                       hgrn_norm[l:l + 1], dproj, name="hgrn_bwd")
        above = Rider("scatter", [pend_in]) if pend_in is not None else None
        dx_args = (dproj, w["w_in"], sv["x"], norm_pre[l:l + 1], sv["scale"], g)
        if l > 0:
            if above:
                g, dshift, dscale, dnpre, big_in[l + 1] = w_in_dx_prenorm_bwd(*dx_args, rider=above,
                                                                              name="w_in_dx_prenorm_bwd_scatter")
            else:
                g, dshift, dscale, dnpre = w_in_dx_prenorm_bwd(*dx_args, name="w_in_dx_prenorm_bwd")
            dw_in = matmul_tn(sv["h"], dproj, tm=2048, tn=1024, out_dtype=BF16, name="mm_w_in_dw")
            pend_in, pend_others = unpermute_w_in(dw_in, name="unpermute_w_in"), others
        else:
            if above:
                dw_in, big_in[1] = matmul_tn(sv["h"], dproj, tm=2048, tn=1024, out_dtype=BF16, rider=above,
                                             name="mm_w_in_dw_scatter")
            else:
                dw_in = matmul_tn(sv["h"], dproj, tm=2048, tn=1024, out_dtype=BF16, name="mm_w_in_dw")
            g, dshift, dscale, dnpre, big_in[0] = w_in_dx_prenorm_bwd(
                *dx_args, rider=Rider("scatter", [unpermute_w_in(dw_in, name="unpermute_w_in")]),
                name="w_in_dx_prenorm_bwd_scatter")
        small[l] = jnp.concatenate([dshift, dscale, dgate, dnpre, dnpost, dlb, dhn, dpool_scale], axis=1)
    grad_x = g[None]
    big = [[big_in[l]] + list(big_others[l]) for l in range(depth)]

    small_mine = jnp.concatenate(small, axis=0)
    (small_all,) = all_gather([small_mine], name="gather_small")
    small_sum = sum_parts(small_all, name="sum_small")
    dmod_all = small_all[:, :, :3 * d]
    dmod_cols = jnp.transpose(lax.dynamic_slice_in_dim(dmod_all, me * ada_cols, ada_cols, axis=2), (1, 0, 2))
    g_w_ada = w_ada_grad(c_all, dmod_cols, name="w_ada_grad")
    off = 3 * d
    g_b_ada = small_sum[:, :off]
    g_npre = small_sum[:, off:off + d]
    g_npost = small_sum[:, off + d:off + 2 * d]
    g_lb_tab = small_sum[:, off + 2 * d:off + 2 * d + WIDTH]
    g_hn = small_sum[:, off + 2 * d + WIDTH:off + 2 * d + 2 * WIDTH]
    g_ps = small_sum[:, off + 2 * d + 2 * WIDTH:]
    g_lower = lb_table_bwd(lower_bounds, g_lb_tab, name="lb_table_bwd")

    def update(wt, mt, vt, gparts, shape2, name):
        outs = adamw(wt.reshape(shape2), mt.reshape(shape2), vt.reshape(shape2), gparts, name=name)
        return [o.reshape(wt.shape) for o in outs]

    def update_layers(wt, mt, vt, kind, name):
        shape3 = (depth, -1, wt.shape[-1])
        w3 = wt.reshape(shape3)
        gps = [big[l][kind].reshape((N_DEV,) + w3.shape[1:]) for l in range(depth)]
        outs = adamw_layers(w3, mt.reshape(shape3), vt.reshape(shape3), gps, name=name)
        return [o.reshape(wt.shape) for o in outs]

    def update_small(wt, mt, vt, gt, name):
        shape2 = (-1, wt.shape[-1])
        return update(wt, mt, vt, gt.reshape(shape2)[None], shape2, name)

    res = {
        "w_ada": update_small(w_ada, m_w_ada, v_w_ada, g_w_ada, "adamw_w_ada"),
        "b_ada": update_small(b_ada, m_b_ada, v_b_ada, g_b_ada, "adamw_b_ada"),
        "norm_pre": update_small(norm_pre, m_norm_pre, v_norm_pre, g_npre, "adamw_norm_pre"),
        "norm_post": update_small(norm_post, m_norm_post, v_norm_post, g_npost, "adamw_norm_post"),
        "w_in": update_layers(w_in, m_w_in, v_w_in, 0, "adamw_w_in"),
        "lower_bounds": update_small(lower_bounds, m_lower_bounds, v_lower_bounds, g_lower, "adamw_lower_bounds"),
        "hgrn_norm": update_small(hgrn_norm, m_hgrn_norm, v_hgrn_norm, g_hn, "adamw_hgrn_norm"),
        "pool_w": update_layers(pool_w, m_pool_w, v_pool_w, 4, "adamw_pool_w"),
        "pool_scale": update_small(pool_scale, m_pool_scale, v_pool_scale, g_ps, "adamw_pool_scale"),
        "w_proj_a": update_layers(w_proj_a, m_w_proj_a, v_w_proj_a, 1, "adamw_w_proj_a"),
        "w_proj_b": update_layers(w_proj_b, m_w_proj_b, v_w_proj_b, 2, "adamw_w_proj_b"),
        "w_out": update_layers(w_out, m_w_out, v_w_out, 3, "adamw_w_out"),
    }
    order = ["w_ada", "b_ada", "norm_pre", "norm_post", "w_in", "lower_bounds", "hgrn_norm", "pool_w",
             "pool_scale", "w_proj_a", "w_proj_b", "w_out"]
    outs = [loss, grad_x]
    for k in range(4):
        outs += [res[nm][k] for nm in order]
    return tuple(outs)
```

```python
import functools

import jax
import jax.numpy as jnp
from jax import lax
from jax.experimental import pallas as pl
from jax.experimental.pallas import tpu as pltpu

F32 = jnp.float32
BF16 = jnp.bfloat16
MESH = pl.DeviceIdType.MESH

N_DEV = 8
EPS = 1e-6
MIN_FORGET = 1e-30
D_MODEL = 2048
HEADS = 8
HEAD_DIM = 128
CHUNK = 64
SUB = 16
N_SUB = CHUNK // SUB
WIDTH = 1024
POOL_WINDOWS = (2, 4, 8, 16)
POOL_GW = 256
HALO = 16
IN_COLS = 10240
LANE = 128
N_COLBLK = IN_COLS // LANE
GATE_COLS = 4096
HEAD_COLS = 4 * HEAD_DIM
POOL_COLS = 2 * POOL_GW
GP_COLS = GATE_COLS + len(POOL_WINDOWS) * POOL_COLS
MAX_EXP = 80.0

ADAM_LR = 0.001
ADAM_B1 = 0.9
ADAM_B2 = 0.999
ADAM_EPS = 1e-08
ADAM_WD = 0.01
ADAM_STEP = 10

VMEM_LIMIT = 62 * 1024 * 1024


def _cparams(sem=None):
    return pltpu.CompilerParams(dimension_semantics=sem, vmem_limit_bytes=VMEM_LIMIT)


def _sigmoid(v):
    return 1.0 / (1.0 + jnp.exp(-v))


def _dot(a, b):
    return jnp.dot(a, b, preferred_element_type=F32)


def _dot_nt(a, b):
    return lax.dot_general(a, b, (((1,), (1,)), ((), ())), preferred_element_type=F32)


def _dot_tn(a, b):
    return lax.dot_general(a, b, (((0,), (0,)), ((), ())), preferred_element_type=F32)


def _colsum(v):
    return jnp.sum(v, axis=0, keepdims=True)


def _rowmean(v):
    return jnp.mean(v, axis=-1, keepdims=True)


def _orig_block_static(n):
    if n < 32:
        return n + 48
    if n < 48:
        m = n - 32
        t = m % 4
        return 32 + 2 * (m // 4) + (t % 2) + 8 * (t // 2)
    m = n - 48
    return 8 * (m % 4) + m // 4


def _accumulate(step, steps, prod, o_ref, acc_ref):
    if steps == 1:
        o_ref[...] = prod.astype(o_ref.dtype)
        return

    @pl.when(step == 0)
    def _():
        acc_ref[...] = prod

    @pl.when(step > 0)
    def _():
        acc_ref[...] += prod

    @pl.when(step == steps - 1)
    def _():
        o_ref[...] = acc_ref[...].astype(o_ref.dtype)


def _matmul_call(dot, a, b, *, grid, in_specs, out_spec, out_shape, acc_shape, steps, rider, name):
    nr = rider.n if rider else 0
    hbm = pl.BlockSpec(memory_space=pl.ANY)
    has_acc = steps > 1

    def body(*refs):
        a_ref, b_ref = refs[:2]
        o_ref = refs[2 + nr]
        scratch = refs[3 + 2 * nr:]
        if rider:
            total = grid[0] * grid[1]
            rider.emit(pl.program_id(0) * grid[1] + pl.program_id(1), total, _rider_mid_step(total),
                       refs[2:2 + nr], refs[3 + nr:3 + 2 * nr], scratch[1 if has_acc else 0:])
        _accumulate(pl.program_id(1), steps, dot(a_ref[...], b_ref[...]), o_ref, scratch[0] if has_acc else None)

    outs = pl.pallas_call(
        body, grid=grid,
        in_specs=in_specs + [hbm] * nr, out_specs=[out_spec] + [hbm] * nr,
        out_shape=[out_shape] + (rider.out_shape if rider else []),
        scratch_shapes=([pltpu.VMEM(acc_shape, F32)] if has_acc else []) + (rider.scratch if rider else []),
        compiler_params=_cparams(("arbitrary", "arbitrary")),
        name=name)(a, b, *(rider.arrs if rider else []))
    return outs if rider else outs[0]


def matmul_nn(a, b, *, tm, tn, out_dtype, name, rider=None, cols=None):
    m, k = a.shape
    col0, n = cols if cols else (0, b.shape[1])
    tm = min(tm, m)
    j0 = col0 // tn
    return _matmul_call(
        _dot, a, b, grid=(n // tn, m // tm),
        in_specs=[pl.BlockSpec((tm, k), lambda j, i: (i, 0)), pl.BlockSpec((k, tn), lambda j, i: (0, j0 + j))],
        out_spec=pl.BlockSpec((tm, tn), lambda j, i: (i, j)),
        out_shape=jax.ShapeDtypeStruct((m, n), out_dtype), acc_shape=None, steps=1, rider=rider, name=name)


def matmul_nt(a, b, *, tm, tn, out_dtype, name, rider=None):
    m, n = a.shape
    k = b.shape[0]
    tm = min(tm, m)
    return _matmul_call(
        _dot_nt, a, b, grid=(m // tm, n // tn),
        in_specs=[pl.BlockSpec((tm, tn), lambda i, j: (i, j)), pl.BlockSpec((k, tn), lambda i, j: (0, j))],
        out_spec=pl.BlockSpec((tm, k), lambda i, j: (i, 0)),
        out_shape=jax.ShapeDtypeStruct((m, k), out_dtype), acc_shape=(tm, k), steps=n // tn, rider=rider, name=name)


def matmul_tn(a, b, *, tm, tn, out_dtype, name, rider=None):
    m, k = a.shape
    n = b.shape[1]
    tm = min(tm, m)
    return _matmul_call(
        _dot_tn, a, b, grid=(n // tn, m // tm),
        in_specs=[pl.BlockSpec((tm, k), lambda j, i: (i, 0)), pl.BlockSpec((tm, tn), lambda j, i: (i, j))],
        out_spec=pl.BlockSpec((k, tn), lambda j, i: (0, j)),
        out_shape=jax.ShapeDtypeStruct((k, n), out_dtype), acc_shape=(k, tn), steps=m // tm, rider=rider, name=name)


def permute_w_in(staged_lo, staged_hi, *, name):
    k = staged_lo.shape[1]
    own = IN_COLS // N_DEV
    half = own // 2
    tr = min(256, k)

    def body(lo_ref, hi_ref, o_ref):
        for nb in range(N_COLBLK):
            dev, col = divmod(_orig_block_static(nb) * LANE, own)
            src = lo_ref[dev, :, col:col + LANE] if col < half else hi_ref[dev, :, col - half:col - half + LANE]
            o_ref[:, nb * LANE:(nb + 1) * LANE] = src

    spec = pl.BlockSpec((N_DEV, tr, half), lambda i: (0, i, 0))
    return pl.pallas_call(
        body, grid=(k // tr,),
        in_specs=[spec, spec],
        out_specs=pl.BlockSpec((tr, IN_COLS), lambda i: (i, 0)),
        out_shape=jax.ShapeDtypeStruct((k, IN_COLS), staged_lo.dtype),
        compiler_params=_cparams(("arbitrary",)), name=name)(staged_lo, staged_hi)


def unpermute_w_in(dw, *, name):
    k = dw.shape[0]
    own = IN_COLS // N_DEV
    tr = min(256, k)

    def body(i_ref, o_ref):
        for nb in range(N_COLBLK):
            dev, col = divmod(_orig_block_static(nb) * LANE, own)
            o_ref[dev, :, col:col + LANE] = i_ref[:, nb * LANE:(nb + 1) * LANE]

    return pl.pallas_call(
        body, grid=(k // tr,),
        in_specs=[pl.BlockSpec((tr, IN_COLS), lambda i: (i, 0))],
        out_specs=pl.BlockSpec((N_DEV, tr, own), lambda i: (0, i, 0)),
        out_shape=jax.ShapeDtypeStruct((N_DEV, k, own), dw.dtype),
        compiler_params=_cparams(("arbitrary",)), name=name)(dw)


def _row_tile(s):
    return min(256, s)


def _norm_tile(s):
    return min(512, s)


def _row_spec(t, w, col=0):
    return pl.BlockSpec((t, w), lambda i: (i, col))


def _vec_spec(w):
    return pl.BlockSpec((1, w), lambda i: (0, 0))


def prenorm_fwd(x, gain, shift, scale, *, name):
    s, d = x.shape
    t = _norm_tile(s)

    def body(x_ref, g_ref, sh_ref, sc_ref, h_ref):
        xv = x_ref[...]
        r = lax.rsqrt(_rowmean(xv * xv) + EPS)
        h_ref[...] = ((xv * r) * g_ref[...] * (1.0 + sc_ref[...]) + sh_ref[...]).astype(h_ref.dtype)

    return pl.pallas_call(
        body, grid=(s // t,),
        in_specs=[_row_spec(t, d), _vec_spec(d), _vec_spec(d), _vec_spec(d)],
        out_specs=_row_spec(t, d), out_shape=jax.ShapeDtypeStruct((s, d), BF16),
        compiler_params=_cparams(("arbitrary",)), name=name)(x, gain, shift, scale)


def w_in_dx_prenorm_bwd(dproj, w_in, x, gain, scale, g_res, *, rider=None, name):
    s, n = dproj.shape
    d = w_in.shape[0]
    tm = min(512, s)
    tn = 2048
    steps = n // tn
    nrow = s // tm
    nr = rider.n if rider else 0
    hbm = pl.BlockSpec(memory_space=pl.ANY)

    def body(*refs):
        dp_ref, w_ref, x_ref, g_ref, sc_ref, gr_ref = refs[:6]
        dx_ref, dsh_ref, dsc_ref, dg_ref = refs[6 + nr:10 + nr]
        acc_ref = refs[10 + 2 * nr]
        i, j = pl.program_id(0), pl.program_id(1)
        if rider:
            total = nrow * steps
            rider.emit(i * steps + j, total, _rider_mid_step(total),
                       refs[6:6 + nr], refs[10 + nr:10 + 2 * nr], refs[11 + 2 * nr:])
        prod = _dot_nt(dp_ref[...], w_ref[...])

        @pl.when(j == 0)
        def _():
            acc_ref[...] = prod

        @pl.when(j > 0)
        def _():
            acc_ref[...] += prod

        @pl.when(j == steps - 1)
        def _():
            @pl.when(i == 0)
            def _():
                dsh_ref[...] = jnp.zeros_like(dsh_ref)
                dsc_ref[...] = jnp.zeros_like(dsc_ref)
                dg_ref[...] = jnp.zeros_like(dg_ref)

            gain_v = g_ref[...]
            one_sc = 1.0 + sc_ref[...]
            sub = min(256, tm)

            def piece(k, carry):
                rows = pl.ds(pl.multiple_of(k * sub, sub), sub)
                xv = x_ref[rows, :]
                dhv = acc_ref[rows, :]
                r = lax.rsqrt(_rowmean(xv * xv) + EPS)
                xn = xv * r
                dyn = dhv * one_sc
                dxn = dyn * gain_v
                dx_ref[rows, :] = r * (dxn - xn * _rowmean(dxn * xn)) + gr_ref[rows, :]
                dsh_ref[...] += _colsum(dhv)
                dsc_ref[...] += _colsum(dhv * (xn * gain_v))
                dg_ref[...] += _colsum(dyn * xn)
                return carry

            lax.fori_loop(0, tm // sub, piece, 0)

    rows = pl.BlockSpec((tm, d), lambda i, j: (i, 0))
    vec_in = pl.BlockSpec((1, d), lambda i, j: (0, 0))
    vec = jax.ShapeDtypeStruct((1, d), F32)
    outs = pl.pallas_call(
        body, grid=(nrow, steps),
        in_specs=[pl.BlockSpec((tm, tn), lambda i, j: (i, j)), pl.BlockSpec((d, tn), lambda i, j: (0, j)),
                  rows, vec_in, vec_in, rows] + [hbm] * nr,
        out_specs=[rows, vec_in, vec_in, vec_in] + [hbm] * nr,
        out_shape=[jax.ShapeDtypeStruct((s, d), F32), vec, vec, vec] + (rider.out_shape if rider else []),
        scratch_shapes=[pltpu.VMEM((tm, d), F32)] + (rider.scratch if rider else []),
        compiler_params=_cparams(("arbitrary", "arbitrary")),
        name=name)(dproj, w_in, x, gain, scale, g_res, *(rider.arrs if rider else []))
    return outs


def w_out_postnorm_fwd(merged, w_out, x, gain, gate, nxt=None, *, name):
    s, d = x.shape
    t = _norm_tile(s)

    def body(m_ref, w_ref, x_ref, g_ref, gt_ref, *rest):
        o_ref, y_ref = rest[-3:-1] if nxt else rest
        ov = _dot(m_ref[...], w_ref[...])
        o_ref[...] = ov
        r = lax.rsqrt(_rowmean(ov * ov) + EPS)
        y = x_ref[...] + gt_ref[...] * ((ov * r) * g_ref[...])
        y_ref[...] = y
        if nxt:
            gn_ref, sh_ref, sc_ref, h_ref = rest[0], rest[1], rest[2], rest[-1]
            rn = lax.rsqrt(_rowmean(y * y) + EPS)
            h_ref[...] = ((y * rn) * gn_ref[...] * (1.0 + sc_ref[...]) + sh_ref[...]).astype(h_ref.dtype)

    shp = jax.ShapeDtypeStruct((s, d), F32)
    extra = list(nxt) if nxt else []
    return pl.pallas_call(
        body, grid=(s // t,),
        in_specs=[_row_spec(t, d), _full_spec((d, d)), _row_spec(t, d), _vec_spec(d), _vec_spec(d)]
        + [_vec_spec(d)] * len(extra),
        out_specs=[_row_spec(t, d)] * (3 if nxt else 2),
        out_shape=[shp, shp] + ([jax.ShapeDtypeStruct((s, d), BF16)] if nxt else []),
        compiler_params=_cparams(("arbitrary",)), name=name)(merged, w_out, x, gain, gate, *extra)


def loss_head(y, target, *, name):
    s, d = y.shape
    t = _norm_tile(s)
    steps = s // t

    def body(y_ref, t_ref, dy_ref, loss_ref, acc_ref):
        i = pl.program_id(0)
        err = y_ref[...] - t_ref[...]
        dy_ref[...] = err * (1.0 / d)
        part = _colsum(err * err)

        @pl.when(i == 0)
        def _():
            acc_ref[...] = part

        @pl.when(i > 0)
        def _():
            acc_ref[...] += part

        @pl.when(i == steps - 1)
        def _():
            loss_ref[...] = jnp.sum(acc_ref[...], axis=1, keepdims=True) * (0.5 / d)

    return pl.pallas_call(
        body, grid=(steps,),
        in_specs=[_row_spec(t, d), _row_spec(t, d)],
        out_specs=[_row_spec(t, d), pl.BlockSpec((1, 1), lambda i: (0, 0))],
        out_shape=[jax.ShapeDtypeStruct((s, d), F32), jax.ShapeDtypeStruct((1, 1), F32)],
        scratch_shapes=[pltpu.VMEM((1, d), F32)],
        compiler_params=_cparams(("arbitrary",)), name=name)(y, target)


def _full_spec(shape):
    return pl.BlockSpec(shape, lambda i: (0,) * len(shape))


def proj_gate_fwd(y_a, y_b, w_pa, w_pb, proj, *, name):
    s, width = y_a.shape
    d = w_pa.shape[1]
    t = _norm_tile(s)

    def body(ya_ref, yb_ref, wa_ref, wb_ref, ga_ref, gb_ref, pa_ref, pb_ref, m_ref):
        pa = _dot(ya_ref[...], wa_ref[...])
        pb = _dot(yb_ref[...], wb_ref[...])
        pa_ref[...] = pa.astype(pa_ref.dtype)
        pb_ref[...] = pb.astype(pb_ref.dtype)
        m_ref[...] = (_sigmoid(ga_ref[...].astype(F32)) * pa
                      + _sigmoid(gb_ref[...].astype(F32)) * pb).astype(m_ref.dtype)

    out = jax.ShapeDtypeStruct((s, d), BF16)
    return pl.pallas_call(
        body, grid=(s // t,),
        in_specs=[_row_spec(t, width), _row_spec(t, width), _full_spec((width, d)), _full_spec((width, d)),
                  _row_spec(t, d, 0), _row_spec(t, d, 1)],
        out_specs=[_row_spec(t, d)] * 3, out_shape=[out] * 3,
        compiler_params=_cparams(("arbitrary",)), name=name)(y_a, y_b, w_pa, w_pb, proj, proj)


def postnorm_gate_bwd(g, out, gain, gate, w_out, proj, pa, pb, *, name):
    s, d = pa.shape
    t = _row_tile(s)

    def body(g_ref, o_ref, gn_ref, gt_ref, w_ref, ga_ref, gb_ref, pa_ref, pb_ref,
             do_ref, dpa_ref, dpb_ref, dp_ref, dgt_ref, dgn_ref):
        i = pl.program_id(0)
        ov = o_ref[...]
        gv = g_ref[...]
        r = lax.rsqrt(_rowmean(ov * ov) + EPS)
        on = ov * r
        gain_v = gn_ref[...]
        dn = gv * gt_ref[...]
        don = dn * gain_v
        dout = (r * (don - on * _rowmean(don * on))).astype(BF16)
        do_ref[...] = dout
        p_gt = _colsum(gv * (on * gain_v))
        p_gn = _colsum(dn * on)

        @pl.when(i == 0)
        def _():
            dgt_ref[...] = p_gt
            dgn_ref[...] = p_gn

        @pl.when(i > 0)
        def _():
            dgt_ref[...] += p_gt
            dgn_ref[...] += p_gn

        dm = _dot_nt(dout, w_ref[...])
        sa = _sigmoid(ga_ref[...].astype(F32))
        sb = _sigmoid(gb_ref[...].astype(F32))
        dpa_ref[...] = (dm * sa).astype(dpa_ref.dtype)
        dpb_ref[...] = (dm * sb).astype(dpb_ref.dtype)
        dp_ref[:, :d] = (dm * pa_ref[...].astype(F32) * sa * (1.0 - sa)).astype(dp_ref.dtype)
        dp_ref[:, d:] = (dm * pb_ref[...].astype(F32) * sb * (1.0 - sb)).astype(dp_ref.dtype)

    act = jax.ShapeDtypeStruct((s, d), BF16)
    vec = jax.ShapeDtypeStruct((1, d), F32)
    return pl.pallas_call(
        body, grid=(s // t,),
        in_specs=[_row_spec(t, d), _row_spec(t, d), _vec_spec(d), _vec_spec(d), _full_spec((d, d)),
                  _row_spec(t, d, 0), _row_spec(t, d, 1), _row_spec(t, d), _row_spec(t, d)],
        out_specs=[_row_spec(t, d), _row_spec(t, d), _row_spec(t, d), _row_spec(t, 2 * d, 0),
                   _vec_spec(d), _vec_spec(d)],
        out_shape=[act, act, act, jax.ShapeDtypeStruct((s, IN_COLS), BF16), vec, vec],
        compiler_params=_cparams(("arbitrary",)), name=name)(g, out, gain, gate, w_out, proj, proj, pa, pb)


def _pool_tile(s):
    return min(512, s)


def pool_fwd(proj, pw, ps, *, name):
    s = proj.shape[0]
    t = _pool_tile(s)
    pool_blk = GATE_COLS // (len(POOL_WINDOWS) * POOL_COLS)

    def body(p_ref, halo_ref, pw_ref, ps_ref, yb_ref, pooled_ref, mixed_ref):
        i = pl.program_id(0)
        halo = jnp.where(i == 0, 0.0, halo_ref[...].astype(F32))
        row = i * t + lax.broadcasted_iota(jnp.int32, (t, 1), 0)
        for g, w in enumerate(POOL_WINDOWS):
            vb = p_ref[:, g * POOL_COLS:g * POOL_COLS + POOL_GW].astype(F32)
            zb = p_ref[:, g * POOL_COLS + POOL_GW:(g + 1) * POOL_COLS].astype(F32)
            acc = jnp.concatenate([halo[:, g * POOL_COLS:g * POOL_COLS + POOL_GW], vb], axis=0)
            sh = 1
            while sh < w:
                acc = acc + pltpu.roll(acc, sh, axis=0)
                sh *= 2
            cnt = jnp.minimum(row + 1, w).astype(F32)
            pooled = acc[HALO:, :] / cnt - vb
            mixed = _dot(pooled.astype(BF16), pw_ref[g])
            cols = slice(g * POOL_GW, (g + 1) * POOL_GW)
            yb = mixed * ps_ref[:, cols] * (zb * _sigmoid(zb))
            yb_ref[:, cols] = yb.astype(yb_ref.dtype)
            pooled_ref[:, cols] = pooled.astype(pooled_ref.dtype)
            mixed_ref[:, cols] = mixed

    wide = len(POOL_WINDOWS) * POOL_COLS
    return pl.pallas_call(
        body, grid=(s // t,),
        in_specs=[pl.BlockSpec((t, wide), lambda i: (i, pool_blk)),
                  pl.BlockSpec((HALO, wide), lambda i: (jnp.maximum(i * (t // HALO) - 1, 0), pool_blk)),
                  pl.BlockSpec((len(POOL_WINDOWS), POOL_GW, POOL_GW), lambda i: (0, 0, 0)),
                  _vec_spec(WIDTH)],
        out_specs=[_row_spec(t, WIDTH)] * 3,
        out_shape=[jax.ShapeDtypeStruct((s, WIDTH), BF16), jax.ShapeDtypeStruct((s, WIDTH), BF16),
                   jax.ShapeDtypeStruct((s, WIDTH), F32)],
        compiler_params=_cparams(("arbitrary",)), name=name)(proj, proj, pw, ps)


def pool_bwd(dyb, proj, pooled, mixed, pw, ps, dproj, *, name):
    s = proj.shape[0]
    t = _pool_tile(s)
    nblk = s // t
    ng = len(POOL_WINDOWS)
    wide = ng * POOL_COLS
    pool_blk = GATE_COLS // wide

    def body(dy_ref, p_ref, pooled_ref, mixed_ref, pw_ref, ps_ref, dp_any, dp_ref, dpw_ref, dps_ref, carry):
        del dp_any
        i = pl.program_id(0)
        ii = nblk - 1 - i

        @pl.when(i == 0)
        def _():
            carry[...] = jnp.zeros_like(carry)
            dpw_ref[...] = jnp.zeros_like(dpw_ref)
            dps_ref[...] = jnp.zeros_like(dps_ref)

        row = ii * t + lax.broadcasted_iota(jnp.int32, (t, 1), 0)
        for g, w in enumerate(POOL_WINDOWS):
            cols = slice(g * POOL_GW, (g + 1) * POOL_GW)
            zb = p_ref[:, g * POOL_COLS + POOL_GW:(g + 1) * POOL_COLS].astype(F32)
            dy = dy_ref[:, cols]
            mx = mixed_ref[:, cols]
            sc = ps_ref[:, cols]
            sg = _sigmoid(zb)
            dzb = dy * (mx * sc) * (sg * (1.0 + zb * (1.0 - sg)))
            dpm = dy * (zb * sg)
            dps_ref[:, cols] += _colsum(dpm * mx)
            dmixed = (dpm * sc).astype(BF16)
            dpooled = _dot_nt(dmixed, pw_ref[g])
            dpw_ref[g] += _dot_tn(pooled_ref[:, cols], dmixed)
            cnt = jnp.minimum(row + 1, w).astype(F32)
            u = dpooled / cnt
            acc = jnp.concatenate([u, carry[:, cols]], axis=0)
            sh = 1
            while sh < w:
                acc = acc + pltpu.roll(acc, t + HALO - sh, axis=0)
                sh *= 2
            carry[:, cols] = u[:HALO, :]
            dp_ref[:, g * POOL_COLS:g * POOL_COLS + POOL_GW] = (acc[:t, :] - dpooled).astype(dp_ref.dtype)
            dp_ref[:, g * POOL_COLS + POOL_GW:(g + 1) * POOL_COLS] = dzb.astype(dp_ref.dtype)

    rev = lambda i: (nblk - 1 - i, 0)
    return pl.pallas_call(
        body, grid=(nblk,),
        in_specs=[pl.BlockSpec((t, WIDTH), rev),
                  pl.BlockSpec((t, wide), lambda i: (nblk - 1 - i, pool_blk)),
                  pl.BlockSpec((t, WIDTH), rev), pl.BlockSpec((t, WIDTH), rev),
                  pl.BlockSpec((ng, POOL_GW, POOL_GW), lambda i: (0, 0, 0)),
                  _vec_spec(WIDTH),
                  pl.BlockSpec(memory_space=pl.ANY)],
        out_specs=[pl.BlockSpec((t, wide), lambda i: (nblk - 1 - i, pool_blk)),
                   pl.BlockSpec((ng, POOL_GW, POOL_GW), lambda i: (0, 0, 0)),
                   _vec_spec(WIDTH)],
        out_shape=[jax.ShapeDtypeStruct(dproj.shape, dproj.dtype),
                   jax.ShapeDtypeStruct((ng, POOL_GW, POOL_GW), F32),
                   jax.ShapeDtypeStruct((1, WIDTH), F32)],
        scratch_shapes=[pltpu.VMEM((HALO, WIDTH), F32)],
        input_output_aliases={6: 0},
        compiler_params=_cparams(("arbitrary",)), name=name)(dyb, proj, pooled, mixed, pw, ps, dproj)


def _hgrn_tile(s):
    return min(2048, s)


def _chunk_consts():
    tt = lax.broadcasted_iota(jnp.int32, (CHUNK, CHUNK), 0)
    ss = lax.broadcasted_iota(jnp.int32, (CHUNK, CHUNK), 1)
    within = (ss <= tt) & (ss // SUB == tt // SUB)
    before = ss < (tt // SUB) * SUB
    cums = jnp.concatenate([within.astype(F32), before.astype(F32)], axis=0).astype(BF16)
    causal = ss <= tt
    upper = (ss >= tt).astype(F32).astype(BF16)
    row = lax.broadcasted_iota(jnp.int32, (CHUNK, 1), 0)
    return cums, causal, upper, row


def _dot_split(mat01, v):
    hi = v.astype(BF16)
    r1 = v - hi.astype(F32)
    mid = r1.astype(BF16)
    lo = (r1 - mid.astype(F32)).astype(BF16)
    return _dot(mat01, hi) + _dot(mat01, mid) + _dot(mat01, lo)


def _hgrn_chunks(qas, fas, lb, cums, row):
    gates = [_hgrn_gates(qa, fa, lb) for qa, fa in zip(qas, fas)]
    cbs = [_dot_split(cums, g["lf"]) for g in gates]
    return [_hgrn_decay(g, cb, row) for g, cb in zip(gates, cbs)]


def _hgrn_gates(qa, fa, lb):
    sq = _sigmoid(qa)
    sa = _sigmoid(fa)
    sna = 1.0 - sa
    oml = 1.0 - lb
    f = lb + oml * sa
    fc = jnp.maximum(f, MIN_FORGET)
    return dict(sq=sq, q=qa * sq, sa=sa, sna=sna, oml=oml, f=f, fc=fc, lf=jnp.log(fc), k=oml * sna)


def _hgrn_decay(g, cb, row):
    sq, q, sa, sna, oml, f, fc, k = (g[n] for n in ("sq", "q", "sa", "sna", "oml", "f", "fc", "k"))
    c = cb[:CHUNK]
    bt = cb[CHUNK:]
    ec = jnp.exp(c)
    enc = jnp.exp(jnp.minimum(-c, MAX_EXP))
    qt = q * ec
    kt = k * enc
    dms, lhs, rhs = [], [], []
    for j in range(N_SUB):
        bj = bt[j * SUB:j * SUB + 1, :]
        dm = jnp.where(row >= j * SUB, jnp.exp(jnp.minimum(bt - bj, 0.0)), 0.0)
        dms.append(dm)
        lhs.append(qt * dm)
        rhs.append(jnp.where(row // SUB == j, kt, 0.0))
    lhs = jnp.concatenate(lhs, axis=1).astype(BF16)
    rhs = jnp.concatenate(rhs, axis=1).astype(BF16)
    b = bt + c
    bl = b[CHUNK - 1:CHUNK, :]
    ebl = jnp.exp(bl)
    edec = jnp.exp(bl - b)
    eb = ec * dms[0]
    return dict(sq=sq, q=q, sa=sa, sna=sna, oml=oml, f=f, fc=fc, k=k, ec=ec, enc=enc, dms=dms,
                lhs=lhs, rhs=rhs, ebl=ebl, edec=edec, eb=eb, qd=q * eb, kdec=k * edec)


def _rider_mid_step(total):
    return total - max(1, total // 8)


def hgrn_fwd(proj, lb, hn, *, rider=None, name):
    s = proj.shape[0]
    t = _hgrn_tile(s)
    nblk = s // t
    ncht = t // CHUNK
    nr = rider.n if rider else 0
    hbm = pl.BlockSpec(memory_space=pl.ANY)

    def body(*refs):
        p_ref, lb_ref, hn_ref = refs[:3]
        ya_ref, o_ref, st_ref = refs[3 + nr:6 + nr]
        state = refs[6 + 2 * nr]
        i = pl.program_id(1)
        if rider:
            total = HEADS * nblk
            rider.emit(pl.program_id(0) * nblk + i, total, _rider_mid_step(total),
                       refs[3:3 + nr], refs[6 + nr:6 + 2 * nr], refs[7 + 2 * nr:])

        @pl.when(i == 0)
        def _():
            state[...] = jnp.zeros_like(state)

        cums, causal, _, row = _chunk_consts()
        lbv = lb_ref[...]
        hnv = hn_ref[...]

        rows = [slice(ci * CHUNK, (ci + 1) * CHUNK) for ci in range(ncht)]
        pres = _hgrn_chunks([p_ref[r, 0:HEAD_DIM] for r in rows], [p_ref[r, HEAD_DIM:2 * HEAD_DIM] for r in rows],
                            lbv, cums, row)
        vas = [p_ref[r, 2 * HEAD_DIM:3 * HEAD_DIM].astype(BF16) for r in rows]
        scores = [jnp.where(causal, _dot_nt(pre["lhs"], pre["rhs"]), 0.0).astype(BF16) for pre in pres]
        intra = [_dot(a, va) for a, va in zip(scores, vas)]
        qds = [pre["qd"].astype(BF16) for pre in pres]
        kdecs = [pre["kdec"].astype(BF16) for pre in pres]
        st = state[...]
        outs = []
        for ci in range(ncht):
            stb = st.astype(BF16)
            st_ref[ci, 0] = stb
            outs.append(intra[ci] + _dot_nt(qds[ci], stb))
            st = st * pres[ci]["ebl"] + _dot_tn(vas[ci], kdecs[ci])
        state[...] = st
        for r, o in zip(rows, outs):
            za = p_ref[r, 3 * HEAD_DIM:4 * HEAD_DIM]
            o_ref[r, :] = o
            ya_ref[r, :] = ((o * lax.rsqrt(_rowmean(o * o) + EPS)) * hnv * (za * _sigmoid(za))).astype(ya_ref.dtype)

    return pl.pallas_call(
        body, grid=(HEADS, nblk),
        in_specs=[pl.BlockSpec((t, HEAD_COLS), lambda h, i: (i, h)),
                  pl.BlockSpec((1, HEAD_DIM), lambda h, i: (0, h)),
                  pl.BlockSpec((1, HEAD_DIM), lambda h, i: (0, h))] + [hbm] * nr,
        out_specs=[pl.BlockSpec((t, HEAD_DIM), lambda h, i: (i, h)),
                   pl.BlockSpec((t, HEAD_DIM), lambda h, i: (i, h)),
                   pl.BlockSpec((ncht, 1, HEAD_DIM, HEAD_DIM), lambda h, i: (i, h, 0, 0))] + [hbm] * nr,
        out_shape=[jax.ShapeDtypeStruct((s, WIDTH), BF16), jax.ShapeDtypeStruct((s, WIDTH), F32),
                   jax.ShapeDtypeStruct((s // CHUNK, HEADS, HEAD_DIM, HEAD_DIM), BF16)]
        + (rider.out_shape if rider else []),
        scratch_shapes=[pltpu.VMEM((HEAD_DIM, HEAD_DIM), F32)] + (rider.scratch if rider else []),
        compiler_params=_cparams(("arbitrary", "arbitrary")), name=name)(proj, lb, hn, *(rider.arrs if rider else []))


def hgrn_bwd(dya, proj, o_all, states, lb, hn, dproj, *, rider=None, name):
    s = proj.shape[0]
    t = _hgrn_tile(s)
    nblk = s // t
    ncht = t // CHUNK
    head_blk0 = GP_COLS // HEAD_COLS
    nr = rider.n if rider else 0
    hbm = pl.BlockSpec(memory_space=pl.ANY)

    def body(*refs):
        dy_ref, p_ref, o_ref, st_ref, lb_ref, hn_ref = refs[:6]
        dp_ref, dhn_ref, dlb_ref = refs[7 + nr:10 + nr]
        dstate = refs[10 + 2 * nr]
        i = pl.program_id(1)
        if rider:
            total = HEADS * nblk
            rider.emit(pl.program_id(0) * nblk + i, total, _rider_mid_step(total),
                       refs[7:7 + nr], refs[10 + nr:10 + 2 * nr], refs[11 + 2 * nr:])

        @pl.when(i == 0)
        def _():
            dstate[...] = jnp.zeros_like(dstate)
            dhn_ref[...] = jnp.zeros_like(dhn_ref)
            dlb_ref[...] = jnp.zeros_like(dlb_ref)

        cums, causal, upper, row = _chunk_consts()
        lbv = lb_ref[...]
        hnv = hn_ref[...]

        chunks = range(ncht)
        rows = [slice(ci * CHUNK, (ci + 1) * CHUNK) for ci in chunks]
        qas = [p_ref[r, 0:HEAD_DIM] for r in rows]
        vbs = [p_ref[r, 2 * HEAD_DIM:3 * HEAD_DIM].astype(BF16) for r in rows]
        st0s = [st_ref[ci, 0] for ci in chunks]
        dzas, dobs = [], []
        dhn_acc = jnp.zeros_like(hnv)
        for r in rows:
            za = p_ref[r, 3 * HEAD_DIM:4 * HEAD_DIM]
            o = o_ref[r, :]
            dy = dy_ref[r, :]
            rn = lax.rsqrt(_rowmean(o * o) + EPS)
            on = o * rn
            sgz = _sigmoid(za)
            sz = za * sgz
            dzas.append(dy * on * hnv * (sgz * (1.0 + za * (1.0 - sgz))))
            dhn_acc = dhn_acc + _colsum(dy * on * sz)
            don = dy * hnv * sz
            dobs.append((rn * (don - on * _rowmean(don * on))).astype(BF16))
        pres = _hgrn_chunks(qas, [p_ref[r, HEAD_DIM:2 * HEAD_DIM] for r in rows], lbv, cums, row)
        scores = [jnp.where(causal, _dot_nt(pre["lhs"], pre["rhs"]), 0.0).astype(BF16) for pre in pres]
        das = [jnp.where(causal, _dot_nt(dob, vb), 0.0).astype(BF16) for dob, vb in zip(dobs, vbs)]
        dlhss = [_dot(da, pre["rhs"]) for da, pre in zip(das, pres)]
        drhss = [_dot_tn(da, pre["lhs"]) for da, pre in zip(das, pres)]
        dv_intra = [_dot_tn(a, dob) for a, dob in zip(scores, dobs)]
        dq_inter = [_dot(dob, st0) * pre["eb"] for dob, st0, pre in zip(dobs, st0s, pres)]
        qds = [pre["qd"].astype(BF16) for pre in pres]
        kdecs = [pre["kdec"].astype(BF16) for pre in pres]
        dst1 = dstate[...]
        dvs, dk_states, dbl_states = [None] * ncht, [None] * ncht, [None] * ncht
        for ci in reversed(chunks):
            dst1b = dst1.astype(BF16)
            dvs[ci] = dv_intra[ci] + _dot_nt(kdecs[ci], dst1b)
            dk_states[ci] = _dot(vbs[ci], dst1b) * pres[ci]["edec"]
            dbl_states[ci] = pres[ci]["ebl"] * _colsum(dst1 * st0s[ci].astype(F32))
            dst1 = dst1 * pres[ci]["ebl"] + _dot_tn(dobs[ci], qds[ci])
        dstate[...] = dst1
        dqs, dks, dbs, dbls = [], [], [], []
        for ci in chunks:
            pre = pres[ci]
            q, k = pre["q"], pre["k"]
            dq_a = jnp.zeros_like(q)
            dk_a = jnp.zeros_like(k)
            db = q * dq_inter[ci] - k * dk_states[ci]
            for j in range(N_SUB):
                cols = slice(j * HEAD_DIM, (j + 1) * HEAD_DIM)
                dlhs, drhs = dlhss[ci][:, cols], drhss[ci][:, cols]
                dq_a = dq_a + pre["dms"][j] * dlhs
                dk_a = dk_a + jnp.where(row // SUB == j, drhs, 0.0)
                db = db + (pre["lhs"][:, cols].astype(F32) * dlhs - pre["rhs"][:, cols].astype(F32) * drhs)
            dqs.append(dq_inter[ci] + pre["ec"] * dq_a)
            dks.append(dk_states[ci] + pre["enc"] * dk_a)
            dbs.append(db)
            dbls.append(_colsum(k * dk_states[ci]) + dbl_states[ci])
        dlfs = [_dot_split(upper, db) + dbl for db, dbl in zip(dbs, dbls)]
        dlb_acc = jnp.zeros_like(lbv)
        for ci in chunks:
            pre = pres[ci]
            sq, sa, sna, oml = pre["sq"], pre["sa"], pre["sna"], pre["oml"]
            dqa = dqs[ci] * (sq * (1.0 + qas[ci] * (1.0 - sq)))
            diff = jnp.where(pre["f"] >= MIN_FORGET, dlfs[ci] / pre["fc"], 0.0) - dks[ci]
            dlb_acc = dlb_acc + _colsum(diff * sna)
            dfa = diff * (oml * sa * sna)
            dp_ref[rows[ci], :] = jnp.concatenate([dqa, dfa, dvs[ci], dzas[ci]], axis=1).astype(dp_ref.dtype)
        dhn_ref[...] += dhn_acc
        dlb_ref[...] += dlb_acc

    rev = lambda h, i: (nblk - 1 - i, h)
    return pl.pallas_call(
        body, grid=(HEADS, nblk),
        in_specs=[pl.BlockSpec((t, HEAD_DIM), rev),
                  pl.BlockSpec((t, HEAD_COLS), lambda h, i: (nblk - 1 - i, h)),
                  pl.BlockSpec((t, HEAD_DIM), rev),
                  pl.BlockSpec((ncht, 1, HEAD_DIM, HEAD_DIM), lambda h, i: (nblk - 1 - i, h, 0, 0)),
                  pl.BlockSpec((1, HEAD_DIM), lambda h, i: (0, h)),
                  pl.BlockSpec((1, HEAD_DIM), lambda h, i: (0, h)),
                  hbm] + [hbm] * nr,
        out_specs=[pl.BlockSpec((t, HEAD_COLS), lambda h, i: (nblk - 1 - i, head_blk0 + h)),
                   pl.BlockSpec((1, HEAD_DIM), lambda h, i: (0, h)),
                   pl.BlockSpec((1, HEAD_DIM), lambda h, i: (0, h))] + [hbm] * nr,
        out_shape=[jax.ShapeDtypeStruct(dproj.shape, dproj.dtype),
                   jax.ShapeDtypeStruct((1, WIDTH), F32), jax.ShapeDtypeStruct((1, WIDTH), F32)]
        + (rider.out_shape if rider else []),
        scratch_shapes=[pltpu.VMEM((HEAD_DIM, HEAD_DIM), F32)] + (rider.scratch if rider else []),
        input_output_aliases={6: 0},
        compiler_params=_cparams(("arbitrary", "arbitrary")),
        name=name)(dya, proj, o_all, states, lb, hn, dproj, *(rider.arrs if rider else []))


def _softmax_rows(lower):
    mx = jnp.max(lower, axis=0, keepdims=True)
    e = jnp.exp(lower - mx)
    return e / jnp.sum(e, axis=0, keepdims=True)


def lb_table(lower, *, name):
    depth, w = lower.shape

    def body(l_ref, o_ref):
        sm = _softmax_rows(l_ref[...])
        acc = jnp.zeros((1, w), F32)
        o_ref[0:1, :] = acc
        for l in range(1, depth):
            acc = acc + sm[l:l + 1, :]
            o_ref[l:l + 1, :] = acc

    return pl.pallas_call(body, out_shape=jax.ShapeDtypeStruct((depth, w), F32), name=name)(lower)


def lb_table_bwd(lower, dlb, *, name):
    depth, w = lower.shape

    def body(l_ref, d_ref, o_ref):
        sm = _softmax_rows(l_ref[...])
        dlbv = d_ref[...]
        dsm = [jnp.zeros((1, w), F32)]
        for i in range(1, depth):
            acc = jnp.zeros((1, w), F32)
            for l in range(i, depth):
                acc = acc + dlbv[l:l + 1, :]
            dsm.append(acc)
        inner = jnp.zeros((1, w), F32)
        for i in range(depth):
            inner = inner + sm[i:i + 1, :] * dsm[i]
        for i in range(depth):
            o_ref[i:i + 1, :] = sm[i:i + 1, :] * (dsm[i] - inner)

    return pl.pallas_call(body, out_shape=jax.ShapeDtypeStruct((depth, w), F32), name=name)(lower, dlb)


def w_ada_grad(c_all, dmod_cols, *, name):
    depth, _, cols = dmod_cols.shape
    d = c_all.shape[1]

    def body(c_ref, dm_ref, o_ref):
        cv = c_ref[...]
        ca = cv * _sigmoid(cv)
        o_ref[...] = _dot_tn(ca, dm_ref[...])

    return pl.pallas_call(
        body, grid=(depth,),
        in_specs=[pl.BlockSpec((N_DEV, d), lambda l: (0, 0)), pl.BlockSpec((None, N_DEV, cols), lambda l: (l, 0, 0))],
        out_specs=pl.BlockSpec((None, d, cols), lambda l: (l, 0, 0)),
        out_shape=jax.ShapeDtypeStruct((depth, d, cols), F32),
        compiler_params=_cparams(("arbitrary",)), name=name)(c_all, dmod_cols)


def sum_parts(parts, *, name):
    p, r, c = parts.shape

    def body(p_ref, o_ref):
        acc = p_ref[0]
        for j in range(1, p):
            acc = acc + p_ref[j]
        o_ref[...] = acc

    return pl.pallas_call(body, out_shape=jax.ShapeDtypeStruct((r, c), F32), name=name)(parts)


def _adam_rows(r, c):
    tr = r
    while tr * c * 4 > (1 << 20) and tr % 16 == 0:
        tr //= 2
    return tr


def _adam_update(w_ref, m_ref, v_ref, g_ref, go_ref, d_ref, mo_ref, vo_ref):
    g = g_ref[0].astype(F32)
    for j in range(1, g_ref.shape[0]):
        g = g + g_ref[j].astype(F32)
    mn = ADAM_B1 * m_ref[...] + (1.0 - ADAM_B1) * g
    vn = ADAM_B2 * v_ref[...] + (1.0 - ADAM_B2) * (g * g)
    m_hat = mn / (1.0 - ADAM_B1 ** ADAM_STEP)
    v_hat = vn / (1.0 - ADAM_B2 ** ADAM_STEP)
    go_ref[...] = g
    d_ref[...] = -ADAM_LR * (m_hat / (jnp.sqrt(v_hat) + ADAM_EPS) + ADAM_WD * w_ref[...])
    mo_ref[...] = mn
    vo_ref[...] = vn


def adamw(w, m, v, gparts, *, name):
    r, c = w.shape
    p = gparts.shape[0]
    tr = _adam_rows(r, c)
    spec = pl.BlockSpec((tr, c), lambda i: (i, 0))
    shp = jax.ShapeDtypeStruct((r, c), F32)
    return pl.pallas_call(
        functools.partial(_adam_update), grid=(r // tr,),
        in_specs=[spec, spec, spec, pl.BlockSpec((p, tr, c), lambda i: (0, i, 0))],
        out_specs=[spec] * 4, out_shape=[shp] * 4,
        compiler_params=_cparams(("arbitrary",)), name=name)(w, m, v, gparts)


def adamw_layers(w, m, v, gparts, *, name):
    depth, r, c = w.shape
    p = gparts[0].shape[0]
    tr = _adam_rows(r, c)

    def body(w_ref, m_ref, v_ref, *rest):
        g_refs, outs = rest[:depth], rest[depth:]
        layer = pl.program_id(0)
        for k in range(depth):
            @pl.when(layer == k)
            def _(k=k):
                _adam_update(w_ref, m_ref, v_ref, g_refs[k], *outs)

    spec = pl.BlockSpec((None, tr, c), lambda l, i: (l, i, 0))
    g_specs = [pl.BlockSpec((p, tr, c), functools.partial(lambda l, i, k: (0, jnp.where(l == k, i, 0), 0), k=k))
               for k in range(depth)]
    shp = jax.ShapeDtypeStruct((depth, r, c), F32)
    return pl.pallas_call(
        body, grid=(depth, r // tr),
        in_specs=[spec, spec, spec] + g_specs,
        out_specs=[spec] * 4, out_shape=[shp] * 4,
        compiler_params=_cparams(("arbitrary", "arbitrary")), name=name)(w, m, v, *gparts)


def _position():
    x, y, c = lax.axis_index("x"), lax.axis_index("y"), lax.axis_index("c")
    return x, y, c


def _dev_index(x, y, c):
    return 4 * x + 2 * y + c


def _gather_phases(ins, outs, send_sems, recv_sems, local_sems):
    n = len(ins)
    x, y, c = _position()
    me, sibling = (x, y, c), (x, y, 1 - c)
    chips = [(1 - x, y), (x, 1 - y), (1 - x, 1 - y)]

    def copy(a, k, block, to, own=False):
        slot = outs[a].at[_dev_index(*block)]
        return pltpu.make_async_remote_copy(
            src_ref=ins[a] if own else slot, dst_ref=slot,
            send_sem=send_sems.at[a * 7 + k], recv_sem=recv_sems.at[a * 7 + k],
            device_id=to, device_id_type=MESH)

    def mine(a):
        return pltpu.make_async_copy(ins[a], outs[a].at[_dev_index(*me)], local_sems.at[a])

    def first(a):
        return [copy(a, 0, me, sibling, True)] + [copy(a, 1 + j, me, (*chip, c), True) for j, chip in enumerate(chips)]

    def passed(a):
        return [copy(a, 4 + j, (*chip, c), sibling) for j, chip in enumerate(chips)]

    def start():
        for a in range(n):
            mine(a).start()
        for a in range(n):
            for cp in first(a):
                cp.start()

    def mid():
        for j, chip in enumerate(chips):
            for a in range(n):
                copy(a, 1 + j, (*chip, c), me).wait_recv()
                passed(a)[j].start()

    def finish():
        for a in range(n):
            copy(a, 0, sibling, me).wait_recv()
            for j, chip in enumerate(chips):
                copy(a, 4 + j, (*chip, 1 - c), me).wait_recv()
        for a in range(n):
            for cp in first(a) + passed(a):
                cp.wait_send()
            mine(a).wait()

    return start, mid, finish


def _scatter_phases(ins, outs, send_sems, recv_sems, local_sems):
    n = len(ins)
    x, y, c = _position()
    me = _dev_index(x, y, c)

    def peer(r):
        return (x ^ (r >> 2), y ^ ((r >> 1) & 1), c ^ (r & 1))

    def copy(a, r):
        to = peer(r)
        return pltpu.make_async_remote_copy(
            src_ref=ins[a].at[_dev_index(*to)], dst_ref=outs[a].at[me],
            send_sem=send_sems.at[a * 7 + r - 1], recv_sem=recv_sems.at[a * 7 + r - 1],
            device_id=to, device_id_type=MESH)

    def arrival(a, r):
        return pltpu.make_async_remote_copy(
            src_ref=ins[a].at[me], dst_ref=outs[a].at[_dev_index(*peer(r))],
            send_sem=send_sems.at[a * 7 + r - 1], recv_sem=recv_sems.at[a * 7 + r - 1],
            device_id=peer(r), device_id_type=MESH)

    def mine(a):
        return pltpu.make_async_copy(ins[a].at[me], outs[a].at[me], local_sems.at[a])

    def start():
        for a in range(n):
            mine(a).start()
        for r in range(1, N_DEV):
            for a in range(n):
                copy(a, r).start()

    def finish():
        for r in range(1, N_DEV):
            for a in range(n):
                arrival(a, r).wait_recv()
        for r in range(1, N_DEV):
            for a in range(n):
                copy(a, r).wait_send()
        for a in range(n):
            mine(a).wait()

    return start, None, finish


class Rider:
    def __init__(self, kind, arrs):
        self.kind, self.arrs, self.n = kind, list(arrs), len(arrs)
        lead = (N_DEV,) if kind == "gather" else ()
        self.out_shape = [jax.ShapeDtypeStruct(lead + a.shape, a.dtype) for a in self.arrs]
        self.scratch = [pltpu.SemaphoreType.DMA((7 * self.n,)), pltpu.SemaphoreType.DMA((7 * self.n,)),
                        pltpu.SemaphoreType.DMA((self.n,))]

    def phases(self, ins, outs, sems):
        make = _gather_phases if self.kind == "gather" else _scatter_phases
        return make(ins, outs, *sems)

    def emit(self, step, total, mid_step, ins, outs, sems):
        start, mid, finish = self.phases(ins, outs, sems)
        pl.when(step == 0)(start)
        if mid is not None:
            pl.when(step == mid_step)(mid)
        pl.when(step == total - 1)(finish)


def _standalone(rider, name):
    n = rider.n
    hbm = pl.BlockSpec(memory_space=pl.ANY)

    def body(*refs):
        start, mid, finish = rider.phases(refs[:n], refs[n:2 * n], refs[2 * n:])
        start()
        if mid is not None:
            mid()
        finish()

    return pl.pallas_call(body, out_shape=rider.out_shape, in_specs=[hbm] * n, out_specs=[hbm] * n,
                          scratch_shapes=rider.scratch, name=name)(*rider.arrs)


def all_gather(arrs, *, name):
    return _standalone(Rider("gather", arrs), name)


def scatter_parts(arrs, *, name):
    return _standalone(Rider("scatter", arrs), name)


def mod_exchange(c_all, w_ada, b_cols, *, name):
    depth, d, cols = w_ada.shape
    hbm = pl.BlockSpec(memory_space=pl.ANY)
    vmem = pl.BlockSpec(memory_space=pltpu.VMEM)

    def body(c_ref, w_ref, b_ref, out_ref, wbuf, sendbuf, send_sems, recv_sems, load_sem):
        x, y, c = _position()
        me = _dev_index(x, y, c)
        cv = c_ref[...]
        ca = cv * _sigmoid(cv)
        for l in range(depth):
            load = pltpu.make_async_copy(w_ref.at[l], wbuf, load_sem)
            load.start()
            load.wait()
            part = jnp.dot(ca, wbuf[...], preferred_element_type=F32,
                           precision=lax.Precision.HIGHEST) + b_ref[l:l + 1, :]
            for bi in range(N_DEV):
                sendbuf[bi, l:l + 1, :] = part[bi:bi + 1, :]

        def peer(r):
            return (x ^ (r >> 2), y ^ ((r >> 1) & 1), c ^ (r & 1))

        def copy(r):
            to = peer(r)
            return pltpu.make_async_remote_copy(
                src_ref=sendbuf.at[_dev_index(*to)], dst_ref=out_ref.at[me],
                send_sem=send_sems.at[r - 1], recv_sem=recv_sems.at[r - 1],
                device_id=to, device_id_type=MESH)

        def arrival(r):
            return pltpu.make_async_remote_copy(
                src_ref=sendbuf.at[me], dst_ref=out_ref.at[_dev_index(*peer(r))],
                send_sem=send_sems.at[r - 1], recv_sem=recv_sems.at[r - 1],
                device_id=peer(r), device_id_type=MESH)

        out_ref[me] = sendbuf[me]
        sends = [copy(r) for r in range(1, N_DEV)]
        for cp in sends:
            cp.start()
        for r in range(1, N_DEV):
            arrival(r).wait_recv()
        for cp in sends:
            cp.wait_send()

    return pl.pallas_call(
        body,
        out_shape=jax.ShapeDtypeStruct((N_DEV, depth, cols), F32),
        in_specs=[vmem, hbm, vmem], out_specs=vmem,
        scratch_shapes=[pltpu.VMEM((d, cols), F32), pltpu.VMEM((N_DEV, depth, cols), F32),
                        pltpu.SemaphoreType.DMA((7,)), pltpu.SemaphoreType.DMA((7,)), pltpu.SemaphoreType.DMA],
        compiler_params=pltpu.CompilerParams(vmem_limit_bytes=VMEM_LIMIT),
        name=name)(c_all, w_ada, b_cols)


def kernel(x, c, w_ada, b_ada, norm_pre, norm_post, w_in, lower_bounds, hgrn_norm, pool_w, pool_scale, w_proj_a, w_proj_b, w_out, loss_target, m_w_ada, m_b_ada, m_norm_pre, m_norm_post, m_w_in, m_lower_bounds, m_hgrn_norm, m_pool_w, m_pool_scale, m_w_proj_a, m_w_proj_b, m_w_out, v_w_ada, v_b_ada, v_norm_pre, v_norm_post, v_w_in, v_lower_bounds, v_hgrn_norm, v_pool_w, v_pool_scale, v_w_proj_a, v_w_proj_b, v_w_out):
    depth = w_in.shape[0]
    d = D_MODEL
    ada_cols = w_ada.shape[2]
    xi, yi, ci = _position()
    me = _dev_index(xi, yi, ci)
    xs = x[0]
    target = loss_target[0]
    ng = len(POOL_WINDOWS)

    def shards(l):
        w_in_l = w_in[l].astype(BF16)
        half = w_in_l.shape[1] // 2
        return [w_in_l[:, :half], w_in_l[:, half:], w_proj_a[l].astype(BF16), w_proj_b[l].astype(BF16),
                w_out[l].astype(BF16), pool_w[l].astype(BF16)]

    def other_weights(g_pa, g_pb, g_out, g_pool):
        return dict(
            pa=jnp.transpose(g_pa, (1, 0, 2)).reshape(WIDTH, d),
            pb=jnp.transpose(g_pb, (1, 0, 2)).reshape(WIDTH, d),
            w_out=g_out.reshape(d, d),
            pool=jnp.transpose(g_pool, (1, 0, 2, 3)).reshape(ng, POOL_GW, POOL_GW))

    w_in_full = [permute_w_in(*all_gather(shards(0)[:2], name="gather_w_in"), name="permute_w_in")]
    others_full = []
    gathered = []

    (c_all,) = all_gather([c], name="gather_c")
    c_all = c_all.reshape(N_DEV, d)
    b_cols = lax.dynamic_slice_in_dim(b_ada, me * ada_cols, ada_cols, axis=1)
    mod_parts = mod_exchange(c_all, w_ada, b_cols, name="mod_exchange")
    mod = jnp.transpose(mod_parts, (1, 0, 2)).reshape(depth, 3 * d)
    lb_all = lb_table(lower_bounds, name="lb_table")

    saved = []
    cur = xs
    for l in range(depth):
        shift, scale, gate = mod[l:l + 1, :d], mod[l:l + 1, d:2 * d], mod[l:l + 1, 2 * d:]
        if l == 0:
            h = prenorm_fwd(cur, norm_pre[l:l + 1], shift, scale, name="prenorm_fwd")
        head_cols = (GP_COLS, IN_COLS - GP_COLS)
        if l + 1 < depth:
            proj_gp, nxt_lo = matmul_nn(h, w_in_full[l], cols=(0, GP_COLS), tm=2048, tn=1024, out_dtype=BF16,
                                        rider=Rider("gather", shards(l + 1)[:1]), name="mm_w_in_gp_gather")
            proj_h, nxt_hi = matmul_nn(h, w_in_full[l], cols=head_cols, tm=2048, tn=1024, out_dtype=F32,
                                       rider=Rider("gather", shards(l + 1)[1:2]), name="mm_w_in_heads_gather")
            w_in_full.append(permute_w_in(nxt_lo, nxt_hi, name="permute_w_in"))
        else:
            proj_gp = matmul_nn(h, w_in_full[l], cols=(0, GP_COLS), tm=2048, tn=1024, out_dtype=BF16,
                                name="mm_w_in_gp")
            proj_h = matmul_nn(h, w_in_full[l], cols=head_cols, tm=2048, tn=1024, out_dtype=F32,
                               name="mm_w_in_heads")
        ride = (shards(0)[2:] if l == 0 else []) + (shards(l + 1)[2:] if l + 1 < depth else [])
        if ride:
            y_a, o_all, states, *got = hgrn_fwd(proj_h, lb_all[l:l + 1], hgrn_norm[l:l + 1],
                                                 rider=Rider("gather", ride), name="hgrn_fwd_gather%d" % len(ride))
            for k in range(0, len(got), 4):
                others_full.append(other_weights(*got[k:k + 4]))
        else:
            y_a, o_all, states = hgrn_fwd(proj_h, lb_all[l:l + 1], hgrn_norm[l:l + 1], name="hgrn_fwd")
        w = dict(w_in=w_in_full[l], **others_full[l])
        gathered.append(w)
        y_b, pooled, mixed = pool_fwd(proj_gp, w["pool"], pool_scale[l:l + 1], name="pool_fwd")
        pa, pb, merged = proj_gate_fwd(y_a, y_b, w["pa"], w["pb"], proj_gp, name="proj_gate_fwd")
        saved.append(dict(x=cur, h=h, proj_gp=proj_gp, proj_h=proj_h, y_a=y_a, o=o_all, states=states, y_b=y_b,
                          pooled=pooled, mixed=mixed, pa=pa, pb=pb, merged=merged, scale=scale, gate=gate))
        if l + 1 < depth:
            nxt_mod = (norm_pre[l + 1:l + 2], mod[l + 1:l + 2, :d], mod[l + 1:l + 2, d:2 * d])
            saved[l]["out"], cur, h = w_out_postnorm_fwd(merged, w["w_out"], cur, norm_post[l:l + 1], gate, nxt_mod,
                                                          name="w_out_postnorm_prenorm_fwd")
        else:
            saved[l]["out"], cur = w_out_postnorm_fwd(merged, w["w_out"], cur, norm_post[l:l + 1], gate,
                                                       name="w_out_postnorm_fwd")

    g, loss_part = loss_head(cur, target, name="loss_head")
    loss = lax.psum(loss_part[0, 0], ("x", "y", "c"))

    small = [None] * depth
    big_in = [None] * depth
    big_others = [None] * depth
    pend_in, pend_others = None, None
    for l in reversed(range(depth)):
        w, sv = gathered[l], saved[l]
        dout, dpa, dpb, dproj, dgate, dnpost = postnorm_gate_bwd(
            g, sv["out"], norm_post[l:l + 1], sv["gate"], w["w_out"], sv["proj_gp"], sv["pa"], sv["pb"],
            name="postnorm_gate_bwd")
        dw_out = matmul_tn(sv["merged"], dout, tm=2048, tn=1024, out_dtype=BF16, name="mm_w_out_dw")
        dya = matmul_nt(dpa, w["pa"], tm=1024, tn=2048, out_dtype=F32, name="mm_proj_a_dx")
        dyb = matmul_nt(dpb, w["pb"], tm=1024, tn=2048, out_dtype=F32, name="mm_proj_b_dx")
        dw_pa = matmul_tn(sv["y_a"], dpa, tm=2048, tn=2048, out_dtype=BF16, name="mm_proj_a_dw")
        dw_pb = matmul_tn(sv["y_b"], dpb, tm=2048, tn=2048, out_dtype=BF16, name="mm_proj_b_dw")
        dproj, dpool_w, dpool_scale = pool_bwd(dyb, sv["proj_gp"], sv["pooled"], sv["mixed"], w["pool"],
                                               pool_scale[l:l + 1], dproj, name="pool_bwd")
        by_owner = lambda t: jnp.transpose(t.reshape(WIDTH, N_DEV, d // N_DEV), (1, 0, 2))
        others = [by_owner(dw_pa), by_owner(dw_pb), dw_out.reshape(N_DEV, d // N_DEV, d),
                  jnp.transpose(dpool_w.astype(BF16).reshape(ng, N_DEV, POOL_GW // N_DEV, POOL_GW), (1, 0, 2, 3))]
        ride = (pend_others or []) + (others if l == 0 else [])
        if ride:
            dproj, dhn, dlb, *recv = hgrn_bwd(dya, sv["proj_h"], sv["o"], sv["states"], lb_all[l:l + 1],
                                              hgrn_norm[l:l + 1], dproj, rider=Rider("scatter", ride),
                                              name="hgrn_bwd_scatter%d" % len(ride))
            if pend_others:
                big_others[l + 1], recv = recv[:len(pend_others)], recv[len(pend_others):]
            if l == 0:
                big_others[0] = recv
        else:
            dproj, dhn, dlb = hgrn_bwd(dya, sv["proj_h"], sv["o"], sv["states"], lb_all[l:l + 1],
                                       hgrn_norm[l:l + 1], dproj, name="hgrn_bwd")
        above = Rider("scatter", [pend_in]) if pend_in is not None else None
        dx_args = (dproj, w["w_in"], sv["x"], norm_pre[l:l + 1], sv["scale"], g)
        if l > 0:
            if above:
                g, dshift, dscale, dnpre, big_in[l + 1] = w_in_dx_prenorm_bwd(*dx_args, rider=above,
                                                                              name="w_in_dx_prenorm_bwd_scatter")
            else:
                g, dshift, dscale, dnpre = w_in_dx_prenorm_bwd(*dx_args, name="w_in_dx_prenorm_bwd")
            dw_in = matmul_tn(sv["h"], dproj, tm=2048, tn=1024, out_dtype=BF16, name="mm_w_in_dw")
            pend_in, pend_others = unpermute_w_in(dw_in, name="unpermute_w_in"), others
        else:
            if above:
                dw_in, big_in[1] = matmul_tn(sv["h"], dproj, tm=2048, tn=1024, out_dtype=BF16, rider=above,
                                             name="mm_w_in_dw_scatter")
            else:
                dw_in = matmul_tn(sv["h"], dproj, tm=2048, tn=1024, out_dtype=BF16, name="mm_w_in_dw")
            g, dshift, dscale, dnpre, big_in[0] = w_in_dx_prenorm_bwd(
                *dx_args, rider=Rider("scatter", [unpermute_w_in(dw_in, name="unpermute_w_in")]),
                name="w_in_dx_prenorm_bwd_scatter")
        small[l] = jnp.concatenate([dshift, dscale, dgate, dnpre, dnpost, dlb, dhn, dpool_scale], axis=1)
    grad_x = g[None]
    big = [[big_in[l]] + list(big_others[l]) for l in range(depth)]

    small_mine = jnp.concatenate(small, axis=0)
    (small_all,) = all_gather([small_mine], name="gather_small")
    small_sum = sum_parts(small_all, name="sum_small")
    dmod_all = small_all[:, :, :3 * d]
    dmod_cols = jnp.transpose(lax.dynamic_slice_in_dim(dmod_all, me * ada_cols, ada_cols, axis=2), (1, 0, 2))
    g_w_ada = w_ada_grad(c_all, dmod_cols, name="w_ada_grad")
    off = 3 * d
    g_b_ada = small_sum[:, :off]
    g_npre = small_sum[:, off:off + d]
    g_npost = small_sum[:, off + d:off + 2 * d]
    g_lb_tab = small_sum[:, off + 2 * d:off + 2 * d + WIDTH]
    g_hn = small_sum[:, off + 2 * d + WIDTH:off + 2 * d + 2 * WIDTH]
    g_ps = small_sum[:, off + 2 * d + 2 * WIDTH:]
    g_lower = lb_table_bwd(lower_bounds, g_lb_tab, name="lb_table_bwd")

    def update(wt, mt, vt, gparts, shape2, name):
        outs = adamw(wt.reshape(shape2), mt.reshape(shape2), vt.reshape(shape2), gparts, name=name)
        return [o.reshape(wt.shape) for o in outs]

    def update_layers(wt, mt, vt, kind, name):
        shape3 = (depth, -1, wt.shape[-1])
        w3 = wt.reshape(shape3)
        gps = [big[l][kind].reshape((N_DEV,) + w3.shape[1:]) for l in range(depth)]
        outs = adamw_layers(w3, mt.reshape(shape3), vt.reshape(shape3), gps, name=name)
        return [o.reshape(wt.shape) for o in outs]

    def update_small(wt, mt, vt, gt, name):
        shape2 = (-1, wt.shape[-1])
        return update(wt, mt, vt, gt.reshape(shape2)[None], shape2, name)

    res = {
        "w_ada": update_small(w_ada, m_w_ada, v_w_ada, g_w_ada, "adamw_w_ada"),
        "b_ada": update_small(b_ada, m_b_ada, v_b_ada, g_b_ada, "adamw_b_ada"),
        "norm_pre": update_small(norm_pre, m_norm_pre, v_norm_pre, g_npre, "adamw_norm_pre"),
        "norm_post": update_small(norm_post, m_norm_post, v_norm_post, g_npost, "adamw_norm_post"),
        "w_in": update_layers(w_in, m_w_in, v_w_in, 0, "adamw_w_in"),
        "lower_bounds": update_small(lower_bounds, m_lower_bounds, v_lower_bounds, g_lower, "adamw_lower_bounds"),
        "hgrn_norm": update_small(hgrn_norm, m_hgrn_norm, v_hgrn_norm, g_hn, "adamw_hgrn_norm"),
        "pool_w": update_layers(pool_w, m_pool_w, v_pool_w, 4, "adamw_pool_w"),
        "pool_scale": update_small(pool_scale, m_pool_scale, v_pool_scale, g_ps, "adamw_pool_scale"),
        "w_proj_a": update_layers(w_proj_a, m_w_proj_a, v_w_proj_a, 1, "adamw_w_proj_a"),
        "w_proj_b": update_layers(w_proj_b, m_w_proj_b, v_w_proj_b, 2, "adamw_w_proj_b"),
        "w_out": update_layers(w_out, m_w_out, v_w_out, 3, "adamw_w_out"),
    }
    order = ["w_ada", "b_ada", "norm_pre", "norm_post", "w_in", "lower_bounds", "hgrn_norm", "pool_w",
             "pool_scale", "w_proj_a", "w_proj_b", "w_out"]
    outs = [loss, grad_x]
    for k in range(4):
        outs += [res[nm][k] for nm in order]
    return tuple(outs)
```

```python
import functools

import jax
import jax.numpy as jnp
from jax import lax
from jax.experimental import pallas as pl
from jax.experimental.pallas import tpu as pltpu

F32 = jnp.float32
BF16 = jnp.bfloat16
MESH = pl.DeviceIdType.MESH

N_DEV = 8
EPS = 1e-6
MIN_FORGET = 1e-30
D_MODEL = 2048
HEADS = 8
HEAD_DIM = 128
CHUNK = 64
SUB = 16
N_SUB = CHUNK // SUB
WIDTH = 1024
POOL_WINDOWS = (2, 4, 8, 16)
POOL_GW = 256
HALO = 16
IN_COLS = 10240
LANE = 128
N_COLBLK = IN_COLS // LANE
GATE_COLS = 4096
HEAD_COLS = 4 * HEAD_DIM
POOL_COLS = 2 * POOL_GW
GP_COLS = GATE_COLS + len(POOL_WINDOWS) * POOL_COLS
MAX_EXP = 80.0

ADAM_LR = 0.001
ADAM_B1 = 0.9
ADAM_B2 = 0.999
ADAM_EPS = 1e-08
ADAM_WD = 0.01
ADAM_STEP = 10

VMEM_LIMIT = 62 * 1024 * 1024


def _cparams(sem=None):
    return pltpu.CompilerParams(dimension_semantics=sem, vmem_limit_bytes=VMEM_LIMIT)


def _sigmoid(v):
    return 1.0 / (1.0 + jnp.exp(-v))


def _dot(a, b):
    return jnp.dot(a, b, preferred_element_type=F32)


def _dot_nt(a, b):
    return lax.dot_general(a, b, (((1,), (1,)), ((), ())), preferred_element_type=F32)


def _dot_tn(a, b):
    return lax.dot_general(a, b, (((0,), (0,)), ((), ())), preferred_element_type=F32)


def _colsum(v):
    return jnp.sum(v, axis=0, keepdims=True)


def _rowmean(v):
    return jnp.mean(v, axis=-1, keepdims=True)


def _orig_block_static(n):
    if n < 32:
        return n + 48
    if n < 48:
        m = n - 32
        t = m % 4
        return 32 + 2 * (m // 4) + (t % 2) + 8 * (t // 2)
    m = n - 48
    return 8 * (m % 4) + m // 4


def _accumulate(step, steps, prod, o_ref, acc_ref):
    if steps == 1:
        o_ref[...] = prod.astype(o_ref.dtype)
        return

    @pl.when(step == 0)
    def _():
        acc_ref[...] = prod

    @pl.when(step > 0)
    def _():
        acc_ref[...] += prod

    @pl.when(step == steps - 1)
    def _():
        o_ref[...] = acc_ref[...].astype(o_ref.dtype)


def _matmul_call(dot, a, b, *, grid, in_specs, out_spec, out_shape, acc_shape, steps, rider, name):
    nr = rider.n if rider else 0
    hbm = pl.BlockSpec(memory_space=pl.ANY)
    has_acc = steps > 1

    def body(*refs):
        a_ref, b_ref = refs[:2]
        o_ref = refs[2 + nr]
        scratch = refs[3 + 2 * nr:]
        if rider:
            total = grid[0] * grid[1]
            rider.emit(pl.program_id(0) * grid[1] + pl.program_id(1), total, _rider_mid_step(total),
                       refs[2:2 + nr], refs[3 + nr:3 + 2 * nr], scratch[1 if has_acc else 0:])
        _accumulate(pl.program_id(1), steps, dot(a_ref[...], b_ref[...]), o_ref, scratch[0] if has_acc else None)

    outs = pl.pallas_call(
        body, grid=grid,
        in_specs=in_specs + [hbm] * nr, out_specs=[out_spec] + [hbm] * nr,
        out_shape=[out_shape] + (rider.out_shape if rider else []),
        scratch_shapes=([pltpu.VMEM(acc_shape, F32)] if has_acc else []) + (rider.scratch if rider else []),
        compiler_params=_cparams(("arbitrary", "arbitrary")),
        name=name)(a, b, *(rider.arrs if rider else []))
    return outs if rider else outs[0]


def matmul_nn(a, b, *, tm, tn, out_dtype, name, rider=None, cols=None):
    m, k = a.shape
    col0, n = cols if cols else (0, b.shape[1])
    tm = min(tm, m)
    j0 = col0 // tn
    return _matmul_call(
        _dot, a, b, grid=(n // tn, m // tm),
        in_specs=[pl.BlockSpec((tm, k), lambda j, i: (i, 0)), pl.BlockSpec((k, tn), lambda j, i: (0, j0 + j))],
        out_spec=pl.BlockSpec((tm, tn), lambda j, i: (i, j)),
        out_shape=jax.ShapeDtypeStruct((m, n), out_dtype), acc_shape=None, steps=1, rider=rider, name=name)


def matmul_nt(a, b, *, tm, tn, out_dtype, name, rider=None):
    m, n = a.shape
    k = b.shape[0]
    tm = min(tm, m)
    return _matmul_call(
        _dot_nt, a, b, grid=(m // tm, n // tn),
        in_specs=[pl.BlockSpec((tm, tn), lambda i, j: (i, j)), pl.BlockSpec((k, tn), lambda i, j: (0, j))],
        out_spec=pl.BlockSpec((tm, k), lambda i, j: (i, 0)),
        out_shape=jax.ShapeDtypeStruct((m, k), out_dtype), acc_shape=(tm, k), steps=n // tn, rider=rider, name=name)


def matmul_tn(a, b, *, tm, tn, out_dtype, name, rider=None):
    m, k = a.shape
    n = b.shape[1]
    tm = min(tm, m)
    return _matmul_call(
        _dot_tn, a, b, grid=(n // tn, m // tm),
        in_specs=[pl.BlockSpec((tm, k), lambda j, i: (i, 0)), pl.BlockSpec((tm, tn), lambda j, i: (i, j))],
        out_spec=pl.BlockSpec((k, tn), lambda j, i: (0, j)),
        out_shape=jax.ShapeDtypeStruct((k, n), out_dtype), acc_shape=(k, tn), steps=m // tm, rider=rider, name=name)


def permute_w_in(staged_lo, staged_hi, *, name):
    k = staged_lo.shape[1]
    own = IN_COLS // N_DEV
    half = own // 2
    tr = min(256, k)

    def body(lo_ref, hi_ref, o_ref):
        for nb in range(N_COLBLK):
            dev, col = divmod(_orig_block_static(nb) * LANE, own)
            src = lo_ref[dev, :, col:col + LANE] if col < half else hi_ref[dev, :, col - half:col - half + LANE]
            o_ref[:, nb * LANE:(nb + 1) * LANE] = src

    spec = pl.BlockSpec((N_DEV, tr, half), lambda i: (0, i, 0))
    return pl.pallas_call(
        body, grid=(k // tr,),
        in_specs=[spec, spec],
        out_specs=pl.BlockSpec((tr, IN_COLS), lambda i: (i, 0)),
        out_shape=jax.ShapeDtypeStruct((k, IN_COLS), staged_lo.dtype),
        compiler_params=_cparams(("arbitrary",)), name=name)(staged_lo, staged_hi)


def unpermute_w_in(dw, *, name):
    k = dw.shape[0]
    own = IN_COLS // N_DEV
    tr = min(256, k)

    def body(i_ref, o_ref):
        for nb in range(N_COLBLK):
            dev, col = divmod(_orig_block_static(nb) * LANE, own)
            o_ref[dev, :, col:col + LANE] = i_ref[:, nb * LANE:(nb + 1) * LANE]

    return pl.pallas_call(
        body, grid=(k // tr,),
        in_specs=[pl.BlockSpec((tr, IN_COLS), lambda i: (i, 0))],
        out_specs=pl.BlockSpec((N_DEV, tr, own), lambda i: (0, i, 0)),
        out_shape=jax.ShapeDtypeStruct((N_DEV, k, own), dw.dtype),
        compiler_params=_cparams(("arbitrary",)), name=name)(dw)


def _row_tile(s):
    return min(256, s)


def _norm_tile(s):
    return min(512, s)


def _row_spec(t, w, col=0):
    return pl.BlockSpec((t, w), lambda i: (i, col))


def _vec_spec(w):
    return pl.BlockSpec((1, w), lambda i: (0, 0))


def prenorm_fwd(x, gain, shift, scale, *, name):
    s, d = x.shape
    t = _norm_tile(s)

    def body(x_ref, g_ref, sh_ref, sc_ref, h_ref):
        xv = x_ref[...]
        r = lax.rsqrt(_rowmean(xv * xv) + EPS)
        h_ref[...] = ((xv * r) * g_ref[...] * (1.0 + sc_ref[...]) + sh_ref[...]).astype(h_ref.dtype)

    return pl.pallas_call(
        body, grid=(s // t,),
        in_specs=[_row_spec(t, d), _vec_spec(d), _vec_spec(d), _vec_spec(d)],
        out_specs=_row_spec(t, d), out_shape=jax.ShapeDtypeStruct((s, d), BF16),
        compiler_params=_cparams(("arbitrary",)), name=name)(x, gain, shift, scale)


def w_in_dx_prenorm_bwd(dproj, w_in, x, gain, scale, g_res, *, rider=None, name):
    s, n = dproj.shape
    d = w_in.shape[0]
    tm = min(512, s)
    tn = 2048
    steps = n // tn
    nrow = s // tm
    nr = rider.n if rider else 0
    hbm = pl.BlockSpec(memory_space=pl.ANY)

    def body(*refs):
        dp_ref, w_ref, x_ref, g_ref, sc_ref, gr_ref = refs[:6]
        dx_ref, dsh_ref, dsc_ref, dg_ref = refs[6 + nr:10 + nr]
        acc_ref = refs[10 + 2 * nr]
        i, j = pl.program_id(0), pl.program_id(1)
        if rider:
            total = nrow * steps
            rider.emit(i * steps + j, total, _rider_mid_step(total),
                       refs[6:6 + nr], refs[10 + nr:10 + 2 * nr], refs[11 + 2 * nr:])
        prod = _dot_nt(dp_ref[...], w_ref[...])

        @pl.when(j == 0)
        def _():
            acc_ref[...] = prod

        @pl.when(j > 0)
        def _():
            acc_ref[...] += prod

        @pl.when(j == steps - 1)
        def _():
            @pl.when(i == 0)
            def _():
                dsh_ref[...] = jnp.zeros_like(dsh_ref)
                dsc_ref[...] = jnp.zeros_like(dsc_ref)
                dg_ref[...] = jnp.zeros_like(dg_ref)

            gain_v = g_ref[...]
            one_sc = 1.0 + sc_ref[...]
            sub = min(256, tm)

            def piece(k, carry):
                rows = pl.ds(pl.multiple_of(k * sub, sub), sub)
                xv = x_ref[rows, :]
                dhv = acc_ref[rows, :]
                r = lax.rsqrt(_rowmean(xv * xv) + EPS)
                xn = xv * r
                dyn = dhv * one_sc
                dxn = dyn * gain_v
                dx_ref[rows, :] = r * (dxn - xn * _rowmean(dxn * xn)) + gr_ref[rows, :]
                dsh_ref[...] += _colsum(dhv)
                dsc_ref[...] += _colsum(dhv * (xn * gain_v))
                dg_ref[...] += _colsum(dyn * xn)
                return carry

            lax.fori_loop(0, tm // sub, piece, 0)

    rows = pl.BlockSpec((tm, d), lambda i, j: (i, 0))
    vec_in = pl.BlockSpec((1, d), lambda i, j: (0, 0))
    vec = jax.ShapeDtypeStruct((1, d), F32)
    outs = pl.pallas_call(
        body, grid=(nrow, steps),
        in_specs=[pl.BlockSpec((tm, tn), lambda i, j: (i, j)), pl.BlockSpec((d, tn), lambda i, j: (0, j)),
                  rows, vec_in, vec_in, rows] + [hbm] * nr,
        out_specs=[rows, vec_in, vec_in, vec_in] + [hbm] * nr,
        out_shape=[jax.ShapeDtypeStruct((s, d), F32), vec, vec, vec] + (rider.out_shape if rider else []),
        scratch_shapes=[pltpu.VMEM((tm, d), F32)] + (rider.scratch if rider else []),
        compiler_params=_cparams(("arbitrary", "arbitrary")),
        name=name)(dproj, w_in, x, gain, scale, g_res, *(rider.arrs if rider else []))
    return outs


def w_out_postnorm_fwd(merged, w_out, x, gain, gate, nxt=None, *, name):
    s, d = x.shape
    t = _norm_tile(s)

    def body(m_ref, w_ref, x_ref, g_ref, gt_ref, *rest):
        o_ref, y_ref = rest[-3:-1] if nxt else rest
        ov = _dot(m_ref[...], w_ref[...])
        o_ref[...] = ov
        r = lax.rsqrt(_rowmean(ov * ov) + EPS)
        y = x_ref[...] + gt_ref[...] * ((ov * r) * g_ref[...])
        y_ref[...] = y
        if nxt:
            gn_ref, sh_ref, sc_ref, h_ref = rest[0], rest[1], rest[2], rest[-1]
            rn = lax.rsqrt(_rowmean(y * y) + EPS)
            h_ref[...] = ((y * rn) * gn_ref[...] * (1.0 + sc_ref[...]) + sh_ref[...]).astype(h_ref.dtype)

    shp = jax.ShapeDtypeStruct((s, d), F32)
    extra = list(nxt) if nxt else []
    return pl.pallas_call(
        body, grid=(s // t,),
        in_specs=[_row_spec(t, d), _full_spec((d, d)), _row_spec(t, d), _vec_spec(d), _vec_spec(d)]
        + [_vec_spec(d)] * len(extra),
        out_specs=[_row_spec(t, d)] * (3 if nxt else 2),
        out_shape=[shp, shp] + ([jax.ShapeDtypeStruct((s, d), BF16)] if nxt else []),
        compiler_params=_cparams(("arbitrary",)), name=name)(merged, w_out, x, gain, gate, *extra)


def loss_head(y, target, *, name):
    s, d = y.shape
    t = _norm_tile(s)
    steps = s // t

    def body(y_ref, t_ref, dy_ref, loss_ref, acc_ref):
        i = pl.program_id(0)
        err = y_ref[...] - t_ref[...]
        dy_ref[...] = err * (1.0 / d)
        part = _colsum(err * err)

        @pl.when(i == 0)
        def _():
            acc_ref[...] = part

        @pl.when(i > 0)
        def _():
            acc_ref[...] += part

        @pl.when(i == steps - 1)
        def _():
            loss_ref[...] = jnp.sum(acc_ref[...], axis=1, keepdims=True) * (0.5 / d)

    return pl.pallas_call(
        body, grid=(steps,),
        in_specs=[_row_spec(t, d), _row_spec(t, d)],
        out_specs=[_row_spec(t, d), pl.BlockSpec((1, 1), lambda i: (0, 0))],
        out_shape=[jax.ShapeDtypeStruct((s, d), F32), jax.ShapeDtypeStruct((1, 1), F32)],
        scratch_shapes=[pltpu.VMEM((1, d), F32)],
        compiler_params=_cparams(("arbitrary",)), name=name)(y, target)


def _full_spec(shape):
    return pl.BlockSpec(shape, lambda i: (0,) * len(shape))


def proj_gate_fwd(y_a, y_b, w_pa, w_pb, proj, *, name):
    s, width = y_a.shape
    d = w_pa.shape[1]
    t = _norm_tile(s)

    def body(ya_ref, yb_ref, wa_ref, wb_ref, ga_ref, gb_ref, pa_ref, pb_ref, m_ref):
        pa = _dot(ya_ref[...], wa_ref[...])
        pb = _dot(yb_ref[...], wb_ref[...])
        pa_ref[...] = pa.astype(pa_ref.dtype)
        pb_ref[...] = pb.astype(pb_ref.dtype)
        m_ref[...] = (_sigmoid(ga_ref[...].astype(F32)) * pa
                      + _sigmoid(gb_ref[...].astype(F32)) * pb).astype(m_ref.dtype)

    out = jax.ShapeDtypeStruct((s, d), BF16)
    return pl.pallas_call(
        body, grid=(s // t,),
        in_specs=[_row_spec(t, width), _row_spec(t, width), _full_spec((width, d)), _full_spec((width, d)),
                  _row_spec(t, d, 0), _row_spec(t, d, 1)],
        out_specs=[_row_spec(t, d)] * 3, out_shape=[out] * 3,
        compiler_params=_cparams(("arbitrary",)), name=name)(y_a, y_b, w_pa, w_pb, proj, proj)


def postnorm_gate_bwd(g, out, gain, gate, w_out, proj, pa, pb, *, rider=None, name):
    s, d = pa.shape
    t = _row_tile(s)
    nr = rider.n if rider else 0
    hbm = pl.BlockSpec(memory_space=pl.ANY)

    def body(*refs):
        g_ref, o_ref, gn_ref, gt_ref, w_ref, ga_ref, gb_ref, pa_ref, pb_ref = refs[:9]
        do_ref, dpa_ref, dpb_ref, dp_ref, dgt_ref, dgn_ref = refs[9 + nr:15 + nr]
        i = pl.program_id(0)
        if rider:
            rider.emit(i, s // t, _rider_mid_step(s // t), refs[9:9 + nr], refs[15 + nr:15 + 2 * nr],
                       refs[15 + 2 * nr:])
        ov = o_ref[...]
        gv = g_ref[...]
        r = lax.rsqrt(_rowmean(ov * ov) + EPS)
        on = ov * r
        gain_v = gn_ref[...]
        dn = gv * gt_ref[...]
        don = dn * gain_v
        dout = (r * (don - on * _rowmean(don * on))).astype(BF16)
        do_ref[...] = dout
        p_gt = _colsum(gv * (on * gain_v))
        p_gn = _colsum(dn * on)

        @pl.when(i == 0)
        def _():
            dgt_ref[...] = p_gt
            dgn_ref[...] = p_gn

        @pl.when(i > 0)
        def _():
            dgt_ref[...] += p_gt
            dgn_ref[...] += p_gn

        dm = _dot_nt(dout, w_ref[...])
        sa = _sigmoid(ga_ref[...].astype(F32))
        sb = _sigmoid(gb_ref[...].astype(F32))
        dpa_ref[...] = (dm * sa).astype(dpa_ref.dtype)
        dpb_ref[...] = (dm * sb).astype(dpb_ref.dtype)
        dp_ref[:, :d] = (dm * pa_ref[...].astype(F32) * sa * (1.0 - sa)).astype(dp_ref.dtype)
        dp_ref[:, d:] = (dm * pb_ref[...].astype(F32) * sb * (1.0 - sb)).astype(dp_ref.dtype)

    act = jax.ShapeDtypeStruct((s, d), BF16)
    vec = jax.ShapeDtypeStruct((1, d), F32)
    return pl.pallas_call(
        body, grid=(s // t,),
        in_specs=[_row_spec(t, d), _row_spec(t, d), _vec_spec(d), _vec_spec(d), _full_spec((d, d)),
                  _row_spec(t, d, 0), _row_spec(t, d, 1), _row_spec(t, d), _row_spec(t, d)] + [hbm] * nr,
        out_specs=[_row_spec(t, d), _row_spec(t, d), _row_spec(t, d), _row_spec(t, 2 * d, 0),
                   _vec_spec(d), _vec_spec(d)] + [hbm] * nr,
        out_shape=[act, act, act, jax.ShapeDtypeStruct((s, IN_COLS), BF16), vec, vec]
        + (rider.out_shape if rider else []),
        scratch_shapes=rider.scratch if rider else [],
        compiler_params=_cparams(("arbitrary",)),
        name=name)(g, out, gain, gate, w_out, proj, proj, pa, pb, *(rider.arrs if rider else []))


def _pool_tile(s):
    return min(512, s)


def pool_fwd(proj, pw, ps, *, name):
    s = proj.shape[0]
    t = _pool_tile(s)
    pool_blk = GATE_COLS // (len(POOL_WINDOWS) * POOL_COLS)

    def body(p_ref, halo_ref, pw_ref, ps_ref, yb_ref, pooled_ref, mixed_ref):
        i = pl.program_id(0)
        halo = jnp.where(i == 0, 0.0, halo_ref[...].astype(F32))
        row = i * t + lax.broadcasted_iota(jnp.int32, (t, 1), 0)
        for g, w in enumerate(POOL_WINDOWS):
            vb = p_ref[:, g * POOL_COLS:g * POOL_COLS + POOL_GW].astype(F32)
            zb = p_ref[:, g * POOL_COLS + POOL_GW:(g + 1) * POOL_COLS].astype(F32)
            acc = jnp.concatenate([halo[:, g * POOL_COLS:g * POOL_COLS + POOL_GW], vb], axis=0)
            sh = 1
            while sh < w:
                acc = acc + pltpu.roll(acc, sh, axis=0)
                sh *= 2
            cnt = jnp.minimum(row + 1, w).astype(F32)
            pooled = acc[HALO:, :] / cnt - vb
            mixed = _dot(pooled.astype(BF16), pw_ref[g])
            cols = slice(g * POOL_GW, (g + 1) * POOL_GW)
            yb = mixed * ps_ref[:, cols] * (zb * _sigmoid(zb))
            yb_ref[:, cols] = yb.astype(yb_ref.dtype)
            pooled_ref[:, cols] = pooled.astype(pooled_ref.dtype)
            mixed_ref[:, cols] = mixed

    wide = len(POOL_WINDOWS) * POOL_COLS
    return pl.pallas_call(
        body, grid=(s // t,),
        in_specs=[pl.BlockSpec((t, wide), lambda i: (i, pool_blk)),
                  pl.BlockSpec((HALO, wide), lambda i: (jnp.maximum(i * (t // HALO) - 1, 0), pool_blk)),
                  pl.BlockSpec((len(POOL_WINDOWS), POOL_GW, POOL_GW), lambda i: (0, 0, 0)),
                  _vec_spec(WIDTH)],
        out_specs=[_row_spec(t, WIDTH)] * 3,
        out_shape=[jax.ShapeDtypeStruct((s, WIDTH), BF16), jax.ShapeDtypeStruct((s, WIDTH), BF16),
                   jax.ShapeDtypeStruct((s, WIDTH), F32)],
        compiler_params=_cparams(("arbitrary",)), name=name)(proj, proj, pw, ps)


def pool_bwd(dyb, proj, pooled, mixed, pw, ps, dproj, *, name):
    s = proj.shape[0]
    t = _pool_tile(s)
    nblk = s // t
    ng = len(POOL_WINDOWS)
    wide = ng * POOL_COLS
    pool_blk = GATE_COLS // wide

    def body(dy_ref, p_ref, pooled_ref, mixed_ref, pw_ref, ps_ref, dp_any, dp_ref, dpw_ref, dps_ref, carry):
        del dp_any
        i = pl.program_id(0)
        ii = nblk - 1 - i

        @pl.when(i == 0)
        def _():
            carry[...] = jnp.zeros_like(carry)
            dpw_ref[...] = jnp.zeros_like(dpw_ref)
            dps_ref[...] = jnp.zeros_like(dps_ref)

        row = ii * t + lax.broadcasted_iota(jnp.int32, (t, 1), 0)
        for g, w in enumerate(POOL_WINDOWS):
            cols = slice(g * POOL_GW, (g + 1) * POOL_GW)
            zb = p_ref[:, g * POOL_COLS + POOL_GW:(g + 1) * POOL_COLS].astype(F32)
            dy = dy_ref[:, cols]
            mx = mixed_ref[:, cols]
            sc = ps_ref[:, cols]
            sg = _sigmoid(zb)
            dzb = dy * (mx * sc) * (sg * (1.0 + zb * (1.0 - sg)))
            dpm = dy * (zb * sg)
            dps_ref[:, cols] += _colsum(dpm * mx)
            dmixed = (dpm * sc).astype(BF16)
            dpooled = _dot_nt(dmixed, pw_ref[g])
            dpw_ref[g] += _dot_tn(pooled_ref[:, cols], dmixed)
            cnt = jnp.minimum(row + 1, w).astype(F32)
            u = dpooled / cnt
            acc = jnp.concatenate([u, carry[:, cols]], axis=0)
            sh = 1
            while sh < w:
                acc = acc + pltpu.roll(acc, t + HALO - sh, axis=0)
                sh *= 2
            carry[:, cols] = u[:HALO, :]
            dp_ref[:, g * POOL_COLS:g * POOL_COLS + POOL_GW] = (acc[:t, :] - dpooled).astype(dp_ref.dtype)
            dp_ref[:, g * POOL_COLS + POOL_GW:(g + 1) * POOL_COLS] = dzb.astype(dp_ref.dtype)

    rev = lambda i: (nblk - 1 - i, 0)
    return pl.pallas_call(
        body, grid=(nblk,),
        in_specs=[pl.BlockSpec((t, WIDTH), rev),
                  pl.BlockSpec((t, wide), lambda i: (nblk - 1 - i, pool_blk)),
                  pl.BlockSpec((t, WIDTH), rev), pl.BlockSpec((t, WIDTH), rev),
                  pl.BlockSpec((ng, POOL_GW, POOL_GW), lambda i: (0, 0, 0)),
                  _vec_spec(WIDTH),
                  pl.BlockSpec(memory_space=pl.ANY)],
        out_specs=[pl.BlockSpec((t, wide), lambda i: (nblk - 1 - i, pool_blk)),
                   pl.BlockSpec((ng, POOL_GW, POOL_GW), lambda i: (0, 0, 0)),
                   _vec_spec(WIDTH)],
        out_shape=[jax.ShapeDtypeStruct(dproj.shape, dproj.dtype),
                   jax.ShapeDtypeStruct((ng, POOL_GW, POOL_GW), F32),
                   jax.ShapeDtypeStruct((1, WIDTH), F32)],
        scratch_shapes=[pltpu.VMEM((HALO, WIDTH), F32)],
        input_output_aliases={6: 0},
        compiler_params=_cparams(("arbitrary",)), name=name)(dyb, proj, pooled, mixed, pw, ps, dproj)


def _hgrn_tile(s):
    return min(2048, s)


def _chunk_consts():
    tt = lax.broadcasted_iota(jnp.int32, (CHUNK, CHUNK), 0)
    ss = lax.broadcasted_iota(jnp.int32, (CHUNK, CHUNK), 1)
    within = (ss <= tt) & (ss // SUB == tt // SUB)
    before = ss < (tt // SUB) * SUB
    cums = jnp.concatenate([within.astype(F32), before.astype(F32)], axis=0).astype(BF16)
    causal = ss <= tt
    upper = (ss >= tt).astype(F32).astype(BF16)
    row = lax.broadcasted_iota(jnp.int32, (CHUNK, 1), 0)
    return cums, causal, upper, row


def _dot_split(mat01, v):
    hi = v.astype(BF16)
    r1 = v - hi.astype(F32)
    mid = r1.astype(BF16)
    lo = (r1 - mid.astype(F32)).astype(BF16)
    return _dot(mat01, hi) + _dot(mat01, mid) + _dot(mat01, lo)


def _hgrn_chunks(qas, fas, lb, cums, row):
    gates = [_hgrn_gates(qa, fa, lb) for qa, fa in zip(qas, fas)]
    cbs = [_dot_split(cums, g["lf"]) for g in gates]
    return [_hgrn_decay(g, cb, row) for g, cb in zip(gates, cbs)]


def _hgrn_gates(qa, fa, lb):
    sq = _sigmoid(qa)
    sa = _sigmoid(fa)
    sna = 1.0 - sa
    oml = 1.0 - lb
    f = lb + oml * sa
    fc = jnp.maximum(f, MIN_FORGET)
    return dict(sq=sq, q=qa * sq, sa=sa, sna=sna, oml=oml, f=f, fc=fc, lf=jnp.log(fc), k=oml * sna)


def _hgrn_decay(g, cb, row):
    sq, q, sa, sna, oml, f, fc, k = (g[n] for n in ("sq", "q", "sa", "sna", "oml", "f", "fc", "k"))
    c = cb[:CHUNK]
    bt = cb[CHUNK:]
    ec = jnp.exp(c)
    enc = jnp.exp(jnp.minimum(-c, MAX_EXP))
    qt = q * ec
    kt = k * enc
    dms, lhs, rhs = [], [], []
    for j in range(N_SUB):
        bj = bt[j * SUB:j * SUB + 1, :]
        dm = jnp.where(row >= j * SUB, jnp.exp(jnp.minimum(bt - bj, 0.0)), 0.0)
        dms.append(dm)
        lhs.append(qt * dm)
        rhs.append(jnp.where(row // SUB == j, kt, 0.0))
    lhs = jnp.concatenate(lhs, axis=1).astype(BF16)
    rhs = jnp.concatenate(rhs, axis=1).astype(BF16)
    b = bt + c
    bl = b[CHUNK - 1:CHUNK, :]
    ebl = jnp.exp(bl)
    edec = jnp.exp(bl - b)
    eb = ec * dms[0]
    return dict(sq=sq, q=q, sa=sa, sna=sna, oml=oml, f=f, fc=fc, k=k, ec=ec, enc=enc, dms=dms,
                lhs=lhs, rhs=rhs, ebl=ebl, edec=edec, eb=eb, qd=q * eb, kdec=k * edec)


def _rider_mid_step(total):
    return total - max(1, total // 8)


def hgrn_fwd(proj, lb, hn, *, rider=None, name):
    s = proj.shape[0]
    t = _hgrn_tile(s)
    nblk = s // t
    ncht = t // CHUNK
    nr = rider.n if rider else 0
    hbm = pl.BlockSpec(memory_space=pl.ANY)

    def body(*refs):
        p_ref, lb_ref, hn_ref = refs[:3]
        ya_ref, o_ref, st_ref = refs[3 + nr:6 + nr]
        state = refs[6 + 2 * nr]
        i = pl.program_id(1)
        if rider:
            total = HEADS * nblk
            rider.emit(pl.program_id(0) * nblk + i, total, _rider_mid_step(total),
                       refs[3:3 + nr], refs[6 + nr:6 + 2 * nr], refs[7 + 2 * nr:])

        @pl.when(i == 0)
        def _():
            state[...] = jnp.zeros_like(state)

        cums, causal, _, row = _chunk_consts()
        lbv = lb_ref[...]
        hnv = hn_ref[...]

        rows = [slice(ci * CHUNK, (ci + 1) * CHUNK) for ci in range(ncht)]
        pres = _hgrn_chunks([p_ref[r, 0:HEAD_DIM] for r in rows], [p_ref[r, HEAD_DIM:2 * HEAD_DIM] for r in rows],
                            lbv, cums, row)
        vas = [p_ref[r, 2 * HEAD_DIM:3 * HEAD_DIM].astype(BF16) for r in rows]
        scores = [jnp.where(causal, _dot_nt(pre["lhs"], pre["rhs"]), 0.0).astype(BF16) for pre in pres]
        intra = [_dot(a, va) for a, va in zip(scores, vas)]
        qds = [pre["qd"].astype(BF16) for pre in pres]
        kdecs = [pre["kdec"].astype(BF16) for pre in pres]
        st = state[...]
        outs = []
        for ci in range(ncht):
            stb = st.astype(BF16)
            st_ref[ci, 0] = stb
            outs.append(intra[ci] + _dot_nt(qds[ci], stb))
            st = st * pres[ci]["ebl"] + _dot_tn(vas[ci], kdecs[ci])
        state[...] = st
        for r, o in zip(rows, outs):
            za = p_ref[r, 3 * HEAD_DIM:4 * HEAD_DIM]
            o_ref[r, :] = o
            ya_ref[r, :] = ((o * lax.rsqrt(_rowmean(o * o) + EPS)) * hnv * (za * _sigmoid(za))).astype(ya_ref.dtype)

    return pl.pallas_call(
        body, grid=(HEADS, nblk),
        in_specs=[pl.BlockSpec((t, HEAD_COLS), lambda h, i: (i, h)),
                  pl.BlockSpec((1, HEAD_DIM), lambda h, i: (0, h)),
                  pl.BlockSpec((1, HEAD_DIM), lambda h, i: (0, h))] + [hbm] * nr,
        out_specs=[pl.BlockSpec((t, HEAD_DIM), lambda h, i: (i, h)),
                   pl.BlockSpec((t, HEAD_DIM), lambda h, i: (i, h)),
                   pl.BlockSpec((ncht, 1, HEAD_DIM, HEAD_DIM), lambda h, i: (i, h, 0, 0))] + [hbm] * nr,
        out_shape=[jax.ShapeDtypeStruct((s, WIDTH), BF16), jax.ShapeDtypeStruct((s, WIDTH), F32),
                   jax.ShapeDtypeStruct((s // CHUNK, HEADS, HEAD_DIM, HEAD_DIM), BF16)]
        + (rider.out_shape if rider else []),
        scratch_shapes=[pltpu.VMEM((HEAD_DIM, HEAD_DIM), F32)] + (rider.scratch if rider else []),
        compiler_params=_cparams(("arbitrary", "arbitrary")), name=name)(proj, lb, hn, *(rider.arrs if rider else []))


def hgrn_bwd(dya, proj, o_all, states, lb, hn, dproj, *, rider=None, name):
    s = proj.shape[0]
    t = _hgrn_tile(s)
    nblk = s // t
    ncht = t // CHUNK
    head_blk0 = GP_COLS // HEAD_COLS
    nr = rider.n if rider else 0
    hbm = pl.BlockSpec(memory_space=pl.ANY)

    def body(*refs):
        dy_ref, p_ref, o_ref, st_ref, lb_ref, hn_ref = refs[:6]
        dp_ref, dhn_ref, dlb_ref = refs[7 + nr:10 + nr]
        dstate = refs[10 + 2 * nr]
        i = pl.program_id(1)
        if rider:
            total = HEADS * nblk
            rider.emit(pl.program_id(0) * nblk + i, total, _rider_mid_step(total),
                       refs[7:7 + nr], refs[10 + nr:10 + 2 * nr], refs[11 + 2 * nr:])

        @pl.when(i == 0)
        def _():
            dstate[...] = jnp.zeros_like(dstate)
            dhn_ref[...] = jnp.zeros_like(dhn_ref)
            dlb_ref[...] = jnp.zeros_like(dlb_ref)

        cums, causal, upper, row = _chunk_consts()
        lbv = lb_ref[...]
        hnv = hn_ref[...]

        chunks = range(ncht)
        rows = [slice(ci * CHUNK, (ci + 1) * CHUNK) for ci in chunks]
        qas = [p_ref[r, 0:HEAD_DIM] for r in rows]
        vbs = [p_ref[r, 2 * HEAD_DIM:3 * HEAD_DIM].astype(BF16) for r in rows]
        st0s = [st_ref[ci, 0] for ci in chunks]
        dzas, dobs = [], []
        dhn_acc = jnp.zeros_like(hnv)
        for r in rows:
            za = p_ref[r, 3 * HEAD_DIM:4 * HEAD_DIM]
            o = o_ref[r, :]
            dy = dy_ref[r, :]
            rn = lax.rsqrt(_rowmean(o * o) + EPS)
            on = o * rn
            sgz = _sigmoid(za)
            sz = za * sgz
            dzas.append(dy * on * hnv * (sgz * (1.0 + za * (1.0 - sgz))))
            dhn_acc = dhn_acc + _colsum(dy * on * sz)
            don = dy * hnv * sz
            dobs.append((rn * (don - on * _rowmean(don * on))).astype(BF16))
        pres = _hgrn_chunks(qas, [p_ref[r, HEAD_DIM:2 * HEAD_DIM] for r in rows], lbv, cums, row)
        scores = [jnp.where(causal, _dot_nt(pre["lhs"], pre["rhs"]), 0.0).astype(BF16) for pre in pres]
        das = [jnp.where(causal, _dot_nt(dob, vb), 0.0).astype(BF16) for dob, vb in zip(dobs, vbs)]
        dlhss = [_dot(da, pre["rhs"]) for da, pre in zip(das, pres)]
        drhss = [_dot_tn(da, pre["lhs"]) for da, pre in zip(das, pres)]
        dv_intra = [_dot_tn(a, dob) for a, dob in zip(scores, dobs)]
        dq_inter = [_dot(dob, st0) * pre["eb"] for dob, st0, pre in zip(dobs, st0s, pres)]
        qds = [pre["qd"].astype(BF16) for pre in pres]
        kdecs = [pre["kdec"].astype(BF16) for pre in pres]
        dst1 = dstate[...]
        dvs, dk_states, dbl_states = [None] * ncht, [None] * ncht, [None] * ncht
        for ci in reversed(chunks):
            dst1b = dst1.astype(BF16)
            dvs[ci] = dv_intra[ci] + _dot_nt(kdecs[ci], dst1b)
            dk_states[ci] = _dot(vbs[ci], dst1b) * pres[ci]["edec"]
            dbl_states[ci] = pres[ci]["ebl"] * _colsum(dst1 * st0s[ci].astype(F32))
            dst1 = dst1 * pres[ci]["ebl"] + _dot_tn(dobs[ci], qds[ci])
        dstate[...] = dst1
        dqs, dks, dbs, dbls = [], [], [], []
        for ci in chunks:
            pre = pres[ci]
            q, k = pre["q"], pre["k"]
            dq_a = jnp.zeros_like(q)
            dk_a = jnp.zeros_like(k)
            db = q * dq_inter[ci] - k * dk_states[ci]
            for j in range(N_SUB):
                cols = slice(j * HEAD_DIM, (j + 1) * HEAD_DIM)
                dlhs, drhs = dlhss[ci][:, cols], drhss[ci][:, cols]
                dq_a = dq_a + pre["dms"][j] * dlhs
                dk_a = dk_a + jnp.where(row // SUB == j, drhs, 0.0)
                db = db + (pre["lhs"][:, cols].astype(F32) * dlhs - pre["rhs"][:, cols].astype(F32) * drhs)
            dqs.append(dq_inter[ci] + pre["ec"] * dq_a)
            dks.append(dk_states[ci] + pre["enc"] * dk_a)
            dbs.append(db)
            dbls.append(_colsum(k * dk_states[ci]) + dbl_states[ci])
        dlfs = [_dot_split(upper, db) + dbl for db, dbl in zip(dbs, dbls)]
        dlb_acc = jnp.zeros_like(lbv)
        for ci in chunks:
            pre = pres[ci]
            sq, sa, sna, oml = pre["sq"], pre["sa"], pre["sna"], pre["oml"]
            dqa = dqs[ci] * (sq * (1.0 + qas[ci] * (1.0 - sq)))
            diff = jnp.where(pre["f"] >= MIN_FORGET, dlfs[ci] / pre["fc"], 0.0) - dks[ci]
            dlb_acc = dlb_acc + _colsum(diff * sna)
            dfa = diff * (oml * sa * sna)
            dp_ref[rows[ci], :] = jnp.concatenate([dqa, dfa, dvs[ci], dzas[ci]], axis=1).astype(dp_ref.dtype)
        dhn_ref[...] += dhn_acc
        dlb_ref[...] += dlb_acc

    rev = lambda h, i: (nblk - 1 - i, h)
    return pl.pallas_call(
        body, grid=(HEADS, nblk),
        in_specs=[pl.BlockSpec((t, HEAD_DIM), rev),
                  pl.BlockSpec((t, HEAD_COLS), lambda h, i: (nblk - 1 - i, h)),
                  pl.BlockSpec((t, HEAD_DIM), rev),
                  pl.BlockSpec((ncht, 1, HEAD_DIM, HEAD_DIM), lambda h, i: (nblk - 1 - i, h, 0, 0)),
                  pl.BlockSpec((1, HEAD_DIM), lambda h, i: (0, h)),
                  pl.BlockSpec((1, HEAD_DIM), lambda h, i: (0, h)),
                  hbm] + [hbm] * nr,
        out_specs=[pl.BlockSpec((t, HEAD_COLS), lambda h, i: (nblk - 1 - i, head_blk0 + h)),
                   pl.BlockSpec((1, HEAD_DIM), lambda h, i: (0, h)),
                   pl.BlockSpec((1, HEAD_DIM), lambda h, i: (0, h))] + [hbm] * nr,
        out_shape=[jax.ShapeDtypeStruct(dproj.shape, dproj.dtype),
                   jax.ShapeDtypeStruct((1, WIDTH), F32), jax.ShapeDtypeStruct((1, WIDTH), F32)]
        + (rider.out_shape if rider else []),
        scratch_shapes=[pltpu.VMEM((HEAD_DIM, HEAD_DIM), F32)] + (rider.scratch if rider else []),
        input_output_aliases={6: 0},
        compiler_params=_cparams(("arbitrary", "arbitrary")),
        name=name)(dya, proj, o_all, states, lb, hn, dproj, *(rider.arrs if rider else []))


def _softmax_rows(lower):
    mx = jnp.max(lower, axis=0, keepdims=True)
    e = jnp.exp(lower - mx)
    return e / jnp.sum(e, axis=0, keepdims=True)


def lb_table(lower, *, name):
    depth, w = lower.shape

    def body(l_ref, o_ref):
        sm = _softmax_rows(l_ref[...])
        acc = jnp.zeros((1, w), F32)
        o_ref[0:1, :] = acc
        for l in range(1, depth):
            acc = acc + sm[l:l + 1, :]
            o_ref[l:l + 1, :] = acc

    return pl.pallas_call(body, out_shape=jax.ShapeDtypeStruct((depth, w), F32), name=name)(lower)


def lb_table_bwd(lower, dlb, *, name):
    depth, w = lower.shape

    def body(l_ref, d_ref, o_ref):
        sm = _softmax_rows(l_ref[...])
        dlbv = d_ref[...]
        dsm = [jnp.zeros((1, w), F32)]
        for i in range(1, depth):
            acc = jnp.zeros((1, w), F32)
            for l in range(i, depth):
                acc = acc + dlbv[l:l + 1, :]
            dsm.append(acc)
        inner = jnp.zeros((1, w), F32)
        for i in range(depth):
            inner = inner + sm[i:i + 1, :] * dsm[i]
        for i in range(depth):
            o_ref[i:i + 1, :] = sm[i:i + 1, :] * (dsm[i] - inner)

    return pl.pallas_call(body, out_shape=jax.ShapeDtypeStruct((depth, w), F32), name=name)(lower, dlb)


def w_ada_grad(c_all, dmod_cols, *, name):
    depth, _, cols = dmod_cols.shape
    d = c_all.shape[1]

    def body(c_ref, dm_ref, o_ref):
        cv = c_ref[...]
        ca = cv * _sigmoid(cv)
        o_ref[...] = _dot_tn(ca, dm_ref[...])

    return pl.pallas_call(
        body, grid=(depth,),
        in_specs=[pl.BlockSpec((N_DEV, d), lambda l: (0, 0)), pl.BlockSpec((None, N_DEV, cols), lambda l: (l, 0, 0))],
        out_specs=pl.BlockSpec((None, d, cols), lambda l: (l, 0, 0)),
        out_shape=jax.ShapeDtypeStruct((depth, d, cols), F32),
        compiler_params=_cparams(("arbitrary",)), name=name)(c_all, dmod_cols)


def sum_parts(parts, *, name):
    p, r, c = parts.shape

    def body(p_ref, o_ref):
        acc = p_ref[0]
        for j in range(1, p):
            acc = acc + p_ref[j]
        o_ref[...] = acc

    return pl.pallas_call(body, out_shape=jax.ShapeDtypeStruct((r, c), F32), name=name)(parts)


def _adam_rows(r, c):
    tr = r
    while tr * c * 4 > (1 << 20) and tr % 16 == 0:
        tr //= 2
    return tr


def _adam_update(w_ref, m_ref, v_ref, g_ref, go_ref, d_ref, mo_ref, vo_ref):
    g = g_ref[0].astype(F32)
    for j in range(1, g_ref.shape[0]):
        g = g + g_ref[j].astype(F32)
    mn = ADAM_B1 * m_ref[...] + (1.0 - ADAM_B1) * g
    vn = ADAM_B2 * v_ref[...] + (1.0 - ADAM_B2) * (g * g)
    m_hat = mn / (1.0 - ADAM_B1 ** ADAM_STEP)
    v_hat = vn / (1.0 - ADAM_B2 ** ADAM_STEP)
    go_ref[...] = g
    d_ref[...] = -ADAM_LR * (m_hat / (jnp.sqrt(v_hat) + ADAM_EPS) + ADAM_WD * w_ref[...])
    mo_ref[...] = mn
    vo_ref[...] = vn


def adamw(w, m, v, gparts, *, name):
    r, c = w.shape
    p = gparts.shape[0]
    tr = _adam_rows(r, c)
    spec = pl.BlockSpec((tr, c), lambda i: (i, 0))
    shp = jax.ShapeDtypeStruct((r, c), F32)
    return pl.pallas_call(
        functools.partial(_adam_update), grid=(r // tr,),
        in_specs=[spec, spec, spec, pl.BlockSpec((p, tr, c), lambda i: (0, i, 0))],
        out_specs=[spec] * 4, out_shape=[shp] * 4,
        compiler_params=_cparams(("arbitrary",)), name=name)(w, m, v, gparts)


def adamw_layers(w, m, v, gparts, *, name):
    depth, r, c = w.shape
    p = gparts[0].shape[0]
    tr = _adam_rows(r, c)

    def body(w_ref, m_ref, v_ref, *rest):
        g_refs, outs = rest[:depth], rest[depth:]
        layer = pl.program_id(0)
        for k in range(depth):
            @pl.when(layer == k)
            def _(k=k):
                _adam_update(w_ref, m_ref, v_ref, g_refs[k], *outs)

    spec = pl.BlockSpec((None, tr, c), lambda l, i: (l, i, 0))
    g_specs = [pl.BlockSpec((p, tr, c), functools.partial(lambda l, i, k: (0, jnp.where(l == k, i, 0), 0), k=k))
               for k in range(depth)]
    shp = jax.ShapeDtypeStruct((depth, r, c), F32)
    return pl.pallas_call(
        body, grid=(depth, r // tr),
        in_specs=[spec, spec, spec] + g_specs,
        out_specs=[spec] * 4, out_shape=[shp] * 4,
        compiler_params=_cparams(("arbitrary", "arbitrary")), name=name)(w, m, v, *gparts)


def _position():
    x, y, c = lax.axis_index("x"), lax.axis_index("y"), lax.axis_index("c")
    return x, y, c


def _dev_index(x, y, c):
    return 4 * x + 2 * y + c


def _gather_phases(ins, outs, send_sems, recv_sems, local_sems):
    n = len(ins)
    x, y, c = _position()
    me, sibling = (x, y, c), (x, y, 1 - c)
    chips = [(1 - x, y), (x, 1 - y), (1 - x, 1 - y)]

    def copy(a, k, block, to, own=False):
        slot = outs[a].at[_dev_index(*block)]
        return pltpu.make_async_remote_copy(
            src_ref=ins[a] if own else slot, dst_ref=slot,
            send_sem=send_sems.at[a * 7 + k], recv_sem=recv_sems.at[a * 7 + k],
            device_id=to, device_id_type=MESH)

    def mine(a):
        return pltpu.make_async_copy(ins[a], outs[a].at[_dev_index(*me)], local_sems.at[a])

    def first(a):
        return [copy(a, 0, me, sibling, True)] + [copy(a, 1 + j, me, (*chip, c), True) for j, chip in enumerate(chips)]

    def passed(a):
        return [copy(a, 4 + j, (*chip, c), sibling) for j, chip in enumerate(chips)]

    def start():
        for a in range(n):
            mine(a).start()
        for a in range(n):
            for cp in first(a):
                cp.start()

    def mid():
        for j, chip in enumerate(chips):
            for a in range(n):
                copy(a, 1 + j, (*chip, c), me).wait_recv()
                passed(a)[j].start()

    def finish():
        for a in range(n):
            copy(a, 0, sibling, me).wait_recv()
            for j, chip in enumerate(chips):
                copy(a, 4 + j, (*chip, 1 - c), me).wait_recv()
        for a in range(n):
            for cp in first(a) + passed(a):
                cp.wait_send()
            mine(a).wait()

    return start, mid, finish


def _scatter_phases(ins, outs, send_sems, recv_sems, local_sems):
    n = len(ins)
    x, y, c = _position()
    me = _dev_index(x, y, c)

    def peer(r):
        return (x ^ (r >> 2), y ^ ((r >> 1) & 1), c ^ (r & 1))

    def copy(a, r):
        to = peer(r)
        return pltpu.make_async_remote_copy(
            src_ref=ins[a].at[_dev_index(*to)], dst_ref=outs[a].at[me],
            send_sem=send_sems.at[a * 7 + r - 1], recv_sem=recv_sems.at[a * 7 + r - 1],
            device_id=to, device_id_type=MESH)

    def arrival(a, r):
        return pltpu.make_async_remote_copy(
            src_ref=ins[a].at[me], dst_ref=outs[a].at[_dev_index(*peer(r))],
            send_sem=send_sems.at[a * 7 + r - 1], recv_sem=recv_sems.at[a * 7 + r - 1],
            device_id=peer(r), device_id_type=MESH)

    def mine(a):
        return pltpu.make_async_copy(ins[a].at[me], outs[a].at[me], local_sems.at[a])

    def start():
        for a in range(n):
            mine(a).start()
        for r in range(1, N_DEV):
            for a in range(n):
                copy(a, r).start()

    def finish():
        for r in range(1, N_DEV):
            for a in range(n):
                arrival(a, r).wait_recv()
        for r in range(1, N_DEV):
            for a in range(n):
                copy(a, r).wait_send()
        for a in range(n):
            mine(a).wait()

    return start, None, finish


class Rider:
    def __init__(self, kind, arrs):
        self.kind, self.arrs, self.n = kind, list(arrs), len(arrs)
        lead = (N_DEV,) if kind == "gather" else ()
        self.out_shape = [jax.ShapeDtypeStruct(lead + a.shape, a.dtype) for a in self.arrs]
        self.scratch = [pltpu.SemaphoreType.DMA((7 * self.n,)), pltpu.SemaphoreType.DMA((7 * self.n,)),
                        pltpu.SemaphoreType.DMA((self.n,))]

    def phases(self, ins, outs, sems):
        make = _gather_phases if self.kind == "gather" else _scatter_phases
        return make(ins, outs, *sems)

    def emit(self, step, total, mid_step, ins, outs, sems):
        start, mid, finish = self.phases(ins, outs, sems)
        pl.when(step == 0)(start)
        if mid is not None:
            pl.when(step == mid_step)(mid)
        pl.when(step == total - 1)(finish)


def _standalone(rider, name):
    n = rider.n
    hbm = pl.BlockSpec(memory_space=pl.ANY)

    def body(*refs):
        start, mid, finish = rider.phases(refs[:n], refs[n:2 * n], refs[2 * n:])
        start()
        if mid is not None:
            mid()
        finish()

    return pl.pallas_call(body, out_shape=rider.out_shape, in_specs=[hbm] * n, out_specs=[hbm] * n,
                          scratch_shapes=rider.scratch, name=name)(*rider.arrs)


def all_gather(arrs, *, name):
    return _standalone(Rider("gather", arrs), name)


def scatter_parts(arrs, *, name):
    return _standalone(Rider("scatter", arrs), name)


def mod_exchange(c_all, w_ada, b_cols, *, name):
    depth, d, cols = w_ada.shape
    hbm = pl.BlockSpec(memory_space=pl.ANY)
    vmem = pl.BlockSpec(memory_space=pltpu.VMEM)

    def body(c_ref, w_ref, b_ref, out_ref, wbuf, sendbuf, send_sems, recv_sems, load_sem):
        x, y, c = _position()
        me = _dev_index(x, y, c)
        cv = c_ref[...]
        ca = cv * _sigmoid(cv)
        for l in range(depth):
            load = pltpu.make_async_copy(w_ref.at[l], wbuf, load_sem)
            load.start()
            load.wait()
            part = jnp.dot(ca, wbuf[...], preferred_element_type=F32,
                           precision=lax.Precision.HIGHEST) + b_ref[l:l + 1, :]
            for bi in range(N_DEV):
                sendbuf[bi, l:l + 1, :] = part[bi:bi + 1, :]

        def peer(r):
            return (x ^ (r >> 2), y ^ ((r >> 1) & 1), c ^ (r & 1))

        def copy(r):
            to = peer(r)
            return pltpu.make_async_remote_copy(
                src_ref=sendbuf.at[_dev_index(*to)], dst_ref=out_ref.at[me],
                send_sem=send_sems.at[r - 1], recv_sem=recv_sems.at[r - 1],
                device_id=to, device_id_type=MESH)

        def arrival(r):
            return pltpu.make_async_remote_copy(
                src_ref=sendbuf.at[me], dst_ref=out_ref.at[_dev_index(*peer(r))],
                send_sem=send_sems.at[r - 1], recv_sem=recv_sems.at[r - 1],
                device_id=peer(r), device_id_type=MESH)

        out_ref[me] = sendbuf[me]
        sends = [copy(r) for r in range(1, N_DEV)]
        for cp in sends:
            cp.start()
        for r in range(1, N_DEV):
            arrival(r).wait_recv()
        for cp in sends:
            cp.wait_send()

    return pl.pallas_call(
        body,
        out_shape=jax.ShapeDtypeStruct((N_DEV, depth, cols), F32),
        in_specs=[vmem, hbm, vmem], out_specs=vmem,
        scratch_shapes=[pltpu.VMEM((d, cols), F32), pltpu.VMEM((N_DEV, depth, cols), F32),
                        pltpu.SemaphoreType.DMA((7,)), pltpu.SemaphoreType.DMA((7,)), pltpu.SemaphoreType.DMA],
        compiler_params=pltpu.CompilerParams(vmem_limit_bytes=VMEM_LIMIT),
        name=name)(c_all, w_ada, b_cols)


def kernel(x, c, w_ada, b_ada, norm_pre, norm_post, w_in, lower_bounds, hgrn_norm, pool_w, pool_scale, w_proj_a, w_proj_b, w_out, loss_target, m_w_ada, m_b_ada, m_norm_pre, m_norm_post, m_w_in, m_lower_bounds, m_hgrn_norm, m_pool_w, m_pool_scale, m_w_proj_a, m_w_proj_b, m_w_out, v_w_ada, v_b_ada, v_norm_pre, v_norm_post, v_w_in, v_lower_bounds, v_hgrn_norm, v_pool_w, v_pool_scale, v_w_proj_a, v_w_proj_b, v_w_out):
    depth = w_in.shape[0]
    d = D_MODEL
    ada_cols = w_ada.shape[2]
    xi, yi, ci = _position()
    me = _dev_index(xi, yi, ci)
    xs = x[0]
    target = loss_target[0]
    ng = len(POOL_WINDOWS)

    def shards(l):
        w_in_l = w_in[l].astype(BF16)
        half = w_in_l.shape[1] // 2
        return [w_in_l[:, :half], w_in_l[:, half:], w_proj_a[l].astype(BF16), w_proj_b[l].astype(BF16),
                w_out[l].astype(BF16), pool_w[l].astype(BF16)]

    def other_weights(g_pa, g_pb, g_out, g_pool):
        return dict(
            pa=jnp.transpose(g_pa, (1, 0, 2)).reshape(WIDTH, d),
            pb=jnp.transpose(g_pb, (1, 0, 2)).reshape(WIDTH, d),
            w_out=g_out.reshape(d, d),
            pool=jnp.transpose(g_pool, (1, 0, 2, 3)).reshape(ng, POOL_GW, POOL_GW))

    w_in_full = [permute_w_in(*all_gather(shards(0)[:2], name="gather_w_in"), name="permute_w_in")]
    others_full = []
    gathered = []

    (c_all,) = all_gather([c], name="gather_c")
    c_all = c_all.reshape(N_DEV, d)
    b_cols = lax.dynamic_slice_in_dim(b_ada, me * ada_cols, ada_cols, axis=1)
    mod_parts = mod_exchange(c_all, w_ada, b_cols, name="mod_exchange")
    mod = jnp.transpose(mod_parts, (1, 0, 2)).reshape(depth, 3 * d)
    lb_all = lb_table(lower_bounds, name="lb_table")

    saved = []
    cur = xs
    for l in range(depth):
        shift, scale, gate = mod[l:l + 1, :d], mod[l:l + 1, d:2 * d], mod[l:l + 1, 2 * d:]
        if l == 0:
            h = prenorm_fwd(cur, norm_pre[l:l + 1], shift, scale, name="prenorm_fwd")
        head_cols = (GP_COLS, IN_COLS - GP_COLS)
        if l + 1 < depth:
            proj_gp, nxt_lo = matmul_nn(h, w_in_full[l], cols=(0, GP_COLS), tm=2048, tn=1024, out_dtype=BF16,
                                        rider=Rider("gather", shards(l + 1)[:1]), name="mm_w_in_gp_gather")
            proj_h, nxt_hi = matmul_nn(h, w_in_full[l], cols=head_cols, tm=2048, tn=1024, out_dtype=F32,
                                       rider=Rider("gather", shards(l + 1)[1:2]), name="mm_w_in_heads_gather")
            w_in_full.append(permute_w_in(nxt_lo, nxt_hi, name="permute_w_in"))
        else:
            proj_gp = matmul_nn(h, w_in_full[l], cols=(0, GP_COLS), tm=2048, tn=1024, out_dtype=BF16,
                                name="mm_w_in_gp")
            proj_h = matmul_nn(h, w_in_full[l], cols=head_cols, tm=2048, tn=1024, out_dtype=F32,
                               name="mm_w_in_heads")
        ride = (shards(0)[2:] if l == 0 else []) + (shards(l + 1)[2:] if l + 1 < depth else [])
        if ride:
            y_a, o_all, states, *got = hgrn_fwd(proj_h, lb_all[l:l + 1], hgrn_norm[l:l + 1],
                                                 rider=Rider("gather", ride), name="hgrn_fwd_gather%d" % len(ride))
            for k in range(0, len(got), 4):
                others_full.append(other_weights(*got[k:k + 4]))
        else:
            y_a, o_all, states = hgrn_fwd(proj_h, lb_all[l:l + 1], hgrn_norm[l:l + 1], name="hgrn_fwd")
        w = dict(w_in=w_in_full[l], **others_full[l])
        gathered.append(w)
        y_b, pooled, mixed = pool_fwd(proj_gp, w["pool"], pool_scale[l:l + 1], name="pool_fwd")
        pa, pb, merged = proj_gate_fwd(y_a, y_b, w["pa"], w["pb"], proj_gp, name="proj_gate_fwd")
        saved.append(dict(x=cur, h=h, proj_gp=proj_gp, proj_h=proj_h, y_a=y_a, o=o_all, states=states, y_b=y_b,
                          pooled=pooled, mixed=mixed, pa=pa, pb=pb, merged=merged, scale=scale, gate=gate))
        if l + 1 < depth:
            nxt_mod = (norm_pre[l + 1:l + 2], mod[l + 1:l + 2, :d], mod[l + 1:l + 2, d:2 * d])
            saved[l]["out"], cur, h = w_out_postnorm_fwd(merged, w["w_out"], cur, norm_post[l:l + 1], gate, nxt_mod,
                                                          name="w_out_postnorm_prenorm_fwd")
        else:
            saved[l]["out"], cur = w_out_postnorm_fwd(merged, w["w_out"], cur, norm_post[l:l + 1], gate,
                                                       name="w_out_postnorm_fwd")

    g, loss_part = loss_head(cur, target, name="loss_head")
    loss = lax.psum(loss_part[0, 0], ("x", "y", "c"))

    small = [None] * depth
    big_in = [None] * depth
    big_others = [None] * depth
    pend_in, pend_others = None, None
    for l in reversed(range(depth)):
        w, sv = gathered[l], saved[l]
        gate_args = (g, sv["out"], norm_post[l:l + 1], sv["gate"], w["w_out"], sv["proj_gp"], sv["pa"], sv["pb"])
        if pend_others:
            dout, dpa, dpb, dproj, dgate, dnpost, *big_others[l + 1] = postnorm_gate_bwd(
                *gate_args, rider=Rider("scatter", pend_others), name="postnorm_gate_bwd_scatter")
        else:
            dout, dpa, dpb, dproj, dgate, dnpost = postnorm_gate_bwd(*gate_args, name="postnorm_gate_bwd")
        dw_out = matmul_tn(sv["merged"], dout, tm=2048, tn=1024, out_dtype=BF16, name="mm_w_out_dw")
        dya = matmul_nt(dpa, w["pa"], tm=1024, tn=2048, out_dtype=F32, name="mm_proj_a_dx")
        dyb = matmul_nt(dpb, w["pb"], tm=1024, tn=2048, out_dtype=F32, name="mm_proj_b_dx")
        dw_pa = matmul_tn(sv["y_a"], dpa, tm=2048, tn=2048, out_dtype=BF16, name="mm_proj_a_dw")
        dw_pb = matmul_tn(sv["y_b"], dpb, tm=2048, tn=2048, out_dtype=BF16, name="mm_proj_b_dw")
        dproj, dpool_w, dpool_scale = pool_bwd(dyb, sv["proj_gp"], sv["pooled"], sv["mixed"], w["pool"],
                                               pool_scale[l:l + 1], dproj, name="pool_bwd")
        by_owner = lambda t: jnp.transpose(t.reshape(WIDTH, N_DEV, d // N_DEV), (1, 0, 2))
        others = [by_owner(dw_pa), by_owner(dw_pb), dw_out.reshape(N_DEV, d // N_DEV, d),
                  jnp.transpose(dpool_w.astype(BF16).reshape(ng, N_DEV, POOL_GW // N_DEV, POOL_GW), (1, 0, 2, 3))]
        if l == 0:
            dproj, dhn, dlb, *big_others[0] = hgrn_bwd(dya, sv["proj_h"], sv["o"], sv["states"], lb_all[l:l + 1],
                                                       hgrn_norm[l:l + 1], dproj, rider=Rider("scatter", others),
                                                       name="hgrn_bwd_scatter")
        else:
            dproj, dhn, dlb = hgrn_bwd(dya, sv["proj_h"], sv["o"], sv["states"], lb_all[l:l + 1],
                                       hgrn_norm[l:l + 1], dproj, name="hgrn_bwd")
        above = Rider("scatter", [pend_in]) if pend_in is not None else None
        dx_args = (dproj, w["w_in"], sv["x"], norm_pre[l:l + 1], sv["scale"], g)
        if l > 0:
            if above:
                g, dshift, dscale, dnpre, big_in[l + 1] = w_in_dx_prenorm_bwd(*dx_args, rider=above,
                                                                              name="w_in_dx_prenorm_bwd_scatter")
            else:
                g, dshift, dscale, dnpre = w_in_dx_prenorm_bwd(*dx_args, name="w_in_dx_prenorm_bwd")
            dw_in = matmul_tn(sv["h"], dproj, tm=2048, tn=1024, out_dtype=BF16, name="mm_w_in_dw")
            pend_in, pend_others = unpermute_w_in(dw_in, name="unpermute_w_in"), others
        else:
            if above:
                dw_in, big_in[1] = matmul_tn(sv["h"], dproj, tm=2048, tn=1024, out_dtype=BF16, rider=above,
                                             name="mm_w_in_dw_scatter")
            else:
                dw_in = matmul_tn(sv["h"], dproj, tm=2048, tn=1024, out_dtype=BF16, name="mm_w_in_dw")
            g, dshift, dscale, dnpre, big_in[0] = w_in_dx_prenorm_bwd(
                *dx_args, rider=Rider("scatter", [unpermute_w_in(dw_in, name="unpermute_w_in")]),
                name="w_in_dx_prenorm_bwd_scatter")
        small[l] = jnp.concatenate([dshift, dscale, dgate, dnpre, dnpost, dlb, dhn, dpool_scale], axis=1)
    grad_x = g[None]
    big = [[big_in[l]] + list(big_others[l]) for l in range(depth)]

    small_mine = jnp.concatenate(small, axis=0)
    (small_all,) = all_gather([small_mine], name="gather_small")
    small_sum = sum_parts(small_all, name="sum_small")
    dmod_all = small_all[:, :, :3 * d]
    dmod_cols = jnp.transpose(lax.dynamic_slice_in_dim(dmod_all, me * ada_cols, ada_cols, axis=2), (1, 0, 2))
    g_w_ada = w_ada_grad(c_all, dmod_cols, name="w_ada_grad")
    off = 3 * d
    g_b_ada = small_sum[:, :off]
    g_npre = small_sum[:, off:off + d]
    g_npost = small_sum[:, off + d:off + 2 * d]
    g_lb_tab = small_sum[:, off + 2 * d:off + 2 * d + WIDTH]
    g_hn = small_sum[:, off + 2 * d + WIDTH:off + 2 * d + 2 * WIDTH]
    g_ps = small_sum[:, off + 2 * d + 2 * WIDTH:]
    g_lower = lb_table_bwd(lower_bounds, g_lb_tab, name="lb_table_bwd")

    def update(wt, mt, vt, gparts, shape2, name):
        outs = adamw(wt.reshape(shape2), mt.reshape(shape2), vt.reshape(shape2), gparts, name=name)
        return [o.reshape(wt.shape) for o in outs]

    def update_layers(wt, mt, vt, kind, name):
        shape3 = (depth, -1, wt.shape[-1])
        w3 = wt.reshape(shape3)
        gps = [big[l][kind].reshape((N_DEV,) + w3.shape[1:]) for l in range(depth)]
        outs = adamw_layers(w3, mt.reshape(shape3), vt.reshape(shape3), gps, name=name)
        return [o.reshape(wt.shape) for o in outs]

    def update_small(wt, mt, vt, gt, name):
        shape2 = (-1, wt.shape[-1])
        return update(wt, mt, vt, gt.reshape(shape2)[None], shape2, name)

    res = {
        "w_ada": update_small(w_ada, m_w_ada, v_w_ada, g_w_ada, "adamw_w_ada"),
        "b_ada": update_small(b_ada, m_b_ada, v_b_ada, g_b_ada, "adamw_b_ada"),
        "norm_pre": update_small(norm_pre, m_norm_pre, v_norm_pre, g_npre, "adamw_norm_pre"),
        "norm_post": update_small(norm_post, m_norm_post, v_norm_post, g_npost, "adamw_norm_post"),
        "w_in": update_layers(w_in, m_w_in, v_w_in, 0, "adamw_w_in"),
        "lower_bounds": update_small(lower_bounds, m_lower_bounds, v_lower_bounds, g_lower, "adamw_lower_bounds"),
        "hgrn_norm": update_small(hgrn_norm, m_hgrn_norm, v_hgrn_norm, g_hn, "adamw_hgrn_norm"),
        "pool_w": update_layers(pool_w, m_pool_w, v_pool_w, 4, "adamw_pool_w"),
        "pool_scale": update_small(pool_scale, m_pool_scale, v_pool_scale, g_ps, "adamw_pool_scale"),
        "w_proj_a": update_layers(w_proj_a, m_w_proj_a, v_w_proj_a, 1, "adamw_w_proj_a"),
        "w_proj_b": update_layers(w_proj_b, m_w_proj_b, v_w_proj_b, 2, "adamw_w_proj_b"),
        "w_out": update_layers(w_out, m_w_out, v_w_out, 3, "adamw_w_out"),
    }
    order = ["w_ada", "b_ada", "norm_pre", "norm_post", "w_in", "lower_bounds", "hgrn_norm", "pool_w",
             "pool_scale", "w_proj_a", "w_proj_b", "w_out"]
    outs = [loss, grad_x]
    for k in range(4):
        outs += [res[nm][k] for nm in order]
    return tuple(outs)
```
